```python
import math
import jax
import jax.numpy as jnp
from jax import lax
import numpy as np

D_MODEL = 1024
BATCH = 4
SEQ = 8192
DEPTH = 2

N_META = 16
ROPE_THETA = 500000.0
EPS = 1e-6
Q_BLOCK = 128
CHUNK = 128
NEG = -1e30

MLA_HEADS = 8
MLA_NOPE = 64
MLA_ROPE = 32
MLA_QK = MLA_NOPE + MLA_ROPE
MLA_V = 64
MLA_Q_RANK = 384
MLA_KV_RANK = 256

SSD_HEADS = 8
SSD_HEAD_DIM = 64
SSD_INNER = SSD_HEADS * SSD_HEAD_DIM
SSD_GROUPS = 2
SSD_STATE = 64
SSD_CONV = 5
SSD_CONV_CH = SSD_INNER + 2 * SSD_GROUPS * SSD_STATE

DIFF_HEADS = 4
DIFF_QK = 64
DIFF_V = 2 * DIFF_QK
DIFF_ROPE = DIFF_QK // 4

MLSTM_HEADS = 4
MLSTM_QK = 64
MLSTM_V = 128

D_MIX = MLA_HEADS * MLA_V + SSD_INNER + DIFF_HEADS * DIFF_V + MLSTM_HEADS * MLSTM_V
D_FF = -(-8 * D_MODEL // (3 * 256)) * 256

IN_SPLITS = (
    MLA_Q_RANK, MLA_KV_RANK, MLA_ROPE,
    SSD_INNER, SSD_CONV_CH, 2 * SSD_HEADS,
    2 * DIFF_HEADS * DIFF_QK, 2 * DIFF_HEADS * DIFF_QK, DIFF_HEADS * DIFF_V,
    MLSTM_HEADS * MLSTM_QK, MLSTM_HEADS * MLSTM_QK, MLSTM_HEADS * MLSTM_V,
    MLSTM_HEADS * MLSTM_V, 2 * MLSTM_HEADS, 2 * MLSTM_HEADS,
)
IN_COLS = sum(IN_SPLITS)

kernel_name = 'hybrid_mla_ssd_diff_mlstm_encoder'


def rms_norm(x, g):
    xf = x.astype(jnp.float32)
    y = xf * lax.rsqrt(jnp.mean(xf * xf, axis=-1, keepdims=True) + EPS)
    return (y * g.astype(jnp.float32)).astype(x.dtype)


def split_columns(u, sizes):
    out, start = [], 0
    for s in sizes:
        out.append(u[..., start:start + s])
        start += s
    return out


def rope_tables(n_pos, rot_dim):
    inv = 1.0 / (ROPE_THETA ** (jnp.arange(0, rot_dim, 2, dtype=jnp.float32) / rot_dim))
    ang = jnp.arange(n_pos, dtype=jnp.float32)[:, None] * inv[None, :]
    return jnp.cos(ang), jnp.sin(ang)


def apply_partial_rope(x, cos, sin):
    r = cos.shape[-1]
    c = cos[None, :, None, :].astype(x.dtype)
    s = sin[None, :, None, :].astype(x.dtype)
    x1, x2, rest = x[..., :r], x[..., r:2 * r], x[..., 2 * r:]
    return jnp.concatenate([x1 * c - x2 * s, x2 * c + x1 * s, rest], axis=-1)


def softmax_probs(qb, k, scale):
    s = jnp.einsum('bqhd,bkhd->bhqk', qb, k).astype(jnp.float32) * scale
    return jax.nn.softmax(s, axis=-1)


def sweep_query_blocks(attend, q):
    b, t, h, d = q.shape
    n_blk = (t - N_META) // Q_BLOCK
    out_meta = attend(q[:, :N_META])
    q_blocks = jnp.transpose(q[:, N_META:].reshape(b, n_blk, Q_BLOCK, h, d), (1, 0, 2, 3, 4))
    ob = lax.map(attend, q_blocks)
    out_real = jnp.transpose(ob, (1, 0, 2, 3, 4)).reshape((b, n_blk * Q_BLOCK) + ob.shape[3:])
    return jnp.concatenate([out_meta, out_real], axis=1)


def pad_front(u, n, value=0.0):
    return jnp.pad(u, [(0, 0), (n, 0)] + [(0, 0)] * (u.ndim - 2), constant_values=value)


def mla_mixer(c_q, c_kv, k_rope, q_norm_g, kv_norm_g, w_uq, w_ukv, q_head_g, k_head_g, out_g, cos, sin):
    b, t, _ = c_q.shape
    q = (rms_norm(c_q, q_norm_g) @ w_uq).reshape(b, t, MLA_HEADS, MLA_QK)
    kv = (rms_norm(c_kv, kv_norm_g) @ w_ukv).reshape(b, t, MLA_HEADS, MLA_NOPE + MLA_V)
    k_nope, v = kv[..., :MLA_NOPE], kv[..., MLA_NOPE:]
    k_r = jnp.broadcast_to(k_rope[:, :, None, :], (b, t, MLA_HEADS, MLA_ROPE))
    k = jnp.concatenate([k_r, k_nope], axis=-1)
    q = apply_partial_rope(rms_norm(q, q_head_g), cos, sin)
    k = apply_partial_rope(rms_norm(k, k_head_g), cos, sin)
    scale = MLA_QK ** -0.5

    def attend(qb):
        p = softmax_probs(qb, k, scale)
        return jnp.einsum('bhqk,bkhd->bqhd', p.astype(v.dtype), v)

    o = sweep_query_blocks(attend, q).reshape(b, t, MLA_HEADS * MLA_V)
    return rms_norm(o, out_g)


def depthwise_conv_centred(u, w, bias):
    y = lax.conv_general_dilated(
        u, w[:, None, :].astype(u.dtype), window_strides=(1,),
        padding=[(SSD_CONV // 2, SSD_CONV // 2)],
        dimension_numbers=('NWC', 'WIO', 'NWC'), feature_group_count=u.shape[-1])
    return y + bias.astype(u.dtype)


def segsum(a):
    l = a.shape[-1]
    xr = jnp.broadcast_to(a[..., :, None], a.shape + (l,))
    xr = jnp.where(jnp.tril(jnp.ones((l, l), dtype=bool), -1), xr, 0.0)
    out = jnp.cumsum(xr, axis=-2)
    return jnp.where(jnp.tril(jnp.ones((l, l), dtype=bool)), out, -jnp.inf)


def ssd_chunked(x, a, bm, cm):
    b, t, h, p = x.shape
    c = t // CHUNK
    x = x.reshape(b, c, CHUNK, h, p)
    bm = bm.reshape(b, c, CHUNK, h, -1)
    cm = cm.reshape(b, c, CHUNK, h, -1)
    a = jnp.transpose(a.reshape(b, c, CHUNK, h), (0, 3, 1, 2))
    a_cs = jnp.cumsum(a, axis=-1)
    scores = jnp.einsum('bclhn,bcshn->bhcls', cm, bm) * jnp.exp(segsum(a))
    y_diag = jnp.einsum('bhcls,bcshp->bclhp', scores, x)
    decay_states = jnp.transpose(jnp.exp(a_cs[..., -1:] - a_cs), (0, 2, 3, 1))[..., None]
    states = jnp.einsum('bclhn,bclhp->bchpn', bm * decay_states, x)
    states = jnp.concatenate([jnp.zeros_like(states[:, :1]), states], axis=1)
    decay_chunk = jnp.exp(segsum(jnp.pad(a_cs[..., -1], [(0, 0), (0, 0), (1, 0)])))
    states = jnp.einsum('bhzc,bchpn->bzhpn', decay_chunk, states)[:, :-1]
    state_decay = jnp.transpose(jnp.exp(a_cs), (0, 2, 3, 1))[..., None]
    y_off = jnp.einsum('bclhn,bchpn->bclhp', cm, states) * state_decay
    return (y_diag + y_off).reshape(b, t, h, p)


def ssd_direction(xs, dt, a, bm, cm, reverse):
    n_pad = CHUNK - N_META

    def prep(u):
        u = pad_front(u, n_pad)
        return u[:, ::-1] if reverse else u

    heads_per_group = SSD_HEADS // SSD_GROUPS
    y = ssd_chunked(prep(xs * dt[..., None]), prep(dt * a),
                    prep(jnp.repeat(bm, heads_per_group, axis=2)),
                    prep(jnp.repeat(cm, heads_per_group, axis=2)))
    y = y[:, ::-1] if reverse else y
    return y[:, n_pad:]


def ssd_mixer(z, xbc, dt_raw, conv_w, conv_b, dt_bias, a_log, d_skip, norm_g):
    b, t, _ = xbc.shape
    xbc = jax.nn.silu(depthwise_conv_centred(xbc, conv_w, conv_b))
    xs, bm, cm = split_columns(xbc, (SSD_INNER, SSD_GROUPS * SSD_STATE, SSD_GROUPS * SSD_STATE))
    xs = xs.reshape(b, t, SSD_HEADS, SSD_HEAD_DIM)
    bm = bm.reshape(b, t, SSD_GROUPS, SSD_STATE)
    cm = cm.reshape(b, t, SSD_GROUPS, SSD_STATE)
    dt = jax.nn.softplus(dt_raw.reshape(b, t, 2, SSD_HEADS).astype(jnp.float32) + dt_bias.astype(jnp.float32))
    a = -jnp.exp(a_log.astype(jnp.float32))
    y = (ssd_direction(xs, dt[:, :, 0], a[0], bm, cm, False)
         + ssd_direction(xs, dt[:, :, 1], a[1], bm, cm, True)
         + d_skip[None, None, :, None] * xs)
    y = y.reshape(b, t, SSD_INNER)
    return rms_norm(y * jax.nn.silu(z), norm_g)


def diff_mixer(q, k, v, q_head_g, k_head_g, lam, out_g, lambda_init, cos, sin):
    b, t, _ = q.shape
    q = apply_partial_rope(rms_norm(q.reshape(b, t, 2 * DIFF_HEADS, DIFF_QK), q_head_g), cos, sin)
    k = apply_partial_rope(rms_norm(k.reshape(b, t, 2 * DIFF_HEADS, DIFF_QK), k_head_g), cos, sin)
    v = v.reshape(b, t, DIFF_HEADS, DIFF_V)
    lam = lam.astype(jnp.float32)
    lam_full = jnp.exp(jnp.sum(lam[0] * lam[1])) - jnp.exp(jnp.sum(lam[2] * lam[3])) + lambda_init
    scale = DIFF_QK ** -0.5

    def attend(qb):
        p = softmax_probs(qb, k, scale).reshape(b, DIFF_HEADS, 2, qb.shape[1], t)
        amap = p[:, :, 0] - lam_full * p[:, :, 1]
        return jnp.einsum('bhqk,bkhd->bqhd', amap.astype(v.dtype), v)

    o = sweep_query_blocks(attend, q)
    o = rms_norm(o, out_g) * (1.0 - lambda_init)
    return o.reshape(b, t, DIFF_HEADS * DIFF_V)


def mlstm_chunk(carry, inputs):
    c_st, n_st, m_st = carry
    qc, kc, vc, ic, fc = inputs
    ic = jnp.swapaxes(ic, 1, 2)
    fc = jnp.swapaxes(fc, 1, 2)
    bcum = jnp.cumsum(fc, axis=-1)
    causal = jnp.tril(jnp.ones((CHUNK, CHUNK), dtype=bool))
    dmat = jnp.where(causal, bcum[..., :, None] - bcum[..., None, :] + ic[..., None, :], -jnp.inf)
    inter = bcum + m_st[..., None]
    m_row = jnp.maximum(jnp.max(dmat, axis=-1), inter)
    w_intra = jnp.exp(dmat - m_row[..., None])
    w_inter = jnp.exp(inter - m_row)
    s = jnp.einsum('blhd,bshd->bhls', qc, kc).astype(jnp.float32) * w_intra
    num = (jnp.einsum('bhls,bshv->blhv', s, vc)
           + jnp.swapaxes(w_inter, 1, 2)[..., None] * jnp.einsum('bhvd,blhd->blhv', c_st, qc))
    den = jnp.sum(s, axis=-1) + w_inter * jnp.einsum('bhd,blhd->bhl', n_st, qc)
    den = jnp.maximum(jnp.abs(den), jnp.exp(-m_row))
    h = num / jnp.swapaxes(den, 1, 2)[..., None]
    g_last = bcum[..., -1]
    d_last = g_last[..., None] - bcum + ic
    m_new = jnp.maximum(g_last + m_st, jnp.max(d_last, axis=-1))
    w_s = jnp.swapaxes(jnp.exp(d_last - m_new[..., None]), 1, 2)[..., None]
    w_prev = jnp.exp(g_last + m_st - m_new)
    c_new = w_prev[..., None, None] * c_st + jnp.einsum('bshv,bshd->bhvd', vc * w_s, kc)
    n_new = w_prev[..., None] * n_st + jnp.sum(kc * w_s, axis=1)
    return (c_new.astype(jnp.float32), n_new.astype(jnp.float32), m_new.astype(jnp.float32)), h.astype(jnp.float32)


def mlstm_direction(q, k, v, i_log, f_log, reverse):
    n_pad = CHUNK - N_META

    def prep(u, value=0.0):
        u = pad_front(u, n_pad, value)
        return u[:, ::-1] if reverse else u

    qp, kp, vp = prep(q), prep(k), prep(v)
    ip, fp = prep(i_log, NEG), prep(f_log)
    b, t = qp.shape[:2]
    c = t // CHUNK

    def to_chunks(u):
        return jnp.moveaxis(u.reshape((b, c, CHUNK) + u.shape[2:]), 1, 0)

    init = (jnp.zeros((b, MLSTM_HEADS, MLSTM_V, MLSTM_QK), jnp.float32),
            jnp.zeros((b, MLSTM_HEADS, MLSTM_QK), jnp.float32),
            jnp.full((b, MLSTM_HEADS), NEG, jnp.float32))
    _, h = lax.scan(mlstm_chunk, init, (to_chunks(qp), to_chunks(kp), to_chunks(vp), to_chunks(ip), to_chunks(fp)))
    h = jnp.moveaxis(h, 0, 1).reshape(b, t, MLSTM_HEADS, MLSTM_V)
    h = h[:, ::-1] if reverse else h
    return h[:, n_pad:]


def mlstm_mixer(q, k, v, o_raw, i_raw, f_raw, i_bias, f_bias, out_g):
    b, t, _ = q.shape
    q = q.reshape(b, t, MLSTM_HEADS, MLSTM_QK) * (MLSTM_QK ** -0.5)
    k = k.reshape(b, t, MLSTM_HEADS, MLSTM_QK)
    v = v.reshape(b, t, MLSTM_HEADS, MLSTM_V)
    i_log = i_raw.reshape(b, t, 2, MLSTM_HEADS).astype(jnp.float32) + i_bias.astype(jnp.float32)
    f_log = jax.nn.log_sigmoid(f_raw.reshape(b, t, 2, MLSTM_HEADS).astype(jnp.float32) + f_bias.astype(jnp.float32))
    h = (mlstm_direction(q, k, v, i_log[:, :, 0], f_log[:, :, 0], False)
         + mlstm_direction(q, k, v, i_log[:, :, 1], f_log[:, :, 1], True))
    h = rms_norm(h, out_g).reshape(b, t, MLSTM_HEADS * MLSTM_V)
    return jax.nn.sigmoid(o_raw) * h


def setup_inputs(seed: int = 0) -> dict:
    key = jax.random.key(seed)
    ks = iter(jax.random.split(key, 32))
    L = DEPTH

    def nrm(shape, scale):
        return scale * jax.random.normal(next(ks), shape, jnp.float32)

    def gain(shape):
        return 1.0 + 0.02 * jax.random.normal(next(ks), shape, jnp.float32)

    x = nrm((BATCH, SEQ, D_MODEL), 1.0)
    meta_tokens = nrm((N_META, D_MODEL), 1.0)
    attn_norm_g = gain((L, D_MODEL))
    w_in = nrm((L, D_MODEL, IN_COLS), D_MODEL ** -0.5)
    mla_q_norm_g = gain((L, MLA_Q_RANK))
    mla_kv_norm_g = gain((L, MLA_KV_RANK))
    mla_w_uq = nrm((L, MLA_Q_RANK, MLA_HEADS * MLA_QK), MLA_Q_RANK ** -0.5)
    mla_w_ukv = nrm((L, MLA_KV_RANK, MLA_HEADS * (MLA_NOPE + MLA_V)), MLA_KV_RANK ** -0.5)
    mla_q_head_g = gain((L, MLA_QK))
    mla_k_head_g = gain((L, MLA_QK))
    mla_out_g = gain((L, MLA_HEADS * MLA_V))
    ssd_conv_w = nrm((L, SSD_CONV, SSD_CONV_CH), SSD_CONV ** -0.5)
    ssd_conv_b = nrm((L, SSD_CONV_CH), 0.02)
    dt0 = jnp.exp(jax.random.uniform(next(ks), (L, 2, SSD_HEADS), jnp.float32, math.log(1e-3), math.log(1e-1)))
    ssd_dt_bias = dt0 + jnp.log(-jnp.expm1(-dt0))
    ssd_a_log = jnp.log(jax.random.uniform(next(ks), (L, 2, SSD_HEADS), jnp.float32, 1.0, 16.0))
    ssd_d = gain((L, SSD_HEADS))
    ssd_norm_g = gain((L, SSD_INNER))
    diff_q_head_g = gain((L, DIFF_QK))
    diff_k_head_g = gain((L, DIFF_QK))
    diff_lambda = nrm((L, 4, DIFF_QK), 0.1)
    diff_out_g = gain((L, DIFF_V))
    mlstm_i_bias = nrm((L, 2, MLSTM_HEADS), 0.1)
    mlstm_f_bias = 3.0 + nrm((L, 2, MLSTM_HEADS), 0.5)
    mlstm_out_g = gain((L, MLSTM_V))
    w_out = nrm((L, D_MIX, D_MODEL), D_MIX ** -0.5)
    ffn_norm_g = gain((L, D_MODEL))
    w_ffn_in = nrm((L, D_MODEL, 2 * D_FF), D_MODEL ** -0.5)
    w_ffn_out = nrm((L, D_FF, D_MODEL), D_FF ** -0.5)
    return {'x': x, 'meta_tokens': meta_tokens, 'attn_norm_g': attn_norm_g, 'w_in': w_in,
            'mla_q_norm_g': mla_q_norm_g, 'mla_kv_norm_g': mla_kv_norm_g, 'mla_w_uq': mla_w_uq,
            'mla_w_ukv': mla_w_ukv, 'mla_q_head_g': mla_q_head_g, 'mla_k_head_g': mla_k_head_g,
            'mla_out_g': mla_out_g, 'ssd_conv_w': ssd_conv_w, 'ssd_conv_b': ssd_conv_b,
            'ssd_dt_bias': ssd_dt_bias, 'ssd_a_log': ssd_a_log, 'ssd_d': ssd_d, 'ssd_norm_g': ssd_norm_g,
            'diff_q_head_g': diff_q_head_g, 'diff_k_head_g': diff_k_head_g, 'diff_lambda': diff_lambda,
            'diff_out_g': diff_out_g, 'mlstm_i_bias': mlstm_i_bias, 'mlstm_f_bias': mlstm_f_bias,
            'mlstm_out_g': mlstm_out_g, 'w_out': w_out, 'ffn_norm_g': ffn_norm_g,
            'w_ffn_in': w_ffn_in, 'w_ffn_out': w_ffn_out}


def reference(x, meta_tokens, attn_norm_g, w_in, mla_q_norm_g, mla_kv_norm_g, mla_w_uq, mla_w_ukv,
              mla_q_head_g, mla_k_head_g, mla_out_g, ssd_conv_w, ssd_conv_b, ssd_dt_bias, ssd_a_log,
              ssd_d, ssd_norm_g, diff_q_head_g, diff_k_head_g, diff_lambda, diff_out_g,
              mlstm_i_bias, mlstm_f_bias, mlstm_out_g, w_out, ffn_norm_g, w_ffn_in, w_ffn_out):
    b = x.shape[0]
    meta = jnp.broadcast_to(meta_tokens[None].astype(x.dtype), (b, N_META, D_MODEL))
    h = jnp.concatenate([meta, x], axis=1)
    t = h.shape[1]
    cos_d, sin_d = rope_tables(t, DIFF_ROPE)
    cos_m, sin_m = rope_tables(t, MLA_ROPE)
    for l in range(DEPTH):
        lambda_init = 0.8 - 0.6 * math.exp(-0.3 * l)
        hn = rms_norm(h, attn_norm_g[l])
        proj = hn @ w_in[l]
        (c_q, c_kv, k_rope, ssd_z, ssd_xbc, ssd_dt, d_q, d_k, d_v,
         m_q, m_k, m_v, m_o, m_i, m_f) = split_columns(proj, IN_SPLITS)
        y_a = mla_mixer(c_q, c_kv, k_rope, mla_q_norm_g[l], mla_kv_norm_g[l], mla_w_uq[l], mla_w_ukv[l],
                        mla_q_head_g[l], mla_k_head_g[l], mla_out_g[l], cos_m, sin_m)
        y_b = ssd_mixer(ssd_z, ssd_xbc, ssd_dt, ssd_conv_w[l], ssd_conv_b[l], ssd_dt_bias[l],
                        ssd_a_log[l], ssd_d[l], ssd_norm_g[l])
        y_c = diff_mixer(d_q, d_k, d_v, diff_q_head_g[l], diff_k_head_g[l], diff_lambda[l],
                         diff_out_g[l], lambda_init, cos_d, sin_d)
        y_d = mlstm_mixer(m_q, m_k, m_v, m_o, m_i, m_f, mlstm_i_bias[l], mlstm_f_bias[l], mlstm_out_g[l])
        mixed = jnp.concatenate([y_a, y_b, y_c, y_d], axis=-1)
        h = h + mixed @ w_out[l]
        hn = rms_norm(h, ffn_norm_g[l])
        gate, up = jnp.split(hn @ w_ffn_in[l], 2, axis=-1)
        h = h + (jax.nn.silu(gate) * up) @ w_ffn_out[l]
    return h[:, N_META:].astype(x.dtype)
```

```python
import functools
import math

import jax
import jax.numpy as jnp
from jax import lax
from jax.experimental import pallas as pl
from jax.experimental.pallas import tpu as pltpu

N_META = 16
ROPE_THETA = 500000.0
EPS = 1e-6
CHUNK = 128
N_PAD = CHUNK - N_META
NEG = -1e30
LOG2E = 1.4426950408889634

MLA_HEADS, MLA_NOPE, MLA_ROPE, MLA_V = 8, 64, 32, 64
MLA_QK = MLA_NOPE + MLA_ROPE
MLA_Q_RANK, MLA_KV_RANK = 384, 256
SSD_HEADS, SSD_HEAD_DIM, SSD_GROUPS, SSD_STATE, SSD_CONV = 8, 64, 2, 64, 5
SSD_INNER = SSD_HEADS * SSD_HEAD_DIM
SSD_CONV_CH = SSD_INNER + 2 * SSD_GROUPS * SSD_STATE
DIFF_HEADS, DIFF_QK = 4, 64
DIFF_V = 2 * DIFF_QK
DIFF_ROPE = DIFF_QK // 4
MLSTM_HEADS, MLSTM_QK, MLSTM_V = 4, 64, 128
HEAD_PAD = 128
LANE = 128
VMEM_LIMIT = 52 * 1024 * 1024

F32 = jnp.float32
BF16 = jnp.bfloat16


def _dot(a, b):
    return jnp.dot(a, b, preferred_element_type=F32)


def _dot_nt(a, b):
    return lax.dot_general(a, b, (((1,), (1,)), ((), ())), preferred_element_type=F32)


def _rms_rows(x, g):
    ms = jnp.mean(x * x, axis=-1, keepdims=True)
    return x * lax.rsqrt(ms + EPS) * g


def _split3(a):
    hi = a.astype(BF16)
    r1 = a - hi.astype(F32)
    mid = r1.astype(BF16)
    lo = (r1 - mid.astype(F32)).astype(BF16)
    return hi, mid, lo


def _cumsum_cols(tri, a):
    hi, mid, lo = _split3(a)
    return _dot(tri, hi) + _dot(tri, mid) + _dot(tri, lo)


def _cumsum_rows(a, tri):
    hi, mid, lo = _split3(a)
    return _dot(hi, tri) + _dot(mid, tri) + _dot(lo, tri)


def _softplus(x):
    return jnp.maximum(x, 0.0) + jnp.log(1.0 + jnp.exp(-jnp.abs(x)))


def _log_sigmoid(x):
    return jnp.minimum(x, 0.0) - jnp.log(1.0 + jnp.exp(-jnp.abs(x)))


def _sigmoid(x):
    return 1.0 / (1.0 + jnp.exp(-x))


def _pick_tile(n, candidates):
    for c in candidates:
        if n % c == 0:
            return c
    raise ValueError(f"no tile in {candidates} divides {n}")


def _params(*sem):
    return pltpu.CompilerParams(dimension_semantics=sem, vmem_limit_bytes=VMEM_LIMIT)


def _norm_rope_t(blk, g_col, cos, sin, n_real):
    r = cos.shape[0]
    ms = jnp.sum(blk * blk, axis=0, keepdims=True) * (1.0 / n_real)
    y = blk * lax.rsqrt(ms + EPS) * g_col
    x1, x2, rest = y[:r], y[r:2 * r], y[2 * r:]
    return jnp.concatenate([x1 * cos - x2 * sin, x2 * cos + x1 * sin, rest], axis=0)


def _mla_in_kernel(h_ref, g_ref, wa_ref, gq_ref, gkv_ref, wuq_ref, wuk_ref, wuv_ref, wkr_ref,
                   qhg_ref, khg_ref, cos_ref, sin_ref, qT_ref, k_ref, vT_ref):
    hn = _rms_rows(h_ref[0], g_ref[...]).astype(BF16)
    acc = _dot(hn, wa_ref[...])
    cqn = _rms_rows(acc[:, :MLA_Q_RANK], gq_ref[...]).astype(BF16)
    ckvn = _rms_rows(acc[:, MLA_Q_RANK:], gkv_ref[...]).astype(BF16)
    qT = _dot_nt(wuq_ref[...], cqn)
    knT = _dot_nt(wuk_ref[...], ckvn)
    vT = _dot_nt(wuv_ref[...], ckvn)
    krT = _dot_nt(wkr_ref[...], hn)
    cos, sin = cos_ref[...], sin_ref[...]
    t = krT.shape[1]
    zpad = jnp.zeros((HEAD_PAD - MLA_QK, t), F32)
    q_scale = (MLA_QK ** -0.5) * LOG2E
    for h in range(MLA_HEADS):
        qb = _norm_rope_t(qT[h * HEAD_PAD:(h + 1) * HEAD_PAD], qhg_ref[...], cos, sin, MLA_QK)
        qT_ref[0, h * HEAD_PAD:(h + 1) * HEAD_PAD, :] = (qb * q_scale).astype(BF16)
        kb = jnp.concatenate([krT, knT[h * MLA_NOPE:(h + 1) * MLA_NOPE], zpad], axis=0)
        kb = _norm_rope_t(kb, khg_ref[...], cos, sin, MLA_QK)
        k_ref[0, h] = kb.T.astype(BF16)
    vT_ref[0] = vT.astype(BF16)


def _diff_in_kernel(h_ref, g_ref, wc_ref, qhg_ref, khg_ref, cos_ref, sin_ref, qT_ref, k_ref, vT_ref):
    hn = _rms_rows(h_ref[0], g_ref[...]).astype(BF16)
    pT = _dot_nt(wc_ref[...], hn)
    cos, sin = cos_ref[...], sin_ref[...]
    t = pT.shape[1]
    zpad = jnp.zeros((HEAD_PAD - DIFF_QK, t), F32)
    nq = 2 * DIFF_HEADS * DIFF_QK
    q_scale = (DIFF_QK ** -0.5) * LOG2E
    for h in range(2 * DIFF_HEADS):
        qb = jnp.concatenate([pT[h * DIFF_QK:(h + 1) * DIFF_QK], zpad], axis=0)
        qb = _norm_rope_t(qb, qhg_ref[...], cos, sin, DIFF_QK)
        qT_ref[0, h * HEAD_PAD:(h + 1) * HEAD_PAD, :] = (qb * q_scale).astype(BF16)
        kb = jnp.concatenate([pT[nq + h * DIFF_QK:nq + (h + 1) * DIFF_QK], zpad], axis=0)
        kb = _norm_rope_t(kb, khg_ref[...], cos, sin, DIFF_QK)
        k_ref[0, h] = kb.T.astype(BF16)
    vT_ref[0] = pT[2 * nq:].astype(BF16)


def _ssd_in_kernel(h_ref, g_ref, wb_ref, wdt_ref, z_ref, xbc_ref, dt_ref, dtT_ref):
    hn = _rms_rows(h_ref[0], g_ref[...]).astype(BF16)
    acc = _dot(hn, wb_ref[...])
    z_ref[0] = acc[:, :SSD_INNER]
    xbc_ref[0] = acc[:, SSD_INNER:SSD_INNER + SSD_CONV_CH]
    dt_ref[0] = acc[:, SSD_INNER + SSD_CONV_CH:SSD_INNER + SSD_CONV_CH + 2 * SSD_HEADS]
    dtT_ref[0] = _dot_nt(wdt_ref[...], hn)


def _mlstm_in_kernel(h_ref, g_ref, wd_ref, wg_ref, q_ref, k_ref, v_ref, o_ref, gt_ref, gtT_ref):
    hn = _rms_rows(h_ref[0], g_ref[...]).astype(BF16)
    acc = _dot(hn, wd_ref[...])
    nqk = MLSTM_HEADS * MLSTM_QK
    nv = MLSTM_HEADS * MLSTM_V
    q_ref[0] = acc[:, :nqk]
    k_ref[0] = acc[:, nqk:2 * nqk]
    v_ref[0] = acc[:, 2 * nqk:2 * nqk + nv]
    o_ref[0] = acc[:, 2 * nqk + nv:2 * nqk + 2 * nv]
    gt_ref[0] = acc[:, 2 * nqk + 2 * nv:2 * nqk + 2 * nv + 4 * MLSTM_HEADS]
    gtT_ref[0] = _dot_nt(wg_ref[...], hn)


def _full(shape):
    nd = len(shape)
    return pl.BlockSpec(shape, lambda *_: (0,) * nd)


def _in_proj_call(body, h, consts, outs, tm, name):
    b, tp, d = h.shape
    in_specs = [pl.BlockSpec((1, tm, d), lambda i, j: (i, j, 0))]
    for c in consts:
        if isinstance(c, tuple):
            in_specs.append(pl.BlockSpec((c[0].shape[0], tm), lambda i, j: (0, j)))
        else:
            in_specs.append(_full(c.shape))
    out_shapes, out_specs = [], []
    for shape, dtype, kind in outs:
        out_shapes.append(jax.ShapeDtypeStruct((b,) + shape, dtype))
        if kind == "row":
            out_specs.append(pl.BlockSpec((1, tm, shape[1]), lambda i, j: (i, j, 0)))
        elif kind == "col":
            out_specs.append(pl.BlockSpec((1, shape[0], tm), lambda i, j: (i, 0, j)))
        else:
            out_specs.append(pl.BlockSpec((1, shape[0], tm, shape[2]), lambda i, j: (i, 0, j, 0)))
    args = [h] + [c[0] if isinstance(c, tuple) else c for c in consts]
    return pl.pallas_call(
        body, grid=(b, tp // tm), in_specs=in_specs, out_specs=out_specs, out_shape=out_shapes,
        compiler_params=_params("parallel", "parallel"), name=name)(*args)


def _attn_kernel(*refs, diff, seq, tk, out_scale):
    if diff:
        qT_ref, k_ref, vT_ref, lam_ref, og_ref, o_ref, m_scr, acc_scr = refs
    else:
        qT_ref, k_ref, vT_ref, o_ref, m_scr, acc_scr = refs
    dv = DIFF_V if diff else MLA_V
    tq = qT_ref.shape[2]
    n_chunks = seq // tk
    outs = []
    for hh in range(2):
        q = qT_ref[0, hh * HEAD_PAD:(hh + 1) * HEAD_PAD, :]
        v_lo = 0 if diff else hh * dv
        m_scr[...] = jnp.full(m_scr.shape, NEG, F32)
        acc_scr[...] = jnp.zeros(acc_scr.shape, F32)

        def step(k_c, v_c, mask_pad, q=q):
            s = _dot(k_c, q)
            if mask_pad:
                row = lax.broadcasted_iota(jnp.int32, s.shape, 0)
                s = jnp.where(row >= N_PAD, s, NEG)
            m_old = m_scr[0:1, :]
            m_new = jnp.maximum(m_old, jnp.max(s, axis=0, keepdims=True))
            alpha = jnp.exp2(m_old - m_new)
            p = jnp.exp2(s - m_new).astype(BF16)
            ones = jnp.ones((16, v_c.shape[1]), BF16)
            v_aug = jnp.concatenate([v_c, ones], axis=0)
            acc_scr[...] = alpha * acc_scr[...] + _dot(v_aug, p)
            m_scr[0:1, :] = m_new

        def body(c, carry, hh=hh, v_lo=v_lo, step=step):
            off = pl.multiple_of(c * tk, tk)
            step(k_ref[0, hh, pl.ds(off, tk), :], vT_ref[0, v_lo:v_lo + dv, pl.ds(off, tk)], False)
            return carry

        lax.fori_loop(0, n_chunks, body, 0)
        step(k_ref[0, hh, seq:seq + CHUNK, :], vT_ref[0, v_lo:v_lo + dv, seq:seq + CHUNK], True)
        acc = acc_scr[...]
        outs.append(acc[:dv] / acc[dv:dv + 1])
    if diff:
        o = outs[0] - lam_ref[0:1, 0:1] * outs[1]
        ms = jnp.mean(o * o, axis=0, keepdims=True)
        o = o * lax.rsqrt(ms + EPS) * (og_ref[...] * out_scale)
    else:
        o = jnp.concatenate(outs, axis=0)
    o_ref[0] = o.T


def _attention(qT, k, vT, extra, *, diff, seq, tq, tk, out_scale, name):
    b, _, tp = qT.shape
    groups = qT.shape[1] // (2 * HEAD_PAD)
    dv = DIFF_V if diff else MLA_V
    v_rows = dv if diff else 2 * dv
    in_specs = [
        pl.BlockSpec((1, 2 * HEAD_PAD, tq), lambda i, g, j: (i, g, j)),
        pl.BlockSpec((1, 2, tp, HEAD_PAD), lambda i, g, j: (i, g, 0, 0)),
        pl.BlockSpec((1, v_rows, tp), lambda i, g, j: (i, g, 0)),
    ] + [_full(e.shape) for e in extra]
    return pl.pallas_call(
        functools.partial(_attn_kernel, diff=diff, seq=seq, tk=tk, out_scale=out_scale),
        grid=(b, groups, tp // tq),
        in_specs=in_specs,
        out_specs=pl.BlockSpec((1, tq, LANE), lambda i, g, j: (i, j, g)),
        out_shape=jax.ShapeDtypeStruct((b, tp, groups * LANE), F32),
        scratch_shapes=[pltpu.VMEM((8, tq), F32), pltpu.VMEM((dv + 16, tq), F32)],
        compiler_params=_params("parallel", "parallel", "arbitrary"), name=name)(qT, k, vT, *extra)


def _conv_kernel(x_ref, prev_ref, next_ref, w_ref, b_ref, o_ref, scr):
    tc = x_ref.shape[1]
    scr[0:8, :] = prev_ref[0]
    scr[8:8 + tc, :] = x_ref[0]
    scr[8 + tc:16 + tc, :] = next_ref[0]
    acc = jnp.broadcast_to(b_ref[...], (tc, b_ref.shape[1]))
    for j in range(SSD_CONV):
        acc = acc + w_ref[j:j + 1, :] * scr[8 - SSD_CONV // 2 + j:8 - SSD_CONV // 2 + j + tc, :]
    o_ref[0] = acc * _sigmoid(acc)


def _ssd_conv(xbc, w8, bias, tc):
    b, tp, c = xbc.shape
    nb8 = tp // 8
    r8 = tc // 8
    return pl.pallas_call(
        _conv_kernel, grid=(b, tp // tc),
        in_specs=[pl.BlockSpec((1, tc, c), lambda i, j: (i, j, 0)),
                  pl.BlockSpec((1, 8, c), lambda i, j: (i, (j * r8 + nb8 - 1) % nb8, 0)),
                  pl.BlockSpec((1, 8, c), lambda i, j: (i, ((j + 1) * r8) % nb8, 0)),
                  _full(w8.shape), _full(bias.shape)],
        out_specs=pl.BlockSpec((1, tc, c), lambda i, j: (i, j, 0)),
        out_shape=jax.ShapeDtypeStruct((b, tp, c), F32),
        scratch_shapes=[pltpu.VMEM((tc + 16, c), F32)],
        compiler_params=_params("parallel", "parallel"), name="ssd_conv")(xbc, xbc, xbc, w8, bias)


def _tri_masks(reverse):
    row = lax.broadcasted_iota(jnp.int32, (CHUNK, CHUNK), 0)
    col = lax.broadcasted_iota(jnp.int32, (CHUNK, CHUNK), 1)
    keep = (col >= row) if reverse else (col <= row)
    tri_c = keep.astype(BF16)
    tri_r = ((row >= col) if reverse else (row <= col)).astype(BF16)
    return keep, tri_c, tri_r, row, col


def _ssd_kernel(*refs, reverse, n_chunks):
    if reverse:
        (x_ref, dt_ref, dtT_ref, bias_r, bias_c, a_r, a_c, z_ref, yf_ref, dsk_ref, ng_ref,
         y_ref, st_ref) = refs
    else:
        x_ref, dt_ref, dtT_ref, bias_r, bias_c, a_r, a_c, y_ref, st_ref = refs
    step = pl.program_id(1)
    is_meta = step == (n_chunks - 1 if reverse else 0)
    d = 1 if reverse else 0
    h8 = SSD_HEADS

    @pl.when(step == 0)
    def _():
        st_ref[...] = jnp.zeros(st_ref.shape, F32)

    keep, tri_c, tri_r, row, col = _tri_masks(reverse)
    xbc = x_ref[0]
    xs = xbc[:, :SSD_INNER]
    bm = xbc[:, SSD_INNER:SSD_INNER + CHUNK]
    cm = xbc[:, SSD_INNER + CHUNK:SSD_INNER + 2 * CHUNK]

    dt_c = _softplus(dt_ref[0][:, d * h8:(d + 1) * h8] + bias_r[:, d * h8:(d + 1) * h8])
    dt_r = _softplus(dtT_ref[0][d * h8:(d + 1) * h8, :] + bias_c[d * h8:(d + 1) * h8, :])
    pad_c = jnp.logical_and(is_meta, row[:, :h8] < N_PAD)
    pad_r = jnp.logical_and(is_meta, col[:h8, :] < N_PAD)
    dt_c = jnp.where(pad_c, 0.0, dt_c)
    dt_r = jnp.where(pad_r, 0.0, dt_r)
    a_col = dt_c * a_r[:, d * h8:(d + 1) * h8]
    a_row = dt_r * a_c[d * h8:(d + 1) * h8, :]
    cs_c = _cumsum_cols(tri_c, a_col)
    cs_r = _cumsum_rows(a_row, tri_r)
    last = 0 if reverse else CHUNK - 1
    tot_r = cs_r[:, last:last + 1]
    tot_c = cs_c[last:last + 1, :]

    lane_lo = col < SSD_STATE
    row_lo = row < SSD_STATE
    blockdiag = jnp.logical_not(jnp.logical_xor(lane_lo, row_lo))
    cm_sw = pltpu.roll(cm, SSD_STATE, 1)
    c_dup = (jnp.where(lane_lo, cm, cm_sw), jnp.where(lane_lo, cm_sw, cm))
    g_mat = (_dot_nt(jnp.where(lane_lo, cm, 0.0).astype(BF16), bm.astype(BF16)),
             _dot_nt(jnp.where(lane_lo, 0.0, cm).astype(BF16), bm.astype(BF16)))
    bT = bm.T

    heads_per_group = SSD_HEADS // SSD_GROUPS
    for j in range(SSD_HEADS // 2):
        g = (2 * j) // heads_per_group
        h0, h1 = 2 * j, 2 * j + 1
        xp = xs[:, j * CHUNK:(j + 1) * CHUNK]
        dtp = jnp.where(lane_lo, dt_c[:, h0:h0 + 1], dt_c[:, h1:h1 + 1])
        xdt = xp * dtp
        parts = []
        for h in (h0, h1):
            diff_ = cs_c[:, h:h + 1] - cs_r[h:h + 1, :]
            parts.append((g_mat[g] * jnp.exp(jnp.where(keep, diff_, NEG))).astype(BF16))
        e_col = jnp.where(lane_lo, jnp.exp(cs_c[:, h0:h0 + 1]), jnp.exp(cs_c[:, h1:h1 + 1]))
        parts.append((c_dup[g] * e_col).astype(BF16))
        lhs = jnp.concatenate(parts, axis=1)
        s_old = st_ref[j]
        rhs = jnp.concatenate([jnp.where(lane_lo, xdt, 0.0).astype(BF16),
                               jnp.where(lane_lo, 0.0, xdt).astype(BF16),
                               s_old.astype(BF16)], axis=0)
        y_pair = _dot(lhs, rhs)
        btg = bT[g * SSD_STATE:(g + 1) * SSD_STATE, :]
        bd = jnp.concatenate([btg * jnp.exp(tot_r[h0:h0 + 1, :] - cs_r[h0:h0 + 1, :]),
                              btg * jnp.exp(tot_r[h1:h1 + 1, :] - cs_r[h1:h1 + 1, :])], axis=0)
        s_new = jnp.where(blockdiag, _dot(bd.astype(BF16), xdt.astype(BF16)), 0.0)
        carry = jnp.where(row_lo, jnp.exp(tot_c[:, h0:h0 + 1]), jnp.exp(tot_c[:, h1:h1 + 1]))
        st_ref[j] = s_old * carry + s_new
        if reverse:
            y_ref[0, :, j * CHUNK:(j + 1) * CHUNK] = y_pair + yf_ref[0, :, j * CHUNK:(j + 1) * CHUNK]
        else:
            y_ref[0, :, j * CHUNK:(j + 1) * CHUNK] = y_pair
    if reverse:
        z = z_ref[0]
        y = (y_ref[0] + dsk_ref[...] * xs) * (z * _sigmoid(z))
        y_ref[0] = _rms_rows(y, ng_ref[...])


def _chunk_order(n_chunks, reverse):
    if reverse:
        return lambda c: (2 * n_chunks - 2 - c) % n_chunks
    return lambda c: (c + n_chunks - 1) % n_chunks


def _ssd_pass(xbc_act, dt, dtT, consts, extra, *, reverse):
    b, tp, _ = xbc_act.shape
    n_chunks = tp // CHUNK
    order = _chunk_order(n_chunks, reverse)
    row_spec = lambda c: pl.BlockSpec((1, CHUNK, c), lambda i, s: (i, order(s), 0))
    in_specs = [row_spec(xbc_act.shape[2]), row_spec(dt.shape[2]),
                pl.BlockSpec((1, dtT.shape[1], CHUNK), lambda i, s: (i, 0, order(s)))]
    in_specs += [_full(c.shape) for c in consts]
    args = [xbc_act, dt, dtT] + list(consts)
    for e in extra:
        if e.ndim == 3:
            in_specs.append(row_spec(e.shape[2]))
        else:
            in_specs.append(_full(e.shape))
        args.append(e)
    return pl.pallas_call(
        functools.partial(_ssd_kernel, reverse=reverse, n_chunks=n_chunks),
        grid=(b, n_chunks), in_specs=in_specs, out_specs=row_spec(SSD_INNER),
        out_shape=jax.ShapeDtypeStruct((b, tp, SSD_INNER), F32),
        scratch_shapes=[pltpu.VMEM((SSD_HEADS // 2, CHUNK, CHUNK), F32)],
        compiler_params=_params("parallel", "arbitrary"),
        name="ssd_rev" if reverse else "ssd_fwd")(*args)


def _mlstm_kernel(*refs, reverse, n_chunks):
    if reverse:
        (q_ref, k_ref, v_ref, gt_ref, gtT_ref, gb_r, gb_c, hf_ref, o_ref, og_ref,
         y_ref, st_ref, m_ref) = refs
    else:
        q_ref, k_ref, v_ref, gt_ref, gtT_ref, gb_r, gb_c, y_ref, st_ref, m_ref = refs
    step = pl.program_id(1)
    is_meta = step == (n_chunks - 1 if reverse else 0)
    d = 1 if reverse else 0
    nh = MLSTM_HEADS

    @pl.when(step == 0)
    def _():
        st_ref[...] = jnp.zeros(st_ref.shape, F32)
        m_ref[...] = jnp.full(m_ref.shape, NEG, F32)

    keep, tri_c, tri_r, row, col = _tri_masks(reverse)
    gt = gt_ref[0] + gb_r[...]
    gtT = gtT_ref[0] + gb_c[...]
    i_lo, f_lo = d * nh, 2 * nh + d * nh
    pad_c = jnp.logical_and(is_meta, row[:, :nh] < N_PAD)
    pad_r = jnp.logical_and(is_meta, col[:nh, :] < N_PAD)
    ig_r = jnp.where(pad_r, NEG, gtT[i_lo:i_lo + nh, :])
    fg_c = jnp.where(pad_c, 0.0, _log_sigmoid(gt[:, f_lo:f_lo + nh]))
    fg_r = jnp.where(pad_r, 0.0, _log_sigmoid(gtT[f_lo:f_lo + nh, :]))
    b_c = _cumsum_cols(tri_c, fg_c)
    b_r = _cumsum_rows(fg_r, tri_r)
    last = 0 if reverse else CHUNK - 1

    lane_lo = col < MLSTM_QK
    row_lo = row < MLSTM_QK
    ones = jnp.ones((CHUNK, MLSTM_V), BF16)
    scale = MLSTM_QK ** -0.5
    for h in range(nh):
        pair = h // 2
        lo = (h % 2) == 0
        qp = q_ref[0, :, pair * CHUNK:(pair + 1) * CHUNK]
        kp = k_ref[0, :, pair * CHUNK:(pair + 1) * CHUNK]
        qm = (jnp.where(lane_lo if lo else jnp.logical_not(lane_lo), qp, 0.0) * scale).astype(BF16)
        kT = jnp.where(row_lo if lo else jnp.logical_not(row_lo), kp.T, 0.0)
        v_aug = jnp.concatenate([v_ref[0, :, h * MLSTM_V:(h + 1) * MLSTM_V].astype(BF16), ones], axis=1)
        m_st = m_ref[h][0:1, 0:1]
        bc = b_c[:, h:h + 1]
        br = b_r[h:h + 1, :]
        ir = ig_r[h:h + 1, :]
        dmat = jnp.where(keep, bc - br + ir, -jnp.inf)
        inter = bc + m_st
        m_row = jnp.maximum(jnp.max(dmat, axis=1, keepdims=True), inter)
        w_intra = jnp.exp(dmat - m_row)
        w_inter = jnp.exp(inter - m_row)
        s = _dot_nt(qm, kp.astype(BF16)) * w_intra
        c_st = st_ref[h]
        comb = _dot(s.astype(BF16), v_aug) + w_inter * _dot(qm, c_st.astype(BF16))
        num = comb[:, :MLSTM_V]
        den = jnp.maximum(jnp.abs(comb[:, MLSTM_V:]), jnp.exp(-m_row))
        hout = num / den
        tot = br[:, last:last + 1]
        d_last = tot - br + ir
        m_new = jnp.maximum(tot + m_st, jnp.max(d_last, axis=1, keepdims=True))
        w_s = jnp.exp(d_last - m_new)
        w_prev = jnp.exp(tot + m_st - m_new)
        st_ref[h] = w_prev * c_st + _dot((kT * w_s).astype(BF16), v_aug)
        m_ref[h] = jnp.broadcast_to(m_new, m_ref.shape[1:])
        if reverse:
            hsum = hout + hf_ref[0, :, h * MLSTM_V:(h + 1) * MLSTM_V]
            o_gate = _sigmoid(o_ref[0, :, h * MLSTM_V:(h + 1) * MLSTM_V])
            y_ref[0, :, h * MLSTM_V:(h + 1) * MLSTM_V] = o_gate * _rms_rows(hsum, og_ref[...])
        else:
            y_ref[0, :, h * MLSTM_V:(h + 1) * MLSTM_V] = hout


def _mlstm_pass(q, k, v, gt, gtT, consts, extra, *, reverse):
    b, tp, _ = q.shape
    n_chunks = tp // CHUNK
    order = _chunk_order(n_chunks, reverse)
    row_spec = lambda c: pl.BlockSpec((1, CHUNK, c), lambda i, s: (i, order(s), 0))
    in_specs = [row_spec(q.shape[2]), row_spec(k.shape[2]), row_spec(v.shape[2]), row_spec(gt.shape[2]),
                pl.BlockSpec((1, gtT.shape[1], CHUNK), lambda i, s: (i, 0, order(s)))]
    in_specs += [_full(c.shape) for c in consts]
    args = [q, k, v, gt, gtT] + list(consts)
    for e in extra:
        in_specs.append(row_spec(e.shape[2]) if e.ndim == 3 else _full(e.shape))
        args.append(e)
    nv = MLSTM_HEADS * MLSTM_V
    return pl.pallas_call(
        functools.partial(_mlstm_kernel, reverse=reverse, n_chunks=n_chunks),
        grid=(b, n_chunks), in_specs=in_specs, out_specs=row_spec(nv),
        out_shape=jax.ShapeDtypeStruct((b, tp, nv), F32),
        scratch_shapes=[pltpu.VMEM((MLSTM_HEADS, CHUNK, 2 * MLSTM_V), F32),
                        pltpu.VMEM((MLSTM_HEADS, 8, LANE), F32)],
        compiler_params=_params("parallel", "arbitrary"),
        name="mlstm_rev" if reverse else "mlstm_fwd")(*args)


def _out_kernel(ya_ref, yb_ref, yc_ref, yd_ref, h_ref, ag_ref, w_ref, o_ref, *, tp, seq):
    tm = h_ref.shape[0]
    ya = _rms_rows(ya_ref[...], ag_ref[...])
    mixed = jnp.concatenate([ya.astype(BF16), yb_ref[...].astype(BF16),
                             yc_ref[...].astype(BF16), yd_ref[...].astype(BF16)], axis=1)
    out = h_ref[...] + _dot(mixed, w_ref[...])
    t = (pl.program_id(0) * tm + lax.broadcasted_iota(jnp.int32, (tm, 1), 0)) % tp
    is_pad = jnp.logical_and(t >= seq, t < seq + N_PAD)
    o_ref[...] = jnp.where(is_pad, 0.0, out)


def _out_proj(ya, yb, yc, yd, h, ag, w, *, tp, seq, tm):
    rows, d = h.shape
    row_spec = lambda c: pl.BlockSpec((tm, c), lambda i: (i, 0))
    return pl.pallas_call(
        functools.partial(_out_kernel, tp=tp, seq=seq), grid=(rows // tm,),
        in_specs=[row_spec(ya.shape[1]), row_spec(yb.shape[1]), row_spec(yc.shape[1]),
                  row_spec(yd.shape[1]), row_spec(d), _full(ag.shape), _full(w.shape)],
        out_specs=row_spec(d), out_shape=jax.ShapeDtypeStruct((rows, d), F32),
        compiler_params=_params("parallel"), name="out_proj")(ya, yb, yc, yd, h, ag, w)


def _ffn_kernel(h_ref, g_ref, wg_ref, wu_ref, wo_ref, o_ref, *, n_split):
    x = h_ref[...]
    hn = _rms_rows(x, g_ref[...]).astype(BF16)
    f = wg_ref.shape[1]
    tf = f // n_split
    acc = x
    for c in range(n_split):
        gate = _dot(hn, wg_ref[:, c * tf:(c + 1) * tf])
        up = _dot(hn, wu_ref[:, c * tf:(c + 1) * tf])
        act = (gate * _sigmoid(gate) * up).astype(BF16)
        acc = acc + _dot(act, wo_ref[c * tf:(c + 1) * tf, :])
    o_ref[...] = acc


def _ffn(h, g, wg, wu, wo, *, tm):
    rows, d = h.shape
    f = wg.shape[1]
    n_split = 2 if (f // 2) % LANE == 0 else 1
    row_spec = pl.BlockSpec((tm, d), lambda i: (i, 0))
    resident = lambda a: pl.BlockSpec(a.shape, lambda i: (0, 0), pipeline_mode=pl.Buffered(1))
    return pl.pallas_call(
        functools.partial(_ffn_kernel, n_split=n_split), grid=(rows // tm,),
        in_specs=[row_spec, _full(g.shape), resident(wg), resident(wu), resident(wo)],
        out_specs=row_spec, out_shape=jax.ShapeDtypeStruct((rows, d), F32),
        compiler_params=_params("parallel"), name="ffn")(h, g, wg, wu, wo)


def _rope_tables_t(pos, rot_dim):
    inv = 1.0 / (ROPE_THETA ** (jnp.arange(0, rot_dim, 2, dtype=F32) / rot_dim))
    ang = pos[:, None] * inv[None, :]
    return jnp.cos(ang).T, jnp.sin(ang).T


def _col(v, n=None):
    v = v.astype(F32)
    if n is not None:
        v = jnp.pad(v, (0, n - v.shape[0]))
    return v[:, None]


def _row(v):
    return v.astype(F32)[None, :]


def kernel(x, meta_tokens, attn_norm_g, w_in, mla_q_norm_g, mla_kv_norm_g, mla_w_uq, mla_w_ukv, mla_q_head_g, mla_k_head_g, mla_out_g, ssd_conv_w, ssd_conv_b, ssd_dt_bias, ssd_a_log, ssd_d, ssd_norm_g, diff_q_head_g, diff_k_head_g, diff_lambda, diff_out_g, mlstm_i_bias, mlstm_f_bias, mlstm_out_g, w_out, ffn_norm_g, w_ffn_in, w_ffn_out):
    b, seq, d = x.shape
    depth = w_in.shape[0]
    tp = seq + CHUNK
    assert seq % CHUNK == 0
    tm_in = _pick_tile(tp, (640, 384, 128))
    tq = _pick_tile(tp, (1664, 640, 384, 128))
    tk = _pick_tile(seq, (512, 384, 256, 128))
    tm_flat = _pick_tile(b * tp, (512, 256, 128))

    meta = jnp.broadcast_to(meta_tokens[None].astype(x.dtype), (b, N_META, d))
    h = jnp.concatenate([x, jnp.zeros((b, N_PAD, d), x.dtype), meta], axis=1).reshape(b * tp, d)
    pos = jnp.concatenate([N_META + jnp.arange(seq, dtype=F32), jnp.zeros((N_PAD,), F32),
                           jnp.arange(N_META, dtype=F32)])
    cos_m, sin_m = _rope_tables_t(pos, MLA_ROPE)
    cos_d, sin_d = _rope_tables_t(pos, DIFF_ROPE)

    sizes = (MLA_Q_RANK, MLA_KV_RANK, MLA_ROPE, SSD_INNER, SSD_CONV_CH, 2 * SSD_HEADS,
             2 * DIFF_HEADS * DIFF_QK, 2 * DIFF_HEADS * DIFF_QK, DIFF_HEADS * DIFF_V,
             MLSTM_HEADS * MLSTM_QK, MLSTM_HEADS * MLSTM_QK, MLSTM_HEADS * MLSTM_V,
             MLSTM_HEADS * MLSTM_V, 2 * MLSTM_HEADS, 2 * MLSTM_HEADS)
    offs = [0]
    for s_ in sizes:
        offs.append(offs[-1] + s_)

    def cols(w, first, last):
        return w[:, offs[first]:offs[last + 1]]

    def pad_cols(w, n):
        return jnp.pad(w, ((0, 0), (0, n - w.shape[1])))

    for l in range(depth):
        lambda_init = 0.8 - 0.6 * math.exp(-0.3 * l)
        wl = w_in[l]
        h3 = h.reshape(b, tp, d)
        g_attn = _row(attn_norm_g[l])

        w_a = cols(wl, 0, 1).astype(BF16)
        w_kr = cols(wl, 2, 2).T.astype(BF16)
        w_uq = mla_w_uq[l].T.reshape(MLA_HEADS, MLA_QK, MLA_Q_RANK)
        w_uq = jnp.pad(w_uq, ((0, 0), (0, HEAD_PAD - MLA_QK), (0, 0))).reshape(MLA_HEADS * HEAD_PAD, MLA_Q_RANK)
        w_ukv = mla_w_ukv[l].T.reshape(MLA_HEADS, MLA_NOPE + MLA_V, MLA_KV_RANK)
        w_uk = w_ukv[:, :MLA_NOPE].reshape(MLA_HEADS * MLA_NOPE, MLA_KV_RANK)
        w_uv = w_ukv[:, MLA_NOPE:].reshape(MLA_HEADS * MLA_V, MLA_KV_RANK)
        qT_a, k_a, vT_a = _in_proj_call(
            _mla_in_kernel, h3,
            [g_attn, w_a, _row(mla_q_norm_g[l]), _row(mla_kv_norm_g[l]), w_uq.astype(BF16),
             w_uk.astype(BF16), w_uv.astype(BF16), w_kr, _col(mla_q_head_g[l], HEAD_PAD),
             _col(mla_k_head_g[l], HEAD_PAD), (cos_m,), (sin_m,)],
            [((MLA_HEADS * HEAD_PAD, tp), BF16, "col"), ((MLA_HEADS, tp, HEAD_PAD), BF16, "head"),
             ((MLA_HEADS * MLA_V, tp), BF16, "col")], tm_in, "mla_in")
        y_a = _attention(qT_a, k_a, vT_a, [], diff=False, seq=seq, tq=tq, tk=tk, out_scale=1.0,
                         name="mla_attn")

        w_c = cols(wl, 6, 8).T.astype(BF16)
        qT_c, k_c, vT_c = _in_proj_call(
            _diff_in_kernel, h3,
            [g_attn, w_c, _col(diff_q_head_g[l], HEAD_PAD), _col(diff_k_head_g[l], HEAD_PAD),
             (cos_d,), (sin_d,)],
            [((2 * DIFF_HEADS * HEAD_PAD, tp), BF16, "col"), ((2 * DIFF_HEADS, tp, HEAD_PAD), BF16, "head"),
             ((DIFF_HEADS * DIFF_V, tp), BF16, "col")], tm_in, "diff_in")
        lam = diff_lambda[l].astype(F32)
        lam_full = jnp.exp(jnp.sum(lam[0] * lam[1])) - jnp.exp(jnp.sum(lam[2] * lam[3])) + lambda_init
        y_c = _attention(qT_c, k_c, vT_c, [jnp.full((8, LANE), lam_full, F32), _col(diff_out_g[l])],
                         diff=True, seq=seq, tq=tq, tk=tk, out_scale=1.0 - lambda_init, name="diff_attn")

        n_b = SSD_INNER + SSD_CONV_CH + 2 * SSD_HEADS
        w_b = pad_cols(cols(wl, 3, 5), -(-n_b // LANE) * LANE).astype(BF16)
        w_dt = cols(wl, 5, 5).T.astype(BF16)
        z_b, xbc, dt, dtT = _in_proj_call(
            _ssd_in_kernel, h3, [g_attn, w_b, w_dt],
            [((tp, SSD_INNER), F32, "row"), ((tp, SSD_CONV_CH), F32, "row"),
             ((tp, 2 * SSD_HEADS), F32, "row"), ((2 * SSD_HEADS, tp), F32, "col")], tm_in, "ssd_in")
        w8 = jnp.pad(ssd_conv_w[l].astype(F32), ((0, 8 - SSD_CONV), (0, 0)))
        xbc_act = _ssd_conv(xbc, w8, _row(ssd_conv_b[l]), tm_in)
        dt_bias = ssd_dt_bias[l].astype(F32).reshape(-1)
        a_neg = -jnp.exp(ssd_a_log[l].astype(F32)).reshape(-1)
        ssd_consts = [_row(dt_bias), _col(dt_bias), _row(a_neg), _col(a_neg)]
        y_f = _ssd_pass(xbc_act, dt, dtT, ssd_consts, [], reverse=False)
        y_b = _ssd_pass(xbc_act, dt, dtT, ssd_consts,
                        [z_b, y_f, _row(jnp.repeat(ssd_d[l], SSD_HEAD_DIM)), _row(ssd_norm_g[l])],
                        reverse=True)

        n_d = 2 * MLSTM_HEADS * MLSTM_QK + 2 * MLSTM_HEADS * MLSTM_V + 4 * MLSTM_HEADS
        w_d = pad_cols(cols(wl, 9, 14), -(-n_d // LANE) * LANE).astype(BF16)
        w_g = cols(wl, 13, 14).T.astype(BF16)
        q_d, k_d, v_d, o_d, gt, gtT = _in_proj_call(
            _mlstm_in_kernel, h3, [g_attn, w_d, w_g],
            [((tp, MLSTM_HEADS * MLSTM_QK), F32, "row"), ((tp, MLSTM_HEADS * MLSTM_QK), F32, "row"),
             ((tp, MLSTM_HEADS * MLSTM_V), F32, "row"), ((tp, MLSTM_HEADS * MLSTM_V), F32, "row"),
             ((tp, 4 * MLSTM_HEADS), F32, "row"), ((4 * MLSTM_HEADS, tp), F32, "col")], tm_in, "mlstm_in")
        gate_bias = jnp.concatenate([mlstm_i_bias[l].reshape(-1), mlstm_f_bias[l].reshape(-1)]).astype(F32)
        ml_consts = [_row(gate_bias), _col(gate_bias)]
        h_f = _mlstm_pass(q_d, k_d, v_d, gt, gtT, ml_consts, [], reverse=False)
        y_d = _mlstm_pass(q_d, k_d, v_d, gt, gtT, ml_consts, [h_f, o_d, _row(mlstm_out_g[l])], reverse=True)

        flat = lambda a: a.reshape(b * tp, a.shape[2])
        h = _out_proj(flat(y_a), flat(y_b), flat(y_c), flat(y_d), h, _row(mla_out_g[l]),
                      w_out[l].astype(BF16), tp=tp, seq=seq, tm=tm_flat)
        f = w_ffn_out.shape[1]
        h = _ffn(h, _row(ffn_norm_g[l]), w_ffn_in[l][:, :f].astype(BF16), w_ffn_in[l][:, f:].astype(BF16),
                 w_ffn_out[l].astype(BF16), tm=tm_flat)

    return h.reshape(b, tp, d)[:, :seq].astype(x.dtype)
```

```python
import functools
import math

import jax
import jax.numpy as jnp
from jax import lax
from jax.experimental import pallas as pl
from jax.experimental.pallas import tpu as pltpu

N_META = 16
ROPE_THETA = 500000.0
EPS = 1e-6
CHUNK = 128
N_PAD = CHUNK - N_META
NEG = -1e30
LOG2E = 1.4426950408889634

MLA_HEADS, MLA_NOPE, MLA_ROPE, MLA_V = 8, 64, 32, 64
MLA_QK = MLA_NOPE + MLA_ROPE
MLA_Q_RANK, MLA_KV_RANK = 384, 256
SSD_HEADS, SSD_HEAD_DIM, SSD_GROUPS, SSD_STATE, SSD_CONV = 8, 64, 2, 64, 5
SSD_INNER = SSD_HEADS * SSD_HEAD_DIM
SSD_CONV_CH = SSD_INNER + 2 * SSD_GROUPS * SSD_STATE
DIFF_HEADS, DIFF_QK = 4, 64
DIFF_V = 2 * DIFF_QK
DIFF_ROPE = DIFF_QK // 4
MLSTM_HEADS, MLSTM_QK, MLSTM_V = 4, 64, 128
HEAD_PAD = 128
LANE = 128
Q_STRIP = 256
VMEM_LIMIT = 52 * 1024 * 1024

F32 = jnp.float32
BF16 = jnp.bfloat16
EXP_DTYPE = jnp.bfloat16


def _dot(a, b):
    return jnp.dot(a, b, preferred_element_type=F32)


def _dot_nt(a, b):
    return lax.dot_general(a, b, (((1,), (1,)), ((), ())), preferred_element_type=F32)


def _rms_rows(x, g):
    ms = jnp.mean(x * x, axis=-1, keepdims=True)
    return x * lax.rsqrt(ms + EPS) * g


def _split3(a):
    hi = a.astype(BF16)
    r1 = a - hi.astype(F32)
    mid = r1.astype(BF16)
    lo = (r1 - mid.astype(F32)).astype(BF16)
    return hi, mid, lo


def _cumsum_cols(tri, a):
    hi, mid, lo = _split3(a)
    return _dot(tri, hi) + _dot(tri, mid) + _dot(tri, lo)


def _cumsum_rows(a, tri):
    hi, mid, lo = _split3(a)
    return _dot(hi, tri) + _dot(mid, tri) + _dot(lo, tri)


def _softplus(x):
    return jnp.maximum(x, 0.0) + jnp.log(1.0 + jnp.exp(-jnp.abs(x)))


def _log_sigmoid(x):
    return jnp.minimum(x, 0.0) - jnp.log(1.0 + jnp.exp(-jnp.abs(x)))


def _sigmoid(x):
    return 1.0 / (1.0 + jnp.exp(-x))


def _pick_tile(n, candidates):
    for c in candidates:
        if n % c == 0:
            return c
    raise ValueError(f"no tile in {candidates} divides {n}")


def _params(*sem):
    return pltpu.CompilerParams(dimension_semantics=sem, vmem_limit_bytes=VMEM_LIMIT)


def _norm_rope_t(blk, g_col, cos, sin, n_real):
    r = cos.shape[0]
    ms = jnp.sum(blk * blk, axis=0, keepdims=True) * (1.0 / n_real)
    y = blk * lax.rsqrt(ms + EPS) * g_col
    x1, x2, rest = y[:r], y[r:2 * r], y[2 * r:]
    return jnp.concatenate([x1 * cos - x2 * sin, x2 * cos + x1 * sin, rest], axis=0)


def _mla_in_kernel(h_ref, g_ref, wa_ref, gq_ref, gkv_ref, wuq_ref, wuk_ref, wuv_ref, wkr_ref,
                   qhg_ref, khg_ref, cos_ref, sin_ref, qT_ref, k_ref, vT_ref):
    hn = _rms_rows(h_ref[0], g_ref[...]).astype(BF16)
    acc = _dot(hn, wa_ref[...])
    cqn = _rms_rows(acc[:, :MLA_Q_RANK], gq_ref[...]).astype(BF16)
    ckvn = _rms_rows(acc[:, MLA_Q_RANK:], gkv_ref[...]).astype(BF16)
    qT = _dot_nt(wuq_ref[...], cqn)
    knT = _dot_nt(wuk_ref[...], ckvn)
    vT = _dot_nt(wuv_ref[...], ckvn)
    krT = _dot_nt(wkr_ref[...], hn)
    cos, sin = cos_ref[...], sin_ref[...]
    t = krT.shape[1]
    zpad = jnp.zeros((HEAD_PAD - MLA_QK, t), F32)
    q_scale = (MLA_QK ** -0.5) * LOG2E
    for h in range(MLA_HEADS):
        qb = _norm_rope_t(qT[h * HEAD_PAD:(h + 1) * HEAD_PAD], qhg_ref[...], cos, sin, MLA_QK)
        qT_ref[0, h * HEAD_PAD:(h + 1) * HEAD_PAD, :] = (qb * q_scale).astype(BF16)
        kb = jnp.concatenate([krT, knT[h * MLA_NOPE:(h + 1) * MLA_NOPE], zpad], axis=0)
        kb = _norm_rope_t(kb, khg_ref[...], cos, sin, MLA_QK)
        k_ref[0, h] = kb.T.astype(BF16)
    vT_ref[0] = vT.astype(BF16)


def _diff_in_kernel(h_ref, g_ref, wc_ref, qhg_ref, khg_ref, cos_ref, sin_ref, qT_ref, k_ref, vT_ref):
    hn = _rms_rows(h_ref[0], g_ref[...]).astype(BF16)
    pT = _dot_nt(wc_ref[...], hn)
    cos, sin = cos_ref[...], sin_ref[...]
    t = pT.shape[1]
    zpad = jnp.zeros((HEAD_PAD - DIFF_QK, t), F32)
    nq = 2 * DIFF_HEADS * DIFF_QK
    q_scale = (DIFF_QK ** -0.5) * LOG2E
    for h in range(2 * DIFF_HEADS):
        qb = jnp.concatenate([pT[h * DIFF_QK:(h + 1) * DIFF_QK], zpad], axis=0)
        qb = _norm_rope_t(qb, qhg_ref[...], cos, sin, DIFF_QK)
        qT_ref[0, h * HEAD_PAD:(h + 1) * HEAD_PAD, :] = (qb * q_scale).astype(BF16)
        kb = jnp.concatenate([pT[nq + h * DIFF_QK:nq + (h + 1) * DIFF_QK], zpad], axis=0)
        kb = _norm_rope_t(kb, khg_ref[...], cos, sin, DIFF_QK)
        k_ref[0, h] = kb.T.astype(BF16)
    vT_ref[0] = pT[2 * nq:].astype(BF16)


def _ssd_in_kernel(h_ref, g_ref, wb_ref, wdt_ref, z_ref, xbc_ref, dt_ref, dtT_ref):
    hn = _rms_rows(h_ref[0], g_ref[...]).astype(BF16)
    acc = _dot(hn, wb_ref[...])
    z_ref[0] = acc[:, :SSD_INNER]
    xbc_ref[0] = acc[:, SSD_INNER:SSD_INNER + SSD_CONV_CH]
    dt_ref[0] = acc[:, SSD_INNER + SSD_CONV_CH:SSD_INNER + SSD_CONV_CH + 2 * SSD_HEADS]
    dtT_ref[0] = _dot_nt(wdt_ref[...], hn)


def _mlstm_in_kernel(h_ref, g_ref, wd_ref, wg_ref, q_ref, k_ref, v_ref, o_ref, gt_ref, gtT_ref):
    hn = _rms_rows(h_ref[0], g_ref[...]).astype(BF16)
    acc = _dot(hn, wd_ref[...])
    nqk = MLSTM_HEADS * MLSTM_QK
    nv = MLSTM_HEADS * MLSTM_V
    q_ref[0] = acc[:, :nqk]
    k_ref[0] = acc[:, nqk:2 * nqk]
    v_ref[0] = acc[:, 2 * nqk:2 * nqk + nv]
    o_ref[0] = acc[:, 2 * nqk + nv:2 * nqk + 2 * nv]
    gt_ref[0] = acc[:, 2 * nqk + 2 * nv:2 * nqk + 2 * nv + 4 * MLSTM_HEADS]
    gtT_ref[0] = _dot_nt(wg_ref[...], hn)


def _full(shape):
    nd = len(shape)
    return pl.BlockSpec(shape, lambda *_: (0,) * nd)


def _in_proj_call(body, h, consts, outs, tm, name):
    b, tp, d = h.shape
    in_specs = [pl.BlockSpec((1, tm, d), lambda i, j: (i, j, 0))]
    for c in consts:
        if isinstance(c, tuple):
            in_specs.append(pl.BlockSpec((c[0].shape[0], tm), lambda i, j: (0, j)))
        else:
            in_specs.append(_full(c.shape))
    out_shapes, out_specs = [], []
    for shape, dtype, kind in outs:
        out_shapes.append(jax.ShapeDtypeStruct((b,) + shape, dtype))
        if kind == "row":
            out_specs.append(pl.BlockSpec((1, tm, shape[1]), lambda i, j: (i, j, 0)))
        elif kind == "col":
            out_specs.append(pl.BlockSpec((1, shape[0], tm), lambda i, j: (i, 0, j)))
        else:
            out_specs.append(pl.BlockSpec((1, shape[0], tm, shape[2]), lambda i, j: (i, 0, j, 0)))
    args = [h] + [c[0] if isinstance(c, tuple) else c for c in consts]
    return pl.pallas_call(
        body, grid=(b, tp // tm), in_specs=in_specs, out_specs=out_specs, out_shape=out_shapes,
        compiler_params=_params("parallel", "parallel"), name=name)(*args)


def _attn_kernel(*refs, diff, seq, tk, out_scale):
    if diff:
        qT_ref, k_ref, vT_ref, lam_ref, og_ref, o_ref, s_scr, st_scr, m_scr, acc_scr = refs
    else:
        qT_ref, k_ref, vT_ref, o_ref, s_scr, st_scr, m_scr, acc_scr = refs
    dv = DIFF_V if diff else MLA_V
    tq = qT_ref.shape[2]
    n_chunks = seq // tk
    assert n_chunks % 2 == 0
    outs = []
    for hh in range(2):
        v_lo = 0 if diff else hh * dv
        m_scr[...] = jnp.full(m_scr.shape, NEG, F32)
        acc_scr[...] = jnp.zeros(acc_scr.shape, F32)

        def stage(nxt, cur, hh=hh, v_lo=v_lo):
            if nxt is not None:
                k_n = k_ref[0, hh, pl.ds(nxt[1], nxt[2]), :]
            if cur is not None:
                ones = jnp.ones((16, cur[2]), BF16)
                v_aug = jnp.concatenate([vT_ref[0, v_lo:v_lo + dv, pl.ds(cur[1], cur[2])], ones], axis=0)
            for j0 in range(0, tq, Q_STRIP):
                w = min(Q_STRIP, tq - j0)
                if nxt is not None:
                    s_n = _dot(k_n, qT_ref[0, hh * HEAD_PAD:(hh + 1) * HEAD_PAD, j0:j0 + w])
                    if nxt[3]:
                        row = lax.broadcasted_iota(jnp.int32, s_n.shape, 0)
                        s_n = jnp.where(row >= N_PAD, s_n, NEG)
                    nxt[0][:, j0:j0 + w] = s_n
                if cur is not None:
                    s = cur[0][:, j0:j0 + w]
                    m_old = m_scr[0:1, j0:j0 + w]
                    m_new = jnp.maximum(m_old, jnp.max(s, axis=0, keepdims=True))
                    alpha = jnp.exp2(m_old - m_new)
                    p = jnp.exp2((s - m_new).astype(EXP_DTYPE)).astype(BF16)
                    acc_scr[:, j0:j0 + w] = alpha * acc_scr[:, j0:j0 + w] + _dot(v_aug, p)
                    m_scr[0:1, j0:j0 + w] = m_new

        buf0, buf1 = s_scr.at[0], s_scr.at[1]
        stage((buf0, 0, tk, False), None)

        def body(i, carry, stage=stage):
            off = pl.multiple_of(2 * i * tk, 2 * tk)
            stage((buf1, off + tk, tk, False), (buf0, off, tk))
            stage((buf0, off + 2 * tk, tk, False), (buf1, off + tk, tk))
            return carry

        lax.fori_loop(0, n_chunks // 2 - 1, body, 0)
        off = (n_chunks - 2) * tk
        stage((buf1, off + tk, tk, False), (buf0, off, tk))
        stage((st_scr, seq, CHUNK, True), (buf1, off + tk, tk))
        stage(None, (st_scr, seq, CHUNK))
        acc = acc_scr[...]
        outs.append(acc[:dv] / acc[dv:dv + 1])
    if diff:
        o = outs[0] - lam_ref[0:1, 0:1] * outs[1]
        ms = jnp.mean(o * o, axis=0, keepdims=True)
        o = o * lax.rsqrt(ms + EPS) * (og_ref[...] * out_scale)
    else:
        o = jnp.concatenate(outs, axis=0)
    o_ref[0] = o.T


def _attention(qT, k, vT, extra, *, diff, seq, tq, tk, out_scale, name):
    b, _, tp = qT.shape
    groups = qT.shape[1] // (2 * HEAD_PAD)
    dv = DIFF_V if diff else MLA_V
    v_rows = dv if diff else 2 * dv
    in_specs = [
        pl.BlockSpec((1, 2 * HEAD_PAD, tq), lambda i, g, j: (i, g, j)),
        pl.BlockSpec((1, 2, tp, HEAD_PAD), lambda i, g, j: (i, g, 0, 0)),
        pl.BlockSpec((1, v_rows, tp), lambda i, g, j: (i, g, 0)),
    ] + [_full(e.shape) for e in extra]
    return pl.pallas_call(
        functools.partial(_attn_kernel, diff=diff, seq=seq, tk=tk, out_scale=out_scale),
        grid=(b, groups, tp // tq),
        in_specs=in_specs,
        out_specs=pl.BlockSpec((1, tq, LANE), lambda i, g, j: (i, j, g)),
        out_shape=jax.ShapeDtypeStruct((b, tp, groups * LANE), F32),
        scratch_shapes=[pltpu.VMEM((2, tk, tq), F32), pltpu.VMEM((CHUNK, tq), F32),
                        pltpu.VMEM((8, tq), F32), pltpu.VMEM((dv + 16, tq), F32)],
        compiler_params=_params("parallel", "parallel", "arbitrary"), name=name)(qT, k, vT, *extra)


def _conv_kernel(x_ref, prev_ref, next_ref, w_ref, b_ref, o_ref, scr):
    tc = x_ref.shape[1]
    scr[0:8, :] = prev_ref[0]
    scr[8:8 + tc, :] = x_ref[0]
    scr[8 + tc:16 + tc, :] = next_ref[0]
    acc = jnp.broadcast_to(b_ref[...], (tc, b_ref.shape[1]))
    for j in range(SSD_CONV):
        acc = acc + w_ref[j:j + 1, :] * scr[8 - SSD_CONV // 2 + j:8 - SSD_CONV // 2 + j + tc, :]
    o_ref[0] = acc * _sigmoid(acc)


def _ssd_conv(xbc, w8, bias, tc):
    b, tp, c = xbc.shape
    nb8 = tp // 8
    r8 = tc // 8
    return pl.pallas_call(
        _conv_kernel, grid=(b, tp // tc),
        in_specs=[pl.BlockSpec((1, tc, c), lambda i, j: (i, j, 0)),
                  pl.BlockSpec((1, 8, c), lambda i, j: (i, (j * r8 + nb8 - 1) % nb8, 0)),
                  pl.BlockSpec((1, 8, c), lambda i, j: (i, ((j + 1) * r8) % nb8, 0)),
                  _full(w8.shape), _full(bias.shape)],
        out_specs=pl.BlockSpec((1, tc, c), lambda i, j: (i, j, 0)),
        out_shape=jax.ShapeDtypeStruct((b, tp, c), F32),
        scratch_shapes=[pltpu.VMEM((tc + 16, c), F32)],
        compiler_params=_params("parallel", "parallel"), name="ssd_conv")(xbc, xbc, xbc, w8, bias)


def _tri_masks(reverse):
    row = lax.broadcasted_iota(jnp.int32, (CHUNK, CHUNK), 0)
    col = lax.broadcasted_iota(jnp.int32, (CHUNK, CHUNK), 1)
    keep = (col >= row) if reverse else (col <= row)
    tri_c = keep.astype(BF16)
    tri_r = ((row >= col) if reverse else (row <= col)).astype(BF16)
    return keep, tri_c, tri_r, row, col


def _ssd_kernel(*refs, reverse, n_chunks):
    if reverse:
        (x_ref, dt_ref, dtT_ref, bias_r, bias_c, a_r, a_c, z_ref, yf_ref, dsk_ref, ng_ref,
         y_ref, st_ref) = refs
    else:
        x_ref, dt_ref, dtT_ref, bias_r, bias_c, a_r, a_c, y_ref, st_ref = refs
    step = pl.program_id(1)
    is_meta = step == (n_chunks - 1 if reverse else 0)
    d = 1 if reverse else 0
    h8 = SSD_HEADS

    @pl.when(step == 0)
    def _():
        st_ref[...] = jnp.zeros(st_ref.shape, F32)

    keep, tri_c, tri_r, row, col = _tri_masks(reverse)
    xbc = x_ref[0]
    xs = xbc[:, :SSD_INNER]
    bm = xbc[:, SSD_INNER:SSD_INNER + CHUNK]
    cm = xbc[:, SSD_INNER + CHUNK:SSD_INNER + 2 * CHUNK]

    dt_c = _softplus(dt_ref[0][:, d * h8:(d + 1) * h8] + bias_r[:, d * h8:(d + 1) * h8])
    dt_r = _softplus(dtT_ref[0][d * h8:(d + 1) * h8, :] + bias_c[d * h8:(d + 1) * h8, :])
    pad_c = jnp.logical_and(is_meta, row[:, :h8] < N_PAD)
    pad_r = jnp.logical_and(is_meta, col[:h8, :] < N_PAD)
    dt_c = jnp.where(pad_c, 0.0, dt_c)
    dt_r = jnp.where(pad_r, 0.0, dt_r)
    a_col = dt_c * a_r[:, d * h8:(d + 1) * h8]
    a_row = dt_r * a_c[d * h8:(d + 1) * h8, :]
    cs_c = _cumsum_cols(tri_c, a_col)
    cs_r = _cumsum_rows(a_row, tri_r)
    last = 0 if reverse else CHUNK - 1
    tot_r = cs_r[:, last:last + 1]
    tot_c = cs_c[last:last + 1, :]

    lane_lo = col < SSD_STATE
    row_lo = row < SSD_STATE
    blockdiag = jnp.logical_not(jnp.logical_xor(lane_lo, row_lo))
    cm_sw = pltpu.roll(cm, SSD_STATE, 1)
    c_dup = (jnp.where(lane_lo, cm, cm_sw), jnp.where(lane_lo, cm_sw, cm))
    g_mat = (_dot_nt(jnp.where(lane_lo, cm, 0.0).astype(BF16), bm.astype(BF16)),
             _dot_nt(jnp.where(lane_lo, 0.0, cm).astype(BF16), bm.astype(BF16)))
    bT = bm.T

    heads_per_group = SSD_HEADS // SSD_GROUPS
    for j in range(SSD_HEADS // 2):
        g = (2 * j) // heads_per_group
        h0, h1 = 2 * j, 2 * j + 1
        xp = xs[:, j * CHUNK:(j + 1) * CHUNK]
        dtp = jnp.where(lane_lo, dt_c[:, h0:h0 + 1], dt_c[:, h1:h1 + 1])
        xdt = xp * dtp
        parts = []
        for h in (h0, h1):
            diff_ = cs_c[:, h:h + 1] - cs_r[h:h + 1, :]
            parts.append((g_mat[g] * jnp.exp(jnp.where(keep, diff_, NEG))).astype(BF16))
        e_col = jnp.where(lane_lo, jnp.exp(cs_c[:, h0:h0 + 1]), jnp.exp(cs_c[:, h1:h1 + 1]))
        parts.append((c_dup[g] * e_col).astype(BF16))
        lhs = jnp.concatenate(parts, axis=1)
        s_old = st_ref[j]
        rhs = jnp.concatenate([jnp.where(lane_lo, xdt, 0.0).astype(BF16),
                               jnp.where(lane_lo, 0.0, xdt).astype(BF16),
                               s_old.astype(BF16)], axis=0)
        y_pair = _dot(lhs, rhs)
        btg = bT[g * SSD_STATE:(g + 1) * SSD_STATE, :]
        bd = jnp.concatenate([btg * jnp.exp(tot_r[h0:h0 + 1, :] - cs_r[h0:h0 + 1, :]),
                              btg * jnp.exp(tot_r[h1:h1 + 1, :] - cs_r[h1:h1 + 1, :])], axis=0)
        s_new = jnp.where(blockdiag, _dot(bd.astype(BF16), xdt.astype(BF16)), 0.0)
        carry = jnp.where(row_lo, jnp.exp(tot_c[:, h0:h0 + 1]), jnp.exp(tot_c[:, h1:h1 + 1]))
        st_ref[j] = s_old * carry + s_new
        if reverse:
            y_ref[0, :, j * CHUNK:(j + 1) * CHUNK] = y_pair + yf_ref[0, :, j * CHUNK:(j + 1) * CHUNK]
        else:
            y_ref[0, :, j * CHUNK:(j + 1) * CHUNK] = y_pair
    if reverse:
        z = z_ref[0]
        y = (y_ref[0] + dsk_ref[...] * xs) * (z * _sigmoid(z))
        y_ref[0] = _rms_rows(y, ng_ref[...])


def _chunk_order(n_chunks, reverse):
    if reverse:
        return lambda c: (2 * n_chunks - 2 - c) % n_chunks
    return lambda c: (c + n_chunks - 1) % n_chunks


def _ssd_pass(xbc_act, dt, dtT, consts, extra, *, reverse):
    b, tp, _ = xbc_act.shape
    n_chunks = tp // CHUNK
    order = _chunk_order(n_chunks, reverse)
    row_spec = lambda c: pl.BlockSpec((1, CHUNK, c), lambda i, s: (i, order(s), 0))
    in_specs = [row_spec(xbc_act.shape[2]), row_spec(dt.shape[2]),
                pl.BlockSpec((1, dtT.shape[1], CHUNK), lambda i, s: (i, 0, order(s)))]
    in_specs += [_full(c.shape) for c in consts]
    args = [xbc_act, dt, dtT] + list(consts)
    for e in extra:
        if e.ndim == 3:
            in_specs.append(row_spec(e.shape[2]))
        else:
            in_specs.append(_full(e.shape))
        args.append(e)
    return pl.pallas_call(
        functools.partial(_ssd_kernel, reverse=reverse, n_chunks=n_chunks),
        grid=(b, n_chunks), in_specs=in_specs, out_specs=row_spec(SSD_INNER),
        out_shape=jax.ShapeDtypeStruct((b, tp, SSD_INNER), F32),
        scratch_shapes=[pltpu.VMEM((SSD_HEADS // 2, CHUNK, CHUNK), F32)],
        compiler_params=_params("parallel", "arbitrary"),
        name="ssd_rev" if reverse else "ssd_fwd")(*args)


def _mlstm_kernel(*refs, reverse, n_chunks):
    if reverse:
        (q_ref, k_ref, v_ref, gt_ref, gtT_ref, gb_r, gb_c, hf_ref, o_ref, og_ref,
         y_ref, st_ref, m_ref) = refs
    else:
        q_ref, k_ref, v_ref, gt_ref, gtT_ref, gb_r, gb_c, y_ref, st_ref, m_ref = refs
    step = pl.program_id(1)
    is_meta = step == (n_chunks - 1 if reverse else 0)
    d = 1 if reverse else 0
    nh = MLSTM_HEADS

    @pl.when(step == 0)
    def _():
        st_ref[...] = jnp.zeros(st_ref.shape, F32)
        m_ref[...] = jnp.full(m_ref.shape, NEG, F32)

    keep, tri_c, tri_r, row, col = _tri_masks(reverse)
    gt = gt_ref[0] + gb_r[...]
    gtT = gtT_ref[0] + gb_c[...]
    i_lo, f_lo = d * nh, 2 * nh + d * nh
    pad_c = jnp.logical_and(is_meta, row[:, :nh] < N_PAD)
    pad_r = jnp.logical_and(is_meta, col[:nh, :] < N_PAD)
    ig_r = jnp.where(pad_r, NEG, gtT[i_lo:i_lo + nh, :])
    fg_c = jnp.where(pad_c, 0.0, _log_sigmoid(gt[:, f_lo:f_lo + nh]))
    fg_r = jnp.where(pad_r, 0.0, _log_sigmoid(gtT[f_lo:f_lo + nh, :]))
    b_c = _cumsum_cols(tri_c, fg_c)
    b_r = _cumsum_rows(fg_r, tri_r)
    last = 0 if reverse else CHUNK - 1

    lane_lo = col < MLSTM_QK
    row_lo = row < MLSTM_QK
    ones = jnp.ones((CHUNK, MLSTM_V), BF16)
    scale = MLSTM_QK ** -0.5
    for h in range(nh):
        pair = h // 2
        lo = (h % 2) == 0
        qp = q_ref[0, :, pair * CHUNK:(pair + 1) * CHUNK]
        kp = k_ref[0, :, pair * CHUNK:(pair + 1) * CHUNK]
        qm = (jnp.where(lane_lo if lo else jnp.logical_not(lane_lo), qp, 0.0) * scale).astype(BF16)
        kT = jnp.where(row_lo if lo else jnp.logical_not(row_lo), kp.T, 0.0)
        v_aug = jnp.concatenate([v_ref[0, :, h * MLSTM_V:(h + 1) * MLSTM_V].astype(BF16), ones], axis=1)
        m_st = m_ref[h][0:1, 0:1]
        bc = b_c[:, h:h + 1]
        br = b_r[h:h + 1, :]
        ir = ig_r[h:h + 1, :]
        dmat = jnp.where(keep, bc - br + ir, -jnp.inf)
        inter = bc + m_st
        m_row = jnp.maximum(jnp.max(dmat, axis=1, keepdims=True), inter)
        w_intra = jnp.exp(dmat - m_row)
        w_inter = jnp.exp(inter - m_row)
        s = _dot_nt(qm, kp.astype(BF16)) * w_intra
        c_st = st_ref[h]
        comb = _dot(s.astype(BF16), v_aug) + w_inter * _dot(qm, c_st.astype(BF16))
        num = comb[:, :MLSTM_V]
        den = jnp.maximum(jnp.abs(comb[:, MLSTM_V:]), jnp.exp(-m_row))
        hout = num / den
        tot = br[:, last:last + 1]
        d_last = tot - br + ir
        m_new = jnp.maximum(tot + m_st, jnp.max(d_last, axis=1, keepdims=True))
        w_s = jnp.exp(d_last - m_new)
        w_prev = jnp.exp(tot + m_st - m_new)
        st_ref[h] = w_prev * c_st + _dot((kT * w_s).astype(BF16), v_aug)
        m_ref[h] = jnp.broadcast_to(m_new, m_ref.shape[1:])
        if reverse:
            hsum = hout + hf_ref[0, :, h * MLSTM_V:(h + 1) * MLSTM_V]
            o_gate = _sigmoid(o_ref[0, :, h * MLSTM_V:(h + 1) * MLSTM_V])
            y_ref[0, :, h * MLSTM_V:(h + 1) * MLSTM_V] = o_gate * _rms_rows(hsum, og_ref[...])
        else:
            y_ref[0, :, h * MLSTM_V:(h + 1) * MLSTM_V] = hout


def _mlstm_pass(q, k, v, gt, gtT, consts, extra, *, reverse):
    b, tp, _ = q.shape
    n_chunks = tp // CHUNK
    order = _chunk_order(n_chunks, reverse)
    row_spec = lambda c: pl.BlockSpec((1, CHUNK, c), lambda i, s: (i, order(s), 0))
    in_specs = [row_spec(q.shape[2]), row_spec(k.shape[2]), row_spec(v.shape[2]), row_spec(gt.shape[2]),
                pl.BlockSpec((1, gtT.shape[1], CHUNK), lambda i, s: (i, 0, order(s)))]
    in_specs += [_full(c.shape) for c in consts]
    args = [q, k, v, gt, gtT] + list(consts)
    for e in extra:
        in_specs.append(row_spec(e.shape[2]) if e.ndim == 3 else _full(e.shape))
        args.append(e)
    nv = MLSTM_HEADS * MLSTM_V
    return pl.pallas_call(
        functools.partial(_mlstm_kernel, reverse=reverse, n_chunks=n_chunks),
        grid=(b, n_chunks), in_specs=in_specs, out_specs=row_spec(nv),
        out_shape=jax.ShapeDtypeStruct((b, tp, nv), F32),
        scratch_shapes=[pltpu.VMEM((MLSTM_HEADS, CHUNK, 2 * MLSTM_V), F32),
                        pltpu.VMEM((MLSTM_HEADS, 8, LANE), F32)],
        compiler_params=_params("parallel", "arbitrary"),
        name="mlstm_rev" if reverse else "mlstm_fwd")(*args)


def _out_kernel(ya_ref, yb_ref, yc_ref, yd_ref, h_ref, ag_ref, w_ref, o_ref, *, tp, seq):
    tm = h_ref.shape[0]
    ya = _rms_rows(ya_ref[...], ag_ref[...])
    mixed = jnp.concatenate([ya.astype(BF16), yb_ref[...].astype(BF16),
                             yc_ref[...].astype(BF16), yd_ref[...].astype(BF16)], axis=1)
    out = h_ref[...] + _dot(mixed, w_ref[...])
    t = (pl.program_id(0) * tm + lax.broadcasted_iota(jnp.int32, (tm, 1), 0)) % tp
    is_pad = jnp.logical_and(t >= seq, t < seq + N_PAD)
    o_ref[...] = jnp.where(is_pad, 0.0, out)


def _out_proj(ya, yb, yc, yd, h, ag, w, *, tp, seq, tm):
    rows, d = h.shape
    row_spec = lambda c: pl.BlockSpec((tm, c), lambda i: (i, 0))
    return pl.pallas_call(
        functools.partial(_out_kernel, tp=tp, seq=seq), grid=(rows // tm,),
        in_specs=[row_spec(ya.shape[1]), row_spec(yb.shape[1]), row_spec(yc.shape[1]),
                  row_spec(yd.shape[1]), row_spec(d), _full(ag.shape), _full(w.shape)],
        out_specs=row_spec(d), out_shape=jax.ShapeDtypeStruct((rows, d), F32),
        compiler_params=_params("parallel"), name="out_proj")(ya, yb, yc, yd, h, ag, w)


def _ffn_kernel(h_ref, g_ref, wg_ref, wu_ref, wo_ref, o_ref, *, n_split):
    x = h_ref[...]
    hn = _rms_rows(x, g_ref[...]).astype(BF16)
    f = wg_ref.shape[1]
    tf = f // n_split
    acc = x
    for c in range(n_split):
        gate = _dot(hn, wg_ref[:, c * tf:(c + 1) * tf])
        up = _dot(hn, wu_ref[:, c * tf:(c + 1) * tf])
        act = (gate * _sigmoid(gate) * up).astype(BF16)
        acc = acc + _dot(act, wo_ref[c * tf:(c + 1) * tf, :])
    o_ref[...] = acc


def _ffn(h, g, wg, wu, wo, *, tm):
    rows, d = h.shape
    f = wg.shape[1]
    n_split = 2 if (f // 2) % LANE == 0 else 1
    row_spec = pl.BlockSpec((tm, d), lambda i: (i, 0))
    resident = lambda a: pl.BlockSpec(a.shape, lambda i: (0, 0), pipeline_mode=pl.Buffered(1))
    return pl.pallas_call(
        functools.partial(_ffn_kernel, n_split=n_split), grid=(rows // tm,),
        in_specs=[row_spec, _full(g.shape), resident(wg), resident(wu), resident(wo)],
        out_specs=row_spec, out_shape=jax.ShapeDtypeStruct((rows, d), F32),
        compiler_params=_params("parallel"), name="ffn")(h, g, wg, wu, wo)


def _rope_tables_t(pos, rot_dim):
    inv = 1.0 / (ROPE_THETA ** (jnp.arange(0, rot_dim, 2, dtype=F32) / rot_dim))
    ang = pos[:, None] * inv[None, :]
    return jnp.cos(ang).T, jnp.sin(ang).T


def _col(v, n=None):
    v = v.astype(F32)
    if n is not None:
        v = jnp.pad(v, (0, n - v.shape[0]))
    return v[:, None]


def _row(v):
    return v.astype(F32)[None, :]


def kernel(x, meta_tokens, attn_norm_g, w_in, mla_q_norm_g, mla_kv_norm_g, mla_w_uq, mla_w_ukv, mla_q_head_g, mla_k_head_g, mla_out_g, ssd_conv_w, ssd_conv_b, ssd_dt_bias, ssd_a_log, ssd_d, ssd_norm_g, diff_q_head_g, diff_k_head_g, diff_lambda, diff_out_g, mlstm_i_bias, mlstm_f_bias, mlstm_out_g, w_out, ffn_norm_g, w_ffn_in, w_ffn_out):
    b, seq, d = x.shape
    depth = w_in.shape[0]
    tp = seq + CHUNK
    assert seq % CHUNK == 0
    tm_in = _pick_tile(tp, (640, 384, 128))
    tq = _pick_tile(tp, (1664, 640, 384, 128))
    tk = _pick_tile(seq, (512, 384, 256, 128))
    tm_flat = _pick_tile(b * tp, (512, 256, 128))

    meta = jnp.broadcast_to(meta_tokens[None].astype(x.dtype), (b, N_META, d))
    h = jnp.concatenate([x, jnp.zeros((b, N_PAD, d), x.dtype), meta], axis=1).reshape(b * tp, d)
    pos = jnp.concatenate([N_META + jnp.arange(seq, dtype=F32), jnp.zeros((N_PAD,), F32),
                           jnp.arange(N_META, dtype=F32)])
    cos_m, sin_m = _rope_tables_t(pos, MLA_ROPE)
    cos_d, sin_d = _rope_tables_t(pos, DIFF_ROPE)

    sizes = (MLA_Q_RANK, MLA_KV_RANK, MLA_ROPE, SSD_INNER, SSD_CONV_CH, 2 * SSD_HEADS,
             2 * DIFF_HEADS * DIFF_QK, 2 * DIFF_HEADS * DIFF_QK, DIFF_HEADS * DIFF_V,
             MLSTM_HEADS * MLSTM_QK, MLSTM_HEADS * MLSTM_QK, MLSTM_HEADS * MLSTM_V,
             MLSTM_HEADS * MLSTM_V, 2 * MLSTM_HEADS, 2 * MLSTM_HEADS)
    offs = [0]
    for s_ in sizes:
        offs.append(offs[-1] + s_)

    def cols(w, first, last):
        return w[:, offs[first]:offs[last + 1]]

    def pad_cols(w, n):
        return jnp.pad(w, ((0, 0), (0, n - w.shape[1])))

    for l in range(depth):
        lambda_init = 0.8 - 0.6 * math.exp(-0.3 * l)
        wl = w_in[l]
        h3 = h.reshape(b, tp, d)
        g_attn = _row(attn_norm_g[l])

        w_a = cols(wl, 0, 1).astype(BF16)
        w_kr = cols(wl, 2, 2).T.astype(BF16)
        w_uq = mla_w_uq[l].T.reshape(MLA_HEADS, MLA_QK, MLA_Q_RANK)
        w_uq = jnp.pad(w_uq, ((0, 0), (0, HEAD_PAD - MLA_QK), (0, 0))).reshape(MLA_HEADS * HEAD_PAD, MLA_Q_RANK)
        w_ukv = mla_w_ukv[l].T.reshape(MLA_HEADS, MLA_NOPE + MLA_V, MLA_KV_RANK)
        w_uk = w_ukv[:, :MLA_NOPE].reshape(MLA_HEADS * MLA_NOPE, MLA_KV_RANK)
        w_uv = w_ukv[:, MLA_NOPE:].reshape(MLA_HEADS * MLA_V, MLA_KV_RANK)
        qT_a, k_a, vT_a = _in_proj_call(
            _mla_in_kernel, h3,
            [g_attn, w_a, _row(mla_q_norm_g[l]), _row(mla_kv_norm_g[l]), w_uq.astype(BF16),
             w_uk.astype(BF16), w_uv.astype(BF16), w_kr, _col(mla_q_head_g[l], HEAD_PAD),
             _col(mla_k_head_g[l], HEAD_PAD), (cos_m,), (sin_m,)],
            [((MLA_HEADS * HEAD_PAD, tp), BF16, "col"), ((MLA_HEADS, tp, HEAD_PAD), BF16, "head"),
             ((MLA_HEADS * MLA_V, tp), BF16, "col")], tm_in, "mla_in")
        y_a = _attention(qT_a, k_a, vT_a, [], diff=False, seq=seq, tq=tq, tk=tk, out_scale=1.0,
                         name="mla_attn")

        w_c = cols(wl, 6, 8).T.astype(BF16)
        qT_c, k_c, vT_c = _in_proj_call(
            _diff_in_kernel, h3,
            [g_attn, w_c, _col(diff_q_head_g[l], HEAD_PAD), _col(diff_k_head_g[l], HEAD_PAD),
             (cos_d,), (sin_d,)],
            [((2 * DIFF_HEADS * HEAD_PAD, tp), BF16, "col"), ((2 * DIFF_HEADS, tp, HEAD_PAD), BF16, "head"),
             ((DIFF_HEADS * DIFF_V, tp), BF16, "col")], tm_in, "diff_in")
        lam = diff_lambda[l].astype(F32)
        lam_full = jnp.exp(jnp.sum(lam[0] * lam[1])) - jnp.exp(jnp.sum(lam[2] * lam[3])) + lambda_init
        y_c = _attention(qT_c, k_c, vT_c, [jnp.full((8, LANE), lam_full, F32), _col(diff_out_g[l])],
                         diff=True, seq=seq, tq=tq, tk=tk, out_scale=1.0 - lambda_init, name="diff_attn")

        n_b = SSD_INNER + SSD_CONV_CH + 2 * SSD_HEADS
        w_b = pad_cols(cols(wl, 3, 5), -(-n_b // LANE) * LANE).astype(BF16)
        w_dt = cols(wl, 5, 5).T.astype(BF16)
        z_b, xbc, dt, dtT = _in_proj_call(
            _ssd_in_kernel, h3, [g_attn, w_b, w_dt],
            [((tp, SSD_INNER), F32, "row"), ((tp, SSD_CONV_CH), F32, "row"),
             ((tp, 2 * SSD_HEADS), F32, "row"), ((2 * SSD_HEADS, tp), F32, "col")], tm_in, "ssd_in")
        w8 = jnp.pad(ssd_conv_w[l].astype(F32), ((0, 8 - SSD_CONV), (0, 0)))
        xbc_act = _ssd_conv(xbc, w8, _row(ssd_conv_b[l]), tm_in)
        dt_bias = ssd_dt_bias[l].astype(F32).reshape(-1)
        a_neg = -jnp.exp(ssd_a_log[l].astype(F32)).reshape(-1)
        ssd_consts = [_row(dt_bias), _col(dt_bias), _row(a_neg), _col(a_neg)]
        y_f = _ssd_pass(xbc_act, dt, dtT, ssd_consts, [], reverse=False)
        y_b = _ssd_pass(xbc_act, dt, dtT, ssd_consts,
                        [z_b, y_f, _row(jnp.repeat(ssd_d[l], SSD_HEAD_DIM)), _row(ssd_norm_g[l])],
                        reverse=True)

        n_d = 2 * MLSTM_HEADS * MLSTM_QK + 2 * MLSTM_HEADS * MLSTM_V + 4 * MLSTM_HEADS
        w_d = pad_cols(cols(wl, 9, 14), -(-n_d // LANE) * LANE).astype(BF16)
        w_g = cols(wl, 13, 14).T.astype(BF16)
        q_d, k_d, v_d, o_d, gt, gtT = _in_proj_call(
            _mlstm_in_kernel, h3, [g_attn, w_d, w_g],
            [((tp, MLSTM_HEADS * MLSTM_QK), F32, "row"), ((tp, MLSTM_HEADS * MLSTM_QK), F32, "row"),
             ((tp, MLSTM_HEADS * MLSTM_V), F32, "row"), ((tp, MLSTM_HEADS * MLSTM_V), F32, "row"),
             ((tp, 4 * MLSTM_HEADS), F32, "row"), ((4 * MLSTM_HEADS, tp), F32, "col")], tm_in, "mlstm_in")
        gate_bias = jnp.concatenate([mlstm_i_bias[l].reshape(-1), mlstm_f_bias[l].reshape(-1)]).astype(F32)
        ml_consts = [_row(gate_bias), _col(gate_bias)]
        h_f = _mlstm_pass(q_d, k_d, v_d, gt, gtT, ml_consts, [], reverse=False)
        y_d = _mlstm_pass(q_d, k_d, v_d, gt, gtT, ml_consts, [h_f, o_d, _row(mlstm_out_g[l])], reverse=True)

        flat = lambda a: a.reshape(b * tp, a.shape[2])
        h = _out_proj(flat(y_a), flat(y_b), flat(y_c), flat(y_d), h, _row(mla_out_g[l]),
                      w_out[l].astype(BF16), tp=tp, seq=seq, tm=tm_flat)
        f = w_ffn_out.shape[1]
        h = _ffn(h, _row(ffn_norm_g[l]), w_ffn_in[l][:, :f].astype(BF16), w_ffn_in[l][:, f:].astype(BF16),
                 w_ffn_out[l].astype(BF16), tm=tm_flat)

    return h.reshape(b, tp, d)[:, :seq].astype(x.dtype)
```

```python
import functools
import math

import jax
import jax.numpy as jnp
from jax import lax
from jax.experimental import pallas as pl
from jax.experimental.pallas import tpu as pltpu

N_META = 16
ROPE_THETA = 500000.0
EPS = 1e-6
CHUNK = 128
N_PAD = CHUNK - N_META
NEG = -1e30
LOG2E = 1.4426950408889634

MLA_HEADS, MLA_NOPE, MLA_ROPE, MLA_V = 8, 64, 32, 64
MLA_QK = MLA_NOPE + MLA_ROPE
MLA_Q_RANK, MLA_KV_RANK = 384, 256
SSD_HEADS, SSD_HEAD_DIM, SSD_GROUPS, SSD_STATE, SSD_CONV = 8, 64, 2, 64, 5
SSD_INNER = SSD_HEADS * SSD_HEAD_DIM
SSD_CONV_CH = SSD_INNER + 2 * SSD_GROUPS * SSD_STATE
DIFF_HEADS, DIFF_QK = 4, 64
DIFF_V = 2 * DIFF_QK
DIFF_ROPE = DIFF_QK // 4
MLSTM_HEADS, MLSTM_QK, MLSTM_V = 4, 64, 128
HEAD_PAD = 128
LANE = 128
Q_STRIP = 256
VMEM_LIMIT = 52 * 1024 * 1024

F32 = jnp.float32
BF16 = jnp.bfloat16
EXP_DTYPE = jnp.bfloat16


def _dot(a, b):
    return jnp.dot(a, b, preferred_element_type=F32)


def _dot_nt(a, b):
    return lax.dot_general(a, b, (((1,), (1,)), ((), ())), preferred_element_type=F32)


def _rms_rows(x, g):
    ms = jnp.mean(x * x, axis=-1, keepdims=True)
    return x * lax.rsqrt(ms + EPS) * g


def _split3(a):
    hi = a.astype(BF16)
    r1 = a - hi.astype(F32)
    mid = r1.astype(BF16)
    lo = (r1 - mid.astype(F32)).astype(BF16)
    return hi, mid, lo


def _cumsum_cols(tri, a):
    hi, mid, lo = _split3(a)
    return _dot(tri, hi) + _dot(tri, mid) + _dot(tri, lo)


def _cumsum_rows(a, tri):
    hi, mid, lo = _split3(a)
    return _dot(hi, tri) + _dot(mid, tri) + _dot(lo, tri)


def _softplus(x):
    return jnp.maximum(x, 0.0) + jnp.log(1.0 + jnp.exp(-jnp.abs(x)))


def _log_sigmoid(x):
    return jnp.minimum(x, 0.0) - jnp.log(1.0 + jnp.exp(-jnp.abs(x)))


def _sigmoid(x):
    return 1.0 / (1.0 + jnp.exp(-x))


def _pick_tile(n, candidates):
    for c in candidates:
        if n % c == 0:
            return c
    raise ValueError(f"no tile in {candidates} divides {n}")


def _params(*sem):
    return pltpu.CompilerParams(dimension_semantics=sem, vmem_limit_bytes=VMEM_LIMIT)


def _norm_rope_t(blk, g_col, cos, sin, n_real):
    r = cos.shape[0]
    ms = jnp.sum(blk * blk, axis=0, keepdims=True) * (1.0 / n_real)
    y = blk * lax.rsqrt(ms + EPS) * g_col
    x1, x2, rest = y[:r], y[r:2 * r], y[2 * r:]
    return jnp.concatenate([x1 * cos - x2 * sin, x2 * cos + x1 * sin, rest], axis=0)


def _mla_in_kernel(h_ref, g_ref, wa_ref, gq_ref, gkv_ref, wuq_ref, wuk_ref, wuv_ref, wkr_ref,
                   qhg_ref, khg_ref, cos_ref, sin_ref, qT_ref, k_ref, vT_ref):
    hn = _rms_rows(h_ref[0], g_ref[...]).astype(BF16)
    acc = _dot(hn, wa_ref[...])
    cqn = _rms_rows(acc[:, :MLA_Q_RANK], gq_ref[...]).astype(BF16)
    ckvn = _rms_rows(acc[:, MLA_Q_RANK:], gkv_ref[...]).astype(BF16)
    qT = _dot_nt(wuq_ref[...], cqn)
    knT = _dot_nt(wuk_ref[...], ckvn)
    vT = _dot_nt(wuv_ref[...], ckvn)
    krT = _dot_nt(wkr_ref[...], hn)
    cos, sin = cos_ref[...], sin_ref[...]
    t = krT.shape[1]
    zpad = jnp.zeros((HEAD_PAD - MLA_QK, t), F32)
    q_scale = (MLA_QK ** -0.5) * LOG2E
    for h in range(MLA_HEADS):
        qb = _norm_rope_t(qT[h * HEAD_PAD:(h + 1) * HEAD_PAD], qhg_ref[...], cos, sin, MLA_QK)
        qT_ref[0, h * HEAD_PAD:(h + 1) * HEAD_PAD, :] = (qb * q_scale).astype(BF16)
        kb = jnp.concatenate([krT, knT[h * MLA_NOPE:(h + 1) * MLA_NOPE], zpad], axis=0)
        kb = _norm_rope_t(kb, khg_ref[...], cos, sin, MLA_QK)
        k_ref[0, h] = kb.T.astype(BF16)
    vT_ref[0] = vT.astype(BF16)


def _diff_in_kernel(h_ref, g_ref, wc_ref, qhg_ref, khg_ref, cos_ref, sin_ref, qT_ref, k_ref, vT_ref):
    hn = _rms_rows(h_ref[0], g_ref[...]).astype(BF16)
    pT = _dot_nt(wc_ref[...], hn)
    cos, sin = cos_ref[...], sin_ref[...]
    t = pT.shape[1]
    zpad = jnp.zeros((HEAD_PAD - DIFF_QK, t), F32)
    nq = 2 * DIFF_HEADS * DIFF_QK
    q_scale = (DIFF_QK ** -0.5) * LOG2E
    for h in range(2 * DIFF_HEADS):
        qb = jnp.concatenate([pT[h * DIFF_QK:(h + 1) * DIFF_QK], zpad], axis=0)
        qb = _norm_rope_t(qb, qhg_ref[...], cos, sin, DIFF_QK)
        qT_ref[0, h * HEAD_PAD:(h + 1) * HEAD_PAD, :] = (qb * q_scale).astype(BF16)
        kb = jnp.concatenate([pT[nq + h * DIFF_QK:nq + (h + 1) * DIFF_QK], zpad], axis=0)
        kb = _norm_rope_t(kb, khg_ref[...], cos, sin, DIFF_QK)
        k_ref[0, h] = kb.T.astype(BF16)
    vT_ref[0] = pT[2 * nq:].astype(BF16)


def _ssd_in_kernel(h_ref, g_ref, wb_ref, wdt_ref, z_ref, xbc_ref, dt_ref, dtT_ref):
    hn = _rms_rows(h_ref[0], g_ref[...]).astype(BF16)
    acc = _dot(hn, wb_ref[...])
    z_ref[0] = acc[:, :SSD_INNER]
    xbc_ref[0] = acc[:, SSD_INNER:SSD_INNER + SSD_CONV_CH]
    dt_ref[0] = acc[:, SSD_INNER + SSD_CONV_CH:SSD_INNER + SSD_CONV_CH + 2 * SSD_HEADS]
    dtT_ref[0] = _dot_nt(wdt_ref[...], hn)


def _mlstm_in_kernel(h_ref, g_ref, wd_ref, wg_ref, q_ref, k_ref, v_ref, o_ref, gt_ref, gtT_ref):
    hn = _rms_rows(h_ref[0], g_ref[...]).astype(BF16)
    acc = _dot(hn, wd_ref[...])
    nqk = MLSTM_HEADS * MLSTM_QK
    nv = MLSTM_HEADS * MLSTM_V
    q_ref[0] = acc[:, :nqk]
    k_ref[0] = acc[:, nqk:2 * nqk]
    v_ref[0] = acc[:, 2 * nqk:2 * nqk + nv]
    o_ref[0] = acc[:, 2 * nqk + nv:2 * nqk + 2 * nv]
    gt_ref[0] = acc[:, 2 * nqk + 2 * nv:2 * nqk + 2 * nv + 4 * MLSTM_HEADS]
    gtT_ref[0] = _dot_nt(wg_ref[...], hn)


def _full(shape):
    nd = len(shape)
    return pl.BlockSpec(shape, lambda *_: (0,) * nd)


def _in_proj_call(body, h, consts, outs, tm, name):
    b, tp, d = h.shape
    in_specs = [pl.BlockSpec((1, tm, d), lambda i, j: (i, j, 0))]
    for c in consts:
        if isinstance(c, tuple):
            in_specs.append(pl.BlockSpec((c[0].shape[0], tm), lambda i, j: (0, j)))
        else:
            in_specs.append(_full(c.shape))
    out_shapes, out_specs = [], []
    for shape, dtype, kind in outs:
        out_shapes.append(jax.ShapeDtypeStruct((b,) + shape, dtype))
        if kind == "row":
            out_specs.append(pl.BlockSpec((1, tm, shape[1]), lambda i, j: (i, j, 0)))
        elif kind == "col":
            out_specs.append(pl.BlockSpec((1, shape[0], tm), lambda i, j: (i, 0, j)))
        else:
            out_specs.append(pl.BlockSpec((1, shape[0], tm, shape[2]), lambda i, j: (i, 0, j, 0)))
    args = [h] + [c[0] if isinstance(c, tuple) else c for c in consts]
    return pl.pallas_call(
        body, grid=(b, tp // tm), in_specs=in_specs, out_specs=out_specs, out_shape=out_shapes,
        compiler_params=_params("parallel", "parallel"), name=name)(*args)


def _attn_kernel(*refs, diff, seq, tk, out_scale):
    if diff:
        qT_ref, k_ref, vT_ref, lam_ref, og_ref, o_ref, s_scr, st_scr, m_scr, acc_scr = refs
    else:
        qT_ref, k_ref, vT_ref, o_ref, s_scr, st_scr, m_scr, acc_scr = refs
    dv = DIFF_V if diff else MLA_V
    tq = qT_ref.shape[2]
    n_chunks = seq // tk
    assert n_chunks % 2 == 0
    outs = []
    for hh in range(2):
        v_lo = 0 if diff else hh * dv
        m_scr[...] = jnp.full(m_scr.shape, NEG, F32)
        acc_scr[...] = jnp.zeros(acc_scr.shape, F32)

        def stage(nxt, cur, hh=hh, v_lo=v_lo):
            if nxt is not None:
                k_n = k_ref[0, hh, pl.ds(nxt[1], nxt[2]), :]
            if cur is not None:
                ones = jnp.ones((16, cur[2]), BF16)
                v_aug = jnp.concatenate([vT_ref[0, v_lo:v_lo + dv, pl.ds(cur[1], cur[2])], ones], axis=0)
            pending = None
            for j0 in range(0, tq, Q_STRIP):
                w = min(Q_STRIP, tq - j0)
                if cur is not None:
                    s = cur[0][:, j0:j0 + w]
                    m_old = m_scr[0:1, j0:j0 + w]
                    m_new = jnp.maximum(m_old, jnp.max(s, axis=0, keepdims=True))
                    alpha = jnp.exp2(m_old - m_new)
                    p = jnp.exp2((s - m_new).astype(EXP_DTYPE)).astype(BF16)
                    m_scr[0:1, j0:j0 + w] = m_new
                if nxt is not None:
                    s_n = _dot(k_n, qT_ref[0, hh * HEAD_PAD:(hh + 1) * HEAD_PAD, j0:j0 + w])
                    if nxt[3]:
                        row = lax.broadcasted_iota(jnp.int32, s_n.shape, 0)
                        s_n = jnp.where(row >= N_PAD, s_n, NEG)
                    nxt[0][:, j0:j0 + w] = s_n
                if cur is not None:
                    if pending is not None:
                        pj, pw, pa, pp = pending
                        acc_scr[:, pj:pj + pw] = pa * acc_scr[:, pj:pj + pw] + _dot(v_aug, pp)
                    pending = (j0, w, alpha, p)
            if pending is not None:
                pj, pw, pa, pp = pending
                acc_scr[:, pj:pj + pw] = pa * acc_scr[:, pj:pj + pw] + _dot(v_aug, pp)

        buf0, buf1 = s_scr.at[0], s_scr.at[1]
        stage((buf0, 0, tk, False), None)

        def body(i, carry, stage=stage):
            off = pl.multiple_of(2 * i * tk, 2 * tk)
            stage((buf1, off + tk, tk, False), (buf0, off, tk))
            stage((buf0, off + 2 * tk, tk, False), (buf1, off + tk, tk))
            return carry

        lax.fori_loop(0, n_chunks // 2 - 1, body, 0)
        off = (n_chunks - 2) * tk
        stage((buf1, off + tk, tk, False), (buf0, off, tk))
        stage((st_scr, seq, CHUNK, True), (buf1, off + tk, tk))
        stage(None, (st_scr, seq, CHUNK))
        acc = acc_scr[...]
        outs.append(acc[:dv] / acc[dv:dv + 1])
    if diff:
        o = outs[0] - lam_ref[0:1, 0:1] * outs[1]
        ms = jnp.mean(o * o, axis=0, keepdims=True)
        o = o * lax.rsqrt(ms + EPS) * (og_ref[...] * out_scale)
    else:
        o = jnp.concatenate(outs, axis=0)
    o_ref[0] = o.T


def _attention(qT, k, vT, extra, *, diff, seq, tq, tk, out_scale, name):
    b, _, tp = qT.shape
    groups = qT.shape[1] // (2 * HEAD_PAD)
    dv = DIFF_V if diff else MLA_V
    v_rows = dv if diff else 2 * dv
    in_specs = [
        pl.BlockSpec((1, 2 * HEAD_PAD, tq), lambda i, g, j: (i, g, j)),
        pl.BlockSpec((1, 2, tp, HEAD_PAD), lambda i, g, j: (i, g, 0, 0)),
        pl.BlockSpec((1, v_rows, tp), lambda i, g, j: (i, g, 0)),
    ] + [_full(e.shape) for e in extra]
    return pl.pallas_call(
        functools.partial(_attn_kernel, diff=diff, seq=seq, tk=tk, out_scale=out_scale),
        grid=(b, groups, tp // tq),
        in_specs=in_specs,
        out_specs=pl.BlockSpec((1, tq, LANE), lambda i, g, j: (i, j, g)),
        out_shape=jax.ShapeDtypeStruct((b, tp, groups * LANE), F32),
        scratch_shapes=[pltpu.VMEM((2, tk, tq), F32), pltpu.VMEM((CHUNK, tq), F32),
                        pltpu.VMEM((8, tq), F32), pltpu.VMEM((dv + 16, tq), F32)],
        compiler_params=_params("parallel", "parallel", "arbitrary"), name=name)(qT, k, vT, *extra)


def _conv_kernel(x_ref, prev_ref, next_ref, w_ref, b_ref, xs_ref, bc_ref, scr):
    tc = x_ref.shape[1]
    scr[0:8, :] = prev_ref[0]
    scr[8:8 + tc, :] = x_ref[0]
    scr[8 + tc:16 + tc, :] = next_ref[0]
    acc = jnp.broadcast_to(b_ref[...], (tc, b_ref.shape[1]))
    for j in range(SSD_CONV):
        acc = acc + w_ref[j:j + 1, :] * scr[8 - SSD_CONV // 2 + j:8 - SSD_CONV // 2 + j + tc, :]
    act = acc * _sigmoid(acc)
    xs_ref[0] = act[:, :SSD_INNER]
    bc_ref[0] = act[:, SSD_INNER:]


def _ssd_conv(xbc, w8, bias, tc):
    b, tp, c = xbc.shape
    nb8 = tp // 8
    r8 = tc // 8
    row_spec = lambda n: pl.BlockSpec((1, tc, n), lambda i, j: (i, j, 0))
    return pl.pallas_call(
        _conv_kernel, grid=(b, tp // tc),
        in_specs=[row_spec(c),
                  pl.BlockSpec((1, 8, c), lambda i, j: (i, (j * r8 + nb8 - 1) % nb8, 0)),
                  pl.BlockSpec((1, 8, c), lambda i, j: (i, ((j + 1) * r8) % nb8, 0)),
                  _full(w8.shape), _full(bias.shape)],
        out_specs=[row_spec(SSD_INNER), row_spec(c - SSD_INNER)],
        out_shape=[jax.ShapeDtypeStruct((b, tp, SSD_INNER), F32),
                   jax.ShapeDtypeStruct((b, tp, c - SSD_INNER), F32)],
        scratch_shapes=[pltpu.VMEM((tc + 16, c), F32)],
        compiler_params=_params("parallel", "parallel"), name="ssd_conv")(xbc, xbc, xbc, w8, bias)


def _tri_masks(reverse):
    row = lax.broadcasted_iota(jnp.int32, (CHUNK, CHUNK), 0)
    col = lax.broadcasted_iota(jnp.int32, (CHUNK, CHUNK), 1)
    keep = (col >= row) if reverse else (col <= row)
    tri_c = keep.astype(BF16)
    tri_r = ((row >= col) if reverse else (row <= col)).astype(BF16)
    return keep, tri_c, tri_r, row, col


def _ssd_direction(xs_ref, bc_ref, dt_ref, dtT_ref, bias_r, bias_c, a_r, a_c, y_ref, st_ref, *,
                   reverse, is_meta):
    d = 1 if reverse else 0
    h8 = SSD_HEADS
    keep, tri_c, tri_r, row, col = _tri_masks(reverse)
    bm = bc_ref[0, :, :CHUNK]
    cm = bc_ref[0, :, CHUNK:]

    dt_c = _softplus(dt_ref[0][:, d * h8:(d + 1) * h8] + bias_r[:, d * h8:(d + 1) * h8])
    dt_r = _softplus(dtT_ref[0][d * h8:(d + 1) * h8, :] + bias_c[d * h8:(d + 1) * h8, :])
    pad_c = jnp.logical_and(is_meta, row[:, :h8] < N_PAD)
    pad_r = jnp.logical_and(is_meta, col[:h8, :] < N_PAD)
    dt_c = jnp.where(pad_c, 0.0, dt_c)
    dt_r = jnp.where(pad_r, 0.0, dt_r)
    a_col = dt_c * a_r[:, d * h8:(d + 1) * h8]
    a_row = dt_r * a_c[d * h8:(d + 1) * h8, :]
    cs_c = _cumsum_cols(tri_c, a_col)
    cs_r = _cumsum_rows(a_row, tri_r)
    last = 0 if reverse else CHUNK - 1
    tot_r = cs_r[:, last:last + 1]
    tot_c = cs_c[last:last + 1, :]

    lane_lo = col < SSD_STATE
    row_lo = row < SSD_STATE
    blockdiag = jnp.logical_not(jnp.logical_xor(lane_lo, row_lo))
    cm_sw = pltpu.roll(cm, SSD_STATE, 1)
    c_dup = (jnp.where(lane_lo, cm, cm_sw), jnp.where(lane_lo, cm_sw, cm))
    g_mat = (_dot_nt(jnp.where(lane_lo, cm, 0.0).astype(BF16), bm.astype(BF16)),
             _dot_nt(jnp.where(lane_lo, 0.0, cm).astype(BF16), bm.astype(BF16)))
    bT = bm.T

    heads_per_group = SSD_HEADS // SSD_GROUPS
    for j in range(SSD_HEADS // 2):
        g = (2 * j) // heads_per_group
        h0, h1 = 2 * j, 2 * j + 1
        xp = xs_ref[0, :, j * CHUNK:(j + 1) * CHUNK]
        dtp = jnp.where(lane_lo, dt_c[:, h0:h0 + 1], dt_c[:, h1:h1 + 1])
        xdt = xp * dtp
        parts = []
        for h in (h0, h1):
            diff_ = cs_c[:, h:h + 1] - cs_r[h:h + 1, :]
            parts.append((g_mat[g] * jnp.exp(jnp.where(keep, diff_, NEG))).astype(BF16))
        e_col = jnp.where(lane_lo, jnp.exp(cs_c[:, h0:h0 + 1]), jnp.exp(cs_c[:, h1:h1 + 1]))
        parts.append((c_dup[g] * e_col).astype(BF16))
        lhs = jnp.concatenate(parts, axis=1)
        s_old = st_ref[j]
        rhs = jnp.concatenate([jnp.where(lane_lo, xdt, 0.0).astype(BF16),
                               jnp.where(lane_lo, 0.0, xdt).astype(BF16),
                               s_old.astype(BF16)], axis=0)
        y_pair = _dot(lhs, rhs)
        btg = bT[g * SSD_STATE:(g + 1) * SSD_STATE, :]
        bd = jnp.concatenate([btg * jnp.exp(tot_r[h0:h0 + 1, :] - cs_r[h0:h0 + 1, :]),
                              btg * jnp.exp(tot_r[h1:h1 + 1, :] - cs_r[h1:h1 + 1, :])], axis=0)
        s_new = jnp.where(blockdiag, _dot(bd.astype(BF16), xdt.astype(BF16)), 0.0)
        carry = jnp.where(row_lo, jnp.exp(tot_c[:, h0:h0 + 1]), jnp.exp(tot_c[:, h1:h1 + 1]))
        st_ref[j] = s_old * carry + s_new
        y_ref[0, :, j * CHUNK:(j + 1) * CHUNK] = y_pair


def _ssd_kernel(xs_f, bc_f, dt_f, dtT_f, xs_r, bc_r, dt_r, dtT_r, bias_r, bias_c, a_r, a_c,
                yf_ref, yr_ref, st_ref, *, n_chunks):
    step = pl.program_id(1)
    n_pairs = SSD_HEADS // 2

    @pl.when(step == 0)
    def _():
        st_ref[...] = jnp.zeros(st_ref.shape, F32)

    _ssd_direction(xs_f, bc_f, dt_f, dtT_f, bias_r, bias_c, a_r, a_c, yf_ref, st_ref.at[0:n_pairs],
                   reverse=False, is_meta=step == 0)
    _ssd_direction(xs_r, bc_r, dt_r, dtT_r, bias_r, bias_c, a_r, a_c, yr_ref,
                   st_ref.at[n_pairs:2 * n_pairs], reverse=True, is_meta=step == n_chunks - 1)


def _chunk_order(n_chunks, reverse):
    if reverse:
        return lambda c: (2 * n_chunks - 2 - c) % n_chunks
    return lambda c: (c + n_chunks - 1) % n_chunks


def _scan_specs(arrays, n_chunks, reverse):
    order = _chunk_order(n_chunks, reverse)
    specs = []
    for a, transposed in arrays:
        if transposed:
            specs.append(pl.BlockSpec((1, a.shape[1], CHUNK), lambda i, s: (i, 0, order(s))))
        else:
            specs.append(pl.BlockSpec((1, CHUNK, a.shape[2]), lambda i, s: (i, order(s), 0)))
    return specs


def _ssd_scan(xs, bc, dt, dtT, consts):
    b, tp, _ = xs.shape
    n_chunks = tp // CHUNK
    arrays = [(xs, False), (bc, False), (dt, False), (dtT, True)]
    in_specs = (_scan_specs(arrays, n_chunks, False) + _scan_specs(arrays, n_chunks, True)
                + [_full(c.shape) for c in consts])
    out_specs = _scan_specs([(xs, False)], n_chunks, False) + _scan_specs([(xs, False)], n_chunks, True)
    out_shape = [jax.ShapeDtypeStruct((b, tp, SSD_INNER), F32)] * 2
    return pl.pallas_call(
        functools.partial(_ssd_kernel, n_chunks=n_chunks),
        grid=(b, n_chunks), in_specs=in_specs, out_specs=out_specs, out_shape=out_shape,
        scratch_shapes=[pltpu.VMEM((SSD_HEADS, CHUNK, CHUNK), F32)],
        compiler_params=_params("parallel", "arbitrary"),
        name="ssd_scan")(xs, bc, dt, dtT, xs, bc, dt, dtT, *consts)


def _mlstm_direction(q_ref, k_ref, v_ref, gt_ref, gtT_ref, gb_r, gb_c, y_ref, st_ref, m_ref, *,
                     reverse, is_meta):
    d = 1 if reverse else 0
    nh = MLSTM_HEADS
    keep, tri_c, tri_r, row, col = _tri_masks(reverse)
    gt = gt_ref[0] + gb_r[...]
    gtT = gtT_ref[0] + gb_c[...]
    i_lo, f_lo = d * nh, 2 * nh + d * nh
    pad_c = jnp.logical_and(is_meta, row[:, :nh] < N_PAD)
    pad_r = jnp.logical_and(is_meta, col[:nh, :] < N_PAD)
    ig_r = jnp.where(pad_r, NEG, gtT[i_lo:i_lo + nh, :])
    fg_c = jnp.where(pad_c, 0.0, _log_sigmoid(gt[:, f_lo:f_lo + nh]))
    fg_r = jnp.where(pad_r, 0.0, _log_sigmoid(gtT[f_lo:f_lo + nh, :]))
    b_c = _cumsum_cols(tri_c, fg_c)
    b_r = _cumsum_rows(fg_r, tri_r)
    last = 0 if reverse else CHUNK - 1

    lane_lo = col < MLSTM_QK
    row_lo = row < MLSTM_QK
    ones = jnp.ones((CHUNK, MLSTM_V), BF16)
    scale = MLSTM_QK ** -0.5
    for h in range(nh):
        pair = h // 2
        lo = (h % 2) == 0
        qp = q_ref[0, :, pair * CHUNK:(pair + 1) * CHUNK]
        kp = k_ref[0, :, pair * CHUNK:(pair + 1) * CHUNK]
        qm = (jnp.where(lane_lo if lo else jnp.logical_not(lane_lo), qp, 0.0) * scale).astype(BF16)
        kT = jnp.where(row_lo if lo else jnp.logical_not(row_lo), kp.T, 0.0)
        v_aug = jnp.concatenate([v_ref[0, :, h * MLSTM_V:(h + 1) * MLSTM_V].astype(BF16), ones], axis=1)
        m_st = m_ref[h][0:1, 0:1]
        bc = b_c[:, h:h + 1]
        br = b_r[h:h + 1, :]
        ir = ig_r[h:h + 1, :]
        dmat = jnp.where(keep, bc - br + ir, -jnp.inf)
        inter = bc + m_st
        m_row = jnp.maximum(jnp.max(dmat, axis=1, keepdims=True), inter)
        w_intra = jnp.exp(dmat - m_row)
        w_inter = jnp.exp(inter - m_row)
        s = _dot_nt(qm, kp.astype(BF16)) * w_intra
        c_st = st_ref[h]
        comb = _dot(s.astype(BF16), v_aug) + w_inter * _dot(qm, c_st.astype(BF16))
        num = comb[:, :MLSTM_V]
        den = jnp.maximum(jnp.abs(comb[:, MLSTM_V:]), jnp.exp(-m_row))
        hout = num / den
        tot = br[:, last:last + 1]
        d_last = tot - br + ir
        m_new = jnp.maximum(tot + m_st, jnp.max(d_last, axis=1, keepdims=True))
        w_s = jnp.exp(d_last - m_new)
        w_prev = jnp.exp(tot + m_st - m_new)
        st_ref[h] = w_prev * c_st + _dot((kT * w_s).astype(BF16), v_aug)
        m_ref[h] = jnp.broadcast_to(m_new, m_ref.shape[1:])
        y_ref[0, :, h * MLSTM_V:(h + 1) * MLSTM_V] = hout


def _mlstm_kernel(q_f, k_f, v_f, gt_f, gtT_f, q_r, k_r, v_r, gt_r, gtT_r, gb_r, gb_c,
                  yf_ref, yr_ref, st_ref, m_ref, *, n_chunks):
    step = pl.program_id(1)
    nh = MLSTM_HEADS

    @pl.when(step == 0)
    def _():
        st_ref[...] = jnp.zeros(st_ref.shape, F32)
        m_ref[...] = jnp.full(m_ref.shape, NEG, F32)

    _mlstm_direction(q_f, k_f, v_f, gt_f, gtT_f, gb_r, gb_c, yf_ref, st_ref.at[0:nh], m_ref.at[0:nh],
                     reverse=False, is_meta=step == 0)
    _mlstm_direction(q_r, k_r, v_r, gt_r, gtT_r, gb_r, gb_c, yr_ref, st_ref.at[nh:2 * nh],
                     m_ref.at[nh:2 * nh], reverse=True, is_meta=step == n_chunks - 1)


def _mlstm_scan(q, k, v, gt, gtT, consts):
    b, tp, _ = q.shape
    n_chunks = tp // CHUNK
    arrays = [(q, False), (k, False), (v, False), (gt, False), (gtT, True)]
    in_specs = (_scan_specs(arrays, n_chunks, False) + _scan_specs(arrays, n_chunks, True)
                + [_full(c.shape) for c in consts])
    out_specs = _scan_specs([(v, False)], n_chunks, False) + _scan_specs([(v, False)], n_chunks, True)
    out_shape = [jax.ShapeDtypeStruct(v.shape, F32)] * 2
    return pl.pallas_call(
        functools.partial(_mlstm_kernel, n_chunks=n_chunks),
        grid=(b, n_chunks), in_specs=in_specs, out_specs=out_specs, out_shape=out_shape,
        scratch_shapes=[pltpu.VMEM((2 * MLSTM_HEADS, CHUNK, 2 * MLSTM_V), F32),
                        pltpu.VMEM((2 * MLSTM_HEADS, 8, LANE), F32)],
        compiler_params=_params("parallel", "arbitrary"),
        name="mlstm_scan")(q, k, v, gt, gtT, q, k, v, gt, gtT, *consts)


def _out_kernel(ya_ref, yf_ref, yr_ref, xs_ref, z_ref, yc_ref, hf_ref, hr_ref, og_ref, h_ref,
                ag_ref, dsk_ref, ng_ref, mg_ref, w_ref, o_ref, *, tp, seq):
    tm = h_ref.shape[0]
    ya = _rms_rows(ya_ref[...], ag_ref[...])
    z = z_ref[...]
    yb = (yf_ref[...] + yr_ref[...] + dsk_ref[...] * xs_ref[...]) * (z * _sigmoid(z))
    parts = [ya.astype(BF16), _rms_rows(yb, ng_ref[...]).astype(BF16), yc_ref[...].astype(BF16)]
    for hd in range(MLSTM_HEADS):
        lo, hi = hd * MLSTM_V, (hd + 1) * MLSTM_V
        hsum = hf_ref[:, lo:hi] + hr_ref[:, lo:hi]
        parts.append((_sigmoid(og_ref[:, lo:hi]) * _rms_rows(hsum, mg_ref[...])).astype(BF16))
    out = h_ref[...] + _dot(jnp.concatenate(parts, axis=1), w_ref[...])
    t = (pl.program_id(0) * tm + lax.broadcasted_iota(jnp.int32, (tm, 1), 0)) % tp
    is_pad = jnp.logical_and(t >= seq, t < seq + N_PAD)
    o_ref[...] = jnp.where(is_pad, 0.0, out)


def _out_proj(mixer_outs, h, consts, w, *, tp, seq, tm):
    rows, d = h.shape
    row_spec = lambda c: pl.BlockSpec((tm, c), lambda i: (i, 0))
    return pl.pallas_call(
        functools.partial(_out_kernel, tp=tp, seq=seq), grid=(rows // tm,),
        in_specs=([row_spec(a.shape[1]) for a in mixer_outs] + [row_spec(d)]
                  + [_full(c.shape) for c in consts] + [_full(w.shape)]),
        out_specs=row_spec(d), out_shape=jax.ShapeDtypeStruct((rows, d), F32),
        compiler_params=_params("parallel"), name="out_proj")(*mixer_outs, h, *consts, w)


def _ffn_kernel(h_ref, g_ref, wg_ref, wu_ref, wo_ref, o_ref, *, n_split):
    x = h_ref[0]
    hn = _rms_rows(x, g_ref[...]).astype(BF16)
    f = wg_ref.shape[1]
    tf = f // n_split
    acc = x
    for c in range(n_split):
        gate = _dot(hn, wg_ref[:, c * tf:(c + 1) * tf])
        up = _dot(hn, wu_ref[:, c * tf:(c + 1) * tf])
        act = (gate * _sigmoid(gate) * up).astype(BF16)
        acc = acc + _dot(act, wo_ref[c * tf:(c + 1) * tf, :])
    o_ref[0] = acc


def _ffn(h, g, wg, wu, wo, *, t_out, tm):
    b, _, d = h.shape
    f = wg.shape[1]
    n_split = 2 if (f // 2) % LANE == 0 else 1
    row_spec = pl.BlockSpec((1, tm, d), lambda i, j: (i, j, 0))
    resident = lambda a: pl.BlockSpec(a.shape, lambda i, j: (0, 0), pipeline_mode=pl.Buffered(1))
    return pl.pallas_call(
        functools.partial(_ffn_kernel, n_split=n_split), grid=(b, t_out // tm),
        in_specs=[row_spec, _full(g.shape), resident(wg), resident(wu), resident(wo)],
        out_specs=row_spec, out_shape=jax.ShapeDtypeStruct((b, t_out, d), F32),
        compiler_params=_params("parallel", "parallel"), name="ffn")(h, g, wg, wu, wo)


def _rope_tables_t(pos, rot_dim):
    inv = 1.0 / (ROPE_THETA ** (jnp.arange(0, rot_dim, 2, dtype=F32) / rot_dim))
    ang = pos[:, None] * inv[None, :]
    return jnp.cos(ang).T, jnp.sin(ang).T


def _col(v, n=None):
    v = v.astype(F32)
    if n is not None:
        v = jnp.pad(v, (0, n - v.shape[0]))
    return v[:, None]


def _row(v):
    return v.astype(F32)[None, :]


def kernel(x, meta_tokens, attn_norm_g, w_in, mla_q_norm_g, mla_kv_norm_g, mla_w_uq, mla_w_ukv, mla_q_head_g, mla_k_head_g, mla_out_g, ssd_conv_w, ssd_conv_b, ssd_dt_bias, ssd_a_log, ssd_d, ssd_norm_g, diff_q_head_g, diff_k_head_g, diff_lambda, diff_out_g, mlstm_i_bias, mlstm_f_bias, mlstm_out_g, w_out, ffn_norm_g, w_ffn_in, w_ffn_out):
    b, seq, d = x.shape
    depth = w_in.shape[0]
    tp = seq + CHUNK
    assert seq % CHUNK == 0
    tm_in = _pick_tile(tp, (640, 384, 128))
    tq = _pick_tile(tp, (1664, 640, 384, 128))
    tk = _pick_tile(seq, (512, 384, 256, 128))
    tm_flat = _pick_tile(b * tp, (512, 256, 128))
    tm_out = _pick_tile(seq, (512, 256, 128))

    meta = jnp.broadcast_to(meta_tokens[None].astype(x.dtype), (b, N_META, d))
    h = jnp.concatenate([x, jnp.zeros((b, N_PAD, d), x.dtype), meta], axis=1).reshape(b * tp, d)
    pos = jnp.concatenate([N_META + jnp.arange(seq, dtype=F32), jnp.zeros((N_PAD,), F32),
                           jnp.arange(N_META, dtype=F32)])
    cos_m, sin_m = _rope_tables_t(pos, MLA_ROPE)
    cos_d, sin_d = _rope_tables_t(pos, DIFF_ROPE)

    sizes = (MLA_Q_RANK, MLA_KV_RANK, MLA_ROPE, SSD_INNER, SSD_CONV_CH, 2 * SSD_HEADS,
             2 * DIFF_HEADS * DIFF_QK, 2 * DIFF_HEADS * DIFF_QK, DIFF_HEADS * DIFF_V,
             MLSTM_HEADS * MLSTM_QK, MLSTM_HEADS * MLSTM_QK, MLSTM_HEADS * MLSTM_V,
             MLSTM_HEADS * MLSTM_V, 2 * MLSTM_HEADS, 2 * MLSTM_HEADS)
    offs = [0]
    for s_ in sizes:
        offs.append(offs[-1] + s_)

    def cols(w, first, last):
        return w[:, offs[first]:offs[last + 1]]

    def pad_cols(w, n):
        return jnp.pad(w, ((0, 0), (0, n - w.shape[1])))

    for l in range(depth):
        lambda_init = 0.8 - 0.6 * math.exp(-0.3 * l)
        wl = w_in[l]
        h3 = h.reshape(b, tp, d)
        g_attn = _row(attn_norm_g[l])

        w_a = cols(wl, 0, 1).astype(BF16)
        w_kr = cols(wl, 2, 2).T.astype(BF16)
        w_uq = mla_w_uq[l].T.reshape(MLA_HEADS, MLA_QK, MLA_Q_RANK)
        w_uq = jnp.pad(w_uq, ((0, 0), (0, HEAD_PAD - MLA_QK), (0, 0))).reshape(MLA_HEADS * HEAD_PAD, MLA_Q_RANK)
        w_ukv = mla_w_ukv[l].T.reshape(MLA_HEADS, MLA_NOPE + MLA_V, MLA_KV_RANK)
        w_uk = w_ukv[:, :MLA_NOPE].reshape(MLA_HEADS * MLA_NOPE, MLA_KV_RANK)
        w_uv = w_ukv[:, MLA_NOPE:].reshape(MLA_HEADS * MLA_V, MLA_KV_RANK)
        qT_a, k_a, vT_a = _in_proj_call(
            _mla_in_kernel, h3,
            [g_attn, w_a, _row(mla_q_norm_g[l]), _row(mla_kv_norm_g[l]), w_uq.astype(BF16),
             w_uk.astype(BF16), w_uv.astype(BF16), w_kr, _col(mla_q_head_g[l], HEAD_PAD),
             _col(mla_k_head_g[l], HEAD_PAD), (cos_m,), (sin_m,)],
            [((MLA_HEADS * HEAD_PAD, tp), BF16, "col"), ((MLA_HEADS, tp, HEAD_PAD), BF16, "head"),
             ((MLA_HEADS * MLA_V, tp), BF16, "col")], tm_in, "mla_in")
        y_a = _attention(qT_a, k_a, vT_a, [], diff=False, seq=seq, tq=tq, tk=tk, out_scale=1.0,
                         name="mla_attn")

        w_c = cols(wl, 6, 8).T.astype(BF16)
        qT_c, k_c, vT_c = _in_proj_call(
            _diff_in_kernel, h3,
            [g_attn, w_c, _col(diff_q_head_g[l], HEAD_PAD), _col(diff_k_head_g[l], HEAD_PAD),
             (cos_d,), (sin_d,)],
            [((2 * DIFF_HEADS * HEAD_PAD, tp), BF16, "col"), ((2 * DIFF_HEADS, tp, HEAD_PAD), BF16, "head"),
             ((DIFF_HEADS * DIFF_V, tp), BF16, "col")], tm_in, "diff_in")
        lam = diff_lambda[l].astype(F32)
        lam_full = jnp.exp(jnp.sum(lam[0] * lam[1])) - jnp.exp(jnp.sum(lam[2] * lam[3])) + lambda_init
        y_c = _attention(qT_c, k_c, vT_c, [jnp.full((8, LANE), lam_full, F32), _col(diff_out_g[l])],
                         diff=True, seq=seq, tq=tq, tk=tk, out_scale=1.0 - lambda_init, name="diff_attn")

        n_b = SSD_INNER + SSD_CONV_CH + 2 * SSD_HEADS
        w_b = pad_cols(cols(wl, 3, 5), -(-n_b // LANE) * LANE).astype(BF16)
        w_dt = cols(wl, 5, 5).T.astype(BF16)
        z_b, xbc, dt, dtT = _in_proj_call(
            _ssd_in_kernel, h3, [g_attn, w_b, w_dt],
            [((tp, SSD_INNER), F32, "row"), ((tp, SSD_CONV_CH), F32, "row"),
             ((tp, 2 * SSD_HEADS), F32, "row"), ((2 * SSD_HEADS, tp), F32, "col")], tm_in, "ssd_in")
        w8 = jnp.pad(ssd_conv_w[l].astype(F32), ((0, 8 - SSD_CONV), (0, 0)))
        xs_b, bc_b = _ssd_conv(xbc, w8, _row(ssd_conv_b[l]), tm_in)
        dt_bias = ssd_dt_bias[l].astype(F32).reshape(-1)
        a_neg = -jnp.exp(ssd_a_log[l].astype(F32)).reshape(-1)
        y_bf, y_br = _ssd_scan(xs_b, bc_b, dt, dtT, [_row(dt_bias), _col(dt_bias), _row(a_neg), _col(a_neg)])

        n_d = 2 * MLSTM_HEADS * MLSTM_QK + 2 * MLSTM_HEADS * MLSTM_V + 4 * MLSTM_HEADS
        w_d = pad_cols(cols(wl, 9, 14), -(-n_d // LANE) * LANE).astype(BF16)
        w_g = cols(wl, 13, 14).T.astype(BF16)
        q_d, k_d, v_d, o_d, gt, gtT = _in_proj_call(
            _mlstm_in_kernel, h3, [g_attn, w_d, w_g],
            [((tp, MLSTM_HEADS * MLSTM_QK), F32, "row"), ((tp, MLSTM_HEADS * MLSTM_QK), F32, "row"),
             ((tp, MLSTM_HEADS * MLSTM_V), F32, "row"), ((tp, MLSTM_HEADS * MLSTM_V), F32, "row"),
             ((tp, 4 * MLSTM_HEADS), F32, "row"), ((4 * MLSTM_HEADS, tp), F32, "col")], tm_in, "mlstm_in")
        gate_bias = jnp.concatenate([mlstm_i_bias[l].reshape(-1), mlstm_f_bias[l].reshape(-1)]).astype(F32)
        h_df, h_dr = _mlstm_scan(q_d, k_d, v_d, gt, gtT, [_row(gate_bias), _col(gate_bias)])

        flat = lambda a: a.reshape(b * tp, a.shape[2])
        mixer_outs = [flat(a) for a in (y_a, y_bf, y_br, xs_b, z_b, y_c, h_df, h_dr, o_d)]
        out_consts = [_row(mla_out_g[l]), _row(jnp.repeat(ssd_d[l], SSD_HEAD_DIM)), _row(ssd_norm_g[l]),
                      _row(mlstm_out_g[l])]
        h = _out_proj(mixer_outs, h, out_consts, w_out[l].astype(BF16), tp=tp, seq=seq, tm=tm_flat)
        f = w_ffn_out.shape[1]
        last = l == depth - 1
        h = _ffn(h.reshape(b, tp, d), _row(ffn_norm_g[l]), w_ffn_in[l][:, :f].astype(BF16),
                 w_ffn_in[l][:, f:].astype(BF16), w_ffn_out[l].astype(BF16),
                 t_out=seq if last else tp, tm=tm_out if last else tm_in)
        h = h.reshape(-1, d)

    return h.reshape(b, seq, d).astype(x.dtype)
```

```python
import functools
import math

import jax
import jax.numpy as jnp
from jax import lax
from jax.experimental import pallas as pl
from jax.experimental.pallas import tpu as pltpu

N_META = 16
ROPE_THETA = 500000.0
EPS = 1e-6
CHUNK = 128
N_PAD = CHUNK - N_META
NEG = -1e30
LOG2E = 1.4426950408889634

MLA_HEADS, MLA_NOPE, MLA_ROPE, MLA_V = 8, 64, 32, 64
MLA_QK = MLA_NOPE + MLA_ROPE
MLA_Q_RANK, MLA_KV_RANK = 384, 256
SSD_HEADS, SSD_HEAD_DIM, SSD_GROUPS, SSD_STATE, SSD_CONV = 8, 64, 2, 64, 5
SSD_INNER = SSD_HEADS * SSD_HEAD_DIM
SSD_CONV_CH = SSD_INNER + 2 * SSD_GROUPS * SSD_STATE
DIFF_HEADS, DIFF_QK = 4, 64
DIFF_V = 2 * DIFF_QK
DIFF_ROPE = DIFF_QK // 4
MLSTM_HEADS, MLSTM_QK, MLSTM_V = 4, 64, 128
HEAD_PAD = 128
LANE = 128
Q_STRIP = 256
VMEM_LIMIT = 52 * 1024 * 1024

F32 = jnp.float32
BF16 = jnp.bfloat16
EXP_DTYPE = jnp.bfloat16


def _dot(a, b):
    return jnp.dot(a, b, preferred_element_type=F32)


def _dot_nt(a, b):
    return lax.dot_general(a, b, (((1,), (1,)), ((), ())), preferred_element_type=F32)


def _rms_rows(x, g):
    ms = jnp.mean(x * x, axis=-1, keepdims=True)
    return x * lax.rsqrt(ms + EPS) * g


def _split3(a):
    hi = a.astype(BF16)
    r1 = a - hi.astype(F32)
    mid = r1.astype(BF16)
    lo = (r1 - mid.astype(F32)).astype(BF16)
    return hi, mid, lo


def _cumsum_cols(tri, a):
    hi, mid, lo = _split3(a)
    return _dot(tri, hi) + _dot(tri, mid) + _dot(tri, lo)


def _cumsum_rows(a, tri):
    hi, mid, lo = _split3(a)
    return _dot(hi, tri) + _dot(mid, tri) + _dot(lo, tri)


def _softplus(x):
    return jnp.maximum(x, 0.0) + jnp.log(1.0 + jnp.exp(-jnp.abs(x)))


def _log_sigmoid(x):
    return jnp.minimum(x, 0.0) - jnp.log(1.0 + jnp.exp(-jnp.abs(x)))


def _sigmoid(x):
    return 1.0 / (1.0 + jnp.exp(-x))


def _pick_tile(n, candidates):
    for c in candidates:
        if n % c == 0:
            return c
    raise ValueError(f"no tile in {candidates} divides {n}")


def _params(*sem):
    return pltpu.CompilerParams(dimension_semantics=sem, vmem_limit_bytes=VMEM_LIMIT)


def _norm_rope_t(blk, g_col, cos, sin, n_real):
    r = cos.shape[0]
    ms = jnp.sum(blk * blk, axis=0, keepdims=True) * (1.0 / n_real)
    y = blk * lax.rsqrt(ms + EPS) * g_col
    x1, x2, rest = y[:r], y[r:2 * r], y[2 * r:]
    return jnp.concatenate([x1 * cos - x2 * sin, x2 * cos + x1 * sin, rest], axis=0)


def _mla_in_kernel(h_ref, g_ref, wa_ref, gq_ref, gkv_ref, wuq_ref, wuk_ref, wuv_ref, wkr_ref,
                   qhg_ref, khg_ref, cos_ref, sin_ref, qT_ref, k_ref, vT_ref):
    hn = _rms_rows(h_ref[0], g_ref[...]).astype(BF16)
    acc = _dot(hn, wa_ref[...])
    cqn = _rms_rows(acc[:, :MLA_Q_RANK], gq_ref[...]).astype(BF16)
    ckvn = _rms_rows(acc[:, MLA_Q_RANK:], gkv_ref[...]).astype(BF16)
    qT = _dot_nt(wuq_ref[...], cqn)
    knT = _dot_nt(wuk_ref[...], ckvn)
    vT = _dot_nt(wuv_ref[...], ckvn)
    krT = _dot_nt(wkr_ref[...], hn)
    cos, sin = cos_ref[...], sin_ref[...]
    t = krT.shape[1]
    zpad = jnp.zeros((HEAD_PAD - MLA_QK, t), F32)
    q_scale = (MLA_QK ** -0.5) * LOG2E
    for h in range(MLA_HEADS):
        qb = _norm_rope_t(qT[h * HEAD_PAD:(h + 1) * HEAD_PAD], qhg_ref[...], cos, sin, MLA_QK)
        qT_ref[0, h * HEAD_PAD:(h + 1) * HEAD_PAD, :] = (qb * q_scale).astype(BF16)
        kb = jnp.concatenate([krT, knT[h * MLA_NOPE:(h + 1) * MLA_NOPE], zpad], axis=0)
        kb = _norm_rope_t(kb, khg_ref[...], cos, sin, MLA_QK)
        k_ref[0, h] = kb.T.astype(BF16)
    vT_ref[0] = vT.astype(BF16)


def _diff_in_kernel(h_ref, g_ref, wc_ref, qhg_ref, khg_ref, cos_ref, sin_ref, qT_ref, k_ref, vT_ref):
    hn = _rms_rows(h_ref[0], g_ref[...]).astype(BF16)
    pT = _dot_nt(wc_ref[...], hn)
    cos, sin = cos_ref[...], sin_ref[...]
    t = pT.shape[1]
    zpad = jnp.zeros((HEAD_PAD - DIFF_QK, t), F32)
    nq = 2 * DIFF_HEADS * DIFF_QK
    q_scale = (DIFF_QK ** -0.5) * LOG2E
    for h in range(2 * DIFF_HEADS):
        qb = jnp.concatenate([pT[h * DIFF_QK:(h + 1) * DIFF_QK], zpad], axis=0)
        qb = _norm_rope_t(qb, qhg_ref[...], cos, sin, DIFF_QK)
        qT_ref[0, h * HEAD_PAD:(h + 1) * HEAD_PAD, :] = (qb * q_scale).astype(BF16)
        kb = jnp.concatenate([pT[nq + h * DIFF_QK:nq + (h + 1) * DIFF_QK], zpad], axis=0)
        kb = _norm_rope_t(kb, khg_ref[...], cos, sin, DIFF_QK)
        k_ref[0, h] = kb.T.astype(BF16)
    vT_ref[0] = pT[2 * nq:].astype(BF16)


def _ssd_in_kernel(h_ref, g_ref, wb_ref, wdt_ref, z_ref, xbc_ref, dt_ref, dtT_ref):
    hn = _rms_rows(h_ref[0], g_ref[...]).astype(BF16)
    acc = _dot(hn, wb_ref[...])
    z_ref[0] = acc[:, :SSD_INNER]
    xbc_ref[0] = acc[:, SSD_INNER:SSD_INNER + SSD_CONV_CH]
    dt_ref[0] = acc[:, SSD_INNER + SSD_CONV_CH:SSD_INNER + SSD_CONV_CH + 2 * SSD_HEADS]
    dtT_ref[0] = _dot_nt(wdt_ref[...], hn)


def _mlstm_in_kernel(h_ref, g_ref, wd_ref, wg_ref, q_ref, k_ref, v_ref, o_ref, gt_ref, gtT_ref):
    hn = _rms_rows(h_ref[0], g_ref[...]).astype(BF16)
    acc = _dot(hn, wd_ref[...])
    nqk = MLSTM_HEADS * MLSTM_QK
    nv = MLSTM_HEADS * MLSTM_V
    q_ref[0] = acc[:, :nqk]
    k_ref[0] = acc[:, nqk:2 * nqk]
    v_ref[0] = acc[:, 2 * nqk:2 * nqk + nv]
    o_ref[0] = acc[:, 2 * nqk + nv:2 * nqk + 2 * nv]
    gt_ref[0] = acc[:, 2 * nqk + 2 * nv:2 * nqk + 2 * nv + 4 * MLSTM_HEADS]
    gtT_ref[0] = _dot_nt(wg_ref[...], hn)


def _full(shape):
    nd = len(shape)
    return pl.BlockSpec(shape, lambda *_: (0,) * nd)


def _in_proj_call(body, h, consts, outs, tm, name):
    b, tp, d = h.shape
    in_specs = [pl.BlockSpec((1, tm, d), lambda i, j: (i, j, 0))]
    for c in consts:
        if isinstance(c, tuple):
            in_specs.append(pl.BlockSpec((c[0].shape[0], tm), lambda i, j: (0, j)))
        else:
            in_specs.append(_full(c.shape))
    out_shapes, out_specs = [], []
    for shape, dtype, kind in outs:
        out_shapes.append(jax.ShapeDtypeStruct((b,) + shape, dtype))
        if kind == "row":
            out_specs.append(pl.BlockSpec((1, tm, shape[1]), lambda i, j: (i, j, 0)))
        elif kind == "col":
            out_specs.append(pl.BlockSpec((1, shape[0], tm), lambda i, j: (i, 0, j)))
        else:
            out_specs.append(pl.BlockSpec((1, shape[0], tm, shape[2]), lambda i, j: (i, 0, j, 0)))
    args = [h] + [c[0] if isinstance(c, tuple) else c for c in consts]
    return pl.pallas_call(
        body, grid=(b, tp // tm), in_specs=in_specs, out_specs=out_specs, out_shape=out_shapes,
        compiler_params=_params("parallel", "parallel"), name=name)(*args)


def _attn_kernel(*refs, diff, seq, tk, out_scale):
    if diff:
        qT_ref, k_ref, vT_ref, lam_ref, og_ref, o_ref, s_scr, st_scr, m_scr, acc_scr = refs
    else:
        qT_ref, k_ref, vT_ref, o_ref, s_scr, st_scr, m_scr, acc_scr = refs
    dv = DIFF_V if diff else MLA_V
    tq = qT_ref.shape[2]
    n_chunks = seq // tk
    assert n_chunks % 2 == 0
    outs = []
    for hh in range(2):
        v_lo = 0 if diff else hh * dv
        m_scr[...] = jnp.full(m_scr.shape, NEG, F32)
        acc_scr[...] = jnp.zeros(acc_scr.shape, F32)

        def stage(nxt, cur, hh=hh, v_lo=v_lo):
            if nxt is not None:
                k_n = k_ref[0, hh, pl.ds(nxt[1], nxt[2]), :]
            if cur is not None:
                ones = jnp.ones((16, cur[2]), BF16)
                v_aug = jnp.concatenate([vT_ref[0, v_lo:v_lo + dv, pl.ds(cur[1], cur[2])], ones], axis=0)
            pending = None
            for j0 in range(0, tq, Q_STRIP):
                w = min(Q_STRIP, tq - j0)
                if cur is not None:
                    s = cur[0][:, j0:j0 + w]
                    m_old = m_scr[0:1, j0:j0 + w]
                    m_new = jnp.maximum(m_old, jnp.max(s, axis=0, keepdims=True))
                    alpha = jnp.exp2(m_old - m_new)
                    p = jnp.exp2((s - m_new).astype(EXP_DTYPE)).astype(BF16)
                    m_scr[0:1, j0:j0 + w] = m_new
                if nxt is not None:
                    s_n = _dot(k_n, qT_ref[0, hh * HEAD_PAD:(hh + 1) * HEAD_PAD, j0:j0 + w])
                    if nxt[3]:
                        row = lax.broadcasted_iota(jnp.int32, s_n.shape, 0)
                        s_n = jnp.where(row >= N_PAD, s_n, NEG)
                    nxt[0][:, j0:j0 + w] = s_n
                if cur is not None:
                    if pending is not None:
                        pj, pw, pa, pp = pending
                        acc_scr[:, pj:pj + pw] = pa * acc_scr[:, pj:pj + pw] + _dot(v_aug, pp)
                    pending = (j0, w, alpha, p)
            if pending is not None:
                pj, pw, pa, pp = pending
                acc_scr[:, pj:pj + pw] = pa * acc_scr[:, pj:pj + pw] + _dot(v_aug, pp)

        buf0, buf1 = s_scr.at[0], s_scr.at[1]
        stage((buf0, 0, tk, False), None)

        def body(i, carry, stage=stage):
            off = pl.multiple_of(2 * i * tk, 2 * tk)
            stage((buf1, off + tk, tk, False), (buf0, off, tk))
            stage((buf0, off + 2 * tk, tk, False), (buf1, off + tk, tk))
            return carry

        lax.fori_loop(0, n_chunks // 2 - 1, body, 0)
        off = (n_chunks - 2) * tk
        stage((buf1, off + tk, tk, False), (buf0, off, tk))
        stage((st_scr, seq, CHUNK, True), (buf1, off + tk, tk))
        stage(None, (st_scr, seq, CHUNK))
        acc = acc_scr[...]
        outs.append(acc[:dv] / acc[dv:dv + 1])
    if diff:
        o = outs[0] - lam_ref[0:1, 0:1] * outs[1]
        ms = jnp.mean(o * o, axis=0, keepdims=True)
        o = o * lax.rsqrt(ms + EPS) * (og_ref[...] * out_scale)
    else:
        o = jnp.concatenate(outs, axis=0)
    o_ref[0] = o.T.astype(o_ref.dtype)


def _attention(qT, k, vT, extra, *, diff, seq, tq, tk, out_scale, name):
    b, _, tp = qT.shape
    groups = qT.shape[1] // (2 * HEAD_PAD)
    dv = DIFF_V if diff else MLA_V
    v_rows = dv if diff else 2 * dv
    in_specs = [
        pl.BlockSpec((1, 2 * HEAD_PAD, tq), lambda i, g, j: (i, g, j)),
        pl.BlockSpec((1, 2, tp, HEAD_PAD), lambda i, g, j: (i, g, 0, 0)),
        pl.BlockSpec((1, v_rows, tp), lambda i, g, j: (i, g, 0)),
    ] + [_full(e.shape) for e in extra]
    return pl.pallas_call(
        functools.partial(_attn_kernel, diff=diff, seq=seq, tk=tk, out_scale=out_scale),
        grid=(b, groups, tp // tq),
        in_specs=in_specs,
        out_specs=pl.BlockSpec((1, tq, LANE), lambda i, g, j: (i, j, g)),
        out_shape=jax.ShapeDtypeStruct((b, tp, groups * LANE), BF16),
        scratch_shapes=[pltpu.VMEM((2, tk, tq), F32), pltpu.VMEM((CHUNK, tq), F32),
                        pltpu.VMEM((8, tq), F32), pltpu.VMEM((dv + 16, tq), F32)],
        compiler_params=_params("parallel", "parallel", "arbitrary"), name=name)(qT, k, vT, *extra)


def _conv_kernel(x_ref, prev_ref, next_ref, w_ref, b_ref, xs_ref, bc_ref, scr):
    tc = x_ref.shape[1]
    scr[0:8, :] = prev_ref[0]
    scr[8:8 + tc, :] = x_ref[0]
    scr[8 + tc:16 + tc, :] = next_ref[0]
    acc = jnp.broadcast_to(b_ref[...], (tc, b_ref.shape[1]))
    for j in range(SSD_CONV):
        acc = acc + w_ref[j:j + 1, :] * scr[8 - SSD_CONV // 2 + j:8 - SSD_CONV // 2 + j + tc, :]
    act = acc * _sigmoid(acc)
    xs_ref[0] = act[:, :SSD_INNER]
    bc_ref[0] = act[:, SSD_INNER:]


def _ssd_conv(xbc, w8, bias, tc):
    b, tp, c = xbc.shape
    nb8 = tp // 8
    r8 = tc // 8
    row_spec = lambda n: pl.BlockSpec((1, tc, n), lambda i, j: (i, j, 0))
    return pl.pallas_call(
        _conv_kernel, grid=(b, tp // tc),
        in_specs=[row_spec(c),
                  pl.BlockSpec((1, 8, c), lambda i, j: (i, (j * r8 + nb8 - 1) % nb8, 0)),
                  pl.BlockSpec((1, 8, c), lambda i, j: (i, ((j + 1) * r8) % nb8, 0)),
                  _full(w8.shape), _full(bias.shape)],
        out_specs=[row_spec(SSD_INNER), row_spec(c - SSD_INNER)],
        out_shape=[jax.ShapeDtypeStruct((b, tp, SSD_INNER), F32),
                   jax.ShapeDtypeStruct((b, tp, c - SSD_INNER), F32)],
        scratch_shapes=[pltpu.VMEM((tc + 16, c), F32)],
        compiler_params=_params("parallel", "parallel"), name="ssd_conv")(xbc, xbc, xbc, w8, bias)


def _tri_masks(reverse):
    row = lax.broadcasted_iota(jnp.int32, (CHUNK, CHUNK), 0)
    col = lax.broadcasted_iota(jnp.int32, (CHUNK, CHUNK), 1)
    keep = (col >= row) if reverse else (col <= row)
    tri_c = keep.astype(BF16)
    tri_r = ((row >= col) if reverse else (row <= col)).astype(BF16)
    return keep, tri_c, tri_r, row, col


def _ssd_direction(xs_ref, bc_ref, dt_ref, dtT_ref, bias_r, bias_c, a_r, a_c, y_ref, st_ref, *,
                   reverse, is_meta):
    d = 1 if reverse else 0
    h8 = SSD_HEADS
    keep, tri_c, tri_r, row, col = _tri_masks(reverse)
    bm = bc_ref[0, :, :CHUNK]
    cm = bc_ref[0, :, CHUNK:]

    dt_c = _softplus(dt_ref[0][:, d * h8:(d + 1) * h8] + bias_r[:, d * h8:(d + 1) * h8])
    dt_r = _softplus(dtT_ref[0][d * h8:(d + 1) * h8, :] + bias_c[d * h8:(d + 1) * h8, :])
    pad_c = jnp.logical_and(is_meta, row[:, :h8] < N_PAD)
    pad_r = jnp.logical_and(is_meta, col[:h8, :] < N_PAD)
    dt_c = jnp.where(pad_c, 0.0, dt_c)
    dt_r = jnp.where(pad_r, 0.0, dt_r)
    a_col = dt_c * a_r[:, d * h8:(d + 1) * h8]
    a_row = dt_r * a_c[d * h8:(d + 1) * h8, :]
    cs_c = _cumsum_cols(tri_c, a_col)
    cs_r = _cumsum_rows(a_row, tri_r)
    last = 0 if reverse else CHUNK - 1
    tot_r = cs_r[:, last:last + 1]
    tot_c = cs_c[last:last + 1, :]

    lane_lo = col < SSD_STATE
    row_lo = row < SSD_STATE
    blockdiag = jnp.logical_not(jnp.logical_xor(lane_lo, row_lo))
    cm_sw = pltpu.roll(cm, SSD_STATE, 1)
    c_dup = (jnp.where(lane_lo, cm, cm_sw), jnp.where(lane_lo, cm_sw, cm))
    g_mat = (_dot_nt(jnp.where(lane_lo, cm, 0.0).astype(BF16), bm.astype(BF16)),
             _dot_nt(jnp.where(lane_lo, 0.0, cm).astype(BF16), bm.astype(BF16)))
    bT = bm.T

    heads_per_group = SSD_HEADS // SSD_GROUPS

    def pair_chain(j):
        g = (2 * j) // heads_per_group
        h0, h1 = 2 * j, 2 * j + 1
        xp = xs_ref[0, :, j * CHUNK:(j + 1) * CHUNK]
        dtp = jnp.where(lane_lo, dt_c[:, h0:h0 + 1], dt_c[:, h1:h1 + 1])
        xdt = xp * dtp
        e_col = jnp.where(lane_lo, jnp.exp(cs_c[:, h0:h0 + 1]), jnp.exp(cs_c[:, h1:h1 + 1]))
        yield
        parts = []
        for h in (h0, h1):
            diff_ = cs_c[:, h:h + 1] - cs_r[h:h + 1, :]
            parts.append((g_mat[g] * jnp.exp(jnp.where(keep, diff_, NEG))).astype(BF16))
        parts.append((c_dup[g] * e_col).astype(BF16))
        yield
        lhs = jnp.concatenate(parts, axis=1)
        s_old = st_ref[j]
        rhs = jnp.concatenate([jnp.where(lane_lo, xdt, 0.0).astype(BF16),
                               jnp.where(lane_lo, 0.0, xdt).astype(BF16),
                               s_old.astype(BF16)], axis=0)
        y_pair = _dot(lhs, rhs)
        btg = bT[g * SSD_STATE:(g + 1) * SSD_STATE, :]
        bd = jnp.concatenate([btg * jnp.exp(tot_r[h0:h0 + 1, :] - cs_r[h0:h0 + 1, :]),
                              btg * jnp.exp(tot_r[h1:h1 + 1, :] - cs_r[h1:h1 + 1, :])], axis=0)
        s_upd = _dot(bd.astype(BF16), xdt.astype(BF16))
        yield
        carry = jnp.where(row_lo, jnp.exp(tot_c[:, h0:h0 + 1]), jnp.exp(tot_c[:, h1:h1 + 1]))
        st_ref[j] = s_old * carry + jnp.where(blockdiag, s_upd, 0.0)
        y_ref[0, :, j * CHUNK:(j + 1) * CHUNK] = y_pair

    return [pair_chain(j) for j in range(SSD_HEADS // 2)]


def _interleave(chains):
    chains = list(chains)
    while chains:
        alive = []
        for c in chains:
            try:
                next(c)
                alive.append(c)
            except StopIteration:
                pass
        chains = alive


def _ssd_kernel(xs_f, bc_f, dt_f, dtT_f, xs_r, bc_r, dt_r, dtT_r, bias_r, bias_c, a_r, a_c,
                yf_ref, yr_ref, st_ref, *, n_chunks):
    step = pl.program_id(1)
    n_pairs = SSD_HEADS // 2

    @pl.when(step == 0)
    def _():
        st_ref[...] = jnp.zeros(st_ref.shape, F32)

    fwd = _ssd_direction(xs_f, bc_f, dt_f, dtT_f, bias_r, bias_c, a_r, a_c, yf_ref, st_ref.at[0:n_pairs],
                         reverse=False, is_meta=step == 0)
    rev = _ssd_direction(xs_r, bc_r, dt_r, dtT_r, bias_r, bias_c, a_r, a_c, yr_ref,
                         st_ref.at[n_pairs:2 * n_pairs], reverse=True, is_meta=step == n_chunks - 1)
    _interleave([c for fr in zip(fwd, rev) for c in fr])


def _chunk_order(n_chunks, reverse):
    if reverse:
        return lambda c: (2 * n_chunks - 2 - c) % n_chunks
    return lambda c: (c + n_chunks - 1) % n_chunks


def _scan_specs(arrays, n_chunks, reverse):
    order = _chunk_order(n_chunks, reverse)
    specs = []
    for a, transposed in arrays:
        if transposed:
            specs.append(pl.BlockSpec((1, a.shape[1], CHUNK), lambda i, s: (i, 0, order(s))))
        else:
            specs.append(pl.BlockSpec((1, CHUNK, a.shape[2]), lambda i, s: (i, order(s), 0)))
    return specs


def _ssd_scan(xs, bc, dt, dtT, consts):
    b, tp, _ = xs.shape
    n_chunks = tp // CHUNK
    arrays = [(xs, False), (bc, False), (dt, False), (dtT, True)]
    in_specs = (_scan_specs(arrays, n_chunks, False) + _scan_specs(arrays, n_chunks, True)
                + [_full(c.shape) for c in consts])
    out_specs = _scan_specs([(xs, False)], n_chunks, False) + _scan_specs([(xs, False)], n_chunks, True)
    out_shape = [jax.ShapeDtypeStruct((b, tp, SSD_INNER), F32)] * 2
    return pl.pallas_call(
        functools.partial(_ssd_kernel, n_chunks=n_chunks),
        grid=(b, n_chunks), in_specs=in_specs, out_specs=out_specs, out_shape=out_shape,
        scratch_shapes=[pltpu.VMEM((SSD_HEADS, CHUNK, CHUNK), F32)],
        compiler_params=_params("parallel", "arbitrary"),
        name="ssd_scan")(xs, bc, dt, dtT, xs, bc, dt, dtT, *consts)


def _mlstm_direction(q_ref, k_ref, v_ref, gt_ref, gtT_ref, gb_r, gb_c, y_ref, st_ref, m_ref, *,
                     reverse, is_meta):
    d = 1 if reverse else 0
    nh = MLSTM_HEADS
    keep, tri_c, tri_r, row, col = _tri_masks(reverse)
    gt = gt_ref[0] + gb_r[...]
    gtT = gtT_ref[0] + gb_c[...]
    i_lo, f_lo = d * nh, 2 * nh + d * nh
    pad_c = jnp.logical_and(is_meta, row[:, :nh] < N_PAD)
    pad_r = jnp.logical_and(is_meta, col[:nh, :] < N_PAD)
    ig_r = jnp.where(pad_r, NEG, gtT[i_lo:i_lo + nh, :])
    fg_c = jnp.where(pad_c, 0.0, _log_sigmoid(gt[:, f_lo:f_lo + nh]))
    fg_r = jnp.where(pad_r, 0.0, _log_sigmoid(gtT[f_lo:f_lo + nh, :]))
    b_c = _cumsum_cols(tri_c, fg_c)
    b_r = _cumsum_rows(fg_r, tri_r)
    last = 0 if reverse else CHUNK - 1

    lane_lo = col < MLSTM_QK
    row_lo = row < MLSTM_QK
    ones = jnp.ones((CHUNK, MLSTM_V), BF16)
    scale = MLSTM_QK ** -0.5

    def head_chain(h):
        pair = h // 2
        lo = (h % 2) == 0
        qp = q_ref[0, :, pair * CHUNK:(pair + 1) * CHUNK]
        kp = k_ref[0, :, pair * CHUNK:(pair + 1) * CHUNK]
        qm = (jnp.where(lane_lo if lo else jnp.logical_not(lane_lo), qp, 0.0) * scale).astype(BF16)
        kT = jnp.where(row_lo if lo else jnp.logical_not(row_lo), kp.T, 0.0)
        v_aug = jnp.concatenate([v_ref[0, :, h * MLSTM_V:(h + 1) * MLSTM_V].astype(BF16), ones], axis=1)
        m_st = m_ref[h][0:1, 0:1]
        bc = b_c[:, h:h + 1]
        br = b_r[h:h + 1, :]
        ir = ig_r[h:h + 1, :]
        s_raw = _dot_nt(qm, kp.astype(BF16))
        c_st = st_ref[h]
        inter_mm = _dot(qm, c_st.astype(BF16))
        yield
        dmat = jnp.where(keep, bc - br + ir, -jnp.inf)
        inter = bc + m_st
        m_row = jnp.maximum(jnp.max(dmat, axis=1, keepdims=True), inter)
        tot = br[:, last:last + 1]
        d_last = tot - br + ir
        m_new = jnp.maximum(tot + m_st, jnp.max(d_last, axis=1, keepdims=True))
        yield
        w_intra = jnp.exp(dmat - m_row)
        w_inter = jnp.exp(inter - m_row)
        w_s = jnp.exp(d_last - m_new)
        w_prev = jnp.exp(tot + m_st - m_new)
        yield
        s = s_raw * w_intra
        intra_mm = _dot(s.astype(BF16), v_aug)
        upd = _dot((kT * w_s).astype(BF16), v_aug)
        yield
        comb = intra_mm + w_inter * inter_mm
        num = comb[:, :MLSTM_V]
        den = jnp.maximum(jnp.abs(comb[:, MLSTM_V:]), jnp.exp(-m_row))
        y_ref[0, :, h * MLSTM_V:(h + 1) * MLSTM_V] = num / den
        st_ref[h] = w_prev * c_st + upd
        m_ref[h] = jnp.broadcast_to(m_new, m_ref.shape[1:])

    return [head_chain(h) for h in range(nh)]


def _mlstm_kernel(q_f, k_f, v_f, gt_f, gtT_f, q_r, k_r, v_r, gt_r, gtT_r, gb_r, gb_c,
                  yf_ref, yr_ref, st_ref, m_ref, *, n_chunks):
    step = pl.program_id(1)
    nh = MLSTM_HEADS

    @pl.when(step == 0)
    def _():
        st_ref[...] = jnp.zeros(st_ref.shape, F32)
        m_ref[...] = jnp.full(m_ref.shape, NEG, F32)

    fwd = _mlstm_direction(q_f, k_f, v_f, gt_f, gtT_f, gb_r, gb_c, yf_ref, st_ref.at[0:nh], m_ref.at[0:nh],
                           reverse=False, is_meta=step == 0)
    rev = _mlstm_direction(q_r, k_r, v_r, gt_r, gtT_r, gb_r, gb_c, yr_ref, st_ref.at[nh:2 * nh],
                           m_ref.at[nh:2 * nh], reverse=True, is_meta=step == n_chunks - 1)
    _interleave([c for fr in zip(fwd, rev) for c in fr])


def _mlstm_scan(q, k, v, gt, gtT, consts):
    b, tp, _ = q.shape
    n_chunks = tp // CHUNK
    arrays = [(q, False), (k, False), (v, False), (gt, False), (gtT, True)]
    in_specs = (_scan_specs(arrays, n_chunks, False) + _scan_specs(arrays, n_chunks, True)
                + [_full(c.shape) for c in consts])
    out_specs = _scan_specs([(v, False)], n_chunks, False) + _scan_specs([(v, False)], n_chunks, True)
    out_shape = [jax.ShapeDtypeStruct(v.shape, F32)] * 2
    return pl.pallas_call(
        functools.partial(_mlstm_kernel, n_chunks=n_chunks),
        grid=(b, n_chunks), in_specs=in_specs, out_specs=out_specs, out_shape=out_shape,
        scratch_shapes=[pltpu.VMEM((2 * MLSTM_HEADS, CHUNK, 2 * MLSTM_V), F32),
                        pltpu.VMEM((2 * MLSTM_HEADS, 8, LANE), F32)],
        compiler_params=_params("parallel", "arbitrary"),
        name="mlstm_scan")(q, k, v, gt, gtT, q, k, v, gt, gtT, *consts)


def _out_kernel(ya_ref, yf_ref, yr_ref, xs_ref, z_ref, yc_ref, hf_ref, hr_ref, og_ref, h_ref,
                ag_ref, dsk_ref, ng_ref, mg_ref, w_ref, o_ref, *, tp, seq):
    tm = h_ref.shape[0]
    ya = _rms_rows(ya_ref[...].astype(F32), ag_ref[...])
    z = z_ref[...]
    yb = (yf_ref[...] + yr_ref[...] + dsk_ref[...] * xs_ref[...]) * (z * _sigmoid(z))
    parts = [ya.astype(BF16), _rms_rows(yb, ng_ref[...]).astype(BF16), yc_ref[...].astype(BF16)]
    for hd in range(MLSTM_HEADS):
        lo, hi = hd * MLSTM_V, (hd + 1) * MLSTM_V
        hsum = hf_ref[:, lo:hi] + hr_ref[:, lo:hi]
        parts.append((_sigmoid(og_ref[:, lo:hi]) * _rms_rows(hsum, mg_ref[...])).astype(BF16))
    out = h_ref[...] + _dot(jnp.concatenate(parts, axis=1), w_ref[...])
    t = (pl.program_id(0) * tm + lax.broadcasted_iota(jnp.int32, (tm, 1), 0)) % tp
    is_pad = jnp.logical_and(t >= seq, t < seq + N_PAD)
    o_ref[...] = jnp.where(is_pad, 0.0, out)


def _out_proj(mixer_outs, h, consts, w, *, tp, seq, tm):
    rows, d = h.shape
    row_spec = lambda c: pl.BlockSpec((tm, c), lambda i: (i, 0))
    return pl.pallas_call(
        functools.partial(_out_kernel, tp=tp, seq=seq), grid=(rows // tm,),
        in_specs=([row_spec(a.shape[1]) for a in mixer_outs] + [row_spec(d)]
                  + [_full(c.shape) for c in consts] + [_full(w.shape)]),
        out_specs=row_spec(d), out_shape=jax.ShapeDtypeStruct((rows, d), F32),
        compiler_params=_params("parallel"), name="out_proj")(*mixer_outs, h, *consts, w)


def _ffn_kernel(h_ref, g_ref, wg_ref, wu_ref, wo_ref, o_ref, *, n_split):
    x = h_ref[0]
    hn = _rms_rows(x, g_ref[...]).astype(BF16)
    f = wg_ref.shape[1]
    tf = f // n_split
    acc = x
    for c in range(n_split):
        gate = _dot(hn, wg_ref[:, c * tf:(c + 1) * tf])
        up = _dot(hn, wu_ref[:, c * tf:(c + 1) * tf])
        act = (gate * _sigmoid(gate) * up).astype(BF16)
        acc = acc + _dot(act, wo_ref[c * tf:(c + 1) * tf, :])
    o_ref[0] = acc


def _ffn(h, g, wg, wu, wo, *, t_out, tm):
    b, _, d = h.shape
    f = wg.shape[1]
    n_split = 2 if (f // 2) % LANE == 0 else 1
    row_spec = pl.BlockSpec((1, tm, d), lambda i, j: (i, j, 0))
    resident = lambda a: pl.BlockSpec(a.shape, lambda i, j: (0, 0), pipeline_mode=pl.Buffered(1))
    return pl.pallas_call(
        functools.partial(_ffn_kernel, n_split=n_split), grid=(b, t_out // tm),
        in_specs=[row_spec, _full(g.shape), resident(wg), resident(wu), resident(wo)],
        out_specs=row_spec, out_shape=jax.ShapeDtypeStruct((b, t_out, d), F32),
        compiler_params=_params("parallel", "parallel"), name="ffn")(h, g, wg, wu, wo)


def _rope_tables_t(pos, rot_dim):
    inv = 1.0 / (ROPE_THETA ** (jnp.arange(0, rot_dim, 2, dtype=F32) / rot_dim))
    ang = pos[:, None] * inv[None, :]
    return jnp.cos(ang).T, jnp.sin(ang).T


def _col(v, n=None):
    v = v.astype(F32)
    if n is not None:
        v = jnp.pad(v, (0, n - v.shape[0]))
    return v[:, None]


def _row(v):
    return v.astype(F32)[None, :]


def kernel(x, meta_tokens, attn_norm_g, w_in, mla_q_norm_g, mla_kv_norm_g, mla_w_uq, mla_w_ukv, mla_q_head_g, mla_k_head_g, mla_out_g, ssd_conv_w, ssd_conv_b, ssd_dt_bias, ssd_a_log, ssd_d, ssd_norm_g, diff_q_head_g, diff_k_head_g, diff_lambda, diff_out_g, mlstm_i_bias, mlstm_f_bias, mlstm_out_g, w_out, ffn_norm_g, w_ffn_in, w_ffn_out):
    b, seq, d = x.shape
    depth = w_in.shape[0]
    tp = seq + CHUNK
    assert seq % CHUNK == 0
    tm_in = _pick_tile(tp, (640, 384, 128))
    tq = _pick_tile(tp, (1664, 640, 384, 128))
    tk = _pick_tile(seq, (512, 384, 256, 128))
    tm_flat = _pick_tile(b * tp, (512, 256, 128))
    tm_out = _pick_tile(seq, (512, 256, 128))

    meta = jnp.broadcast_to(meta_tokens[None].astype(x.dtype), (b, N_META, d))
    h = jnp.concatenate([x, jnp.zeros((b, N_PAD, d), x.dtype), meta], axis=1).reshape(b * tp, d)
    pos = jnp.concatenate([N_META + jnp.arange(seq, dtype=F32), jnp.zeros((N_PAD,), F32),
                           jnp.arange(N_META, dtype=F32)])
    cos_m, sin_m = _rope_tables_t(pos, MLA_ROPE)
    cos_d, sin_d = _rope_tables_t(pos, DIFF_ROPE)

    sizes = (MLA_Q_RANK, MLA_KV_RANK, MLA_ROPE, SSD_INNER, SSD_CONV_CH, 2 * SSD_HEADS,
             2 * DIFF_HEADS * DIFF_QK, 2 * DIFF_HEADS * DIFF_QK, DIFF_HEADS * DIFF_V,
             MLSTM_HEADS * MLSTM_QK, MLSTM_HEADS * MLSTM_QK, MLSTM_HEADS * MLSTM_V,
             MLSTM_HEADS * MLSTM_V, 2 * MLSTM_HEADS, 2 * MLSTM_HEADS)
    offs = [0]
    for s_ in sizes:
        offs.append(offs[-1] + s_)

    def cols(w, first, last):
        return w[:, offs[first]:offs[last + 1]]

    def pad_cols(w, n):
        return jnp.pad(w, ((0, 0), (0, n - w.shape[1])))

    for l in range(depth):
        lambda_init = 0.8 - 0.6 * math.exp(-0.3 * l)
        wl = w_in[l]
        h3 = h.reshape(b, tp, d)
        g_attn = _row(attn_norm_g[l])

        w_a = cols(wl, 0, 1).astype(BF16)
        w_kr = cols(wl, 2, 2).T.astype(BF16)
        w_uq = mla_w_uq[l].T.reshape(MLA_HEADS, MLA_QK, MLA_Q_RANK)
        w_uq = jnp.pad(w_uq, ((0, 0), (0, HEAD_PAD - MLA_QK), (0, 0))).reshape(MLA_HEADS * HEAD_PAD, MLA_Q_RANK)
        w_ukv = mla_w_ukv[l].T.reshape(MLA_HEADS, MLA_NOPE + MLA_V, MLA_KV_RANK)
        w_uk = w_ukv[:, :MLA_NOPE].reshape(MLA_HEADS * MLA_NOPE, MLA_KV_RANK)
        w_uv = w_ukv[:, MLA_NOPE:].reshape(MLA_HEADS * MLA_V, MLA_KV_RANK)
        qT_a, k_a, vT_a = _in_proj_call(
            _mla_in_kernel, h3,
            [g_attn, w_a, _row(mla_q_norm_g[l]), _row(mla_kv_norm_g[l]), w_uq.astype(BF16),
             w_uk.astype(BF16), w_uv.astype(BF16), w_kr, _col(mla_q_head_g[l], HEAD_PAD),
             _col(mla_k_head_g[l], HEAD_PAD), (cos_m,), (sin_m,)],
            [((MLA_HEADS * HEAD_PAD, tp), BF16, "col"), ((MLA_HEADS, tp, HEAD_PAD), BF16, "head"),
             ((MLA_HEADS * MLA_V, tp), BF16, "col")], tm_in, "mla_in")
        y_a = _attention(qT_a, k_a, vT_a, [], diff=False, seq=seq, tq=tq, tk=tk, out_scale=1.0,
                         name="mla_attn")

        w_c = cols(wl, 6, 8).T.astype(BF16)
        qT_c, k_c, vT_c = _in_proj_call(
            _diff_in_kernel, h3,
            [g_attn, w_c, _col(diff_q_head_g[l], HEAD_PAD), _col(diff_k_head_g[l], HEAD_PAD),
             (cos_d,), (sin_d,)],
            [((2 * DIFF_HEADS * HEAD_PAD, tp), BF16, "col"), ((2 * DIFF_HEADS, tp, HEAD_PAD), BF16, "head"),
             ((DIFF_HEADS * DIFF_V, tp), BF16, "col")], tm_in, "diff_in")
        lam = diff_lambda[l].astype(F32)
        lam_full = jnp.exp(jnp.sum(lam[0] * lam[1])) - jnp.exp(jnp.sum(lam[2] * lam[3])) + lambda_init
        y_c = _attention(qT_c, k_c, vT_c, [jnp.full((8, LANE), lam_full, F32), _col(diff_out_g[l])],
                         diff=True, seq=seq, tq=tq, tk=tk, out_scale=1.0 - lambda_init, name="diff_attn")

        n_b = SSD_INNER + SSD_CONV_CH + 2 * SSD_HEADS
        w_b = pad_cols(cols(wl, 3, 5), -(-n_b // LANE) * LANE).astype(BF16)
        w_dt = cols(wl, 5, 5).T.astype(BF16)
        z_b, xbc, dt, dtT = _in_proj_call(
            _ssd_in_kernel, h3, [g_attn, w_b, w_dt],
            [((tp, SSD_INNER), F32, "row"), ((tp, SSD_CONV_CH), F32, "row"),
             ((tp, 2 * SSD_HEADS), F32, "row"), ((2 * SSD_HEADS, tp), F32, "col")], tm_in, "ssd_in")
        w8 = jnp.pad(ssd_conv_w[l].astype(F32), ((0, 8 - SSD_CONV), (0, 0)))
        xs_b, bc_b = _ssd_conv(xbc, w8, _row(ssd_conv_b[l]), tm_in)
        dt_bias = ssd_dt_bias[l].astype(F32).reshape(-1)
        a_neg = -jnp.exp(ssd_a_log[l].astype(F32)).reshape(-1)
        y_bf, y_br = _ssd_scan(xs_b, bc_b, dt, dtT, [_row(dt_bias), _col(dt_bias), _row(a_neg), _col(a_neg)])

        n_d = 2 * MLSTM_HEADS * MLSTM_QK + 2 * MLSTM_HEADS * MLSTM_V + 4 * MLSTM_HEADS
        w_d = pad_cols(cols(wl, 9, 14), -(-n_d // LANE) * LANE).astype(BF16)
        w_g = cols(wl, 13, 14).T.astype(BF16)
        q_d, k_d, v_d, o_d, gt, gtT = _in_proj_call(
            _mlstm_in_kernel, h3, [g_attn, w_d, w_g],
            [((tp, MLSTM_HEADS * MLSTM_QK), F32, "row"), ((tp, MLSTM_HEADS * MLSTM_QK), F32, "row"),
             ((tp, MLSTM_HEADS * MLSTM_V), F32, "row"), ((tp, MLSTM_HEADS * MLSTM_V), F32, "row"),
             ((tp, 4 * MLSTM_HEADS), F32, "row"), ((4 * MLSTM_HEADS, tp), F32, "col")], tm_in, "mlstm_in")
        gate_bias = jnp.concatenate([mlstm_i_bias[l].reshape(-1), mlstm_f_bias[l].reshape(-1)]).astype(F32)
        h_df, h_dr = _mlstm_scan(q_d, k_d, v_d, gt, gtT, [_row(gate_bias), _col(gate_bias)])

        flat = lambda a: a.reshape(b * tp, a.shape[2])
        mixer_outs = [flat(a) for a in (y_a, y_bf, y_br, xs_b, z_b, y_c, h_df, h_dr, o_d)]
        out_consts = [_row(mla_out_g[l]), _row(jnp.repeat(ssd_d[l], SSD_HEAD_DIM)), _row(ssd_norm_g[l]),
                      _row(mlstm_out_g[l])]
        h = _out_proj(mixer_outs, h, out_consts, w_out[l].astype(BF16), tp=tp, seq=seq, tm=tm_flat)
        f = w_ffn_out.shape[1]
        last = l == depth - 1
        h = _ffn(h.reshape(b, tp, d), _row(ffn_norm_g[l]), w_ffn_in[l][:, :f].astype(BF16),
                 w_ffn_in[l][:, f:].astype(BF16), w_ffn_out[l].astype(BF16),
                 t_out=seq if last else tp, tm=tm_out if last else tm_in)
        h = h.reshape(-1, d)

    return h.reshape(b, seq, d).astype(x.dtype)
```

```python
import functools
import math

import jax
import jax.numpy as jnp
from jax import lax
from jax.experimental import pallas as pl
from jax.experimental.pallas import tpu as pltpu

N_META = 16
ROPE_THETA = 500000.0
EPS = 1e-6
CHUNK = 128
N_PAD = CHUNK - N_META
NEG = -1e30
LOG2E = 1.4426950408889634

MLA_HEADS, MLA_NOPE, MLA_ROPE, MLA_V = 8, 64, 32, 64
MLA_QK = MLA_NOPE + MLA_ROPE
MLA_Q_RANK, MLA_KV_RANK = 384, 256
SSD_HEADS, SSD_HEAD_DIM, SSD_GROUPS, SSD_STATE, SSD_CONV = 8, 64, 2, 64, 5
SSD_INNER = SSD_HEADS * SSD_HEAD_DIM
SSD_CONV_CH = SSD_INNER + 2 * SSD_GROUPS * SSD_STATE
DIFF_HEADS, DIFF_QK = 4, 64
DIFF_V = 2 * DIFF_QK
DIFF_ROPE = DIFF_QK // 4
MLSTM_HEADS, MLSTM_QK, MLSTM_V = 4, 64, 128
HEAD_PAD = 128
LANE = 128
Q_STRIP = 256
SSD_SKEW, MLSTM_SKEW = 4, 3
VMEM_LIMIT = 52 * 1024 * 1024

F32 = jnp.float32
BF16 = jnp.bfloat16
EXP_DTYPE = jnp.bfloat16


def _dot(a, b):
    return jnp.dot(a, b, preferred_element_type=F32)


def _dot_nt(a, b):
    return lax.dot_general(a, b, (((1,), (1,)), ((), ())), preferred_element_type=F32)


def _rms_rows(x, g):
    ms = jnp.mean(x * x, axis=-1, keepdims=True)
    return x * lax.rsqrt(ms + EPS) * g


def _split3(a):
    hi = a.astype(BF16)
    r1 = a - hi.astype(F32)
    mid = r1.astype(BF16)
    lo = (r1 - mid.astype(F32)).astype(BF16)
    return hi, mid, lo


def _cumsum_cols(tri, a):
    hi, mid, lo = _split3(a)
    return _dot(tri, hi) + _dot(tri, mid) + _dot(tri, lo)


def _cumsum_rows(a, tri):
    hi, mid, lo = _split3(a)
    return _dot(hi, tri) + _dot(mid, tri) + _dot(lo, tri)


def _softplus(x):
    return jnp.maximum(x, 0.0) + jnp.log(1.0 + jnp.exp(-jnp.abs(x)))


def _log_sigmoid(x):
    return jnp.minimum(x, 0.0) - jnp.log(1.0 + jnp.exp(-jnp.abs(x)))


def _sigmoid(x):
    return 1.0 / (1.0 + jnp.exp(-x))


def _pick_tile(n, candidates):
    for c in candidates:
        if n % c == 0:
            return c
    raise ValueError(f"no tile in {candidates} divides {n}")


def _params(*sem):
    return pltpu.CompilerParams(dimension_semantics=sem, vmem_limit_bytes=VMEM_LIMIT)


def _norm_rope_t(blk, g_col, cos, sin, n_real):
    r = cos.shape[0]
    ms = jnp.sum(blk * blk, axis=0, keepdims=True) * (1.0 / n_real)
    y = blk * lax.rsqrt(ms + EPS) * g_col
    x1, x2, rest = y[:r], y[r:2 * r], y[2 * r:]
    return jnp.concatenate([x1 * cos - x2 * sin, x2 * cos + x1 * sin, rest], axis=0)


def _mla_in_kernel(h_ref, g_ref, wa_ref, gq_ref, gkv_ref, wuq_ref, wuk_ref, wuv_ref, wkr_ref,
                   qhg_ref, khg_ref, cos_ref, sin_ref, qT_ref, k_ref, vT_ref):
    hn = _rms_rows(h_ref[0], g_ref[...]).astype(BF16)
    acc = _dot(hn, wa_ref[...])
    cqn = _rms_rows(acc[:, :MLA_Q_RANK], gq_ref[...]).astype(BF16)
    ckvn = _rms_rows(acc[:, MLA_Q_RANK:], gkv_ref[...]).astype(BF16)
    qT = _dot_nt(wuq_ref[...], cqn)
    knT = _dot_nt(wuk_ref[...], ckvn)
    vT = _dot_nt(wuv_ref[...], ckvn)
    krT = _dot_nt(wkr_ref[...], hn)
    cos, sin = cos_ref[...], sin_ref[...]
    t = krT.shape[1]
    zpad = jnp.zeros((HEAD_PAD - MLA_QK, t), F32)
    q_scale = (MLA_QK ** -0.5) * LOG2E
    for h in range(MLA_HEADS):
        qb = _norm_rope_t(qT[h * HEAD_PAD:(h + 1) * HEAD_PAD], qhg_ref[...], cos, sin, MLA_QK)
        qT_ref[0, h * HEAD_PAD:(h + 1) * HEAD_PAD, :] = (qb * q_scale).astype(BF16)
        kb = jnp.concatenate([krT, knT[h * MLA_NOPE:(h + 1) * MLA_NOPE], zpad], axis=0)
        kb = _norm_rope_t(kb, khg_ref[...], cos, sin, MLA_QK)
        k_ref[0, h] = kb.T.astype(BF16)
    vT_ref[0] = vT.astype(BF16)


def _diff_in_kernel(h_ref, g_ref, wc_ref, qhg_ref, khg_ref, cos_ref, sin_ref, qT_ref, k_ref, vT_ref):
    hn = _rms_rows(h_ref[0], g_ref[...]).astype(BF16)
    pT = _dot_nt(wc_ref[...], hn)
    cos, sin = cos_ref[...], sin_ref[...]
    t = pT.shape[1]
    zpad = jnp.zeros((HEAD_PAD - DIFF_QK, t), F32)
    nq = 2 * DIFF_HEADS * DIFF_QK
    q_scale = (DIFF_QK ** -0.5) * LOG2E
    for h in range(2 * DIFF_HEADS):
        qb = jnp.concatenate([pT[h * DIFF_QK:(h + 1) * DIFF_QK], zpad], axis=0)
        qb = _norm_rope_t(qb, qhg_ref[...], cos, sin, DIFF_QK)
        qT_ref[0, h * HEAD_PAD:(h + 1) * HEAD_PAD, :] = (qb * q_scale).astype(BF16)
        kb = jnp.concatenate([pT[nq + h * DIFF_QK:nq + (h + 1) * DIFF_QK], zpad], axis=0)
        kb = _norm_rope_t(kb, khg_ref[...], cos, sin, DIFF_QK)
        k_ref[0, h] = kb.T.astype(BF16)
    vT_ref[0] = pT[2 * nq:].astype(BF16)


def _ssd_in_kernel(h_ref, g_ref, wb_ref, wdt_ref, z_ref, xbc_ref, dt_ref, dtT_ref):
    hn = _rms_rows(h_ref[0], g_ref[...]).astype(BF16)
    acc = _dot(hn, wb_ref[...])
    z_ref[0] = acc[:, :SSD_INNER]
    xbc_ref[0] = acc[:, SSD_INNER:SSD_INNER + SSD_CONV_CH]
    dt_ref[0] = acc[:, SSD_INNER + SSD_CONV_CH:SSD_INNER + SSD_CONV_CH + 2 * SSD_HEADS]
    dtT_ref[0] = _dot_nt(wdt_ref[...], hn)


def _mlstm_in_kernel(h_ref, g_ref, wd_ref, wg_ref, q_ref, k_ref, v_ref, o_ref, gt_ref, gtT_ref):
    hn = _rms_rows(h_ref[0], g_ref[...]).astype(BF16)
    acc = _dot(hn, wd_ref[...])
    nqk = MLSTM_HEADS * MLSTM_QK
    nv = MLSTM_HEADS * MLSTM_V
    q_ref[0] = acc[:, :nqk]
    k_ref[0] = acc[:, nqk:2 * nqk]
    v_ref[0] = acc[:, 2 * nqk:2 * nqk + nv]
    o_ref[0] = acc[:, 2 * nqk + nv:2 * nqk + 2 * nv]
    gt_ref[0] = acc[:, 2 * nqk + 2 * nv:2 * nqk + 2 * nv + 4 * MLSTM_HEADS]
    gtT_ref[0] = _dot_nt(wg_ref[...], hn)


def _full(shape):
    nd = len(shape)
    return pl.BlockSpec(shape, lambda *_: (0,) * nd)


def _in_proj_call(body, h, consts, outs, tm, name):
    b, tp, d = h.shape
    in_specs = [pl.BlockSpec((1, tm, d), lambda i, j: (i, j, 0))]
    for c in consts:
        if isinstance(c, tuple):
            in_specs.append(pl.BlockSpec((c[0].shape[0], tm), lambda i, j: (0, j)))
        else:
            in_specs.append(_full(c.shape))
    out_shapes, out_specs = [], []
    for shape, dtype, kind in outs:
        out_shapes.append(jax.ShapeDtypeStruct((b,) + shape, dtype))
        if kind == "row":
            out_specs.append(pl.BlockSpec((1, tm, shape[1]), lambda i, j: (i, j, 0)))
        elif kind == "col":
            out_specs.append(pl.BlockSpec((1, shape[0], tm), lambda i, j: (i, 0, j)))
        else:
            out_specs.append(pl.BlockSpec((1, shape[0], tm, shape[2]), lambda i, j: (i, 0, j, 0)))
    args = [h] + [c[0] if isinstance(c, tuple) else c for c in consts]
    return pl.pallas_call(
        body, grid=(b, tp // tm), in_specs=in_specs, out_specs=out_specs, out_shape=out_shapes,
        compiler_params=_params("parallel", "parallel"), name=name)(*args)


def _attn_kernel(*refs, diff, seq, tk, out_scale):
    if diff:
        qT_ref, k_ref, vT_ref, lam_ref, og_ref, o_ref, s_scr, st_scr, m_scr, acc_scr = refs
    else:
        qT_ref, k_ref, vT_ref, o_ref, s_scr, st_scr, m_scr, acc_scr = refs
    dv = DIFF_V if diff else MLA_V
    tq = qT_ref.shape[2]
    n_chunks = seq // tk
    assert n_chunks % 2 == 0
    outs = []
    for hh in range(2):
        v_lo = 0 if diff else hh * dv
        m_scr[...] = jnp.full(m_scr.shape, NEG, F32)
        acc_scr[...] = jnp.zeros(acc_scr.shape, F32)

        def stage(nxt, cur, hh=hh, v_lo=v_lo):
            if nxt is not None:
                k_n = k_ref[0, hh, pl.ds(nxt[1], nxt[2]), :]
            if cur is not None:
                ones = jnp.ones((16, cur[2]), BF16)
                v_aug = jnp.concatenate([vT_ref[0, v_lo:v_lo + dv, pl.ds(cur[1], cur[2])], ones], axis=0)
            pending = None
            for j0 in range(0, tq, Q_STRIP):
                w = min(Q_STRIP, tq - j0)
                if cur is not None:
                    s = cur[0][:, j0:j0 + w]
                    m_old = m_scr[0:1, j0:j0 + w]
                    m_new = jnp.maximum(m_old, jnp.max(s, axis=0, keepdims=True))
                    alpha = jnp.exp2(m_old - m_new)
                    p = jnp.exp2((s - m_new).astype(EXP_DTYPE)).astype(BF16)
                    m_scr[0:1, j0:j0 + w] = m_new
                if nxt is not None:
                    s_n = _dot(k_n, qT_ref[0, hh * HEAD_PAD:(hh + 1) * HEAD_PAD, j0:j0 + w])
                    if nxt[3]:
                        row = lax.broadcasted_iota(jnp.int32, s_n.shape, 0)
                        s_n = jnp.where(row >= N_PAD, s_n, NEG)
                    nxt[0][:, j0:j0 + w] = s_n
                if cur is not None:
                    if pending is not None:
                        pj, pw, pa, pp = pending
                        acc_scr[:, pj:pj + pw] = pa * acc_scr[:, pj:pj + pw] + _dot(v_aug, pp)
                    pending = (j0, w, alpha, p)
            if pending is not None:
                pj, pw, pa, pp = pending
                acc_scr[:, pj:pj + pw] = pa * acc_scr[:, pj:pj + pw] + _dot(v_aug, pp)

        buf0, buf1 = s_scr.at[0], s_scr.at[1]
        stage((buf0, 0, tk, False), None)

        def body(i, carry, stage=stage):
            off = pl.multiple_of(2 * i * tk, 2 * tk)
            stage((buf1, off + tk, tk, False), (buf0, off, tk))
            stage((buf0, off + 2 * tk, tk, False), (buf1, off + tk, tk))
            return carry

        lax.fori_loop(0, n_chunks // 2 - 1, body, 0)
        off = (n_chunks - 2) * tk
        stage((buf1, off + tk, tk, False), (buf0, off, tk))
        stage((st_scr, seq, CHUNK, True), (buf1, off + tk, tk))
        stage(None, (st_scr, seq, CHUNK))
        acc = acc_scr[...]
        outs.append(acc[:dv] / acc[dv:dv + 1])
    if diff:
        o = outs[0] - lam_ref[0:1, 0:1] * outs[1]
        ms = jnp.mean(o * o, axis=0, keepdims=True)
        o = o * lax.rsqrt(ms + EPS) * (og_ref[...] * out_scale)
    else:
        o = jnp.concatenate(outs, axis=0)
    o_ref[0] = o.T.astype(o_ref.dtype)


def _attention(qT, k, vT, extra, *, diff, seq, tq, tk, out_scale, name):
    b, _, tp = qT.shape
    groups = qT.shape[1] // (2 * HEAD_PAD)
    dv = DIFF_V if diff else MLA_V
    v_rows = dv if diff else 2 * dv
    in_specs = [
        pl.BlockSpec((1, 2 * HEAD_PAD, tq), lambda i, g, j: (i, g, j)),
        pl.BlockSpec((1, 2, tp, HEAD_PAD), lambda i, g, j: (i, g, 0, 0)),
        pl.BlockSpec((1, v_rows, tp), lambda i, g, j: (i, g, 0)),
    ] + [_full(e.shape) for e in extra]
    return pl.pallas_call(
        functools.partial(_attn_kernel, diff=diff, seq=seq, tk=tk, out_scale=out_scale),
        grid=(b, groups, tp // tq),
        in_specs=in_specs,
        out_specs=pl.BlockSpec((1, tq, LANE), lambda i, g, j: (i, j, g)),
        out_shape=jax.ShapeDtypeStruct((b, tp, groups * LANE), BF16),
        scratch_shapes=[pltpu.VMEM((2, tk, tq), F32), pltpu.VMEM((CHUNK, tq), F32),
                        pltpu.VMEM((8, tq), F32), pltpu.VMEM((dv + 16, tq), F32)],
        compiler_params=_params("parallel", "parallel", "arbitrary"), name=name)(qT, k, vT, *extra)


def _conv_kernel(x_ref, prev_ref, next_ref, w_ref, b_ref, xs_ref, bc_ref, scr):
    tc = x_ref.shape[1]
    scr[0:8, :] = prev_ref[0]
    scr[8:8 + tc, :] = x_ref[0]
    scr[8 + tc:16 + tc, :] = next_ref[0]
    acc = jnp.broadcast_to(b_ref[...], (tc, b_ref.shape[1]))
    for j in range(SSD_CONV):
        acc = acc + w_ref[j:j + 1, :] * scr[8 - SSD_CONV // 2 + j:8 - SSD_CONV // 2 + j + tc, :]
    act = acc * _sigmoid(acc)
    xs_ref[0] = act[:, :SSD_INNER]
    bc_ref[0] = act[:, SSD_INNER:]


def _ssd_conv(xbc, w8, bias, tc):
    b, tp, c = xbc.shape
    nb8 = tp // 8
    r8 = tc // 8
    row_spec = lambda n: pl.BlockSpec((1, tc, n), lambda i, j: (i, j, 0))
    return pl.pallas_call(
        _conv_kernel, grid=(b, tp // tc),
        in_specs=[row_spec(c),
                  pl.BlockSpec((1, 8, c), lambda i, j: (i, (j * r8 + nb8 - 1) % nb8, 0)),
                  pl.BlockSpec((1, 8, c), lambda i, j: (i, ((j + 1) * r8) % nb8, 0)),
                  _full(w8.shape), _full(bias.shape)],
        out_specs=[row_spec(SSD_INNER), row_spec(c - SSD_INNER)],
        out_shape=[jax.ShapeDtypeStruct((b, tp, SSD_INNER), F32),
                   jax.ShapeDtypeStruct((b, tp, c - SSD_INNER), F32)],
        scratch_shapes=[pltpu.VMEM((tc + 16, c), F32)],
        compiler_params=_params("parallel", "parallel"), name="ssd_conv")(xbc, xbc, xbc, w8, bias)


def _tri_masks(reverse):
    row = lax.broadcasted_iota(jnp.int32, (CHUNK, CHUNK), 0)
    col = lax.broadcasted_iota(jnp.int32, (CHUNK, CHUNK), 1)
    keep = (col >= row) if reverse else (col <= row)
    tri_c = keep.astype(BF16)
    tri_r = ((row >= col) if reverse else (row <= col)).astype(BF16)
    return keep, tri_c, tri_r, row, col


def _ssd_direction(xs_ref, bc_ref, dt_ref, dtT_ref, bias_r, bias_c, a_r, a_c, y_ref, st_ref, *,
                   bi, reverse, is_meta):
    d = 1 if reverse else 0
    h8 = SSD_HEADS
    keep, tri_c, tri_r, row, col = _tri_masks(reverse)
    bm = bc_ref[bi, :, :CHUNK]
    cm = bc_ref[bi, :, CHUNK:]

    dt_c = _softplus(dt_ref[bi][:, d * h8:(d + 1) * h8] + bias_r[:, d * h8:(d + 1) * h8])
    dt_r = _softplus(dtT_ref[bi][d * h8:(d + 1) * h8, :] + bias_c[d * h8:(d + 1) * h8, :])
    pad_c = jnp.logical_and(is_meta, row[:, :h8] < N_PAD)
    pad_r = jnp.logical_and(is_meta, col[:h8, :] < N_PAD)
    dt_c = jnp.where(pad_c, 0.0, dt_c)
    dt_r = jnp.where(pad_r, 0.0, dt_r)
    a_col = dt_c * a_r[:, d * h8:(d + 1) * h8]
    a_row = dt_r * a_c[d * h8:(d + 1) * h8, :]
    cs_c = _cumsum_cols(tri_c, a_col)
    cs_r = _cumsum_rows(a_row, tri_r)
    last = 0 if reverse else CHUNK - 1
    tot_r = cs_r[:, last:last + 1]
    tot_c = cs_c[last:last + 1, :]

    lane_lo = col < SSD_STATE
    row_lo = row < SSD_STATE
    blockdiag = jnp.logical_not(jnp.logical_xor(lane_lo, row_lo))
    cm_sw = pltpu.roll(cm, SSD_STATE, 1)
    c_dup = (jnp.where(lane_lo, cm, cm_sw), jnp.where(lane_lo, cm_sw, cm))
    g_mat = (_dot_nt(jnp.where(lane_lo, cm, 0.0).astype(BF16), bm.astype(BF16)),
             _dot_nt(jnp.where(lane_lo, 0.0, cm).astype(BF16), bm.astype(BF16)))
    bT = bm.T

    heads_per_group = SSD_HEADS // SSD_GROUPS

    def pair_chain(j):
        g = (2 * j) // heads_per_group
        h0, h1 = 2 * j, 2 * j + 1
        xp = xs_ref[bi, :, j * CHUNK:(j + 1) * CHUNK]
        dtp = jnp.where(lane_lo, dt_c[:, h0:h0 + 1], dt_c[:, h1:h1 + 1])
        xdt = xp * dtp
        cs_b = [jnp.broadcast_to(cs_c[:, h:h + 1], (CHUNK, CHUNK)) for h in (h0, h1)]
        yield
        parts = []
        for h, cs_h in zip((h0, h1), cs_b):
            diff_ = cs_h - cs_r[h:h + 1, :]
            parts.append((g_mat[g] * jnp.exp(jnp.where(keep, diff_, NEG))).astype(BF16))
        parts.append((c_dup[g] * jnp.exp(jnp.where(lane_lo, cs_b[0], cs_b[1]))).astype(BF16))
        yield
        lhs = jnp.concatenate(parts, axis=1)
        s_old = st_ref[j]
        rhs = jnp.concatenate([jnp.where(lane_lo, xdt, 0.0).astype(BF16),
                               jnp.where(lane_lo, 0.0, xdt).astype(BF16),
                               s_old.astype(BF16)], axis=0)
        y_pair = _dot(lhs, rhs)
        btg = bT[g * SSD_STATE:(g + 1) * SSD_STATE, :]
        bd = jnp.concatenate([btg * jnp.exp(tot_r[h0:h0 + 1, :] - cs_r[h0:h0 + 1, :]),
                              btg * jnp.exp(tot_r[h1:h1 + 1, :] - cs_r[h1:h1 + 1, :])], axis=0)
        s_upd = _dot(bd.astype(BF16), xdt.astype(BF16))
        yield
        carry = jnp.where(row_lo, jnp.exp(tot_c[:, h0:h0 + 1]), jnp.exp(tot_c[:, h1:h1 + 1]))
        st_ref[j] = s_old * carry + jnp.where(blockdiag, s_upd, 0.0)
        y_ref[bi, :, j * CHUNK:(j + 1) * CHUNK] = y_pair

    return [pair_chain(j) for j in range(SSD_HEADS // 2)]


def _interleave(chains, period=1):
    pending = [(i % period, c) for i, c in enumerate(chains)]
    rnd = 0
    while pending:
        alive = []
        for delay, c in pending:
            if rnd >= delay:
                try:
                    next(c)
                except StopIteration:
                    continue
            alive.append((delay, c))
        pending = alive
        rnd += 1


def _ssd_kernel(xs_f, bc_f, dt_f, dtT_f, xs_r, bc_r, dt_r, dtT_r, bias_r, bias_c, a_r, a_c,
                yf_ref, yr_ref, st_ref, *, n_chunks):
    step = pl.program_id(1)
    n_pairs = SSD_HEADS // 2

    @pl.when(step == 0)
    def _():
        st_ref[...] = jnp.zeros(st_ref.shape, F32)

    scans = []
    for bi in range(xs_f.shape[0]):
        lo = 2 * bi * n_pairs
        scans.append(_ssd_direction(xs_f, bc_f, dt_f, dtT_f, bias_r, bias_c, a_r, a_c, yf_ref,
                                    st_ref.at[lo:lo + n_pairs], bi=bi, reverse=False, is_meta=step == 0))
        scans.append(_ssd_direction(xs_r, bc_r, dt_r, dtT_r, bias_r, bias_c, a_r, a_c, yr_ref,
                                    st_ref.at[lo + n_pairs:lo + 2 * n_pairs], bi=bi, reverse=True,
                                    is_meta=step == n_chunks - 1))
    _interleave([c for group in zip(*scans) for c in group], period=SSD_SKEW)


def _chunk_order(n_chunks, reverse):
    if reverse:
        return lambda c: (2 * n_chunks - 2 - c) % n_chunks
    return lambda c: (c + n_chunks - 1) % n_chunks


def _scan_specs(arrays, n_chunks, reverse, nb):
    order = _chunk_order(n_chunks, reverse)
    specs = []
    for a, transposed in arrays:
        if transposed:
            specs.append(pl.BlockSpec((nb, a.shape[1], CHUNK), lambda i, s: (i, 0, order(s))))
        else:
            specs.append(pl.BlockSpec((nb, CHUNK, a.shape[2]), lambda i, s: (i, order(s), 0)))
    return specs


def _scan_batch(b):
    return 2 if b % 2 == 0 else 1


def _ssd_scan(xs, bc, dt, dtT, consts):
    b, tp, _ = xs.shape
    n_chunks = tp // CHUNK
    nb = _scan_batch(b)
    arrays = [(xs, False), (bc, False), (dt, False), (dtT, True)]
    in_specs = (_scan_specs(arrays, n_chunks, False, nb) + _scan_specs(arrays, n_chunks, True, nb)
                + [_full(c.shape) for c in consts])
    out_specs = (_scan_specs([(xs, False)], n_chunks, False, nb)
                 + _scan_specs([(xs, False)], n_chunks, True, nb))
    out_shape = [jax.ShapeDtypeStruct((b, tp, SSD_INNER), F32)] * 2
    return pl.pallas_call(
        functools.partial(_ssd_kernel, n_chunks=n_chunks),
        grid=(b // nb, n_chunks), in_specs=in_specs, out_specs=out_specs, out_shape=out_shape,
        scratch_shapes=[pltpu.VMEM((nb * SSD_HEADS, CHUNK, CHUNK), F32)],
        compiler_params=_params("parallel", "arbitrary"),
        name="ssd_scan")(xs, bc, dt, dtT, xs, bc, dt, dtT, *consts)


def _mlstm_direction(q_ref, k_ref, v_ref, gt_ref, gtT_ref, gb_r, gb_c, y_ref, st_ref, m_ref, *,
                     bi, reverse, is_meta):
    d = 1 if reverse else 0
    nh = MLSTM_HEADS
    keep, tri_c, tri_r, row, col = _tri_masks(reverse)
    gt = gt_ref[bi] + gb_r[...]
    gtT = gtT_ref[bi] + gb_c[...]
    i_lo, f_lo = d * nh, 2 * nh + d * nh
    pad_c = jnp.logical_and(is_meta, row[:, :nh] < N_PAD)
    pad_r = jnp.logical_and(is_meta, col[:nh, :] < N_PAD)
    ig_r = jnp.where(pad_r, NEG, gtT[i_lo:i_lo + nh, :])
    fg_c = jnp.where(pad_c, 0.0, _log_sigmoid(gt[:, f_lo:f_lo + nh]))
    fg_r = jnp.where(pad_r, 0.0, _log_sigmoid(gtT[f_lo:f_lo + nh, :]))
    b_c = _cumsum_cols(tri_c, fg_c)
    b_r = _cumsum_rows(fg_r, tri_r)
    last = 0 if reverse else CHUNK - 1

    lane_lo = col < MLSTM_QK
    row_lo = row < MLSTM_QK
    ones = jnp.ones((CHUNK, MLSTM_V), BF16)
    scale = MLSTM_QK ** -0.5

    def head_chain(h):
        pair = h // 2
        lo = (h % 2) == 0
        qp = q_ref[bi, :, pair * CHUNK:(pair + 1) * CHUNK]
        kp = k_ref[bi, :, pair * CHUNK:(pair + 1) * CHUNK]
        qm = (jnp.where(lane_lo if lo else jnp.logical_not(lane_lo), qp, 0.0) * scale).astype(BF16)
        kT = jnp.where(row_lo if lo else jnp.logical_not(row_lo), kp.T, 0.0)
        v_aug = jnp.concatenate([v_ref[bi, :, h * MLSTM_V:(h + 1) * MLSTM_V].astype(BF16), ones], axis=1)
        m_st = m_ref[h][0:1, 0:1]
        bc = b_c[:, h:h + 1]
        br = b_r[h:h + 1, :]
        ir = ig_r[h:h + 1, :]
        s_raw = _dot_nt(qm, kp.astype(BF16))
        c_st = st_ref[h]
        inter_mm = _dot(qm, c_st.astype(BF16))
        yield
        dmat = jnp.where(keep, ir - br, -jnp.inf)
        m_rel = jnp.maximum(jnp.max(dmat, axis=1, keepdims=True), m_st)
        tot = br[:, last:last + 1]
        d_last = tot - br + ir
        m_new = jnp.maximum(tot + m_st, jnp.max(d_last, axis=1, keepdims=True))
        yield
        w_intra = jnp.exp(dmat - m_rel)
        w_inter = jnp.exp(m_st - m_rel)
        w_s = jnp.exp(d_last - m_new)
        w_prev = jnp.exp(tot + m_st - m_new)
        yield
        s = s_raw * w_intra
        intra_mm = _dot(s.astype(BF16), v_aug)
        upd = _dot((kT * w_s).astype(BF16), v_aug)
        yield
        comb = intra_mm + w_inter * inter_mm
        num = comb[:, :MLSTM_V]
        den = jnp.maximum(jnp.abs(comb[:, MLSTM_V:]), jnp.exp(-(bc + m_rel)))
        y_ref[bi, :, h * MLSTM_V:(h + 1) * MLSTM_V] = num / den
        st_ref[h] = w_prev * c_st + upd
        m_ref[h] = jnp.broadcast_to(m_new, m_ref.shape[1:])

    return [head_chain(h) for h in range(nh)]


def _mlstm_kernel(q_f, k_f, v_f, gt_f, gtT_f, q_r, k_r, v_r, gt_r, gtT_r, gb_r, gb_c,
                  yf_ref, yr_ref, st_ref, m_ref, *, n_chunks):
    step = pl.program_id(1)
    nh = MLSTM_HEADS

    @pl.when(step == 0)
    def _():
        st_ref[...] = jnp.zeros(st_ref.shape, F32)
        m_ref[...] = jnp.full(m_ref.shape, NEG, F32)

    scans = []
    for bi in range(q_f.shape[0]):
        lo = 2 * bi * nh
        scans.append(_mlstm_direction(q_f, k_f, v_f, gt_f, gtT_f, gb_r, gb_c, yf_ref, st_ref.at[lo:lo + nh],
                                      m_ref.at[lo:lo + nh], bi=bi, reverse=False, is_meta=step == 0))
        scans.append(_mlstm_direction(q_r, k_r, v_r, gt_r, gtT_r, gb_r, gb_c, yr_ref,
                                      st_ref.at[lo + nh:lo + 2 * nh], m_ref.at[lo + nh:lo + 2 * nh],
                                      bi=bi, reverse=True, is_meta=step == n_chunks - 1))
    _interleave([c for group in zip(*scans) for c in group], period=MLSTM_SKEW)


def _mlstm_scan(q, k, v, gt, gtT, consts):
    b, tp, _ = q.shape
    n_chunks = tp // CHUNK
    nb = _scan_batch(b)
    arrays = [(q, False), (k, False), (v, False), (gt, False), (gtT, True)]
    in_specs = (_scan_specs(arrays, n_chunks, False, nb) + _scan_specs(arrays, n_chunks, True, nb)
                + [_full(c.shape) for c in consts])
    out_specs = (_scan_specs([(v, False)], n_chunks, False, nb)
                 + _scan_specs([(v, False)], n_chunks, True, nb))
    out_shape = [jax.ShapeDtypeStruct(v.shape, F32)] * 2
    return pl.pallas_call(
        functools.partial(_mlstm_kernel, n_chunks=n_chunks),
        grid=(b // nb, n_chunks), in_specs=in_specs, out_specs=out_specs, out_shape=out_shape,
        scratch_shapes=[pltpu.VMEM((nb * 2 * MLSTM_HEADS, CHUNK, 2 * MLSTM_V), F32),
                        pltpu.VMEM((nb * 2 * MLSTM_HEADS, 8, LANE), F32)],
        compiler_params=_params("parallel", "arbitrary"),
        name="mlstm_scan")(q, k, v, gt, gtT, q, k, v, gt, gtT, *consts)


def _out_kernel(ya_ref, yf_ref, yr_ref, xs_ref, z_ref, yc_ref, hf_ref, hr_ref, og_ref, h_ref,
                ag_ref, dsk_ref, ng_ref, mg_ref, w_ref, o_ref, *, tp, seq):
    tm = h_ref.shape[0]
    ya = _rms_rows(ya_ref[...].astype(F32), ag_ref[...])
    z = z_ref[...]
    yb = (yf_ref[...] + yr_ref[...] + dsk_ref[...] * xs_ref[...]) * (z * _sigmoid(z))
    parts = [ya.astype(BF16), _rms_rows(yb, ng_ref[...]).astype(BF16), yc_ref[...].astype(BF16)]
    for hd in range(MLSTM_HEADS):
        lo, hi = hd * MLSTM_V, (hd + 1) * MLSTM_V
        hsum = hf_ref[:, lo:hi] + hr_ref[:, lo:hi]
        parts.append((_sigmoid(og_ref[:, lo:hi]) * _rms_rows(hsum, mg_ref[...])).astype(BF16))
    out = h_ref[...] + _dot(jnp.concatenate(parts, axis=1), w_ref[...])
    t = (pl.program_id(0) * tm + lax.broadcasted_iota(jnp.int32, (tm, 1), 0)) % tp
    is_pad = jnp.logical_and(t >= seq, t < seq + N_PAD)
    o_ref[...] = jnp.where(is_pad, 0.0, out)


def _out_proj(mixer_outs, h, consts, w, *, tp, seq, tm):
    rows, d = h.shape
    row_spec = lambda c: pl.BlockSpec((tm, c), lambda i: (i, 0))
    return pl.pallas_call(
        functools.partial(_out_kernel, tp=tp, seq=seq), grid=(rows // tm,),
        in_specs=([row_spec(a.shape[1]) for a in mixer_outs] + [row_spec(d)]
                  + [_full(c.shape) for c in consts] + [_full(w.shape)]),
        out_specs=row_spec(d), out_shape=jax.ShapeDtypeStruct((rows, d), F32),
        compiler_params=_params("parallel"), name="out_proj")(*mixer_outs, h, *consts, w)


def _ffn_kernel(h_ref, g_ref, wg_ref, wu_ref, wo_ref, o_ref, *, n_split):
    x = h_ref[0]
    hn = _rms_rows(x, g_ref[...]).astype(BF16)
    f = wg_ref.shape[1]
    tf = f // n_split
    acc = x
    for c in range(n_split):
        gate = _dot(hn, wg_ref[:, c * tf:(c + 1) * tf])
        up = _dot(hn, wu_ref[:, c * tf:(c + 1) * tf])
        act = (gate * _sigmoid(gate) * up).astype(BF16)
        acc = acc + _dot(act, wo_ref[c * tf:(c + 1) * tf, :])
    o_ref[0] = acc


def _ffn(h, g, wg, wu, wo, *, t_out, tm):
    b, _, d = h.shape
    f = wg.shape[1]
    n_split = 2 if (f // 2) % LANE == 0 else 1
    row_spec = pl.BlockSpec((1, tm, d), lambda i, j: (i, j, 0))
    resident = lambda a: pl.BlockSpec(a.shape, lambda i, j: (0, 0), pipeline_mode=pl.Buffered(1))
    return pl.pallas_call(
        functools.partial(_ffn_kernel, n_split=n_split), grid=(b, t_out // tm),
        in_specs=[row_spec, _full(g.shape), resident(wg), resident(wu), resident(wo)],
        out_specs=row_spec, out_shape=jax.ShapeDtypeStruct((b, t_out, d), F32),
        compiler_params=_params("parallel", "parallel"), name="ffn")(h, g, wg, wu, wo)


def _rope_tables_t(pos, rot_dim):
    inv = 1.0 / (ROPE_THETA ** (jnp.arange(0, rot_dim, 2, dtype=F32) / rot_dim))
    ang = pos[:, None] * inv[None, :]
    return jnp.cos(ang).T, jnp.sin(ang).T


def _col(v, n=None):
    v = v.astype(F32)
    if n is not None:
        v = jnp.pad(v, (0, n - v.shape[0]))
    return v[:, None]


def _row(v):
    return v.astype(F32)[None, :]


def kernel(x, meta_tokens, attn_norm_g, w_in, mla_q_norm_g, mla_kv_norm_g, mla_w_uq, mla_w_ukv, mla_q_head_g, mla_k_head_g, mla_out_g, ssd_conv_w, ssd_conv_b, ssd_dt_bias, ssd_a_log, ssd_d, ssd_norm_g, diff_q_head_g, diff_k_head_g, diff_lambda, diff_out_g, mlstm_i_bias, mlstm_f_bias, mlstm_out_g, w_out, ffn_norm_g, w_ffn_in, w_ffn_out):
    b, seq, d = x.shape
    depth = w_in.shape[0]
    tp = seq + CHUNK
    assert seq % CHUNK == 0
    tm_in = _pick_tile(tp, (640, 384, 128))
    tm_row = _pick_tile(tp, (1664, 640, 384, 128))
    tq = _pick_tile(tp, (1664, 640, 384, 128))
    tk = _pick_tile(seq, (512, 384, 256, 128))
    tm_flat = _pick_tile(b * tp, (512, 256, 128))
    tm_out = _pick_tile(seq, (512, 256, 128))

    meta = jnp.broadcast_to(meta_tokens[None].astype(x.dtype), (b, N_META, d))
    h = jnp.concatenate([x, jnp.zeros((b, N_PAD, d), x.dtype), meta], axis=1).reshape(b * tp, d)
    pos = jnp.concatenate([N_META + jnp.arange(seq, dtype=F32), jnp.zeros((N_PAD,), F32),
                           jnp.arange(N_META, dtype=F32)])
    cos_m, sin_m = _rope_tables_t(pos, MLA_ROPE)
    cos_d, sin_d = _rope_tables_t(pos, DIFF_ROPE)

    sizes = (MLA_Q_RANK, MLA_KV_RANK, MLA_ROPE, SSD_INNER, SSD_CONV_CH, 2 * SSD_HEADS,
             2 * DIFF_HEADS * DIFF_QK, 2 * DIFF_HEADS * DIFF_QK, DIFF_HEADS * DIFF_V,
             MLSTM_HEADS * MLSTM_QK, MLSTM_HEADS * MLSTM_QK, MLSTM_HEADS * MLSTM_V,
             MLSTM_HEADS * MLSTM_V, 2 * MLSTM_HEADS, 2 * MLSTM_HEADS)
    offs = [0]
    for s_ in sizes:
        offs.append(offs[-1] + s_)

    def cols(w, first, last):
        return w[:, offs[first]:offs[last + 1]]

    def pad_cols(w, n):
        return jnp.pad(w, ((0, 0), (0, n - w.shape[1])))

    for l in range(depth):
        lambda_init = 0.8 - 0.6 * math.exp(-0.3 * l)
        wl = w_in[l]
        h3 = h.reshape(b, tp, d)
        g_attn = _row(attn_norm_g[l])

        w_a = cols(wl, 0, 1).astype(BF16)
        w_kr = cols(wl, 2, 2).T.astype(BF16)
        w_uq = mla_w_uq[l].T.reshape(MLA_HEADS, MLA_QK, MLA_Q_RANK)
        w_uq = jnp.pad(w_uq, ((0, 0), (0, HEAD_PAD - MLA_QK), (0, 0))).reshape(MLA_HEADS * HEAD_PAD, MLA_Q_RANK)
        w_ukv = mla_w_ukv[l].T.reshape(MLA_HEADS, MLA_NOPE + MLA_V, MLA_KV_RANK)
        w_uk = w_ukv[:, :MLA_NOPE].reshape(MLA_HEADS * MLA_NOPE, MLA_KV_RANK)
        w_uv = w_ukv[:, MLA_NOPE:].reshape(MLA_HEADS * MLA_V, MLA_KV_RANK)
        qT_a, k_a, vT_a = _in_proj_call(
            _mla_in_kernel, h3,
            [g_attn, w_a, _row(mla_q_norm_g[l]), _row(mla_kv_norm_g[l]), w_uq.astype(BF16),
             w_uk.astype(BF16), w_uv.astype(BF16), w_kr, _col(mla_q_head_g[l], HEAD_PAD),
             _col(mla_k_head_g[l], HEAD_PAD), (cos_m,), (sin_m,)],
            [((MLA_HEADS * HEAD_PAD, tp), BF16, "col"), ((MLA_HEADS, tp, HEAD_PAD), BF16, "head"),
             ((MLA_HEADS * MLA_V, tp), BF16, "col")], tm_in, "mla_in")
        y_a = _attention(qT_a, k_a, vT_a, [], diff=False, seq=seq, tq=tq, tk=tk, out_scale=1.0,
                         name="mla_attn")

        w_c = cols(wl, 6, 8).T.astype(BF16)
        qT_c, k_c, vT_c = _in_proj_call(
            _diff_in_kernel, h3,
            [g_attn, w_c, _col(diff_q_head_g[l], HEAD_PAD), _col(diff_k_head_g[l], HEAD_PAD),
             (cos_d,), (sin_d,)],
            [((2 * DIFF_HEADS * HEAD_PAD, tp), BF16, "col"), ((2 * DIFF_HEADS, tp, HEAD_PAD), BF16, "head"),
             ((DIFF_HEADS * DIFF_V, tp), BF16, "col")], tm_in, "diff_in")
        lam = diff_lambda[l].astype(F32)
        lam_full = jnp.exp(jnp.sum(lam[0] * lam[1])) - jnp.exp(jnp.sum(lam[2] * lam[3])) + lambda_init
        y_c = _attention(qT_c, k_c, vT_c, [jnp.full((8, LANE), lam_full, F32), _col(diff_out_g[l])],
                         diff=True, seq=seq, tq=tq, tk=tk, out_scale=1.0 - lambda_init, name="diff_attn")

        n_b = SSD_INNER + SSD_CONV_CH + 2 * SSD_HEADS
        w_b = pad_cols(cols(wl, 3, 5), -(-n_b // LANE) * LANE).astype(BF16)
        w_dt = cols(wl, 5, 5).T.astype(BF16)
        z_b, xbc, dt, dtT = _in_proj_call(
            _ssd_in_kernel, h3, [g_attn, w_b, w_dt],
            [((tp, SSD_INNER), F32, "row"), ((tp, SSD_CONV_CH), F32, "row"),
             ((tp, 2 * SSD_HEADS), F32, "row"), ((2 * SSD_HEADS, tp), F32, "col")], tm_row, "ssd_in")
        w8 = jnp.pad(ssd_conv_w[l].astype(F32), ((0, 8 - SSD_CONV), (0, 0)))
        xs_b, bc_b = _ssd_conv(xbc, w8, _row(ssd_conv_b[l]), tm_in)
        dt_bias = ssd_dt_bias[l].astype(F32).reshape(-1)
        a_neg = -jnp.exp(ssd_a_log[l].astype(F32)).reshape(-1)
        y_bf, y_br = _ssd_scan(xs_b, bc_b, dt, dtT, [_row(dt_bias), _col(dt_bias), _row(a_neg), _col(a_neg)])

        n_d = 2 * MLSTM_HEADS * MLSTM_QK + 2 * MLSTM_HEADS * MLSTM_V + 4 * MLSTM_HEADS
        w_d = pad_cols(cols(wl, 9, 14), -(-n_d // LANE) * LANE).astype(BF16)
        w_g = cols(wl, 13, 14).T.astype(BF16)
        q_d, k_d, v_d, o_d, gt, gtT = _in_proj_call(
            _mlstm_in_kernel, h3, [g_attn, w_d, w_g],
            [((tp, MLSTM_HEADS * MLSTM_QK), F32, "row"), ((tp, MLSTM_HEADS * MLSTM_QK), F32, "row"),
             ((tp, MLSTM_HEADS * MLSTM_V), F32, "row"), ((tp, MLSTM_HEADS * MLSTM_V), F32, "row"),
             ((tp, 4 * MLSTM_HEADS), F32, "row"), ((4 * MLSTM_HEADS, tp), F32, "col")], tm_row, "mlstm_in")
        gate_bias = jnp.concatenate([mlstm_i_bias[l].reshape(-1), mlstm_f_bias[l].reshape(-1)]).astype(F32)
        h_df, h_dr = _mlstm_scan(q_d, k_d, v_d, gt, gtT, [_row(gate_bias), _col(gate_bias)])

        flat = lambda a: a.reshape(b * tp, a.shape[2])
        mixer_outs = [flat(a) for a in (y_a, y_bf, y_br, xs_b, z_b, y_c, h_df, h_dr, o_d)]
        out_consts = [_row(mla_out_g[l]), _row(jnp.repeat(ssd_d[l], SSD_HEAD_DIM)), _row(ssd_norm_g[l]),
                      _row(mlstm_out_g[l])]
        h = _out_proj(mixer_outs, h, out_consts, w_out[l].astype(BF16), tp=tp, seq=seq, tm=tm_flat)
        f = w_ffn_out.shape[1]
        last = l == depth - 1
        h = _ffn(h.reshape(b, tp, d), _row(ffn_norm_g[l]), w_ffn_in[l][:, :f].astype(BF16),
                 w_ffn_in[l][:, f:].astype(BF16), w_ffn_out[l].astype(BF16),
                 t_out=seq if last else tp, tm=tm_out if last else tm_in)
        h = h.reshape(-1, d)

    return h.reshape(b, seq, d).astype(x.dtype)
```

```python
import functools
import math

import jax
import jax.numpy as jnp
from jax import lax
from jax.experimental import pallas as pl
from jax.experimental.pallas import tpu as pltpu

N_META = 16
ROPE_THETA = 500000.0
EPS = 1e-6
CHUNK = 128
N_PAD = CHUNK - N_META
NEG = -1e30
LOG2E = 1.4426950408889634

MLA_HEADS, MLA_NOPE, MLA_ROPE, MLA_V = 8, 64, 32, 64
MLA_QK = MLA_NOPE + MLA_ROPE
MLA_Q_RANK, MLA_KV_RANK = 384, 256
SSD_HEADS, SSD_HEAD_DIM, SSD_GROUPS, SSD_STATE, SSD_CONV = 8, 64, 2, 64, 5
SSD_INNER = SSD_HEADS * SSD_HEAD_DIM
SSD_CONV_CH = SSD_INNER + 2 * SSD_GROUPS * SSD_STATE
DIFF_HEADS, DIFF_QK = 4, 64
DIFF_V = 2 * DIFF_QK
DIFF_ROPE = DIFF_QK // 4
MLSTM_HEADS, MLSTM_QK, MLSTM_V = 4, 64, 128
HEAD_PAD = 128
LANE = 128
Q_STRIP = 256
SSD_SKEW, MLSTM_SKEW = 4, 3
VMEM_LIMIT = 52 * 1024 * 1024

F32 = jnp.float32
BF16 = jnp.bfloat16
EXP_DTYPE = jnp.bfloat16


def _dot(a, b):
    return jnp.dot(a, b, preferred_element_type=F32)


def _dot_nt(a, b):
    return lax.dot_general(a, b, (((1,), (1,)), ((), ())), preferred_element_type=F32)


def _rms_rows(x, g):
    ms = jnp.mean(x * x, axis=-1, keepdims=True)
    return x * lax.rsqrt(ms + EPS) * g


def _split3(a):
    hi = a.astype(BF16)
    r1 = a - hi.astype(F32)
    mid = r1.astype(BF16)
    lo = (r1 - mid.astype(F32)).astype(BF16)
    return hi, mid, lo


def _cumsum_cols(tri, a):
    hi, mid, lo = _split3(a)
    return _dot(tri, hi) + _dot(tri, mid) + _dot(tri, lo)


def _cumsum_rows(a, tri):
    hi, mid, lo = _split3(a)
    return _dot(hi, tri) + _dot(mid, tri) + _dot(lo, tri)


def _softplus(x):
    return jnp.maximum(x, 0.0) + jnp.log(1.0 + jnp.exp(-jnp.abs(x)))


def _log_sigmoid(x):
    return jnp.minimum(x, 0.0) - jnp.log(1.0 + jnp.exp(-jnp.abs(x)))


def _sigmoid(x):
    return 1.0 / (1.0 + jnp.exp(-x))


def _pick_tile(n, candidates):
    for c in candidates:
        if n % c == 0:
            return c
    raise ValueError(f"no tile in {candidates} divides {n}")


def _params(*sem):
    return pltpu.CompilerParams(dimension_semantics=sem, vmem_limit_bytes=VMEM_LIMIT)


def _norm_rope_t(blk, g_col, cos, sin, n_real):
    r = cos.shape[0]
    ms = jnp.sum(blk * blk, axis=0, keepdims=True) * (1.0 / n_real)
    y = blk * lax.rsqrt(ms + EPS) * g_col
    x1, x2, rest = y[:r], y[r:2 * r], y[2 * r:]
    return jnp.concatenate([x1 * cos - x2 * sin, x2 * cos + x1 * sin, rest], axis=0)


def _mla_in_kernel(h_ref, g_ref, wa_ref, gq_ref, gkv_ref, wuq_ref, wuk_ref, wuv_ref, wkr_ref,
                   qhg_ref, khg_ref, cos_ref, sin_ref, qT_ref, k_ref, vT_ref):
    hn = _rms_rows(h_ref[0], g_ref[...]).astype(BF16)
    acc = _dot(hn, wa_ref[...])
    cqn = _rms_rows(acc[:, :MLA_Q_RANK], gq_ref[...]).astype(BF16)
    ckvn = _rms_rows(acc[:, MLA_Q_RANK:], gkv_ref[...]).astype(BF16)
    qT = _dot_nt(wuq_ref[...], cqn)
    knT = _dot_nt(wuk_ref[...], ckvn)
    vT = _dot_nt(wuv_ref[...], ckvn)
    krT = _dot_nt(wkr_ref[...], hn)
    cos, sin = cos_ref[...], sin_ref[...]
    t = krT.shape[1]
    zpad = jnp.zeros((HEAD_PAD - MLA_QK, t), F32)
    q_scale = (MLA_QK ** -0.5) * LOG2E
    for h in range(MLA_HEADS):
        qb = _norm_rope_t(qT[h * HEAD_PAD:(h + 1) * HEAD_PAD], qhg_ref[...], cos, sin, MLA_QK)
        qT_ref[0, h * HEAD_PAD:(h + 1) * HEAD_PAD, :] = (qb * q_scale).astype(BF16)
        kb = jnp.concatenate([krT, knT[h * MLA_NOPE:(h + 1) * MLA_NOPE], zpad], axis=0)
        kb = _norm_rope_t(kb, khg_ref[...], cos, sin, MLA_QK)
        k_ref[0, h] = kb.T.astype(BF16)
    vT_ref[0] = vT.astype(BF16)


def _diff_in_kernel(h_ref, g_ref, wc_ref, qhg_ref, khg_ref, cos_ref, sin_ref, qT_ref, k_ref, vT_ref):
    hn = _rms_rows(h_ref[0], g_ref[...]).astype(BF16)
    pT = _dot_nt(wc_ref[...], hn)
    cos, sin = cos_ref[...], sin_ref[...]
    t = pT.shape[1]
    zpad = jnp.zeros((HEAD_PAD - DIFF_QK, t), F32)
    nq = 2 * DIFF_HEADS * DIFF_QK
    q_scale = (DIFF_QK ** -0.5) * LOG2E
    for h in range(2 * DIFF_HEADS):
        qb = jnp.concatenate([pT[h * DIFF_QK:(h + 1) * DIFF_QK], zpad], axis=0)
        qb = _norm_rope_t(qb, qhg_ref[...], cos, sin, DIFF_QK)
        qT_ref[0, h * HEAD_PAD:(h + 1) * HEAD_PAD, :] = (qb * q_scale).astype(BF16)
        kb = jnp.concatenate([pT[nq + h * DIFF_QK:nq + (h + 1) * DIFF_QK], zpad], axis=0)
        kb = _norm_rope_t(kb, khg_ref[...], cos, sin, DIFF_QK)
        k_ref[0, h] = kb.T.astype(BF16)
    vT_ref[0] = pT[2 * nq:].astype(BF16)


def _ssd_in_kernel(h_ref, g_ref, wb_ref, wdt_ref, z_ref, xbc_ref, dt_ref, dtT_ref):
    hn = _rms_rows(h_ref[0], g_ref[...]).astype(BF16)
    acc = _dot(hn, wb_ref[...])
    z_ref[0] = acc[:, :SSD_INNER].astype(z_ref.dtype)
    xbc_ref[0] = acc[:, SSD_INNER:SSD_INNER + SSD_CONV_CH]
    dt_ref[0] = acc[:, SSD_INNER + SSD_CONV_CH:SSD_INNER + SSD_CONV_CH + 2 * SSD_HEADS]
    dtT_ref[0] = _dot_nt(wdt_ref[...], hn)


def _mlstm_in_kernel(h_ref, g_ref, wd_ref, wg_ref, q_ref, k_ref, v_ref, o_ref, gt_ref, gtT_ref):
    hn = _rms_rows(h_ref[0], g_ref[...]).astype(BF16)
    acc = _dot(hn, wd_ref[...])
    nqk = MLSTM_HEADS * MLSTM_QK
    nv = MLSTM_HEADS * MLSTM_V
    q_ref[0] = acc[:, :nqk]
    k_ref[0] = acc[:, nqk:2 * nqk]
    v_ref[0] = acc[:, 2 * nqk:2 * nqk + nv]
    o_ref[0] = acc[:, 2 * nqk + nv:2 * nqk + 2 * nv].astype(o_ref.dtype)
    gt_ref[0] = acc[:, 2 * nqk + 2 * nv:2 * nqk + 2 * nv + 4 * MLSTM_HEADS]
    gtT_ref[0] = _dot_nt(wg_ref[...], hn)


def _full(shape):
    nd = len(shape)
    return pl.BlockSpec(shape, lambda *_: (0,) * nd)


def _in_proj_call(body, h, consts, outs, tm, name):
    b, tp, d = h.shape
    in_specs = [pl.BlockSpec((1, tm, d), lambda i, j: (i, j, 0))]
    for c in consts:
        if isinstance(c, tuple):
            in_specs.append(pl.BlockSpec((c[0].shape[0], tm), lambda i, j: (0, j)))
        else:
            in_specs.append(_full(c.shape))
    out_shapes, out_specs = [], []
    for shape, dtype, kind in outs:
        out_shapes.append(jax.ShapeDtypeStruct((b,) + shape, dtype))
        if kind == "row":
            out_specs.append(pl.BlockSpec((1, tm, shape[1]), lambda i, j: (i, j, 0)))
        elif kind == "col":
            out_specs.append(pl.BlockSpec((1, shape[0], tm), lambda i, j: (i, 0, j)))
        else:
            out_specs.append(pl.BlockSpec((1, shape[0], tm, shape[2]), lambda i, j: (i, 0, j, 0)))
    args = [h] + [c[0] if isinstance(c, tuple) else c for c in consts]
    return pl.pallas_call(
        body, grid=(b, tp // tm), in_specs=in_specs, out_specs=out_specs, out_shape=out_shapes,
        compiler_params=_params("parallel", "parallel"), name=name)(*args)


def _attn_kernel(*refs, diff, seq, tk, out_scale):
    if diff:
        qT_ref, k_ref, vT_ref, lam_ref, og_ref, o_ref, s_scr, st_scr, m_scr, acc_scr = refs
    else:
        qT_ref, k_ref, vT_ref, o_ref, s_scr, st_scr, m_scr, acc_scr = refs
    dv = DIFF_V if diff else MLA_V
    tq = qT_ref.shape[2]
    n_chunks = seq // tk
    assert n_chunks % 2 == 0
    m_scr[...] = jnp.full(m_scr.shape, NEG, F32)
    acc_scr[...] = jnp.zeros(acc_scr.shape, F32)

    def stage(nxt, cur):
        k_n, v_aug = [], []
        for hh in range(2):
            if nxt is not None:
                k_n.append(k_ref[0, hh, pl.ds(nxt[1], nxt[2]), :])
            if cur is not None:
                v_lo = 0 if diff else hh * dv
                ones = jnp.ones((16, cur[2]), BF16)
                v_aug.append(jnp.concatenate([vT_ref[0, v_lo:v_lo + dv, pl.ds(cur[1], cur[2])], ones],
                                             axis=0))
        pending = None
        for j0 in range(0, tq, Q_STRIP):
            w = min(Q_STRIP, tq - j0)
            for hh in range(2):
                if cur is not None:
                    src = st_scr.at[hh] if cur[0] is None else s_scr.at[hh, cur[0]]
                    s = src[:, j0:j0 + w]
                    m_old = m_scr[hh, 0:1, j0:j0 + w]
                    m_new = jnp.maximum(m_old, jnp.max(s, axis=0, keepdims=True))
                    alpha = jnp.exp2(m_old - m_new)
                    p = jnp.exp2((s - m_new).astype(EXP_DTYPE)).astype(BF16)
                    m_scr[hh, 0:1, j0:j0 + w] = m_new
                if nxt is not None:
                    dst = st_scr.at[hh] if nxt[0] is None else s_scr.at[hh, nxt[0]]
                    s_n = _dot(k_n[hh], qT_ref[0, hh * HEAD_PAD:(hh + 1) * HEAD_PAD, j0:j0 + w])
                    if nxt[3]:
                        row = lax.broadcasted_iota(jnp.int32, s_n.shape, 0)
                        s_n = jnp.where(row >= N_PAD, s_n, NEG)
                    dst[:, j0:j0 + w] = s_n
                if cur is not None:
                    if pending is not None:
                        ph, pj, pw, pa, pp = pending
                        acc_scr[ph, :, pj:pj + pw] = pa * acc_scr[ph, :, pj:pj + pw] + _dot(v_aug[ph], pp)
                    pending = (hh, j0, w, alpha, p)
        if pending is not None:
            ph, pj, pw, pa, pp = pending
            acc_scr[ph, :, pj:pj + pw] = pa * acc_scr[ph, :, pj:pj + pw] + _dot(v_aug[ph], pp)

    stage((0, 0, tk, False), None)

    def body(i, carry):
        off = pl.multiple_of(2 * i * tk, 2 * tk)
        stage((1, off + tk, tk, False), (0, off, tk))
        stage((0, off + 2 * tk, tk, False), (1, off + tk, tk))
        return carry

    lax.fori_loop(0, n_chunks // 2 - 1, body, 0)
    off = (n_chunks - 2) * tk
    stage((1, off + tk, tk, False), (0, off, tk))
    stage((None, seq, CHUNK, True), (1, off + tk, tk))
    stage(None, (None, seq, CHUNK))
    outs = []
    for hh in range(2):
        acc = acc_scr[hh]
        outs.append(acc[:dv] / acc[dv:dv + 1])
    if diff:
        o = outs[0] - lam_ref[0:1, 0:1] * outs[1]
        ms = jnp.mean(o * o, axis=0, keepdims=True)
        o = o * lax.rsqrt(ms + EPS) * (og_ref[...] * out_scale)
    else:
        o = jnp.concatenate(outs, axis=0)
    o_ref[0] = o.T.astype(o_ref.dtype)


def _attention(qT, k, vT, extra, *, diff, seq, tq, tk, out_scale, name):
    b, _, tp = qT.shape
    groups = qT.shape[1] // (2 * HEAD_PAD)
    dv = DIFF_V if diff else MLA_V
    v_rows = dv if diff else 2 * dv
    in_specs = [
        pl.BlockSpec((1, 2 * HEAD_PAD, tq), lambda i, g, j: (i, g, j)),
        pl.BlockSpec((1, 2, tp, HEAD_PAD), lambda i, g, j: (i, g, 0, 0)),
        pl.BlockSpec((1, v_rows, tp), lambda i, g, j: (i, g, 0)),
    ] + [_full(e.shape) for e in extra]
    return pl.pallas_call(
        functools.partial(_attn_kernel, diff=diff, seq=seq, tk=tk, out_scale=out_scale),
        grid=(b, groups, tp // tq),
        in_specs=in_specs,
        out_specs=pl.BlockSpec((1, tq, LANE), lambda i, g, j: (i, j, g)),
        out_shape=jax.ShapeDtypeStruct((b, tp, groups * LANE), BF16),
        scratch_shapes=[pltpu.VMEM((2, 2, tk, tq), F32), pltpu.VMEM((2, CHUNK, tq), F32),
                        pltpu.VMEM((2, 8, tq), F32), pltpu.VMEM((2, dv + 16, tq), F32)],
        compiler_params=_params("parallel", "parallel", "arbitrary"), name=name)(qT, k, vT, *extra)


def _conv_kernel(x_ref, prev_ref, next_ref, w_ref, b_ref, xs_ref, bc_ref, scr):
    tc = x_ref.shape[1]
    scr[0:8, :] = prev_ref[0]
    scr[8:8 + tc, :] = x_ref[0]
    scr[8 + tc:16 + tc, :] = next_ref[0]
    acc = jnp.broadcast_to(b_ref[...], (tc, b_ref.shape[1]))
    for j in range(SSD_CONV):
        acc = acc + w_ref[j:j + 1, :] * scr[8 - SSD_CONV // 2 + j:8 - SSD_CONV // 2 + j + tc, :]
    act = acc * _sigmoid(acc)
    xs_ref[0] = act[:, :SSD_INNER]
    bc_ref[0] = act[:, SSD_INNER:]


def _ssd_conv(xbc, w8, bias, tc):
    b, tp, c = xbc.shape
    nb8 = tp // 8
    r8 = tc // 8
    row_spec = lambda n: pl.BlockSpec((1, tc, n), lambda i, j: (i, j, 0))
    return pl.pallas_call(
        _conv_kernel, grid=(b, tp // tc),
        in_specs=[row_spec(c),
                  pl.BlockSpec((1, 8, c), lambda i, j: (i, (j * r8 + nb8 - 1) % nb8, 0)),
                  pl.BlockSpec((1, 8, c), lambda i, j: (i, ((j + 1) * r8) % nb8, 0)),
                  _full(w8.shape), _full(bias.shape)],
        out_specs=[row_spec(SSD_INNER), row_spec(c - SSD_INNER)],
        out_shape=[jax.ShapeDtypeStruct((b, tp, SSD_INNER), F32),
                   jax.ShapeDtypeStruct((b, tp, c - SSD_INNER), F32)],
        scratch_shapes=[pltpu.VMEM((tc + 16, c), F32)],
        compiler_params=_params("parallel", "parallel"), name="ssd_conv")(xbc, xbc, xbc, w8, bias)


def _tri_masks(reverse):
    row = lax.broadcasted_iota(jnp.int32, (CHUNK, CHUNK), 0)
    col = lax.broadcasted_iota(jnp.int32, (CHUNK, CHUNK), 1)
    keep = (col >= row) if reverse else (col <= row)
    tri_c = keep.astype(BF16)
    tri_r = ((row >= col) if reverse else (row <= col)).astype(BF16)
    return keep, tri_c, tri_r, row, col


def _ssd_direction(xs_ref, bc_ref, dt_ref, dtT_ref, bias_r, bias_c, a_r, a_c, y_ref, st_ref, *,
                   bi, reverse, is_meta):
    d = 1 if reverse else 0
    h8 = SSD_HEADS
    keep, tri_c, tri_r, row, col = _tri_masks(reverse)
    bm = bc_ref[bi, :, :CHUNK]
    cm = bc_ref[bi, :, CHUNK:]

    dt_c = _softplus(dt_ref[bi][:, d * h8:(d + 1) * h8] + bias_r[:, d * h8:(d + 1) * h8])
    dt_r = _softplus(dtT_ref[bi][d * h8:(d + 1) * h8, :] + bias_c[d * h8:(d + 1) * h8, :])
    pad_c = jnp.logical_and(is_meta, row[:, :h8] < N_PAD)
    pad_r = jnp.logical_and(is_meta, col[:h8, :] < N_PAD)
    dt_c = jnp.where(pad_c, 0.0, dt_c)
    dt_r = jnp.where(pad_r, 0.0, dt_r)
    a_col = dt_c * a_r[:, d * h8:(d + 1) * h8]
    a_row = dt_r * a_c[d * h8:(d + 1) * h8, :]
    cs_c = _cumsum_cols(tri_c, a_col)
    cs_r = _cumsum_rows(a_row, tri_r)
    last = 0 if reverse else CHUNK - 1
    tot_r = cs_r[:, last:last + 1]
    tot_c = cs_c[last:last + 1, :]

    lane_lo = col < SSD_STATE
    row_lo = row < SSD_STATE
    blockdiag = jnp.logical_not(jnp.logical_xor(lane_lo, row_lo))
    cm_sw = pltpu.roll(cm, SSD_STATE, 1)
    c_dup = (jnp.where(lane_lo, cm, cm_sw), jnp.where(lane_lo, cm_sw, cm))
    g_mat = (_dot_nt(jnp.where(lane_lo, cm, 0.0).astype(BF16), bm.astype(BF16)),
             _dot_nt(jnp.where(lane_lo, 0.0, cm).astype(BF16), bm.astype(BF16)))
    bT = bm.T

    heads_per_group = SSD_HEADS // SSD_GROUPS

    def pair_chain(j):
        g = (2 * j) // heads_per_group
        h0, h1 = 2 * j, 2 * j + 1
        xp = xs_ref[bi, :, j * CHUNK:(j + 1) * CHUNK]
        dtp = jnp.where(lane_lo, dt_c[:, h0:h0 + 1], dt_c[:, h1:h1 + 1])
        xdt = xp * dtp
        cs_b = [jnp.broadcast_to(cs_c[:, h:h + 1], (CHUNK, CHUNK)) for h in (h0, h1)]
        yield
        parts = []
        for h, cs_h in zip((h0, h1), cs_b):
            diff_ = cs_h - cs_r[h:h + 1, :]
            parts.append((g_mat[g] * jnp.exp(jnp.where(keep, diff_, NEG))).astype(BF16))
        parts.append((c_dup[g] * jnp.exp(jnp.where(lane_lo, cs_b[0], cs_b[1]))).astype(BF16))
        yield
        lhs = jnp.concatenate(parts, axis=1)
        s_old = st_ref[j]
        rhs = jnp.concatenate([jnp.where(lane_lo, xdt, 0.0).astype(BF16),
                               jnp.where(lane_lo, 0.0, xdt).astype(BF16),
                               s_old.astype(BF16)], axis=0)
        y_pair = _dot(lhs, rhs)
        btg = bT[g * SSD_STATE:(g + 1) * SSD_STATE, :]
        bd = jnp.concatenate([btg * jnp.exp(tot_r[h0:h0 + 1, :] - cs_r[h0:h0 + 1, :]),
                              btg * jnp.exp(tot_r[h1:h1 + 1, :] - cs_r[h1:h1 + 1, :])], axis=0)
        s_upd = _dot(bd.astype(BF16), xdt.astype(BF16))
        yield
        carry = jnp.where(row_lo, jnp.exp(tot_c[:, h0:h0 + 1]), jnp.exp(tot_c[:, h1:h1 + 1]))
        st_ref[j] = s_old * carry + jnp.where(blockdiag, s_upd, 0.0)
        y_ref[bi, :, j * CHUNK:(j + 1) * CHUNK] = y_pair.astype(y_ref.dtype)

    return [pair_chain(j) for j in range(SSD_HEADS // 2)]


def _interleave(chains, period=1):
    pending = [(i % period, c) for i, c in enumerate(chains)]
    rnd = 0
    while pending:
        alive = []
        for delay, c in pending:
            if rnd >= delay:
                try:
                    next(c)
                except StopIteration:
                    continue
            alive.append((delay, c))
        pending = alive
        rnd += 1


def _ssd_kernel(xs_f, bc_f, dt_f, dtT_f, xs_r, bc_r, dt_r, dtT_r, bias_r, bias_c, a_r, a_c,
                yf_ref, yr_ref, st_ref, *, n_chunks):
    step = pl.program_id(1)
    n_pairs = SSD_HEADS // 2

    @pl.when(step == 0)
    def _():
        st_ref[...] = jnp.zeros(st_ref.shape, F32)

    scans = []
    for bi in range(xs_f.shape[0]):
        lo = 2 * bi * n_pairs
        scans.append(_ssd_direction(xs_f, bc_f, dt_f, dtT_f, bias_r, bias_c, a_r, a_c, yf_ref,
                                    st_ref.at[lo:lo + n_pairs], bi=bi, reverse=False, is_meta=step == 0))
        scans.append(_ssd_direction(xs_r, bc_r, dt_r, dtT_r, bias_r, bias_c, a_r, a_c, yr_ref,
                                    st_ref.at[lo + n_pairs:lo + 2 * n_pairs], bi=bi, reverse=True,
                                    is_meta=step == n_chunks - 1))
    _interleave([c for group in zip(*scans) for c in group], period=SSD_SKEW)


def _chunk_order(n_chunks, reverse):
    if reverse:
        return lambda c: (2 * n_chunks - 2 - c) % n_chunks
    return lambda c: (c + n_chunks - 1) % n_chunks


def _scan_specs(arrays, n_chunks, reverse, nb):
    order = _chunk_order(n_chunks, reverse)
    specs = []
    for a, transposed in arrays:
        if transposed:
            specs.append(pl.BlockSpec((nb, a.shape[1], CHUNK), lambda i, s: (i, 0, order(s))))
        else:
            specs.append(pl.BlockSpec((nb, CHUNK, a.shape[2]), lambda i, s: (i, order(s), 0)))
    return specs


def _scan_batch(b):
    return 2 if b % 2 == 0 else 1


def _ssd_scan(xs, bc, dt, dtT, consts):
    b, tp, _ = xs.shape
    n_chunks = tp // CHUNK
    nb = _scan_batch(b)
    arrays = [(xs, False), (bc, False), (dt, False), (dtT, True)]
    in_specs = (_scan_specs(arrays, n_chunks, False, nb) + _scan_specs(arrays, n_chunks, True, nb)
                + [_full(c.shape) for c in consts])
    out_specs = (_scan_specs([(xs, False)], n_chunks, False, nb)
                 + _scan_specs([(xs, False)], n_chunks, True, nb))
    out_shape = [jax.ShapeDtypeStruct((b, tp, SSD_INNER), BF16)] * 2
    return pl.pallas_call(
        functools.partial(_ssd_kernel, n_chunks=n_chunks),
        grid=(b // nb, n_chunks), in_specs=in_specs, out_specs=out_specs, out_shape=out_shape,
        scratch_shapes=[pltpu.VMEM((nb * SSD_HEADS, CHUNK, CHUNK), F32)],
        compiler_params=_params("parallel", "arbitrary"),
        name="ssd_scan")(xs, bc, dt, dtT, xs, bc, dt, dtT, *consts)


def _mlstm_direction(q_ref, k_ref, v_ref, gt_ref, gtT_ref, gb_r, gb_c, y_ref, st_ref, m_ref, *,
                     bi, reverse, is_meta):
    d = 1 if reverse else 0
    nh = MLSTM_HEADS
    keep, tri_c, tri_r, row, col = _tri_masks(reverse)
    gt = gt_ref[bi] + gb_r[...]
    gtT = gtT_ref[bi] + gb_c[...]
    i_lo, f_lo = d * nh, 2 * nh + d * nh
    pad_c = jnp.logical_and(is_meta, row[:, :nh] < N_PAD)
    pad_r = jnp.logical_and(is_meta, col[:nh, :] < N_PAD)
    ig_r = jnp.where(pad_r, NEG, gtT[i_lo:i_lo + nh, :])
    fg_c = jnp.where(pad_c, 0.0, _log_sigmoid(gt[:, f_lo:f_lo + nh]))
    fg_r = jnp.where(pad_r, 0.0, _log_sigmoid(gtT[f_lo:f_lo + nh, :]))
    b_c = _cumsum_cols(tri_c, fg_c)
    b_r = _cumsum_rows(fg_r, tri_r)
    last = 0 if reverse else CHUNK - 1

    lane_lo = col < MLSTM_QK
    row_lo = row < MLSTM_QK
    ones = jnp.ones((CHUNK, MLSTM_V), BF16)
    scale = MLSTM_QK ** -0.5

    def head_chain(h):
        pair = h // 2
        lo = (h % 2) == 0
        qp = q_ref[bi, :, pair * CHUNK:(pair + 1) * CHUNK]
        kp = k_ref[bi, :, pair * CHUNK:(pair + 1) * CHUNK]
        qm = (jnp.where(lane_lo if lo else jnp.logical_not(lane_lo), qp, 0.0) * scale).astype(BF16)
        kT = jnp.where(row_lo if lo else jnp.logical_not(row_lo), kp.T, 0.0)
        v_aug = jnp.concatenate([v_ref[bi, :, h * MLSTM_V:(h + 1) * MLSTM_V].astype(BF16), ones], axis=1)
        m_st = m_ref[h][0:1, 0:1]
        bc = b_c[:, h:h + 1]
        br = b_r[h:h + 1, :]
        ir = ig_r[h:h + 1, :]
        s_raw = _dot_nt(qm, kp.astype(BF16))
        c_st = st_ref[h]
        inter_mm = _dot(qm, c_st.astype(BF16))
        yield
        dmat = jnp.where(keep, ir - br, -jnp.inf)
        m_rel = jnp.maximum(jnp.max(dmat, axis=1, keepdims=True), m_st)
        tot = br[:, last:last + 1]
        d_last = tot - br + ir
        m_new = jnp.maximum(tot + m_st, jnp.max(d_last, axis=1, keepdims=True))
        yield
        w_intra = jnp.exp(dmat - m_rel)
        w_inter = jnp.exp(m_st - m_rel)
        w_s = jnp.exp(d_last - m_new)
        w_prev = jnp.exp(tot + m_st - m_new)
        yield
        s = s_raw * w_intra
        intra_mm = _dot(s.astype(BF16), v_aug)
        upd = _dot((kT * w_s).astype(BF16), v_aug)
        yield
        comb = intra_mm + w_inter * inter_mm
        num = comb[:, :MLSTM_V]
        den = jnp.maximum(jnp.abs(comb[:, MLSTM_V:]), jnp.exp(-(bc + m_rel)))
        y_ref[bi, :, h * MLSTM_V:(h + 1) * MLSTM_V] = (num / den).astype(y_ref.dtype)
        st_ref[h] = w_prev * c_st + upd
        m_ref[h] = jnp.broadcast_to(m_new, m_ref.shape[1:])

    return [head_chain(h) for h in range(nh)]


def _mlstm_kernel(q_f, k_f, v_f, gt_f, gtT_f, q_r, k_r, v_r, gt_r, gtT_r, gb_r, gb_c,
                  yf_ref, yr_ref, st_ref, m_ref, *, n_chunks):
    step = pl.program_id(1)
    nh = MLSTM_HEADS

    @pl.when(step == 0)
    def _():
        st_ref[...] = jnp.zeros(st_ref.shape, F32)
        m_ref[...] = jnp.full(m_ref.shape, NEG, F32)

    scans = []
    for bi in range(q_f.shape[0]):
        lo = 2 * bi * nh
        scans.append(_mlstm_direction(q_f, k_f, v_f, gt_f, gtT_f, gb_r, gb_c, yf_ref, st_ref.at[lo:lo + nh],
                                      m_ref.at[lo:lo + nh], bi=bi, reverse=False, is_meta=step == 0))
        scans.append(_mlstm_direction(q_r, k_r, v_r, gt_r, gtT_r, gb_r, gb_c, yr_ref,
                                      st_ref.at[lo + nh:lo + 2 * nh], m_ref.at[lo + nh:lo + 2 * nh],
                                      bi=bi, reverse=True, is_meta=step == n_chunks - 1))
    _interleave([c for group in zip(*scans) for c in group], period=MLSTM_SKEW)


def _mlstm_scan(q, k, v, gt, gtT, consts):
    b, tp, _ = q.shape
    n_chunks = tp // CHUNK
    nb = _scan_batch(b)
    arrays = [(q, False), (k, False), (v, False), (gt, False), (gtT, True)]
    in_specs = (_scan_specs(arrays, n_chunks, False, nb) + _scan_specs(arrays, n_chunks, True, nb)
                + [_full(c.shape) for c in consts])
    out_specs = (_scan_specs([(v, False)], n_chunks, False, nb)
                 + _scan_specs([(v, False)], n_chunks, True, nb))
    out_shape = [jax.ShapeDtypeStruct(v.shape, BF16)] * 2
    return pl.pallas_call(
        functools.partial(_mlstm_kernel, n_chunks=n_chunks),
        grid=(b // nb, n_chunks), in_specs=in_specs, out_specs=out_specs, out_shape=out_shape,
        scratch_shapes=[pltpu.VMEM((nb * 2 * MLSTM_HEADS, CHUNK, 2 * MLSTM_V), F32),
                        pltpu.VMEM((nb * 2 * MLSTM_HEADS, 8, LANE), F32)],
        compiler_params=_params("parallel", "arbitrary"),
        name="mlstm_scan")(q, k, v, gt, gtT, q, k, v, gt, gtT, *consts)


def _out_kernel(ya_ref, yf_ref, yr_ref, xs_ref, z_ref, yc_ref, hf_ref, hr_ref, og_ref, h_ref,
                ag_ref, dsk_ref, ng_ref, mg_ref, w_ref, o_ref, *, tp, seq):
    tm = h_ref.shape[0]
    ya = _rms_rows(ya_ref[...].astype(F32), ag_ref[...])
    z = z_ref[...].astype(F32)
    yb = ((yf_ref[...].astype(F32) + yr_ref[...].astype(F32) + dsk_ref[...] * xs_ref[...])
          * (z * _sigmoid(z)))
    parts = [ya.astype(BF16), _rms_rows(yb, ng_ref[...]).astype(BF16), yc_ref[...].astype(BF16)]
    for hd in range(MLSTM_HEADS):
        lo, hi = hd * MLSTM_V, (hd + 1) * MLSTM_V
        hsum = hf_ref[:, lo:hi].astype(F32) + hr_ref[:, lo:hi].astype(F32)
        gate = _sigmoid(og_ref[:, lo:hi].astype(F32))
        parts.append((gate * _rms_rows(hsum, mg_ref[...])).astype(BF16))
    out = h_ref[...] + _dot(jnp.concatenate(parts, axis=1), w_ref[...])
    t = (pl.program_id(0) * tm + lax.broadcasted_iota(jnp.int32, (tm, 1), 0)) % tp
    is_pad = jnp.logical_and(t >= seq, t < seq + N_PAD)
    o_ref[...] = jnp.where(is_pad, 0.0, out)


def _out_proj(mixer_outs, h, consts, w, *, tp, seq, tm):
    rows, d = h.shape
    row_spec = lambda c: pl.BlockSpec((tm, c), lambda i: (i, 0))
    return pl.pallas_call(
        functools.partial(_out_kernel, tp=tp, seq=seq), grid=(rows // tm,),
        in_specs=([row_spec(a.shape[1]) for a in mixer_outs] + [row_spec(d)]
                  + [_full(c.shape) for c in consts] + [_full(w.shape)]),
        out_specs=row_spec(d), out_shape=jax.ShapeDtypeStruct((rows, d), F32),
        compiler_params=_params("parallel"), name="out_proj")(*mixer_outs, h, *consts, w)


def _ffn_kernel(h_ref, g_ref, wg_ref, wu_ref, wo_ref, o_ref, *, n_split):
    x = h_ref[0]
    hn = _rms_rows(x, g_ref[...]).astype(BF16)
    f = wg_ref.shape[1]
    tf = f // n_split
    acc = x
    for c in range(n_split):
        gate = _dot(hn, wg_ref[:, c * tf:(c + 1) * tf])
        up = _dot(hn, wu_ref[:, c * tf:(c + 1) * tf])
        act = (gate * _sigmoid(gate) * up).astype(BF16)
        acc = acc + _dot(act, wo_ref[c * tf:(c + 1) * tf, :])
    o_ref[0] = acc


def _ffn(h, g, wg, wu, wo, *, t_out, tm):
    b, _, d = h.shape
    f = wg.shape[1]
    n_split = 2 if (f // 2) % LANE == 0 else 1
    row_spec = pl.BlockSpec((1, tm, d), lambda i, j: (i, j, 0))
    resident = lambda a: pl.BlockSpec(a.shape, lambda i, j: (0, 0), pipeline_mode=pl.Buffered(1))
    return pl.pallas_call(
        functools.partial(_ffn_kernel, n_split=n_split), grid=(b, t_out // tm),
        in_specs=[row_spec, _full(g.shape), resident(wg), resident(wu), resident(wo)],
        out_specs=row_spec, out_shape=jax.ShapeDtypeStruct((b, t_out, d), F32),
        compiler_params=_params("parallel", "parallel"), name="ffn")(h, g, wg, wu, wo)


def _rope_tables_t(pos, rot_dim):
    inv = 1.0 / (ROPE_THETA ** (jnp.arange(0, rot_dim, 2, dtype=F32) / rot_dim))
    ang = pos[:, None] * inv[None, :]
    return jnp.cos(ang).T, jnp.sin(ang).T


def _col(v, n=None):
    v = v.astype(F32)
    if n is not None:
        v = jnp.pad(v, (0, n - v.shape[0]))
    return v[:, None]


def _row(v):
    return v.astype(F32)[None, :]


def kernel(x, meta_tokens, attn_norm_g, w_in, mla_q_norm_g, mla_kv_norm_g, mla_w_uq, mla_w_ukv, mla_q_head_g, mla_k_head_g, mla_out_g, ssd_conv_w, ssd_conv_b, ssd_dt_bias, ssd_a_log, ssd_d, ssd_norm_g, diff_q_head_g, diff_k_head_g, diff_lambda, diff_out_g, mlstm_i_bias, mlstm_f_bias, mlstm_out_g, w_out, ffn_norm_g, w_ffn_in, w_ffn_out):
    b, seq, d = x.shape
    depth = w_in.shape[0]
    tp = seq + CHUNK
    assert seq % CHUNK == 0
    tm_in = _pick_tile(tp, (640, 384, 128))
    tm_row = _pick_tile(tp, (1664, 640, 384, 128))
    tq = _pick_tile(tp, (1664, 640, 384, 128))
    tk = _pick_tile(seq, (512, 384, 256, 128))
    tm_flat = _pick_tile(b * tp, (512, 256, 128))
    tm_out = _pick_tile(seq, (512, 256, 128))

    meta = jnp.broadcast_to(meta_tokens[None].astype(x.dtype), (b, N_META, d))
    h = jnp.concatenate([x, jnp.zeros((b, N_PAD, d), x.dtype), meta], axis=1).reshape(b * tp, d)
    pos = jnp.concatenate([N_META + jnp.arange(seq, dtype=F32), jnp.zeros((N_PAD,), F32),
                           jnp.arange(N_META, dtype=F32)])
    cos_m, sin_m = _rope_tables_t(pos, MLA_ROPE)
    cos_d, sin_d = _rope_tables_t(pos, DIFF_ROPE)

    sizes = (MLA_Q_RANK, MLA_KV_RANK, MLA_ROPE, SSD_INNER, SSD_CONV_CH, 2 * SSD_HEADS,
             2 * DIFF_HEADS * DIFF_QK, 2 * DIFF_HEADS * DIFF_QK, DIFF_HEADS * DIFF_V,
             MLSTM_HEADS * MLSTM_QK, MLSTM_HEADS * MLSTM_QK, MLSTM_HEADS * MLSTM_V,
             MLSTM_HEADS * MLSTM_V, 2 * MLSTM_HEADS, 2 * MLSTM_HEADS)
    offs = [0]
    for s_ in sizes:
        offs.append(offs[-1] + s_)

    def cols(w, first, last):
        return w[:, offs[first]:offs[last + 1]]

    def pad_cols(w, n):
        return jnp.pad(w, ((0, 0), (0, n - w.shape[1])))

    for l in range(depth):
        lambda_init = 0.8 - 0.6 * math.exp(-0.3 * l)
        wl = w_in[l]
        h3 = h.reshape(b, tp, d)
        g_attn = _row(attn_norm_g[l])

        w_a = cols(wl, 0, 1).astype(BF16)
        w_kr = cols(wl, 2, 2).T.astype(BF16)
        w_uq = mla_w_uq[l].T.reshape(MLA_HEADS, MLA_QK, MLA_Q_RANK)
        w_uq = jnp.pad(w_uq, ((0, 0), (0, HEAD_PAD - MLA_QK), (0, 0))).reshape(MLA_HEADS * HEAD_PAD, MLA_Q_RANK)
        w_ukv = mla_w_ukv[l].T.reshape(MLA_HEADS, MLA_NOPE + MLA_V, MLA_KV_RANK)
        w_uk = w_ukv[:, :MLA_NOPE].reshape(MLA_HEADS * MLA_NOPE, MLA_KV_RANK)
        w_uv = w_ukv[:, MLA_NOPE:].reshape(MLA_HEADS * MLA_V, MLA_KV_RANK)
        qT_a, k_a, vT_a = _in_proj_call(
            _mla_in_kernel, h3,
            [g_attn, w_a, _row(mla_q_norm_g[l]), _row(mla_kv_norm_g[l]), w_uq.astype(BF16),
             w_uk.astype(BF16), w_uv.astype(BF16), w_kr, _col(mla_q_head_g[l], HEAD_PAD),
             _col(mla_k_head_g[l], HEAD_PAD), (cos_m,), (sin_m,)],
            [((MLA_HEADS * HEAD_PAD, tp), BF16, "col"), ((MLA_HEADS, tp, HEAD_PAD), BF16, "head"),
             ((MLA_HEADS * MLA_V, tp), BF16, "col")], tm_in, "mla_in")
        y_a = _attention(qT_a, k_a, vT_a, [], diff=False, seq=seq, tq=tq, tk=tk, out_scale=1.0,
                         name="mla_attn")

        w_c = cols(wl, 6, 8).T.astype(BF16)
        qT_c, k_c, vT_c = _in_proj_call(
            _diff_in_kernel, h3,
            [g_attn, w_c, _col(diff_q_head_g[l], HEAD_PAD), _col(diff_k_head_g[l], HEAD_PAD),
             (cos_d,), (sin_d,)],
            [((2 * DIFF_HEADS * HEAD_PAD, tp), BF16, "col"), ((2 * DIFF_HEADS, tp, HEAD_PAD), BF16, "head"),
             ((DIFF_HEADS * DIFF_V, tp), BF16, "col")], tm_in, "diff_in")
        lam = diff_lambda[l].astype(F32)
        lam_full = jnp.exp(jnp.sum(lam[0] * lam[1])) - jnp.exp(jnp.sum(lam[2] * lam[3])) + lambda_init
        y_c = _attention(qT_c, k_c, vT_c, [jnp.full((8, LANE), lam_full, F32), _col(diff_out_g[l])],
                         diff=True, seq=seq, tq=tq, tk=tk, out_scale=1.0 - lambda_init, name="diff_attn")

        n_b = SSD_INNER + SSD_CONV_CH + 2 * SSD_HEADS
        w_b = pad_cols(cols(wl, 3, 5), -(-n_b // LANE) * LANE).astype(BF16)
        w_dt = cols(wl, 5, 5).T.astype(BF16)
        z_b, xbc, dt, dtT = _in_proj_call(
            _ssd_in_kernel, h3, [g_attn, w_b, w_dt],
            [((tp, SSD_INNER), BF16, "row"), ((tp, SSD_CONV_CH), F32, "row"),
             ((tp, 2 * SSD_HEADS), F32, "row"), ((2 * SSD_HEADS, tp), F32, "col")], tm_row, "ssd_in")
        w8 = jnp.pad(ssd_conv_w[l].astype(F32), ((0, 8 - SSD_CONV), (0, 0)))
        xs_b, bc_b = _ssd_conv(xbc, w8, _row(ssd_conv_b[l]), tm_in)
        dt_bias = ssd_dt_bias[l].astype(F32).reshape(-1)
        a_neg = -jnp.exp(ssd_a_log[l].astype(F32)).reshape(-1)
        y_bf, y_br = _ssd_scan(xs_b, bc_b, dt, dtT, [_row(dt_bias), _col(dt_bias), _row(a_neg), _col(a_neg)])

        n_d = 2 * MLSTM_HEADS * MLSTM_QK + 2 * MLSTM_HEADS * MLSTM_V + 4 * MLSTM_HEADS
        w_d = pad_cols(cols(wl, 9, 14), -(-n_d // LANE) * LANE).astype(BF16)
        w_g = cols(wl, 13, 14).T.astype(BF16)
        q_d, k_d, v_d, o_d, gt, gtT = _in_proj_call(
            _mlstm_in_kernel, h3, [g_attn, w_d, w_g],
            [((tp, MLSTM_HEADS * MLSTM_QK), F32, "row"), ((tp, MLSTM_HEADS * MLSTM_QK), F32, "row"),
             ((tp, MLSTM_HEADS * MLSTM_V), F32, "row"), ((tp, MLSTM_HEADS * MLSTM_V), BF16, "row"),
             ((tp, 4 * MLSTM_HEADS), F32, "row"), ((4 * MLSTM_HEADS, tp), F32, "col")], tm_row, "mlstm_in")
        gate_bias = jnp.concatenate([mlstm_i_bias[l].reshape(-1), mlstm_f_bias[l].reshape(-1)]).astype(F32)
        h_df, h_dr = _mlstm_scan(q_d, k_d, v_d, gt, gtT, [_row(gate_bias), _col(gate_bias)])

        flat = lambda a: a.reshape(b * tp, a.shape[2])
        mixer_outs = [flat(a) for a in (y_a, y_bf, y_br, xs_b, z_b, y_c, h_df, h_dr, o_d)]
        out_consts = [_row(mla_out_g[l]), _row(jnp.repeat(ssd_d[l], SSD_HEAD_DIM)), _row(ssd_norm_g[l]),
                      _row(mlstm_out_g[l])]
        h = _out_proj(mixer_outs, h, out_consts, w_out[l].astype(BF16), tp=tp, seq=seq, tm=tm_flat)
        f = w_ffn_out.shape[1]
        last = l == depth - 1
        h = _ffn(h.reshape(b, tp, d), _row(ffn_norm_g[l]), w_ffn_in[l][:, :f].astype(BF16),
                 w_ffn_in[l][:, f:].astype(BF16), w_ffn_out[l].astype(BF16),
                 t_out=seq if last else tp, tm=tm_out if last else tm_in)
        h = h.reshape(-1, d)

    return h.reshape(b, seq, d).astype(x.dtype)
```

```python
import functools
import math

import jax
import jax.numpy as jnp
from jax import lax
from jax.experimental import pallas as pl
from jax.experimental.pallas import tpu as pltpu

N_META = 16
ROPE_THETA = 500000.0
EPS = 1e-6
CHUNK = 128
N_PAD = CHUNK - N_META
NEG = -1e30
LOG2E = 1.4426950408889634

MLA_HEADS, MLA_NOPE, MLA_ROPE, MLA_V = 8, 64, 32, 64
MLA_QK = MLA_NOPE + MLA_ROPE
MLA_Q_RANK, MLA_KV_RANK = 384, 256
SSD_HEADS, SSD_HEAD_DIM, SSD_GROUPS, SSD_STATE, SSD_CONV = 8, 64, 2, 64, 5
SSD_INNER = SSD_HEADS * SSD_HEAD_DIM
SSD_CONV_CH = SSD_INNER + 2 * SSD_GROUPS * SSD_STATE
DIFF_HEADS, DIFF_QK = 4, 64
DIFF_V = 2 * DIFF_QK
DIFF_ROPE = DIFF_QK // 4
MLSTM_HEADS, MLSTM_QK, MLSTM_V = 4, 64, 128
HEAD_PAD = 128
LANE = 128
Q_STRIP = 256
SSD_SKEW, MLSTM_SKEW = 4, 3
VMEM_LIMIT = 52 * 1024 * 1024

F32 = jnp.float32
BF16 = jnp.bfloat16
EXP_DTYPE = jnp.bfloat16


def _dot(a, b):
    return jnp.dot(a, b, preferred_element_type=F32)


def _dot_nt(a, b):
    return lax.dot_general(a, b, (((1,), (1,)), ((), ())), preferred_element_type=F32)


def _rms_rows(x, g):
    ms = jnp.mean(x * x, axis=-1, keepdims=True)
    return x * lax.rsqrt(ms + EPS) * g


def _split3(a):
    hi = a.astype(BF16)
    r1 = a - hi.astype(F32)
    mid = r1.astype(BF16)
    lo = (r1 - mid.astype(F32)).astype(BF16)
    return hi, mid, lo


def _cumsum_cols(tri, a):
    hi, mid, lo = _split3(a)
    return _dot(tri, hi) + _dot(tri, mid) + _dot(tri, lo)


def _cumsum_rows(a, tri):
    hi, mid, lo = _split3(a)
    return _dot(hi, tri) + _dot(mid, tri) + _dot(lo, tri)


def _softplus(x):
    return jnp.maximum(x, 0.0) + jnp.log(1.0 + jnp.exp(-jnp.abs(x)))


def _log_sigmoid(x):
    return jnp.minimum(x, 0.0) - jnp.log(1.0 + jnp.exp(-jnp.abs(x)))


def _sigmoid(x):
    return 1.0 / (1.0 + jnp.exp(-x))


def _pick_tile(n, candidates):
    for c in candidates:
        if n % c == 0:
            return c
    raise ValueError(f"no tile in {candidates} divides {n}")


def _params(*sem):
    return pltpu.CompilerParams(dimension_semantics=sem, vmem_limit_bytes=VMEM_LIMIT)


def _norm_rope_t(blk, g_col, cos, sin, n_real):
    r = cos.shape[0]
    ms = jnp.sum(blk * blk, axis=0, keepdims=True) * (1.0 / n_real)
    y = blk * lax.rsqrt(ms + EPS) * g_col
    x1, x2, rest = y[:r], y[r:2 * r], y[2 * r:]
    return jnp.concatenate([x1 * cos - x2 * sin, x2 * cos + x1 * sin, rest], axis=0)


def _mla_in_kernel(h_ref, g_ref, wa_ref, gq_ref, gkv_ref, wuq_ref, wuk_ref, wuv_ref, wkr_ref,
                   qhg_ref, khg_ref, cos_ref, sin_ref, qT_ref, k_ref, vT_ref):
    hn = _rms_rows(h_ref[0], g_ref[...]).astype(BF16)
    acc = _dot(hn, wa_ref[...])
    cqn = _rms_rows(acc[:, :MLA_Q_RANK], gq_ref[...]).astype(BF16)
    ckvn = _rms_rows(acc[:, MLA_Q_RANK:], gkv_ref[...]).astype(BF16)
    qT = _dot_nt(wuq_ref[...], cqn)
    knT = _dot_nt(wuk_ref[...], ckvn)
    vT = _dot_nt(wuv_ref[...], ckvn)
    krT = _dot_nt(wkr_ref[...], hn)
    cos, sin = cos_ref[...], sin_ref[...]
    t = krT.shape[1]
    zpad = jnp.zeros((HEAD_PAD - MLA_QK, t), F32)
    q_scale = (MLA_QK ** -0.5) * LOG2E
    for h in range(MLA_HEADS):
        qb = _norm_rope_t(qT[h * HEAD_PAD:(h + 1) * HEAD_PAD], qhg_ref[...], cos, sin, MLA_QK)
        qT_ref[0, h * HEAD_PAD:(h + 1) * HEAD_PAD, :] = (qb * q_scale).astype(BF16)
        kb = jnp.concatenate([krT, knT[h * MLA_NOPE:(h + 1) * MLA_NOPE], zpad], axis=0)
        kb = _norm_rope_t(kb, khg_ref[...], cos, sin, MLA_QK)
        k_ref[0, h] = kb.T.astype(BF16)
    vT_ref[0] = vT.astype(BF16)


def _diff_in_kernel(h_ref, g_ref, wc_ref, qhg_ref, khg_ref, cos_ref, sin_ref, qT_ref, k_ref, vT_ref):
    hn = _rms_rows(h_ref[0], g_ref[...]).astype(BF16)
    pT = _dot_nt(wc_ref[...], hn)
    cos, sin = cos_ref[...], sin_ref[...]
    t = pT.shape[1]
    zpad = jnp.zeros((HEAD_PAD - DIFF_QK, t), F32)
    nq = 2 * DIFF_HEADS * DIFF_QK
    q_scale = (DIFF_QK ** -0.5) * LOG2E
    for h in range(2 * DIFF_HEADS):
        qb = _norm_rope_t(pT[h * DIFF_QK:(h + 1) * DIFF_QK], qhg_ref[0:DIFF_QK, :], cos, sin, DIFF_QK)
        qT_ref[0, h * HEAD_PAD:h * HEAD_PAD + DIFF_QK, :] = (qb * q_scale).astype(BF16)
        qT_ref[0, h * HEAD_PAD + DIFF_QK:(h + 1) * HEAD_PAD, :] = zpad.astype(BF16)
        kb = _norm_rope_t(pT[nq + h * DIFF_QK:nq + (h + 1) * DIFF_QK], khg_ref[0:DIFF_QK, :], cos, sin,
                          DIFF_QK)
        k_ref[0, h] = jnp.concatenate([kb, zpad], axis=0).T.astype(BF16)
    vT_ref[0] = pT[2 * nq:].astype(BF16)


def _ssd_in_kernel(h_ref, g_ref, wb_ref, wdt_ref, z_ref, xbc_ref, dt_ref, dtT_ref):
    hn = _rms_rows(h_ref[0], g_ref[...]).astype(BF16)
    acc = _dot(hn, wb_ref[...])
    z_ref[0] = acc[:, :SSD_INNER].astype(z_ref.dtype)
    xbc_ref[0] = acc[:, SSD_INNER:SSD_INNER + SSD_CONV_CH]
    dt_ref[0] = acc[:, SSD_INNER + SSD_CONV_CH:SSD_INNER + SSD_CONV_CH + 2 * SSD_HEADS]
    dtT_ref[0] = _dot_nt(wdt_ref[...], hn)


def _mlstm_in_kernel(h_ref, g_ref, wd_ref, wg_ref, q_ref, k_ref, v_ref, o_ref, gt_ref, gtT_ref):
    hn = _rms_rows(h_ref[0], g_ref[...]).astype(BF16)
    acc = _dot(hn, wd_ref[...])
    nqk = MLSTM_HEADS * MLSTM_QK
    nv = MLSTM_HEADS * MLSTM_V
    q_ref[0] = acc[:, :nqk].astype(q_ref.dtype)
    k_ref[0] = acc[:, nqk:2 * nqk].astype(k_ref.dtype)
    v_ref[0] = acc[:, 2 * nqk:2 * nqk + nv].astype(v_ref.dtype)
    o_ref[0] = acc[:, 2 * nqk + nv:2 * nqk + 2 * nv].astype(o_ref.dtype)
    gt_ref[0] = acc[:, 2 * nqk + 2 * nv:2 * nqk + 2 * nv + 4 * MLSTM_HEADS]
    gtT_ref[0] = _dot_nt(wg_ref[...], hn)


def _full(shape):
    nd = len(shape)
    return pl.BlockSpec(shape, lambda *_: (0,) * nd)


def _in_proj_call(body, h, consts, outs, tm, name):
    b, tp, d = h.shape
    in_specs = [pl.BlockSpec((1, tm, d), lambda i, j: (i, j, 0))]
    for c in consts:
        if isinstance(c, tuple):
            in_specs.append(pl.BlockSpec((c[0].shape[0], tm), lambda i, j: (0, j)))
        else:
            in_specs.append(_full(c.shape))
    out_shapes, out_specs = [], []
    for shape, dtype, kind in outs:
        out_shapes.append(jax.ShapeDtypeStruct((b,) + shape, dtype))
        if kind == "row":
            out_specs.append(pl.BlockSpec((1, tm, shape[1]), lambda i, j: (i, j, 0)))
        elif kind == "col":
            out_specs.append(pl.BlockSpec((1, shape[0], tm), lambda i, j: (i, 0, j)))
        else:
            out_specs.append(pl.BlockSpec((1, shape[0], tm, shape[2]), lambda i, j: (i, 0, j, 0)))
    args = [h] + [c[0] if isinstance(c, tuple) else c for c in consts]
    return pl.pallas_call(
        body, grid=(b, tp // tm), in_specs=in_specs, out_specs=out_specs, out_shape=out_shapes,
        compiler_params=_params("parallel", "parallel"), name=name)(*args)


def _attn_kernel(*refs, diff, seq, tk, out_scale):
    if diff:
        qT_ref, k_ref, vT_ref, lam_ref, og_ref, o_ref, s_scr, st_scr, m_scr, acc_scr = refs
    else:
        qT_ref, k_ref, vT_ref, o_ref, s_scr, st_scr, m_scr, acc_scr = refs
    dv = DIFF_V if diff else MLA_V
    tq = qT_ref.shape[2]
    n_chunks = seq // tk
    assert n_chunks % 2 == 0
    m_scr[...] = jnp.full(m_scr.shape, NEG, F32)
    acc_scr[...] = jnp.zeros(acc_scr.shape, F32)

    def stage(nxt, cur):
        k_n, v_aug = [], []
        for hh in range(2):
            if nxt is not None:
                k_n.append(k_ref[0, hh, pl.ds(nxt[1], nxt[2]), :])
            if cur is not None:
                v_lo = 0 if diff else hh * dv
                ones = jnp.ones((16, cur[2]), BF16)
                v_aug.append(jnp.concatenate([vT_ref[0, v_lo:v_lo + dv, pl.ds(cur[1], cur[2])], ones],
                                             axis=0))
        pending = None
        for j0 in range(0, tq, Q_STRIP):
            w = min(Q_STRIP, tq - j0)
            for hh in range(2):
                if cur is not None:
                    src = st_scr.at[hh] if cur[0] is None else s_scr.at[hh, cur[0]]
                    s = src[:, j0:j0 + w]
                    m_old = m_scr[hh, 0:1, j0:j0 + w]
                    m_new = jnp.maximum(m_old, jnp.max(s, axis=0, keepdims=True))
                    alpha = jnp.exp2(m_old - m_new)
                    p = jnp.exp2((s - m_new).astype(EXP_DTYPE)).astype(BF16)
                    m_scr[hh, 0:1, j0:j0 + w] = m_new
                if nxt is not None:
                    dst = st_scr.at[hh] if nxt[0] is None else s_scr.at[hh, nxt[0]]
                    s_n = _dot(k_n[hh], qT_ref[0, hh * HEAD_PAD:(hh + 1) * HEAD_PAD, j0:j0 + w])
                    if nxt[3]:
                        row = lax.broadcasted_iota(jnp.int32, s_n.shape, 0)
                        s_n = jnp.where(row >= N_PAD, s_n, NEG)
                    dst[:, j0:j0 + w] = s_n
                if cur is not None:
                    if pending is not None:
                        ph, pj, pw, pa, pp = pending
                        acc_scr[ph, :, pj:pj + pw] = pa * acc_scr[ph, :, pj:pj + pw] + _dot(v_aug[ph], pp)
                    pending = (hh, j0, w, alpha, p)
        if pending is not None:
            ph, pj, pw, pa, pp = pending
            acc_scr[ph, :, pj:pj + pw] = pa * acc_scr[ph, :, pj:pj + pw] + _dot(v_aug[ph], pp)

    stage((0, 0, tk, False), None)

    def body(i, carry):
        off = pl.multiple_of(2 * i * tk, 2 * tk)
        stage((1, off + tk, tk, False), (0, off, tk))
        stage((0, off + 2 * tk, tk, False), (1, off + tk, tk))
        return carry

    lax.fori_loop(0, n_chunks // 2 - 1, body, 0)
    off = (n_chunks - 2) * tk
    stage((1, off + tk, tk, False), (0, off, tk))
    stage((None, seq, CHUNK, True), (1, off + tk, tk))
    stage(None, (None, seq, CHUNK))
    outs = []
    for hh in range(2):
        acc = acc_scr[hh]
        outs.append(acc[:dv] / acc[dv:dv + 1])
    if diff:
        o = outs[0] - lam_ref[0:1, 0:1] * outs[1]
        ms = jnp.mean(o * o, axis=0, keepdims=True)
        o = o * lax.rsqrt(ms + EPS) * (og_ref[...] * out_scale)
    else:
        o = jnp.concatenate(outs, axis=0)
    o_ref[0] = o.T.astype(o_ref.dtype)


def _attention(qT, k, vT, extra, *, diff, seq, tq, tk, out_scale, name):
    b, _, tp = qT.shape
    groups = qT.shape[1] // (2 * HEAD_PAD)
    dv = DIFF_V if diff else MLA_V
    v_rows = dv if diff else 2 * dv
    in_specs = [
        pl.BlockSpec((1, 2 * HEAD_PAD, tq), lambda i, g, j: (i, g, j)),
        pl.BlockSpec((1, 2, tp, HEAD_PAD), lambda i, g, j: (i, g, 0, 0)),
        pl.BlockSpec((1, v_rows, tp), lambda i, g, j: (i, g, 0)),
    ] + [_full(e.shape) for e in extra]
    return pl.pallas_call(
        functools.partial(_attn_kernel, diff=diff, seq=seq, tk=tk, out_scale=out_scale),
        grid=(b, groups, tp // tq),
        in_specs=in_specs,
        out_specs=pl.BlockSpec((1, tq, LANE), lambda i, g, j: (i, j, g)),
        out_shape=jax.ShapeDtypeStruct((b, tp, groups * LANE), BF16),
        scratch_shapes=[pltpu.VMEM((2, 2, tk, tq), F32), pltpu.VMEM((2, CHUNK, tq), F32),
                        pltpu.VMEM((2, 8, tq), F32), pltpu.VMEM((2, dv + 16, tq), F32)],
        compiler_params=_params("parallel", "parallel", "arbitrary"), name=name)(qT, k, vT, *extra)


def _conv_kernel(x_ref, prev_ref, next_ref, w_ref, b_ref, xs_ref, bc_ref, scr):
    tc = x_ref.shape[1]
    scr[0:8, :] = prev_ref[0]
    scr[8:8 + tc, :] = x_ref[0]
    scr[8 + tc:16 + tc, :] = next_ref[0]
    acc = jnp.broadcast_to(b_ref[...], (tc, b_ref.shape[1]))
    for j in range(SSD_CONV):
        acc = acc + w_ref[j:j + 1, :] * scr[8 - SSD_CONV // 2 + j:8 - SSD_CONV // 2 + j + tc, :]
    act = acc * _sigmoid(acc)
    xs_ref[0] = act[:, :SSD_INNER]
    bc_ref[0] = act[:, SSD_INNER:]


def _ssd_conv(xbc, w8, bias, tc):
    b, tp, c = xbc.shape
    nb8 = tp // 8
    r8 = tc // 8
    row_spec = lambda n: pl.BlockSpec((1, tc, n), lambda i, j: (i, j, 0))
    return pl.pallas_call(
        _conv_kernel, grid=(b, tp // tc),
        in_specs=[row_spec(c),
                  pl.BlockSpec((1, 8, c), lambda i, j: (i, (j * r8 + nb8 - 1) % nb8, 0)),
                  pl.BlockSpec((1, 8, c), lambda i, j: (i, ((j + 1) * r8) % nb8, 0)),
                  _full(w8.shape), _full(bias.shape)],
        out_specs=[row_spec(SSD_INNER), row_spec(c - SSD_INNER)],
        out_shape=[jax.ShapeDtypeStruct((b, tp, SSD_INNER), F32),
                   jax.ShapeDtypeStruct((b, tp, c - SSD_INNER), F32)],
        scratch_shapes=[pltpu.VMEM((tc + 16, c), F32)],
        compiler_params=_params("parallel", "parallel"), name="ssd_conv")(xbc, xbc, xbc, w8, bias)


def _tri_masks(reverse):
    row = lax.broadcasted_iota(jnp.int32, (CHUNK, CHUNK), 0)
    col = lax.broadcasted_iota(jnp.int32, (CHUNK, CHUNK), 1)
    keep = (col >= row) if reverse else (col <= row)
    tri_c = keep.astype(BF16)
    tri_r = ((row >= col) if reverse else (row <= col)).astype(BF16)
    return keep, tri_c, tri_r, row, col


def _ssd_direction(xs_ref, bc_ref, dt_ref, dtT_ref, bias_r, bias_c, a_r, a_c, y_ref, st_ref, *,
                   bi, reverse, is_meta):
    d = 1 if reverse else 0
    h8 = SSD_HEADS
    keep, tri_c, tri_r, row, col = _tri_masks(reverse)
    bm = bc_ref[bi, :, :CHUNK]
    cm = bc_ref[bi, :, CHUNK:]

    dt_c = _softplus(dt_ref[bi][:, d * h8:(d + 1) * h8] + bias_r[:, d * h8:(d + 1) * h8])
    dt_r = _softplus(dtT_ref[bi][d * h8:(d + 1) * h8, :] + bias_c[d * h8:(d + 1) * h8, :])
    pad_c = jnp.logical_and(is_meta, row[:, :h8] < N_PAD)
    pad_r = jnp.logical_and(is_meta, col[:h8, :] < N_PAD)
    dt_c = jnp.where(pad_c, 0.0, dt_c)
    dt_r = jnp.where(pad_r, 0.0, dt_r)
    a_col = dt_c * a_r[:, d * h8:(d + 1) * h8]
    a_row = dt_r * a_c[d * h8:(d + 1) * h8, :]
    cs_c = _cumsum_cols(tri_c, a_col)
    cs_r = _cumsum_rows(a_row, tri_r)
    last = 0 if reverse else CHUNK - 1
    tot_r = cs_r[:, last:last + 1]
    tot_c = cs_c[last:last + 1, :]

    lane_lo = col < SSD_STATE
    row_lo = row < SSD_STATE
    blockdiag = jnp.logical_not(jnp.logical_xor(lane_lo, row_lo))
    cm_sw = pltpu.roll(cm, SSD_STATE, 1)
    c_dup = (jnp.where(lane_lo, cm, cm_sw), jnp.where(lane_lo, cm_sw, cm))
    g_mat = (_dot_nt(jnp.where(lane_lo, cm, 0.0).astype(BF16), bm.astype(BF16)),
             _dot_nt(jnp.where(lane_lo, 0.0, cm).astype(BF16), bm.astype(BF16)))
    bT = bm.T

    heads_per_group = SSD_HEADS // SSD_GROUPS

    def pair_chain(j):
        g = (2 * j) // heads_per_group
        h0, h1 = 2 * j, 2 * j + 1
        xp = xs_ref[bi, :, j * CHUNK:(j + 1) * CHUNK]
        dtp = jnp.where(lane_lo, dt_c[:, h0:h0 + 1], dt_c[:, h1:h1 + 1])
        xdt = xp * dtp
        cs_b = [jnp.broadcast_to(cs_c[:, h:h + 1], (CHUNK, CHUNK)) for h in (h0, h1)]
        yield
        parts = []
        for h, cs_h in zip((h0, h1), cs_b):
            diff_ = cs_h - cs_r[h:h + 1, :]
            parts.append((g_mat[g] * jnp.exp(jnp.where(keep, diff_, NEG))).astype(BF16))
        parts.append((c_dup[g] * jnp.exp(jnp.where(lane_lo, cs_b[0], cs_b[1]))).astype(BF16))
        yield
        lhs = jnp.concatenate(parts, axis=1)
        s_old = st_ref[j]
        rhs = jnp.concatenate([jnp.where(lane_lo, xdt, 0.0).astype(BF16),
                               jnp.where(lane_lo, 0.0, xdt).astype(BF16),
                               s_old.astype(BF16)], axis=0)
        y_pair = _dot(lhs, rhs)
        btg = bT[g * SSD_STATE:(g + 1) * SSD_STATE, :]
        bd = jnp.concatenate([btg * jnp.exp(tot_r[h0:h0 + 1, :] - cs_r[h0:h0 + 1, :]),
                              btg * jnp.exp(tot_r[h1:h1 + 1, :] - cs_r[h1:h1 + 1, :])], axis=0)
        s_upd = _dot(bd.astype(BF16), xdt.astype(BF16))
        yield
        carry = jnp.where(row_lo, jnp.exp(tot_c[:, h0:h0 + 1]), jnp.exp(tot_c[:, h1:h1 + 1]))
        st_ref[j] = s_old * carry + jnp.where(blockdiag, s_upd, 0.0)
        y_ref[bi, :, j * CHUNK:(j + 1) * CHUNK] = y_pair.astype(y_ref.dtype)

    return [pair_chain(j) for j in range(SSD_HEADS // 2)]


def _interleave(chains, period=1):
    pending = [(i % period, c) for i, c in enumerate(chains)]
    rnd = 0
    while pending:
        alive = []
        for delay, c in pending:
            if rnd >= delay:
                try:
                    next(c)
                except StopIteration:
                    continue
            alive.append((delay, c))
        pending = alive
        rnd += 1


def _ssd_kernel(xs_f, bc_f, dt_f, dtT_f, xs_r, bc_r, dt_r, dtT_r, bias_r, bias_c, a_r, a_c,
                yf_ref, yr_ref, st_ref, *, n_chunks):
    step = pl.program_id(1)
    n_pairs = SSD_HEADS // 2

    @pl.when(step == 0)
    def _():
        st_ref[...] = jnp.zeros(st_ref.shape, F32)

    scans = []
    for bi in range(xs_f.shape[0]):
        lo = 2 * bi * n_pairs
        scans.append(_ssd_direction(xs_f, bc_f, dt_f, dtT_f, bias_r, bias_c, a_r, a_c, yf_ref,
                                    st_ref.at[lo:lo + n_pairs], bi=bi, reverse=False, is_meta=step == 0))
        scans.append(_ssd_direction(xs_r, bc_r, dt_r, dtT_r, bias_r, bias_c, a_r, a_c, yr_ref,
                                    st_ref.at[lo + n_pairs:lo + 2 * n_pairs], bi=bi, reverse=True,
                                    is_meta=step == n_chunks - 1))
    _interleave([c for group in zip(*scans) for c in group], period=SSD_SKEW)


def _chunk_order(n_chunks, reverse):
    if reverse:
        return lambda c: (2 * n_chunks - 2 - c) % n_chunks
    return lambda c: (c + n_chunks - 1) % n_chunks


def _scan_specs(arrays, n_chunks, reverse, nb):
    order = _chunk_order(n_chunks, reverse)
    specs = []
    for a, transposed in arrays:
        if transposed:
            specs.append(pl.BlockSpec((nb, a.shape[1], CHUNK), lambda i, s: (i, 0, order(s))))
        else:
            specs.append(pl.BlockSpec((nb, CHUNK, a.shape[2]), lambda i, s: (i, order(s), 0)))
    return specs


def _scan_batch(b):
    return 2 if b % 2 == 0 else 1


def _ssd_scan(xs, bc, dt, dtT, consts):
    b, tp, _ = xs.shape
    n_chunks = tp // CHUNK
    nb = _scan_batch(b)
    arrays = [(xs, False), (bc, False), (dt, False), (dtT, True)]
    in_specs = (_scan_specs(arrays, n_chunks, False, nb) + _scan_specs(arrays, n_chunks, True, nb)
                + [_full(c.shape) for c in consts])
    out_specs = (_scan_specs([(xs, False)], n_chunks, False, nb)
                 + _scan_specs([(xs, False)], n_chunks, True, nb))
    out_shape = [jax.ShapeDtypeStruct((b, tp, SSD_INNER), BF16)] * 2
    return pl.pallas_call(
        functools.partial(_ssd_kernel, n_chunks=n_chunks),
        grid=(b // nb, n_chunks), in_specs=in_specs, out_specs=out_specs, out_shape=out_shape,
        scratch_shapes=[pltpu.VMEM((nb * SSD_HEADS, CHUNK, CHUNK), F32)],
        compiler_params=_params("parallel", "arbitrary"),
        name="ssd_scan")(xs, bc, dt, dtT, xs, bc, dt, dtT, *consts)


def _mlstm_direction(q_ref, k_ref, v_ref, gt_ref, gtT_ref, gb_r, gb_c, y_ref, st_ref, m_ref, *,
                     bi, reverse, is_meta):
    d = 1 if reverse else 0
    nh = MLSTM_HEADS
    keep, tri_c, tri_r, row, col = _tri_masks(reverse)
    gt = gt_ref[bi] + gb_r[...]
    gtT = gtT_ref[bi] + gb_c[...]
    i_lo, f_lo = d * nh, 2 * nh + d * nh
    pad_c = jnp.logical_and(is_meta, row[:, :nh] < N_PAD)
    pad_r = jnp.logical_and(is_meta, col[:nh, :] < N_PAD)
    ig_r = jnp.where(pad_r, NEG, gtT[i_lo:i_lo + nh, :])
    fg_c = jnp.where(pad_c, 0.0, _log_sigmoid(gt[:, f_lo:f_lo + nh]))
    fg_r = jnp.where(pad_r, 0.0, _log_sigmoid(gtT[f_lo:f_lo + nh, :]))
    b_c = _cumsum_cols(tri_c, fg_c)
    b_r = _cumsum_rows(fg_r, tri_r)
    last = 0 if reverse else CHUNK - 1

    lane_lo = col < MLSTM_QK
    row_lo = row < MLSTM_QK
    ones = jnp.ones((CHUNK, MLSTM_V), BF16)
    scale = MLSTM_QK ** -0.5

    def head_chain(h):
        pair = h // 2
        lo = (h % 2) == 0
        qp = q_ref[bi, :, pair * CHUNK:(pair + 1) * CHUNK]
        kp = k_ref[bi, :, pair * CHUNK:(pair + 1) * CHUNK]
        qm = (jnp.where(lane_lo if lo else jnp.logical_not(lane_lo), qp, 0.0) * scale).astype(BF16)
        kT = jnp.where(row_lo if lo else jnp.logical_not(row_lo), kp.astype(F32).T, 0.0)
        v_aug = jnp.concatenate([v_ref[bi, :, h * MLSTM_V:(h + 1) * MLSTM_V].astype(BF16), ones], axis=1)
        m_st = m_ref[h][0:1, 0:1]
        bc = b_c[:, h:h + 1]
        br = b_r[h:h + 1, :]
        ir = ig_r[h:h + 1, :]
        s_raw = _dot_nt(qm, kp.astype(BF16))
        c_st = st_ref[h]
        inter_mm = _dot(qm, c_st.astype(BF16))
        yield
        dmat = jnp.where(keep, ir - br, -jnp.inf)
        m_rel = jnp.maximum(jnp.max(dmat, axis=1, keepdims=True), m_st)
        tot = br[:, last:last + 1]
        d_last = tot - br + ir
        m_new = jnp.maximum(tot + m_st, jnp.max(d_last, axis=1, keepdims=True))
        yield
        w_intra = jnp.exp(dmat - m_rel)
        w_inter = jnp.exp(m_st - m_rel)
        w_s = jnp.exp(d_last - m_new)
        w_prev = jnp.exp(tot + m_st - m_new)
        yield
        s = s_raw * w_intra
        intra_mm = _dot(s.astype(BF16), v_aug)
        upd = _dot((kT * w_s).astype(BF16), v_aug)
        yield
        comb = intra_mm + w_inter * inter_mm
        num = comb[:, :MLSTM_V]
        den = jnp.maximum(jnp.abs(comb[:, MLSTM_V:]), jnp.exp(-(bc + m_rel)))
        y_ref[bi, :, h * MLSTM_V:(h + 1) * MLSTM_V] = (num / den).astype(y_ref.dtype)
        st_ref[h] = w_prev * c_st + upd
        m_ref[h] = jnp.broadcast_to(m_new, m_ref.shape[1:])

    return [head_chain(h) for h in range(nh)]


def _mlstm_kernel(q_f, k_f, v_f, gt_f, gtT_f, q_r, k_r, v_r, gt_r, gtT_r, gb_r, gb_c,
                  yf_ref, yr_ref, st_ref, m_ref, *, n_chunks):
    step = pl.program_id(1)
    nh = MLSTM_HEADS

    @pl.when(step == 0)
    def _():
        st_ref[...] = jnp.zeros(st_ref.shape, F32)
        m_ref[...] = jnp.full(m_ref.shape, NEG, F32)

    scans = []
    for bi in range(q_f.shape[0]):
        lo = 2 * bi * nh
        scans.append(_mlstm_direction(q_f, k_f, v_f, gt_f, gtT_f, gb_r, gb_c, yf_ref, st_ref.at[lo:lo + nh],
                                      m_ref.at[lo:lo + nh], bi=bi, reverse=False, is_meta=step == 0))
        scans.append(_mlstm_direction(q_r, k_r, v_r, gt_r, gtT_r, gb_r, gb_c, yr_ref,
                                      st_ref.at[lo + nh:lo + 2 * nh], m_ref.at[lo + nh:lo + 2 * nh],
                                      bi=bi, reverse=True, is_meta=step == n_chunks - 1))
    _interleave([c for group in zip(*scans) for c in group], period=MLSTM_SKEW)


def _mlstm_scan(q, k, v, gt, gtT, consts):
    b, tp, _ = q.shape
    n_chunks = tp // CHUNK
    nb = _scan_batch(b)
    arrays = [(q, False), (k, False), (v, False), (gt, False), (gtT, True)]
    in_specs = (_scan_specs(arrays, n_chunks, False, nb) + _scan_specs(arrays, n_chunks, True, nb)
                + [_full(c.shape) for c in consts])
    out_specs = (_scan_specs([(v, False)], n_chunks, False, nb)
                 + _scan_specs([(v, False)], n_chunks, True, nb))
    out_shape = [jax.ShapeDtypeStruct(v.shape, BF16)] * 2
    return pl.pallas_call(
        functools.partial(_mlstm_kernel, n_chunks=n_chunks),
        grid=(b // nb, n_chunks), in_specs=in_specs, out_specs=out_specs, out_shape=out_shape,
        scratch_shapes=[pltpu.VMEM((nb * 2 * MLSTM_HEADS, CHUNK, 2 * MLSTM_V), F32),
                        pltpu.VMEM((nb * 2 * MLSTM_HEADS, 8, LANE), F32)],
        compiler_params=_params("parallel", "arbitrary"),
        name="mlstm_scan")(q, k, v, gt, gtT, q, k, v, gt, gtT, *consts)


def _out_kernel(ya_ref, yf_ref, yr_ref, xs_ref, z_ref, yc_ref, hf_ref, hr_ref, og_ref, h_ref,
                ag_ref, dsk_ref, ng_ref, mg_ref, w_ref, o_ref, *, tp, seq):
    tm = h_ref.shape[0]
    ya = _rms_rows(ya_ref[...].astype(F32), ag_ref[...])
    z = z_ref[...].astype(F32)
    yb = ((yf_ref[...].astype(F32) + yr_ref[...].astype(F32) + dsk_ref[...] * xs_ref[...])
          * (z * _sigmoid(z)))
    parts = [ya.astype(BF16), _rms_rows(yb, ng_ref[...]).astype(BF16), yc_ref[...].astype(BF16)]
    for hd in range(MLSTM_HEADS):
        lo, hi = hd * MLSTM_V, (hd + 1) * MLSTM_V
        hsum = hf_ref[:, lo:hi].astype(F32) + hr_ref[:, lo:hi].astype(F32)
        gate = _sigmoid(og_ref[:, lo:hi].astype(F32))
        parts.append((gate * _rms_rows(hsum, mg_ref[...])).astype(BF16))
    out = h_ref[...] + _dot(jnp.concatenate(parts, axis=1), w_ref[...])
    t = (pl.program_id(0) * tm + lax.broadcasted_iota(jnp.int32, (tm, 1), 0)) % tp
    is_pad = jnp.logical_and(t >= seq, t < seq + N_PAD)
    o_ref[...] = jnp.where(is_pad, 0.0, out)


def _out_proj(mixer_outs, h, consts, w, *, tp, seq, tm):
    rows, d = h.shape
    row_spec = lambda c: pl.BlockSpec((tm, c), lambda i: (i, 0))
    return pl.pallas_call(
        functools.partial(_out_kernel, tp=tp, seq=seq), grid=(rows // tm,),
        in_specs=([row_spec(a.shape[1]) for a in mixer_outs] + [row_spec(d)]
                  + [_full(c.shape) for c in consts] + [_full(w.shape)]),
        out_specs=row_spec(d), out_shape=jax.ShapeDtypeStruct((rows, d), F32),
        compiler_params=_params("parallel"), name="out_proj")(*mixer_outs, h, *consts, w)


def _ffn_kernel(h_ref, g_ref, wg_ref, wu_ref, wo_ref, o_ref, *, n_split):
    x = h_ref[0]
    hn = _rms_rows(x, g_ref[...]).astype(BF16)
    f = wg_ref.shape[1]
    tf = f // n_split
    acc = x
    for c in range(n_split):
        gate = _dot(hn, wg_ref[:, c * tf:(c + 1) * tf])
        up = _dot(hn, wu_ref[:, c * tf:(c + 1) * tf])
        act = (gate * _sigmoid(gate) * up).astype(BF16)
        acc = acc + _dot(act, wo_ref[c * tf:(c + 1) * tf, :])
    o_ref[0] = acc


def _ffn(h, g, wg, wu, wo, *, t_out, tm):
    b, _, d = h.shape
    f = wg.shape[1]
    n_split = 2 if (f // 2) % LANE == 0 else 1
    row_spec = pl.BlockSpec((1, tm, d), lambda i, j: (i, j, 0))
    resident = lambda a: pl.BlockSpec(a.shape, lambda i, j: (0, 0), pipeline_mode=pl.Buffered(1))
    return pl.pallas_call(
        functools.partial(_ffn_kernel, n_split=n_split), grid=(b, t_out // tm),
        in_specs=[row_spec, _full(g.shape), resident(wg), resident(wu), resident(wo)],
        out_specs=row_spec, out_shape=jax.ShapeDtypeStruct((b, t_out, d), F32),
        compiler_params=_params("parallel", "parallel"), name="ffn")(h, g, wg, wu, wo)


def _rope_tables_t(pos, rot_dim):
    inv = 1.0 / (ROPE_THETA ** (jnp.arange(0, rot_dim, 2, dtype=F32) / rot_dim))
    ang = pos[:, None] * inv[None, :]
    return jnp.cos(ang).T, jnp.sin(ang).T


def _col(v, n=None):
    v = v.astype(F32)
    if n is not None:
        v = jnp.pad(v, (0, n - v.shape[0]))
    return v[:, None]


def _row(v):
    return v.astype(F32)[None, :]


def kernel(x, meta_tokens, attn_norm_g, w_in, mla_q_norm_g, mla_kv_norm_g, mla_w_uq, mla_w_ukv, mla_q_head_g, mla_k_head_g, mla_out_g, ssd_conv_w, ssd_conv_b, ssd_dt_bias, ssd_a_log, ssd_d, ssd_norm_g, diff_q_head_g, diff_k_head_g, diff_lambda, diff_out_g, mlstm_i_bias, mlstm_f_bias, mlstm_out_g, w_out, ffn_norm_g, w_ffn_in, w_ffn_out):
    b, seq, d = x.shape
    depth = w_in.shape[0]
    tp = seq + CHUNK
    assert seq % CHUNK == 0
    tm_in = _pick_tile(tp, (640, 384, 128))
    tm_row = _pick_tile(tp, (1664, 640, 384, 128))
    tq = _pick_tile(tp, (1664, 640, 384, 128))
    tk = _pick_tile(seq, (256, 128))
    tm_flat = _pick_tile(b * tp, (512, 256, 128))
    tm_out = _pick_tile(seq, (512, 256, 128))

    meta = jnp.broadcast_to(meta_tokens[None].astype(x.dtype), (b, N_META, d))
    h = jnp.concatenate([x, jnp.zeros((b, N_PAD, d), x.dtype), meta], axis=1).reshape(b * tp, d)
    pos = jnp.concatenate([N_META + jnp.arange(seq, dtype=F32), jnp.zeros((N_PAD,), F32),
                           jnp.arange(N_META, dtype=F32)])
    cos_m, sin_m = _rope_tables_t(pos, MLA_ROPE)
    cos_d, sin_d = _rope_tables_t(pos, DIFF_ROPE)

    sizes = (MLA_Q_RANK, MLA_KV_RANK, MLA_ROPE, SSD_INNER, SSD_CONV_CH, 2 * SSD_HEADS,
             2 * DIFF_HEADS * DIFF_QK, 2 * DIFF_HEADS * DIFF_QK, DIFF_HEADS * DIFF_V,
             MLSTM_HEADS * MLSTM_QK, MLSTM_HEADS * MLSTM_QK, MLSTM_HEADS * MLSTM_V,
             MLSTM_HEADS * MLSTM_V, 2 * MLSTM_HEADS, 2 * MLSTM_HEADS)
    offs = [0]
    for s_ in sizes:
        offs.append(offs[-1] + s_)

    def cols(w, first, last):
        return w[:, offs[first]:offs[last + 1]]

    def pad_cols(w, n):
        return jnp.pad(w, ((0, 0), (0, n - w.shape[1])))

    for l in range(depth):
        lambda_init = 0.8 - 0.6 * math.exp(-0.3 * l)
        wl = w_in[l]
        h3 = h.reshape(b, tp, d)
        g_attn = _row(attn_norm_g[l])

        w_a = cols(wl, 0, 1).astype(BF16)
        w_kr = cols(wl, 2, 2).T.astype(BF16)
        w_uq = mla_w_uq[l].T.reshape(MLA_HEADS, MLA_QK, MLA_Q_RANK)
        w_uq = jnp.pad(w_uq, ((0, 0), (0, HEAD_PAD - MLA_QK), (0, 0))).reshape(MLA_HEADS * HEAD_PAD, MLA_Q_RANK)
        w_ukv = mla_w_ukv[l].T.reshape(MLA_HEADS, MLA_NOPE + MLA_V, MLA_KV_RANK)
        w_uk = w_ukv[:, :MLA_NOPE].reshape(MLA_HEADS * MLA_NOPE, MLA_KV_RANK)
        w_uv = w_ukv[:, MLA_NOPE:].reshape(MLA_HEADS * MLA_V, MLA_KV_RANK)
        qT_a, k_a, vT_a = _in_proj_call(
            _mla_in_kernel, h3,
            [g_attn, w_a, _row(mla_q_norm_g[l]), _row(mla_kv_norm_g[l]), w_uq.astype(BF16),
             w_uk.astype(BF16), w_uv.astype(BF16), w_kr, _col(mla_q_head_g[l], HEAD_PAD),
             _col(mla_k_head_g[l], HEAD_PAD), (cos_m,), (sin_m,)],
            [((MLA_HEADS * HEAD_PAD, tp), BF16, "col"), ((MLA_HEADS, tp, HEAD_PAD), BF16, "head"),
             ((MLA_HEADS * MLA_V, tp), BF16, "col")], tm_in, "mla_in")
        y_a = _attention(qT_a, k_a, vT_a, [], diff=False, seq=seq, tq=tq, tk=tk, out_scale=1.0,
                         name="mla_attn")

        w_c = cols(wl, 6, 8).T.astype(BF16)
        qT_c, k_c, vT_c = _in_proj_call(
            _diff_in_kernel, h3,
            [g_attn, w_c, _col(diff_q_head_g[l], HEAD_PAD), _col(diff_k_head_g[l], HEAD_PAD),
             (cos_d,), (sin_d,)],
            [((2 * DIFF_HEADS * HEAD_PAD, tp), BF16, "col"), ((2 * DIFF_HEADS, tp, HEAD_PAD), BF16, "head"),
             ((DIFF_HEADS * DIFF_V, tp), BF16, "col")], tm_in, "diff_in")
        lam = diff_lambda[l].astype(F32)
        lam_full = jnp.exp(jnp.sum(lam[0] * lam[1])) - jnp.exp(jnp.sum(lam[2] * lam[3])) + lambda_init
        y_c = _attention(qT_c, k_c, vT_c, [jnp.full((8, LANE), lam_full, F32), _col(diff_out_g[l])],
                         diff=True, seq=seq, tq=tq, tk=tk, out_scale=1.0 - lambda_init, name="diff_attn")

        n_b = SSD_INNER + SSD_CONV_CH + 2 * SSD_HEADS
        w_b = pad_cols(cols(wl, 3, 5), -(-n_b // LANE) * LANE).astype(BF16)
        w_dt = cols(wl, 5, 5).T.astype(BF16)
        z_b, xbc, dt, dtT = _in_proj_call(
            _ssd_in_kernel, h3, [g_attn, w_b, w_dt],
            [((tp, SSD_INNER), BF16, "row"), ((tp, SSD_CONV_CH), F32, "row"),
             ((tp, 2 * SSD_HEADS), F32, "row"), ((2 * SSD_HEADS, tp), F32, "col")], tm_row, "ssd_in")
        w8 = jnp.pad(ssd_conv_w[l].astype(F32), ((0, 8 - SSD_CONV), (0, 0)))
        xs_b, bc_b = _ssd_conv(xbc, w8, _row(ssd_conv_b[l]), tm_in)
        dt_bias = ssd_dt_bias[l].astype(F32).reshape(-1)
        a_neg = -jnp.exp(ssd_a_log[l].astype(F32)).reshape(-1)
        y_bf, y_br = _ssd_scan(xs_b, bc_b, dt, dtT, [_row(dt_bias), _col(dt_bias), _row(a_neg), _col(a_neg)])

        n_d = 2 * MLSTM_HEADS * MLSTM_QK + 2 * MLSTM_HEADS * MLSTM_V + 4 * MLSTM_HEADS
        w_d = pad_cols(cols(wl, 9, 14), -(-n_d // LANE) * LANE).astype(BF16)
        w_g = cols(wl, 13, 14).T.astype(BF16)
        q_d, k_d, v_d, o_d, gt, gtT = _in_proj_call(
            _mlstm_in_kernel, h3, [g_attn, w_d, w_g],
            [((tp, MLSTM_HEADS * MLSTM_QK), BF16, "row"), ((tp, MLSTM_HEADS * MLSTM_QK), BF16, "row"),
             ((tp, MLSTM_HEADS * MLSTM_V), BF16, "row"), ((tp, MLSTM_HEADS * MLSTM_V), BF16, "row"),
             ((tp, 4 * MLSTM_HEADS), F32, "row"), ((4 * MLSTM_HEADS, tp), F32, "col")], tm_row, "mlstm_in")
        gate_bias = jnp.concatenate([mlstm_i_bias[l].reshape(-1), mlstm_f_bias[l].reshape(-1)]).astype(F32)
        h_df, h_dr = _mlstm_scan(q_d, k_d, v_d, gt, gtT, [_row(gate_bias), _col(gate_bias)])

        flat = lambda a: a.reshape(b * tp, a.shape[2])
        mixer_outs = [flat(a) for a in (y_a, y_bf, y_br, xs_b, z_b, y_c, h_df, h_dr, o_d)]
        out_consts = [_row(mla_out_g[l]), _row(jnp.repeat(ssd_d[l], SSD_HEAD_DIM)), _row(ssd_norm_g[l]),
                      _row(mlstm_out_g[l])]
        h = _out_proj(mixer_outs, h, out_consts, w_out[l].astype(BF16), tp=tp, seq=seq, tm=tm_flat)
        f = w_ffn_out.shape[1]
        last = l == depth - 1
        h = _ffn(h.reshape(b, tp, d), _row(ffn_norm_g[l]), w_ffn_in[l][:, :f].astype(BF16),
                 w_ffn_in[l][:, f:].astype(BF16), w_ffn_out[l].astype(BF16),
                 t_out=seq if last else tp, tm=tm_out if last else tm_in)
        h = h.reshape(-1, d)

    return h.reshape(b, seq, d).astype(x.dtype)
```

```python
import functools
import math

import jax
import jax.numpy as jnp
from jax import lax
from jax.experimental import pallas as pl
from jax.experimental.pallas import tpu as pltpu

N_META = 16
ROPE_THETA = 500000.0
EPS = 1e-6
CHUNK = 128
N_PAD = CHUNK - N_META
NEG = -1e30
LOG2E = 1.4426950408889634

MLA_HEADS, MLA_NOPE, MLA_ROPE, MLA_V = 8, 64, 32, 64
MLA_QK = MLA_NOPE + MLA_ROPE
MLA_Q_RANK, MLA_KV_RANK = 384, 256
SSD_HEADS, SSD_HEAD_DIM, SSD_GROUPS, SSD_STATE, SSD_CONV = 8, 64, 2, 64, 5
SSD_INNER = SSD_HEADS * SSD_HEAD_DIM
SSD_CONV_CH = SSD_INNER + 2 * SSD_GROUPS * SSD_STATE
DIFF_HEADS, DIFF_QK = 4, 64
DIFF_V = 2 * DIFF_QK
DIFF_ROPE = DIFF_QK // 4
MLSTM_HEADS, MLSTM_QK, MLSTM_V = 4, 64, 128
HEAD_PAD = 128
LANE = 128
Q_STRIP = 256
MLA_HEADS_PER_STEP = 4
SSD_SKEW, MLSTM_SKEW = 4, 3
VMEM_LIMIT = 52 * 1024 * 1024

F32 = jnp.float32
BF16 = jnp.bfloat16
EXP_DTYPE = jnp.bfloat16


def _dot(a, b):
    return jnp.dot(a, b, preferred_element_type=F32)


def _dot_nt(a, b):
    return lax.dot_general(a, b, (((1,), (1,)), ((), ())), preferred_element_type=F32)


def _rms_rows(x, g):
    ms = jnp.mean(x * x, axis=-1, keepdims=True)
    return x * lax.rsqrt(ms + EPS) * g


def _split3(a):
    hi = a.astype(BF16)
    r1 = a - hi.astype(F32)
    mid = r1.astype(BF16)
    lo = (r1 - mid.astype(F32)).astype(BF16)
    return hi, mid, lo


def _cumsum_cols(tri, a):
    hi, mid, lo = _split3(a)
    return _dot(tri, hi) + _dot(tri, mid) + _dot(tri, lo)


def _cumsum_rows(a, tri):
    hi, mid, lo = _split3(a)
    return _dot(hi, tri) + _dot(mid, tri) + _dot(lo, tri)


def _softplus(x):
    return jnp.maximum(x, 0.0) + jnp.log(1.0 + jnp.exp(-jnp.abs(x)))


def _log_sigmoid(x):
    return jnp.minimum(x, 0.0) - jnp.log(1.0 + jnp.exp(-jnp.abs(x)))


def _sigmoid(x):
    return 1.0 / (1.0 + jnp.exp(-x))


def _pick_tile(n, candidates):
    for c in candidates:
        if n % c == 0:
            return c
    raise ValueError(f"no tile in {candidates} divides {n}")


def _params(*sem):
    return pltpu.CompilerParams(dimension_semantics=sem, vmem_limit_bytes=VMEM_LIMIT)


def _norm_rope_t(blk, g_col, cos, sin, n_real):
    r = cos.shape[0]
    ms = jnp.sum(blk * blk, axis=0, keepdims=True) * (1.0 / n_real)
    y = blk * lax.rsqrt(ms + EPS) * g_col
    x1, x2, rest = y[:r], y[r:2 * r], y[2 * r:]
    return jnp.concatenate([x1 * cos - x2 * sin, x2 * cos + x1 * sin, rest], axis=0)


def _mla_in_kernel(h_ref, g_ref, wa_ref, gq_ref, gkv_ref, wuq_ref, wuk_ref, wuv_ref, wkr_ref,
                   qhg_ref, khg_ref, cos_ref, sin_ref, qT_ref, k_ref, vT_ref):
    hn = _rms_rows(h_ref[0], g_ref[...]).astype(BF16)
    acc = _dot(hn, wa_ref[...])
    cqn = _rms_rows(acc[:, :MLA_Q_RANK], gq_ref[...]).astype(BF16)
    ckvn = _rms_rows(acc[:, MLA_Q_RANK:], gkv_ref[...]).astype(BF16)
    qT = _dot_nt(wuq_ref[...], cqn)
    knT = _dot_nt(wuk_ref[...], ckvn)
    vT = _dot_nt(wuv_ref[...], ckvn)
    krT = _dot_nt(wkr_ref[...], hn)
    cos, sin = cos_ref[...], sin_ref[...]
    t = krT.shape[1]
    zpad = jnp.zeros((HEAD_PAD - MLA_QK, t), F32)
    q_scale = (MLA_QK ** -0.5) * LOG2E
    for h in range(MLA_HEADS):
        qb = _norm_rope_t(qT[h * MLA_QK:(h + 1) * MLA_QK], qhg_ref[0:MLA_QK, :], cos, sin, MLA_QK)
        qT_ref[0, h * HEAD_PAD:h * HEAD_PAD + MLA_QK, :] = (qb * q_scale).astype(BF16)
        qT_ref[0, h * HEAD_PAD + MLA_QK:(h + 1) * HEAD_PAD, :] = zpad.astype(BF16)
        kb = jnp.concatenate([krT, knT[h * MLA_NOPE:(h + 1) * MLA_NOPE]], axis=0)
        kb = _norm_rope_t(kb, khg_ref[0:MLA_QK, :], cos, sin, MLA_QK)
        k_ref[0, h] = jnp.concatenate([kb, zpad], axis=0).T.astype(BF16)
    vT_ref[0] = vT.astype(BF16)


def _diff_in_kernel(h_ref, g_ref, wc_ref, qhg_ref, khg_ref, cos_ref, sin_ref, qT_ref, k_ref, vT_ref):
    hn = _rms_rows(h_ref[0], g_ref[...]).astype(BF16)
    pT = _dot_nt(wc_ref[...], hn)
    cos, sin = cos_ref[...], sin_ref[...]
    t = pT.shape[1]
    zpad = jnp.zeros((HEAD_PAD - DIFF_QK, t), F32)
    nq = 2 * DIFF_HEADS * DIFF_QK
    q_scale = (DIFF_QK ** -0.5) * LOG2E
    for h in range(2 * DIFF_HEADS):
        qb = _norm_rope_t(pT[h * DIFF_QK:(h + 1) * DIFF_QK], qhg_ref[0:DIFF_QK, :], cos, sin, DIFF_QK)
        qT_ref[0, h * HEAD_PAD:h * HEAD_PAD + DIFF_QK, :] = (qb * q_scale).astype(BF16)
        qT_ref[0, h * HEAD_PAD + DIFF_QK:(h + 1) * HEAD_PAD, :] = zpad.astype(BF16)
        kb = _norm_rope_t(pT[nq + h * DIFF_QK:nq + (h + 1) * DIFF_QK], khg_ref[0:DIFF_QK, :], cos, sin,
                          DIFF_QK)
        k_ref[0, h] = jnp.concatenate([kb, zpad], axis=0).T.astype(BF16)
    vT_ref[0] = pT[2 * nq:].astype(BF16)


def _ssd_in_kernel(h_ref, g_ref, wb_ref, wdt_ref, z_ref, xbc_ref, dt_ref, dtT_ref):
    hn = _rms_rows(h_ref[0], g_ref[...]).astype(BF16)
    acc = _dot(hn, wb_ref[...])
    z_ref[0] = acc[:, :SSD_INNER].astype(z_ref.dtype)
    xbc_ref[0] = acc[:, SSD_INNER:SSD_INNER + SSD_CONV_CH]
    dt_ref[0] = acc[:, SSD_INNER + SSD_CONV_CH:SSD_INNER + SSD_CONV_CH + 2 * SSD_HEADS]
    dtT_ref[0] = _dot_nt(wdt_ref[...], hn)


def _mlstm_in_kernel(h_ref, g_ref, wd_ref, wg_ref, q_ref, k_ref, v_ref, o_ref, gt_ref, gtT_ref):
    hn = _rms_rows(h_ref[0], g_ref[...]).astype(BF16)
    acc = _dot(hn, wd_ref[...])
    nqk = MLSTM_HEADS * MLSTM_QK
    nv = MLSTM_HEADS * MLSTM_V
    q_ref[0] = acc[:, :nqk].astype(q_ref.dtype)
    k_ref[0] = acc[:, nqk:2 * nqk].astype(k_ref.dtype)
    v_ref[0] = acc[:, 2 * nqk:2 * nqk + nv].astype(v_ref.dtype)
    o_ref[0] = acc[:, 2 * nqk + nv:2 * nqk + 2 * nv].astype(o_ref.dtype)
    gt_ref[0] = acc[:, 2 * nqk + 2 * nv:2 * nqk + 2 * nv + 4 * MLSTM_HEADS]
    gtT_ref[0] = _dot_nt(wg_ref[...], hn)


def _full(shape):
    nd = len(shape)
    return pl.BlockSpec(shape, lambda *_: (0,) * nd)


def _in_proj_call(body, h, consts, outs, tm, name):
    b, tp, d = h.shape
    in_specs = [pl.BlockSpec((1, tm, d), lambda i, j: (i, j, 0))]
    for c in consts:
        if isinstance(c, tuple):
            in_specs.append(pl.BlockSpec((c[0].shape[0], tm), lambda i, j: (0, j)))
        else:
            in_specs.append(_full(c.shape))
    out_shapes, out_specs = [], []
    for shape, dtype, kind in outs:
        out_shapes.append(jax.ShapeDtypeStruct((b,) + shape, dtype))
        if kind == "row":
            out_specs.append(pl.BlockSpec((1, tm, shape[1]), lambda i, j: (i, j, 0)))
        elif kind == "col":
            out_specs.append(pl.BlockSpec((1, shape[0], tm), lambda i, j: (i, 0, j)))
        else:
            out_specs.append(pl.BlockSpec((1, shape[0], tm, shape[2]), lambda i, j: (i, 0, j, 0)))
    args = [h] + [c[0] if isinstance(c, tuple) else c for c in consts]
    return pl.pallas_call(
        body, grid=(b, tp // tm), in_specs=in_specs, out_specs=out_specs, out_shape=out_shapes,
        compiler_params=_params("parallel", "parallel"), name=name)(*args)


def _attn_kernel(*refs, diff, seq, tk, out_scale, nsh):
    if diff:
        qT_ref, k_ref, vT_ref, lam_ref, og_ref, o_ref, s_scr, st_scr, m_scr, acc_scr = refs
    else:
        qT_ref, k_ref, vT_ref, o_ref, s_scr, st_scr, m_scr, acc_scr = refs
    dv = DIFF_V if diff else MLA_V
    tq = qT_ref.shape[2]
    n_chunks = seq // tk
    assert n_chunks % 2 == 0
    m_scr[...] = jnp.full(m_scr.shape, NEG, F32)
    acc_scr[...] = jnp.zeros(acc_scr.shape, F32)

    def stage(nxt, cur):
        k_n, v_aug = [], []
        for hh in range(nsh):
            if nxt is not None:
                k_n.append(k_ref[0, hh, pl.ds(nxt[1], nxt[2]), :])
            if cur is not None:
                v_lo = (hh // 2) * dv if diff else hh * dv
                ones = jnp.ones((16, cur[2]), BF16)
                v_aug.append(jnp.concatenate([vT_ref[0, v_lo:v_lo + dv, pl.ds(cur[1], cur[2])], ones],
                                             axis=0))
        pending = None
        for j0 in range(0, tq, Q_STRIP):
            w = min(Q_STRIP, tq - j0)
            for hh in range(nsh):
                if cur is not None:
                    src = st_scr.at[hh] if cur[0] is None else s_scr.at[hh, cur[0]]
                    s = src[:, j0:j0 + w]
                    m_old = m_scr[hh, 0:1, j0:j0 + w]
                    m_new = jnp.maximum(m_old, jnp.max(s, axis=0, keepdims=True))
                    alpha = jnp.exp2(m_old - m_new)
                    p = jnp.exp2((s - m_new).astype(EXP_DTYPE)).astype(BF16)
                    m_scr[hh, 0:1, j0:j0 + w] = m_new
                if nxt is not None:
                    dst = st_scr.at[hh] if nxt[0] is None else s_scr.at[hh, nxt[0]]
                    s_n = _dot(k_n[hh], qT_ref[0, hh * HEAD_PAD:(hh + 1) * HEAD_PAD, j0:j0 + w])
                    if nxt[3]:
                        row = lax.broadcasted_iota(jnp.int32, s_n.shape, 0)
                        s_n = jnp.where(row >= N_PAD, s_n, NEG)
                    dst[:, j0:j0 + w] = s_n
                if cur is not None:
                    if pending is not None:
                        ph, pj, pw, pa, pp = pending
                        acc_scr[ph, :, pj:pj + pw] = pa * acc_scr[ph, :, pj:pj + pw] + _dot(v_aug[ph], pp)
                    pending = (hh, j0, w, alpha, p)
        if pending is not None:
            ph, pj, pw, pa, pp = pending
            acc_scr[ph, :, pj:pj + pw] = pa * acc_scr[ph, :, pj:pj + pw] + _dot(v_aug[ph], pp)

    stage((0, 0, tk, False), None)

    def body(i, carry):
        off = pl.multiple_of(2 * i * tk, 2 * tk)
        stage((1, off + tk, tk, False), (0, off, tk))
        stage((0, off + 2 * tk, tk, False), (1, off + tk, tk))
        return carry

    lax.fori_loop(0, n_chunks // 2 - 1, body, 0)
    off = (n_chunks - 2) * tk
    stage((1, off + tk, tk, False), (0, off, tk))
    stage((None, seq, CHUNK, True), (1, off + tk, tk))
    stage(None, (None, seq, CHUNK))
    outs = []
    for hh in range(nsh):
        acc = acc_scr[hh]
        outs.append(acc[:dv] / acc[dv:dv + 1])
    if diff:
        pairs = []
        for hp in range(nsh // 2):
            o = outs[2 * hp] - lam_ref[0:1, 0:1] * outs[2 * hp + 1]
            ms = jnp.mean(o * o, axis=0, keepdims=True)
            pairs.append(o * lax.rsqrt(ms + EPS) * (og_ref[...] * out_scale))
        o = jnp.concatenate(pairs, axis=0) if len(pairs) > 1 else pairs[0]
    else:
        o = jnp.concatenate(outs, axis=0)
    o_ref[0] = o.T.astype(o_ref.dtype)


def _attention(qT, k, vT, extra, *, diff, seq, tq, tk, out_scale, nsh, name):
    b, _, tp = qT.shape
    groups = qT.shape[1] // (nsh * HEAD_PAD)
    dv = DIFF_V if diff else MLA_V
    v_rows = (nsh // 2) * dv if diff else nsh * dv
    o_cols = (nsh // 2) * LANE
    in_specs = [
        pl.BlockSpec((1, nsh * HEAD_PAD, tq), lambda i, g, j: (i, g, j)),
        pl.BlockSpec((1, nsh, tp, HEAD_PAD), lambda i, g, j: (i, g, 0, 0)),
        pl.BlockSpec((1, v_rows, tp), lambda i, g, j: (i, g, 0)),
    ] + [_full(e.shape) for e in extra]
    return pl.pallas_call(
        functools.partial(_attn_kernel, diff=diff, seq=seq, tk=tk, out_scale=out_scale, nsh=nsh),
        grid=(b, groups, tp // tq),
        in_specs=in_specs,
        out_specs=pl.BlockSpec((1, tq, o_cols), lambda i, g, j: (i, j, g)),
        out_shape=jax.ShapeDtypeStruct((b, tp, groups * o_cols), BF16),
        scratch_shapes=[pltpu.VMEM((nsh, 2, tk, tq), F32), pltpu.VMEM((nsh, CHUNK, tq), F32),
                        pltpu.VMEM((nsh, 8, tq), F32), pltpu.VMEM((nsh, dv + 16, tq), F32)],
        compiler_params=_params("parallel", "parallel", "arbitrary"), name=name)(qT, k, vT, *extra)


def _conv_kernel(x_ref, prev_ref, next_ref, w_ref, b_ref, xs_ref, bc_ref, scr):
    tc = x_ref.shape[1]
    scr[0:8, :] = prev_ref[0]
    scr[8:8 + tc, :] = x_ref[0]
    scr[8 + tc:16 + tc, :] = next_ref[0]
    acc = jnp.broadcast_to(b_ref[...], (tc, b_ref.shape[1]))
    for j in range(SSD_CONV):
        acc = acc + w_ref[j:j + 1, :] * scr[8 - SSD_CONV // 2 + j:8 - SSD_CONV // 2 + j + tc, :]
    act = acc * _sigmoid(acc)
    xs_ref[0] = act[:, :SSD_INNER]
    bc_ref[0] = act[:, SSD_INNER:]


def _ssd_conv(xbc, w8, bias, tc):
    b, tp, c = xbc.shape
    nb8 = tp // 8
    r8 = tc // 8
    row_spec = lambda n: pl.BlockSpec((1, tc, n), lambda i, j: (i, j, 0))
    return pl.pallas_call(
        _conv_kernel, grid=(b, tp // tc),
        in_specs=[row_spec(c),
                  pl.BlockSpec((1, 8, c), lambda i, j: (i, (j * r8 + nb8 - 1) % nb8, 0)),
                  pl.BlockSpec((1, 8, c), lambda i, j: (i, ((j + 1) * r8) % nb8, 0)),
                  _full(w8.shape), _full(bias.shape)],
        out_specs=[row_spec(SSD_INNER), row_spec(c - SSD_INNER)],
        out_shape=[jax.ShapeDtypeStruct((b, tp, SSD_INNER), F32),
                   jax.ShapeDtypeStruct((b, tp, c - SSD_INNER), F32)],
        scratch_shapes=[pltpu.VMEM((tc + 16, c), F32)],
        compiler_params=_params("parallel", "parallel"), name="ssd_conv")(xbc, xbc, xbc, w8, bias)


def _tri_masks(reverse):
    row = lax.broadcasted_iota(jnp.int32, (CHUNK, CHUNK), 0)
    col = lax.broadcasted_iota(jnp.int32, (CHUNK, CHUNK), 1)
    keep = (col >= row) if reverse else (col <= row)
    tri_c = keep.astype(BF16)
    tri_r = ((row >= col) if reverse else (row <= col)).astype(BF16)
    return keep, tri_c, tri_r, row, col


def _ssd_direction(xs_ref, bc_ref, dt_ref, dtT_ref, bias_r, bias_c, a_r, a_c, y_ref, st_ref, *,
                   bi, reverse, is_meta):
    d = 1 if reverse else 0
    h8 = SSD_HEADS
    keep, tri_c, tri_r, row, col = _tri_masks(reverse)
    bm = bc_ref[bi, :, :CHUNK]
    cm = bc_ref[bi, :, CHUNK:]

    dt_c = _softplus(dt_ref[bi][:, d * h8:(d + 1) * h8] + bias_r[:, d * h8:(d + 1) * h8])
    dt_r = _softplus(dtT_ref[bi][d * h8:(d + 1) * h8, :] + bias_c[d * h8:(d + 1) * h8, :])
    pad_c = jnp.logical_and(is_meta, row[:, :h8] < N_PAD)
    pad_r = jnp.logical_and(is_meta, col[:h8, :] < N_PAD)
    dt_c = jnp.where(pad_c, 0.0, dt_c)
    dt_r = jnp.where(pad_r, 0.0, dt_r)
    a_col = dt_c * a_r[:, d * h8:(d + 1) * h8]
    a_row = dt_r * a_c[d * h8:(d + 1) * h8, :]
    cs_c = _cumsum_cols(tri_c, a_col)
    cs_r = _cumsum_rows(a_row, tri_r)
    last = 0 if reverse else CHUNK - 1
    tot_r = cs_r[:, last:last + 1]
    tot_c = cs_c[last:last + 1, :]

    lane_lo = col < SSD_STATE
    row_lo = row < SSD_STATE
    blockdiag = jnp.logical_not(jnp.logical_xor(lane_lo, row_lo))
    cm_sw = pltpu.roll(cm, SSD_STATE, 1)
    c_dup = (jnp.where(lane_lo, cm, cm_sw), jnp.where(lane_lo, cm_sw, cm))
    g_mat = (_dot_nt(jnp.where(lane_lo, cm, 0.0).astype(BF16), bm.astype(BF16)),
             _dot_nt(jnp.where(lane_lo, 0.0, cm).astype(BF16), bm.astype(BF16)))
    bT = bm.T

    heads_per_group = SSD_HEADS // SSD_GROUPS

    def pair_chain(j):
        g = (2 * j) // heads_per_group
        h0, h1 = 2 * j, 2 * j + 1
        xp = xs_ref[bi, :, j * CHUNK:(j + 1) * CHUNK]
        dtp = jnp.where(lane_lo, dt_c[:, h0:h0 + 1], dt_c[:, h1:h1 + 1])
        xdt = xp * dtp
        cs_b = [jnp.broadcast_to(cs_c[:, h:h + 1], (CHUNK, CHUNK)) for h in (h0, h1)]
        yield
        parts = []
        for h, cs_h in zip((h0, h1), cs_b):
            diff_ = cs_h - cs_r[h:h + 1, :]
            parts.append((g_mat[g] * jnp.exp(jnp.where(keep, diff_, NEG))).astype(BF16))
        parts.append((c_dup[g] * jnp.exp(jnp.where(lane_lo, cs_b[0], cs_b[1]))).astype(BF16))
        yield
        lhs = jnp.concatenate(parts, axis=1)
        s_old = st_ref[j]
        rhs = jnp.concatenate([jnp.where(lane_lo, xdt, 0.0).astype(BF16),
                               jnp.where(lane_lo, 0.0, xdt).astype(BF16),
                               s_old.astype(BF16)], axis=0)
        y_pair = _dot(lhs, rhs)
        btg = bT[g * SSD_STATE:(g + 1) * SSD_STATE, :]
        bd = jnp.concatenate([btg * jnp.exp(tot_r[h0:h0 + 1, :] - cs_r[h0:h0 + 1, :]),
                              btg * jnp.exp(tot_r[h1:h1 + 1, :] - cs_r[h1:h1 + 1, :])], axis=0)
        s_upd = _dot(bd.astype(BF16), xdt.astype(BF16))
        yield
        carry = jnp.where(row_lo, jnp.exp(tot_c[:, h0:h0 + 1]), jnp.exp(tot_c[:, h1:h1 + 1]))
        st_ref[j] = s_old * carry + jnp.where(blockdiag, s_upd, 0.0)
        y_ref[bi, :, j * CHUNK:(j + 1) * CHUNK] = y_pair.astype(y_ref.dtype)

    return [pair_chain(j) for j in range(SSD_HEADS // 2)]


def _interleave(chains, period=1):
    pending = [(i % period, c) for i, c in enumerate(chains)]
    rnd = 0
    while pending:
        alive = []
        for delay, c in pending:
            if rnd >= delay:
                try:
                    next(c)
                except StopIteration:
                    continue
            alive.append((delay, c))
        pending = alive
        rnd += 1


def _ssd_kernel(xs_f, bc_f, dt_f, dtT_f, xs_r, bc_r, dt_r, dtT_r, bias_r, bias_c, a_r, a_c,
                yf_ref, yr_ref, st_ref, *, n_chunks):
    step = pl.program_id(1)
    n_pairs = SSD_HEADS // 2

    @pl.when(step == 0)
    def _():
        st_ref[...] = jnp.zeros(st_ref.shape, F32)

    scans = []
    for bi in range(xs_f.shape[0]):
        lo = 2 * bi * n_pairs
        scans.append(_ssd_direction(xs_f, bc_f, dt_f, dtT_f, bias_r, bias_c, a_r, a_c, yf_ref,
                                    st_ref.at[lo:lo + n_pairs], bi=bi, reverse=False, is_meta=step == 0))
        scans.append(_ssd_direction(xs_r, bc_r, dt_r, dtT_r, bias_r, bias_c, a_r, a_c, yr_ref,
                                    st_ref.at[lo + n_pairs:lo + 2 * n_pairs], bi=bi, reverse=True,
                                    is_meta=step == n_chunks - 1))
    _interleave([c for group in zip(*scans) for c in group], period=SSD_SKEW)


def _chunk_order(n_chunks, reverse):
    if reverse:
        return lambda c: (2 * n_chunks - 2 - c) % n_chunks
    return lambda c: (c + n_chunks - 1) % n_chunks


def _scan_specs(arrays, n_chunks, reverse, nb):
    order = _chunk_order(n_chunks, reverse)
    specs = []
    for a, transposed in arrays:
        if transposed:
            specs.append(pl.BlockSpec((nb, a.shape[1], CHUNK), lambda i, s: (i, 0, order(s))))
        else:
            specs.append(pl.BlockSpec((nb, CHUNK, a.shape[2]), lambda i, s: (i, order(s), 0)))
    return specs


def _scan_batch(b):
    return 2 if b % 2 == 0 else 1


def _ssd_scan(xs, bc, dt, dtT, consts):
    b, tp, _ = xs.shape
    n_chunks = tp // CHUNK
    nb = _scan_batch(b)
    arrays = [(xs, False), (bc, False), (dt, False), (dtT, True)]
    in_specs = (_scan_specs(arrays, n_chunks, False, nb) + _scan_specs(arrays, n_chunks, True, nb)
                + [_full(c.shape) for c in consts])
    out_specs = (_scan_specs([(xs, False)], n_chunks, False, nb)
                 + _scan_specs([(xs, False)], n_chunks, True, nb))
    out_shape = [jax.ShapeDtypeStruct((b, tp, SSD_INNER), BF16)] * 2
    return pl.pallas_call(
        functools.partial(_ssd_kernel, n_chunks=n_chunks),
        grid=(b // nb, n_chunks), in_specs=in_specs, out_specs=out_specs, out_shape=out_shape,
        scratch_shapes=[pltpu.VMEM((nb * SSD_HEADS, CHUNK, CHUNK), F32)],
        compiler_params=_params("parallel", "arbitrary"),
        name="ssd_scan")(xs, bc, dt, dtT, xs, bc, dt, dtT, *consts)


def _mlstm_direction(q_ref, k_ref, v_ref, gt_ref, gtT_ref, gb_r, gb_c, y_ref, st_ref, m_ref, *,
                     bi, reverse, is_meta):
    d = 1 if reverse else 0
    nh = MLSTM_HEADS
    keep, tri_c, tri_r, row, col = _tri_masks(reverse)
    gt = gt_ref[bi] + gb_r[...]
    gtT = gtT_ref[bi] + gb_c[...]
    i_lo, f_lo = d * nh, 2 * nh + d * nh
    pad_c = jnp.logical_and(is_meta, row[:, :nh] < N_PAD)
    pad_r = jnp.logical_and(is_meta, col[:nh, :] < N_PAD)
    ig_r = jnp.where(pad_r, NEG, gtT[i_lo:i_lo + nh, :])
    fg_c = jnp.where(pad_c, 0.0, _log_sigmoid(gt[:, f_lo:f_lo + nh]))
    fg_r = jnp.where(pad_r, 0.0, _log_sigmoid(gtT[f_lo:f_lo + nh, :]))
    b_c = _cumsum_cols(tri_c, fg_c)
    b_r = _cumsum_rows(fg_r, tri_r)
    last = 0 if reverse else CHUNK - 1

    lane_lo = col < MLSTM_QK
    row_lo = row < MLSTM_QK
    ones = jnp.ones((CHUNK, MLSTM_V), BF16)
    scale = MLSTM_QK ** -0.5

    def head_chain(h):
        pair = h // 2
        lo = (h % 2) == 0
        qp = q_ref[bi, :, pair * CHUNK:(pair + 1) * CHUNK]
        kp = k_ref[bi, :, pair * CHUNK:(pair + 1) * CHUNK]
        qm = (jnp.where(lane_lo if lo else jnp.logical_not(lane_lo), qp, 0.0) * scale).astype(BF16)
        kT = jnp.where(row_lo if lo else jnp.logical_not(row_lo), kp.astype(F32).T, 0.0)
        v_aug = jnp.concatenate([v_ref[bi, :, h * MLSTM_V:(h + 1) * MLSTM_V].astype(BF16), ones], axis=1)
        m_st = m_ref[h][0:1, 0:1]
        bc = b_c[:, h:h + 1]
        br = b_r[h:h + 1, :]
        ir = ig_r[h:h + 1, :]
        s_raw = _dot_nt(qm, kp.astype(BF16))
        c_st = st_ref[h]
        inter_mm = _dot(qm, c_st.astype(BF16))
        yield
        dmat = jnp.where(keep, ir - br, -jnp.inf)
        m_rel = jnp.maximum(jnp.max(dmat, axis=1, keepdims=True), m_st)
        tot = br[:, last:last + 1]
        d_last = tot - br + ir
        m_new = jnp.maximum(tot + m_st, jnp.max(d_last, axis=1, keepdims=True))
        yield
        w_intra = jnp.exp(dmat - m_rel)
        w_inter = jnp.exp(m_st - m_rel)
        w_s = jnp.exp(d_last - m_new)
        w_prev = jnp.exp(tot + m_st - m_new)
        yield
        s = s_raw * w_intra
        intra_mm = _dot(s.astype(BF16), v_aug)
        upd = _dot((kT * w_s).astype(BF16), v_aug)
        yield
        comb = intra_mm + w_inter * inter_mm
        num = comb[:, :MLSTM_V]
        den = jnp.maximum(jnp.abs(comb[:, MLSTM_V:]), jnp.exp(-(bc + m_rel)))
        y_ref[bi, :, h * MLSTM_V:(h + 1) * MLSTM_V] = (num / den).astype(y_ref.dtype)
        st_ref[h] = w_prev * c_st + upd
        m_ref[h] = jnp.broadcast_to(m_new, m_ref.shape[1:])

    return [head_chain(h) for h in range(nh)]


def _mlstm_kernel(q_f, k_f, v_f, gt_f, gtT_f, q_r, k_r, v_r, gt_r, gtT_r, gb_r, gb_c,
                  yf_ref, yr_ref, st_ref, m_ref, *, n_chunks):
    step = pl.program_id(1)
    nh = MLSTM_HEADS

    @pl.when(step == 0)
    def _():
        st_ref[...] = jnp.zeros(st_ref.shape, F32)
        m_ref[...] = jnp.full(m_ref.shape, NEG, F32)

    scans = []
    for bi in range(q_f.shape[0]):
        lo = 2 * bi * nh
        scans.append(_mlstm_direction(q_f, k_f, v_f, gt_f, gtT_f, gb_r, gb_c, yf_ref, st_ref.at[lo:lo + nh],
                                      m_ref.at[lo:lo + nh], bi=bi, reverse=False, is_meta=step == 0))
        scans.append(_mlstm_direction(q_r, k_r, v_r, gt_r, gtT_r, gb_r, gb_c, yr_ref,
                                      st_ref.at[lo + nh:lo + 2 * nh], m_ref.at[lo + nh:lo + 2 * nh],
                                      bi=bi, reverse=True, is_meta=step == n_chunks - 1))
    _interleave([c for group in zip(*scans) for c in group], period=MLSTM_SKEW)


def _mlstm_scan(q, k, v, gt, gtT, consts):
    b, tp, _ = q.shape
    n_chunks = tp // CHUNK
    nb = _scan_batch(b)
    arrays = [(q, False), (k, False), (v, False), (gt, False), (gtT, True)]
    in_specs = (_scan_specs(arrays, n_chunks, False, nb) + _scan_specs(arrays, n_chunks, True, nb)
                + [_full(c.shape) for c in consts])
    out_specs = (_scan_specs([(v, False)], n_chunks, False, nb)
                 + _scan_specs([(v, False)], n_chunks, True, nb))
    out_shape = [jax.ShapeDtypeStruct(v.shape, BF16)] * 2
    return pl.pallas_call(
        functools.partial(_mlstm_kernel, n_chunks=n_chunks),
        grid=(b // nb, n_chunks), in_specs=in_specs, out_specs=out_specs, out_shape=out_shape,
        scratch_shapes=[pltpu.VMEM((nb * 2 * MLSTM_HEADS, CHUNK, 2 * MLSTM_V), F32),
                        pltpu.VMEM((nb * 2 * MLSTM_HEADS, 8, LANE), F32)],
        compiler_params=_params("parallel", "arbitrary"),
        name="mlstm_scan")(q, k, v, gt, gtT, q, k, v, gt, gtT, *consts)


def _out_kernel(ya_ref, yf_ref, yr_ref, xs_ref, z_ref, yc_ref, hf_ref, hr_ref, og_ref, h_ref,
                ag_ref, dsk_ref, ng_ref, mg_ref, w_ref, o_ref, *, tp, seq):
    tm = h_ref.shape[0]
    ya = _rms_rows(ya_ref[...].astype(F32), ag_ref[...])
    z = z_ref[...].astype(F32)
    yb = ((yf_ref[...].astype(F32) + yr_ref[...].astype(F32) + dsk_ref[...] * xs_ref[...])
          * (z * _sigmoid(z)))
    parts = [ya.astype(BF16), _rms_rows(yb, ng_ref[...]).astype(BF16), yc_ref[...].astype(BF16)]
    for hd in range(MLSTM_HEADS):
        lo, hi = hd * MLSTM_V, (hd + 1) * MLSTM_V
        hsum = hf_ref[:, lo:hi].astype(F32) + hr_ref[:, lo:hi].astype(F32)
        gate = _sigmoid(og_ref[:, lo:hi].astype(F32))
        parts.append((gate * _rms_rows(hsum, mg_ref[...])).astype(BF16))
    out = h_ref[...] + _dot(jnp.concatenate(parts, axis=1), w_ref[...])
    t = (pl.program_id(0) * tm + lax.broadcasted_iota(jnp.int32, (tm, 1), 0)) % tp
    is_pad = jnp.logical_and(t >= seq, t < seq + N_PAD)
    o_ref[...] = jnp.where(is_pad, 0.0, out)


def _out_proj(mixer_outs, h, consts, w, *, tp, seq, tm):
    rows, d = h.shape
    row_spec = lambda c: pl.BlockSpec((tm, c), lambda i: (i, 0))
    return pl.pallas_call(
        functools.partial(_out_kernel, tp=tp, seq=seq), grid=(rows // tm,),
        in_specs=([row_spec(a.shape[1]) for a in mixer_outs] + [row_spec(d)]
                  + [_full(c.shape) for c in consts] + [_full(w.shape)]),
        out_specs=row_spec(d), out_shape=jax.ShapeDtypeStruct((rows, d), F32),
        compiler_params=_params("parallel"), name="out_proj")(*mixer_outs, h, *consts, w)


def _ffn_kernel(h_ref, g_ref, wg_ref, wu_ref, wo_ref, o_ref, *, n_split):
    x = h_ref[0]
    hn = _rms_rows(x, g_ref[...]).astype(BF16)
    f = wg_ref.shape[1]
    tf = f // n_split
    acc = x
    for c in range(n_split):
        gate = _dot(hn, wg_ref[:, c * tf:(c + 1) * tf])
        up = _dot(hn, wu_ref[:, c * tf:(c + 1) * tf])
        act = (gate * _sigmoid(gate) * up).astype(BF16)
        acc = acc + _dot(act, wo_ref[c * tf:(c + 1) * tf, :])
    o_ref[0] = acc


def _ffn(h, g, wg, wu, wo, *, t_out, tm):
    b, _, d = h.shape
    f = wg.shape[1]
    n_split = 2 if (f // 2) % LANE == 0 else 1
    row_spec = pl.BlockSpec((1, tm, d), lambda i, j: (i, j, 0))
    resident = lambda a: pl.BlockSpec(a.shape, lambda i, j: (0, 0), pipeline_mode=pl.Buffered(1))
    return pl.pallas_call(
        functools.partial(_ffn_kernel, n_split=n_split), grid=(b, t_out // tm),
        in_specs=[row_spec, _full(g.shape), resident(wg), resident(wu), resident(wo)],
        out_specs=row_spec, out_shape=jax.ShapeDtypeStruct((b, t_out, d), F32),
        compiler_params=_params("parallel", "parallel"), name="ffn")(h, g, wg, wu, wo)


def _rope_tables_t(pos, rot_dim):
    inv = 1.0 / (ROPE_THETA ** (jnp.arange(0, rot_dim, 2, dtype=F32) / rot_dim))
    ang = pos[:, None] * inv[None, :]
    return jnp.cos(ang).T, jnp.sin(ang).T


def _col(v, n=None):
    v = v.astype(F32)
    if n is not None:
        v = jnp.pad(v, (0, n - v.shape[0]))
    return v[:, None]


def _row(v):
    return v.astype(F32)[None, :]


def kernel(x, meta_tokens, attn_norm_g, w_in, mla_q_norm_g, mla_kv_norm_g, mla_w_uq, mla_w_ukv, mla_q_head_g, mla_k_head_g, mla_out_g, ssd_conv_w, ssd_conv_b, ssd_dt_bias, ssd_a_log, ssd_d, ssd_norm_g, diff_q_head_g, diff_k_head_g, diff_lambda, diff_out_g, mlstm_i_bias, mlstm_f_bias, mlstm_out_g, w_out, ffn_norm_g, w_ffn_in, w_ffn_out):
    b, seq, d = x.shape
    depth = w_in.shape[0]
    tp = seq + CHUNK
    assert seq % CHUNK == 0
    tm_in = _pick_tile(tp, (640, 384, 128))
    tm_row = _pick_tile(tp, (1664, 640, 384, 128))
    tq = _pick_tile(tp, (1664, 640, 384, 128))
    tk = _pick_tile(seq, (256, 128))
    tk_diff = _pick_tile(seq, (512, 256, 128))
    tm_flat = _pick_tile(b * tp, (512, 256, 128))
    tm_out = _pick_tile(seq, (512, 256, 128))

    meta = jnp.broadcast_to(meta_tokens[None].astype(x.dtype), (b, N_META, d))
    h = jnp.concatenate([x, jnp.zeros((b, N_PAD, d), x.dtype), meta], axis=1).reshape(b * tp, d)
    pos = jnp.concatenate([N_META + jnp.arange(seq, dtype=F32), jnp.zeros((N_PAD,), F32),
                           jnp.arange(N_META, dtype=F32)])
    cos_m, sin_m = _rope_tables_t(pos, MLA_ROPE)
    cos_d, sin_d = _rope_tables_t(pos, DIFF_ROPE)

    sizes = (MLA_Q_RANK, MLA_KV_RANK, MLA_ROPE, SSD_INNER, SSD_CONV_CH, 2 * SSD_HEADS,
             2 * DIFF_HEADS * DIFF_QK, 2 * DIFF_HEADS * DIFF_QK, DIFF_HEADS * DIFF_V,
             MLSTM_HEADS * MLSTM_QK, MLSTM_HEADS * MLSTM_QK, MLSTM_HEADS * MLSTM_V,
             MLSTM_HEADS * MLSTM_V, 2 * MLSTM_HEADS, 2 * MLSTM_HEADS)
    offs = [0]
    for s_ in sizes:
        offs.append(offs[-1] + s_)

    def cols(w, first, last):
        return w[:, offs[first]:offs[last + 1]]

    def pad_cols(w, n):
        return jnp.pad(w, ((0, 0), (0, n - w.shape[1])))

    for l in range(depth):
        lambda_init = 0.8 - 0.6 * math.exp(-0.3 * l)
        wl = w_in[l]
        h3 = h.reshape(b, tp, d)
        g_attn = _row(attn_norm_g[l])

        w_a = cols(wl, 0, 1).astype(BF16)
        w_kr = cols(wl, 2, 2).T.astype(BF16)
        w_uq = mla_w_uq[l].T
        w_ukv = mla_w_ukv[l].T.reshape(MLA_HEADS, MLA_NOPE + MLA_V, MLA_KV_RANK)
        w_uk = w_ukv[:, :MLA_NOPE].reshape(MLA_HEADS * MLA_NOPE, MLA_KV_RANK)
        w_uv = w_ukv[:, MLA_NOPE:].reshape(MLA_HEADS * MLA_V, MLA_KV_RANK)
        qT_a, k_a, vT_a = _in_proj_call(
            _mla_in_kernel, h3,
            [g_attn, w_a, _row(mla_q_norm_g[l]), _row(mla_kv_norm_g[l]), w_uq.astype(BF16),
             w_uk.astype(BF16), w_uv.astype(BF16), w_kr, _col(mla_q_head_g[l], HEAD_PAD),
             _col(mla_k_head_g[l], HEAD_PAD), (cos_m,), (sin_m,)],
            [((MLA_HEADS * HEAD_PAD, tp), BF16, "col"), ((MLA_HEADS, tp, HEAD_PAD), BF16, "head"),
             ((MLA_HEADS * MLA_V, tp), BF16, "col")], tm_in, "mla_in")
        y_a = _attention(qT_a, k_a, vT_a, [], diff=False, seq=seq, tq=tq, tk=tk, out_scale=1.0,
                         nsh=MLA_HEADS_PER_STEP, name="mla_attn")

        w_c = cols(wl, 6, 8).T.astype(BF16)
        qT_c, k_c, vT_c = _in_proj_call(
            _diff_in_kernel, h3,
            [g_attn, w_c, _col(diff_q_head_g[l], HEAD_PAD), _col(diff_k_head_g[l], HEAD_PAD),
             (cos_d,), (sin_d,)],
            [((2 * DIFF_HEADS * HEAD_PAD, tp), BF16, "col"), ((2 * DIFF_HEADS, tp, HEAD_PAD), BF16, "head"),
             ((DIFF_HEADS * DIFF_V, tp), BF16, "col")], tm_in, "diff_in")
        lam = diff_lambda[l].astype(F32)
        lam_full = jnp.exp(jnp.sum(lam[0] * lam[1])) - jnp.exp(jnp.sum(lam[2] * lam[3])) + lambda_init
        y_c = _attention(qT_c, k_c, vT_c, [jnp.full((8, LANE), lam_full, F32), _col(diff_out_g[l])],
                         diff=True, seq=seq, tq=tq, tk=tk_diff, out_scale=1.0 - lambda_init, nsh=2,
                         name="diff_attn")

        n_b = SSD_INNER + SSD_CONV_CH + 2 * SSD_HEADS
        w_b = pad_cols(cols(wl, 3, 5), -(-n_b // LANE) * LANE).astype(BF16)
        w_dt = cols(wl, 5, 5).T.astype(BF16)
        z_b, xbc, dt, dtT = _in_proj_call(
            _ssd_in_kernel, h3, [g_attn, w_b, w_dt],
            [((tp, SSD_INNER), BF16, "row"), ((tp, SSD_CONV_CH), F32, "row"),
             ((tp, 2 * SSD_HEADS), F32, "row"), ((2 * SSD_HEADS, tp), F32, "col")], tm_row, "ssd_in")
        w8 = jnp.pad(ssd_conv_w[l].astype(F32), ((0, 8 - SSD_CONV), (0, 0)))
        xs_b, bc_b = _ssd_conv(xbc, w8, _row(ssd_conv_b[l]), tm_in)
        dt_bias = ssd_dt_bias[l].astype(F32).reshape(-1)
        a_neg = -jnp.exp(ssd_a_log[l].astype(F32)).reshape(-1)
        y_bf, y_br = _ssd_scan(xs_b, bc_b, dt, dtT, [_row(dt_bias), _col(dt_bias), _row(a_neg), _col(a_neg)])

        n_d = 2 * MLSTM_HEADS * MLSTM_QK + 2 * MLSTM_HEADS * MLSTM_V + 4 * MLSTM_HEADS
        w_d = pad_cols(cols(wl, 9, 14), -(-n_d // LANE) * LANE).astype(BF16)
        w_g = cols(wl, 13, 14).T.astype(BF16)
        q_d, k_d, v_d, o_d, gt, gtT = _in_proj_call(
            _mlstm_in_kernel, h3, [g_attn, w_d, w_g],
            [((tp, MLSTM_HEADS * MLSTM_QK), BF16, "row"), ((tp, MLSTM_HEADS * MLSTM_QK), BF16, "row"),
             ((tp, MLSTM_HEADS * MLSTM_V), BF16, "row"), ((tp, MLSTM_HEADS * MLSTM_V), BF16, "row"),
             ((tp, 4 * MLSTM_HEADS), F32, "row"), ((4 * MLSTM_HEADS, tp), F32, "col")], tm_row, "mlstm_in")
        gate_bias = jnp.concatenate([mlstm_i_bias[l].reshape(-1), mlstm_f_bias[l].reshape(-1)]).astype(F32)
        h_df, h_dr = _mlstm_scan(q_d, k_d, v_d, gt, gtT, [_row(gate_bias), _col(gate_bias)])

        flat = lambda a: a.reshape(b * tp, a.shape[2])
        mixer_outs = [flat(a) for a in (y_a, y_bf, y_br, xs_b, z_b, y_c, h_df, h_dr, o_d)]
        out_consts = [_row(mla_out_g[l]), _row(jnp.repeat(ssd_d[l], SSD_HEAD_DIM)), _row(ssd_norm_g[l]),
                      _row(mlstm_out_g[l])]
        h = _out_proj(mixer_outs, h, out_consts, w_out[l].astype(BF16), tp=tp, seq=seq, tm=tm_flat)
        f = w_ffn_out.shape[1]
        last = l == depth - 1
        h = _ffn(h.reshape(b, tp, d), _row(ffn_norm_g[l]), w_ffn_in[l][:, :f].astype(BF16),
                 w_ffn_in[l][:, f:].astype(BF16), w_ffn_out[l].astype(BF16),
                 t_out=seq if last else tp, tm=tm_out if last else tm_in)
        h = h.reshape(-1, d)

    return h.reshape(b, seq, d).astype(x.dtype)
```

```python
import functools
import math

import jax
import jax.numpy as jnp
from jax import lax
from jax.experimental import pallas as pl
from jax.experimental.pallas import tpu as pltpu

N_META = 16
ROPE_THETA = 500000.0
EPS = 1e-6
CHUNK = 128
N_PAD = CHUNK - N_META
NEG = -1e30
LOG2E = 1.4426950408889634

MLA_HEADS, MLA_NOPE, MLA_ROPE, MLA_V = 8, 64, 32, 64
MLA_QK = MLA_NOPE + MLA_ROPE
MLA_Q_RANK, MLA_KV_RANK = 384, 256
SSD_HEADS, SSD_HEAD_DIM, SSD_GROUPS, SSD_STATE, SSD_CONV = 8, 64, 2, 64, 5
SSD_INNER = SSD_HEADS * SSD_HEAD_DIM
SSD_CONV_CH = SSD_INNER + 2 * SSD_GROUPS * SSD_STATE
DIFF_HEADS, DIFF_QK = 4, 64
DIFF_V = 2 * DIFF_QK
DIFF_ROPE = DIFF_QK // 4
MLSTM_HEADS, MLSTM_QK, MLSTM_V = 4, 64, 128
HEAD_PAD = 128
LANE = 128
Q_STRIP = 256
ATTN_HEADS_PER_STEP = 4
SSD_SKEW, MLSTM_SKEW = 4, 3
VMEM_LIMIT = 52 * 1024 * 1024

F32 = jnp.float32
BF16 = jnp.bfloat16
EXP_DTYPE = jnp.bfloat16


def _dot(a, b):
    return jnp.dot(a, b, preferred_element_type=F32)


def _dot_nt(a, b):
    return lax.dot_general(a, b, (((1,), (1,)), ((), ())), preferred_element_type=F32)


def _rms_rows(x, g):
    ms = jnp.mean(x * x, axis=-1, keepdims=True)
    return x * lax.rsqrt(ms + EPS) * g


def _split3(a):
    hi = a.astype(BF16)
    r1 = a - hi.astype(F32)
    mid = r1.astype(BF16)
    lo = (r1 - mid.astype(F32)).astype(BF16)
    return hi, mid, lo


def _cumsum_cols(tri, a):
    hi, mid, lo = _split3(a)
    return _dot(tri, hi) + _dot(tri, mid) + _dot(tri, lo)


def _cumsum_rows(a, tri):
    hi, mid, lo = _split3(a)
    return _dot(hi, tri) + _dot(mid, tri) + _dot(lo, tri)


def _softplus(x):
    return jnp.maximum(x, 0.0) + jnp.log(1.0 + jnp.exp(-jnp.abs(x)))


def _log_sigmoid(x):
    return jnp.minimum(x, 0.0) - jnp.log(1.0 + jnp.exp(-jnp.abs(x)))


def _sigmoid(x):
    return 1.0 / (1.0 + jnp.exp(-x))


def _pick_tile(n, candidates):
    for c in candidates:
        if n % c == 0:
            return c
    raise ValueError(f"no tile in {candidates} divides {n}")


def _params(*sem):
    return pltpu.CompilerParams(dimension_semantics=sem, vmem_limit_bytes=VMEM_LIMIT)


def _norm_rope_t(blk, g_col, cos, sin, n_real):
    r = cos.shape[0]
    ms = jnp.sum(blk * blk, axis=0, keepdims=True) * (1.0 / n_real)
    y = blk * lax.rsqrt(ms + EPS) * g_col
    x1, x2, rest = y[:r], y[r:2 * r], y[2 * r:]
    return jnp.concatenate([x1 * cos - x2 * sin, x2 * cos + x1 * sin, rest], axis=0)


def _mla_in_kernel(h_ref, g_ref, wa_ref, gq_ref, gkv_ref, wuq_ref, wuk_ref, wuv_ref, wkr_ref,
                   qhg_ref, khg_ref, cos_ref, sin_ref, qT_ref, k_ref, vT_ref):
    hn = _rms_rows(h_ref[0], g_ref[...]).astype(BF16)
    acc = _dot(hn, wa_ref[...])
    cqn = _rms_rows(acc[:, :MLA_Q_RANK], gq_ref[...]).astype(BF16)
    ckvn = _rms_rows(acc[:, MLA_Q_RANK:], gkv_ref[...]).astype(BF16)
    qT = _dot_nt(wuq_ref[...], cqn)
    knT = _dot_nt(wuk_ref[...], ckvn)
    vT = _dot_nt(wuv_ref[...], ckvn)
    krT = _dot_nt(wkr_ref[...], hn)
    cos, sin = cos_ref[...], sin_ref[...]
    t = krT.shape[1]
    zpad = jnp.zeros((HEAD_PAD - MLA_QK, t), F32)
    q_scale = (MLA_QK ** -0.5) * LOG2E
    for h in range(MLA_HEADS):
        qb = _norm_rope_t(qT[h * MLA_QK:(h + 1) * MLA_QK], qhg_ref[0:MLA_QK, :], cos, sin, MLA_QK)
        qT_ref[0, h * HEAD_PAD:h * HEAD_PAD + MLA_QK, :] = (qb * q_scale).astype(BF16)
        qT_ref[0, h * HEAD_PAD + MLA_QK:(h + 1) * HEAD_PAD, :] = zpad.astype(BF16)
        kb = jnp.concatenate([krT, knT[h * MLA_NOPE:(h + 1) * MLA_NOPE]], axis=0)
        kb = _norm_rope_t(kb, khg_ref[0:MLA_QK, :], cos, sin, MLA_QK)
        k_ref[0, h] = jnp.concatenate([kb, zpad], axis=0).T.astype(BF16)
    vT_ref[0] = vT.astype(BF16)


def _diff_in_kernel(h_ref, g_ref, wc_ref, qhg_ref, khg_ref, cos_ref, sin_ref, qT_ref, k_ref, vT_ref):
    hn = _rms_rows(h_ref[0], g_ref[...]).astype(BF16)
    pT = _dot_nt(wc_ref[...], hn)
    cos, sin = cos_ref[...], sin_ref[...]
    t = pT.shape[1]
    zpad = jnp.zeros((HEAD_PAD - DIFF_QK, t), F32)
    nq = 2 * DIFF_HEADS * DIFF_QK
    q_scale = (DIFF_QK ** -0.5) * LOG2E
    for h in range(2 * DIFF_HEADS):
        qb = _norm_rope_t(pT[h * DIFF_QK:(h + 1) * DIFF_QK], qhg_ref[0:DIFF_QK, :], cos, sin, DIFF_QK)
        qT_ref[0, h * HEAD_PAD:h * HEAD_PAD + DIFF_QK, :] = (qb * q_scale).astype(BF16)
        qT_ref[0, h * HEAD_PAD + DIFF_QK:(h + 1) * HEAD_PAD, :] = zpad.astype(BF16)
        kb = _norm_rope_t(pT[nq + h * DIFF_QK:nq + (h + 1) * DIFF_QK], khg_ref[0:DIFF_QK, :], cos, sin,
                          DIFF_QK)
        k_ref[0, h] = jnp.concatenate([kb, zpad], axis=0).T.astype(BF16)
    vT_ref[0] = pT[2 * nq:].astype(BF16)


def _ssd_in_kernel(h_ref, g_ref, wb_ref, wdt_ref, z_ref, xbc_ref, dt_ref, dtT_ref):
    hn = _rms_rows(h_ref[0], g_ref[...]).astype(BF16)
    acc = _dot(hn, wb_ref[...])
    z_ref[0] = acc[:, :SSD_INNER].astype(z_ref.dtype)
    xbc_ref[0] = acc[:, SSD_INNER:SSD_INNER + SSD_CONV_CH]
    dt_ref[0] = acc[:, SSD_INNER + SSD_CONV_CH:SSD_INNER + SSD_CONV_CH + 2 * SSD_HEADS]
    dtT_ref[0] = _dot_nt(wdt_ref[...], hn)


def _mlstm_in_kernel(h_ref, g_ref, wd_ref, wg_ref, q_ref, k_ref, v_ref, o_ref, gt_ref, gtT_ref):
    hn = _rms_rows(h_ref[0], g_ref[...]).astype(BF16)
    acc = _dot(hn, wd_ref[...])
    nqk = MLSTM_HEADS * MLSTM_QK
    nv = MLSTM_HEADS * MLSTM_V
    q_ref[0] = acc[:, :nqk].astype(q_ref.dtype)
    k_ref[0] = acc[:, nqk:2 * nqk].astype(k_ref.dtype)
    v_ref[0] = acc[:, 2 * nqk:2 * nqk + nv].astype(v_ref.dtype)
    o_ref[0] = acc[:, 2 * nqk + nv:2 * nqk + 2 * nv].astype(o_ref.dtype)
    gt_ref[0] = acc[:, 2 * nqk + 2 * nv:2 * nqk + 2 * nv + 4 * MLSTM_HEADS]
    gtT_ref[0] = _dot_nt(wg_ref[...], hn)


def _full(shape):
    nd = len(shape)
    return pl.BlockSpec(shape, lambda *_: (0,) * nd)


def _in_proj_call(body, h, consts, outs, tm, name):
    b, tp, d = h.shape
    in_specs = [pl.BlockSpec((1, tm, d), lambda i, j: (i, j, 0))]
    for c in consts:
        if isinstance(c, tuple):
            in_specs.append(pl.BlockSpec((c[0].shape[0], tm), lambda i, j: (0, j)))
        else:
            in_specs.append(_full(c.shape))
    out_shapes, out_specs = [], []
    for shape, dtype, kind in outs:
        out_shapes.append(jax.ShapeDtypeStruct((b,) + shape, dtype))
        if kind == "row":
            out_specs.append(pl.BlockSpec((1, tm, shape[1]), lambda i, j: (i, j, 0)))
        elif kind == "col":
            out_specs.append(pl.BlockSpec((1, shape[0], tm), lambda i, j: (i, 0, j)))
        else:
            out_specs.append(pl.BlockSpec((1, shape[0], tm, shape[2]), lambda i, j: (i, 0, j, 0)))
    args = [h] + [c[0] if isinstance(c, tuple) else c for c in consts]
    return pl.pallas_call(
        body, grid=(b, tp // tm), in_specs=in_specs, out_specs=out_specs, out_shape=out_shapes,
        compiler_params=_params("parallel", "parallel"), name=name)(*args)


def _attn_kernel(*refs, diff, seq, tk, out_scale, nsh):
    if diff:
        qT_ref, k_ref, vT_ref, lam_ref, og_ref, o_ref, s_scr, st_scr, m_scr, acc_scr = refs
    else:
        qT_ref, k_ref, vT_ref, o_ref, s_scr, st_scr, m_scr, acc_scr = refs
    dv = DIFF_V if diff else MLA_V
    tq = qT_ref.shape[2]
    n_chunks = seq // tk
    assert n_chunks % 2 == 0
    m_scr[...] = jnp.full(m_scr.shape, NEG, F32)
    acc_scr[...] = jnp.zeros(acc_scr.shape, F32)

    def stage(nxt, cur):
        k_n, v_aug = [], []
        for hh in range(nsh):
            if nxt is not None:
                k_n.append(k_ref[0, hh, pl.ds(nxt[1], nxt[2]), :])
            if cur is not None:
                v_lo = (hh // 2) * dv if diff else hh * dv
                ones = jnp.ones((16, cur[2]), BF16)
                v_aug.append(jnp.concatenate([vT_ref[0, v_lo:v_lo + dv, pl.ds(cur[1], cur[2])], ones],
                                             axis=0))
        pending = None
        for j0 in range(0, tq, Q_STRIP):
            w = min(Q_STRIP, tq - j0)
            for hh in range(nsh):
                if cur is not None:
                    src = st_scr.at[hh] if cur[0] is None else s_scr.at[hh, cur[0]]
                    s = src[:, j0:j0 + w]
                    m_old = m_scr[hh, 0:1, j0:j0 + w]
                    m_new = jnp.maximum(m_old, jnp.max(s, axis=0, keepdims=True))
                    alpha = jnp.exp2(m_old - m_new)
                    p = jnp.exp2((s - m_new).astype(EXP_DTYPE)).astype(BF16)
                    m_scr[hh, 0:1, j0:j0 + w] = m_new
                if nxt is not None:
                    dst = st_scr.at[hh] if nxt[0] is None else s_scr.at[hh, nxt[0]]
                    s_n = _dot(k_n[hh], qT_ref[0, hh * HEAD_PAD:(hh + 1) * HEAD_PAD, j0:j0 + w])
                    if nxt[3]:
                        row = lax.broadcasted_iota(jnp.int32, s_n.shape, 0)
                        s_n = jnp.where(row >= N_PAD, s_n, NEG)
                    dst[:, j0:j0 + w] = s_n
                if cur is not None:
                    if pending is not None:
                        ph, pj, pw, pa, pp = pending
                        acc_scr[ph, :, pj:pj + pw] = pa * acc_scr[ph, :, pj:pj + pw] + _dot(v_aug[ph], pp)
                    pending = (hh, j0, w, alpha, p)
        if pending is not None:
            ph, pj, pw, pa, pp = pending
            acc_scr[ph, :, pj:pj + pw] = pa * acc_scr[ph, :, pj:pj + pw] + _dot(v_aug[ph], pp)

    stage((0, 0, tk, False), None)

    def body(i, carry):
        off = pl.multiple_of(2 * i * tk, 2 * tk)
        stage((1, off + tk, tk, False), (0, off, tk))
        stage((0, off + 2 * tk, tk, False), (1, off + tk, tk))
        return carry

    lax.fori_loop(0, n_chunks // 2 - 1, body, 0)
    off = (n_chunks - 2) * tk
    stage((1, off + tk, tk, False), (0, off, tk))
    stage((None, seq, CHUNK, True), (1, off + tk, tk))
    stage(None, (None, seq, CHUNK))
    outs = []
    for hh in range(nsh):
        acc = acc_scr[hh]
        outs.append(acc[:dv] / acc[dv:dv + 1])
    if diff:
        pairs = []
        for hp in range(nsh // 2):
            o = outs[2 * hp] - lam_ref[0:1, 0:1] * outs[2 * hp + 1]
            ms = jnp.mean(o * o, axis=0, keepdims=True)
            pairs.append(o * lax.rsqrt(ms + EPS) * (og_ref[...] * out_scale))
        o = jnp.concatenate(pairs, axis=0) if len(pairs) > 1 else pairs[0]
    else:
        o = jnp.concatenate(outs, axis=0)
    o_ref[0] = o.T.astype(o_ref.dtype)


def _attention(qT, k, vT, extra, *, diff, seq, tq, tk, out_scale, nsh, name):
    b, _, tp = qT.shape
    groups = qT.shape[1] // (nsh * HEAD_PAD)
    dv = DIFF_V if diff else MLA_V
    v_rows = (nsh // 2) * dv if diff else nsh * dv
    o_cols = (nsh // 2) * LANE
    in_specs = [
        pl.BlockSpec((1, nsh * HEAD_PAD, tq), lambda i, g, j: (i, g, j)),
        pl.BlockSpec((1, nsh, tp, HEAD_PAD), lambda i, g, j: (i, g, 0, 0)),
        pl.BlockSpec((1, v_rows, tp), lambda i, g, j: (i, g, 0)),
    ] + [_full(e.shape) for e in extra]
    return pl.pallas_call(
        functools.partial(_attn_kernel, diff=diff, seq=seq, tk=tk, out_scale=out_scale, nsh=nsh),
        grid=(b, groups, tp // tq),
        in_specs=in_specs,
        out_specs=pl.BlockSpec((1, tq, o_cols), lambda i, g, j: (i, j, g)),
        out_shape=jax.ShapeDtypeStruct((b, tp, groups * o_cols), BF16),
        scratch_shapes=[pltpu.VMEM((nsh, 2, tk, tq), F32), pltpu.VMEM((nsh, CHUNK, tq), F32),
                        pltpu.VMEM((nsh, 8, tq), F32), pltpu.VMEM((nsh, dv + 16, tq), F32)],
        compiler_params=_params("parallel", "parallel", "arbitrary"), name=name)(qT, k, vT, *extra)


def _conv_kernel(x_ref, prev_ref, next_ref, w_ref, b_ref, xs_ref, bc_ref, scr):
    tc = x_ref.shape[1]
    scr[0:8, :] = prev_ref[0]
    scr[8:8 + tc, :] = x_ref[0]
    scr[8 + tc:16 + tc, :] = next_ref[0]
    acc = jnp.broadcast_to(b_ref[...], (tc, b_ref.shape[1]))
    for j in range(SSD_CONV):
        acc = acc + w_ref[j:j + 1, :] * scr[8 - SSD_CONV // 2 + j:8 - SSD_CONV // 2 + j + tc, :]
    act = acc * _sigmoid(acc)
    xs_ref[0] = act[:, :SSD_INNER]
    bc_ref[0] = act[:, SSD_INNER:]


def _ssd_conv(xbc, w8, bias, tc):
    b, tp, c = xbc.shape
    nb8 = tp // 8
    r8 = tc // 8
    row_spec = lambda n: pl.BlockSpec((1, tc, n), lambda i, j: (i, j, 0))
    return pl.pallas_call(
        _conv_kernel, grid=(b, tp // tc),
        in_specs=[row_spec(c),
                  pl.BlockSpec((1, 8, c), lambda i, j: (i, (j * r8 + nb8 - 1) % nb8, 0)),
                  pl.BlockSpec((1, 8, c), lambda i, j: (i, ((j + 1) * r8) % nb8, 0)),
                  _full(w8.shape), _full(bias.shape)],
        out_specs=[row_spec(SSD_INNER), row_spec(c - SSD_INNER)],
        out_shape=[jax.ShapeDtypeStruct((b, tp, SSD_INNER), F32),
                   jax.ShapeDtypeStruct((b, tp, c - SSD_INNER), F32)],
        scratch_shapes=[pltpu.VMEM((tc + 16, c), F32)],
        compiler_params=_params("parallel", "parallel"), name="ssd_conv")(xbc, xbc, xbc, w8, bias)


def _tri_masks(reverse):
    row = lax.broadcasted_iota(jnp.int32, (CHUNK, CHUNK), 0)
    col = lax.broadcasted_iota(jnp.int32, (CHUNK, CHUNK), 1)
    keep = (col >= row) if reverse else (col <= row)
    tri_c = keep.astype(BF16)
    tri_r = ((row >= col) if reverse else (row <= col)).astype(BF16)
    return keep, tri_c, tri_r, row, col


def _ssd_direction(xs_ref, bc_ref, dt_ref, dtT_ref, bias_r, bias_c, a_r, a_c, y_ref, st_ref, *,
                   bi, reverse, is_meta):
    d = 1 if reverse else 0
    h8 = SSD_HEADS
    keep, tri_c, tri_r, row, col = _tri_masks(reverse)
    bm = bc_ref[bi, :, :CHUNK]
    cm = bc_ref[bi, :, CHUNK:]

    dt_c = _softplus(dt_ref[bi][:, d * h8:(d + 1) * h8] + bias_r[:, d * h8:(d + 1) * h8])
    dt_r = _softplus(dtT_ref[bi][d * h8:(d + 1) * h8, :] + bias_c[d * h8:(d + 1) * h8, :])
    pad_c = jnp.logical_and(is_meta, row[:, :h8] < N_PAD)
    pad_r = jnp.logical_and(is_meta, col[:h8, :] < N_PAD)
    dt_c = jnp.where(pad_c, 0.0, dt_c)
    dt_r = jnp.where(pad_r, 0.0, dt_r)
    a_col = dt_c * a_r[:, d * h8:(d + 1) * h8]
    a_row = dt_r * a_c[d * h8:(d + 1) * h8, :]
    cs_c = _cumsum_cols(tri_c, a_col)
    cs_r = _cumsum_rows(a_row, tri_r)
    last = 0 if reverse else CHUNK - 1
    tot_r = cs_r[:, last:last + 1]
    tot_c = cs_c[last:last + 1, :]

    lane_lo = col < SSD_STATE
    row_lo = row < SSD_STATE
    blockdiag = jnp.logical_not(jnp.logical_xor(lane_lo, row_lo))
    cm_sw = pltpu.roll(cm, SSD_STATE, 1)
    c_dup = (jnp.where(lane_lo, cm, cm_sw), jnp.where(lane_lo, cm_sw, cm))
    g_mat = (_dot_nt(jnp.where(lane_lo, cm, 0.0).astype(BF16), bm.astype(BF16)),
             _dot_nt(jnp.where(lane_lo, 0.0, cm).astype(BF16), bm.astype(BF16)))
    bT = bm.T

    heads_per_group = SSD_HEADS // SSD_GROUPS

    def pair_chain(j):
        g = (2 * j) // heads_per_group
        h0, h1 = 2 * j, 2 * j + 1
        xp = xs_ref[bi, :, j * CHUNK:(j + 1) * CHUNK]
        dtp = jnp.where(lane_lo, dt_c[:, h0:h0 + 1], dt_c[:, h1:h1 + 1])
        xdt = xp * dtp
        cs_b = [jnp.broadcast_to(cs_c[:, h:h + 1], (CHUNK, CHUNK)) for h in (h0, h1)]
        yield
        parts = []
        for h, cs_h in zip((h0, h1), cs_b):
            diff_ = cs_h - cs_r[h:h + 1, :]
            parts.append((g_mat[g] * jnp.exp(jnp.where(keep, diff_, NEG))).astype(BF16))
        parts.append((c_dup[g] * jnp.exp(jnp.where(lane_lo, cs_b[0], cs_b[1]))).astype(BF16))
        yield
        lhs = jnp.concatenate(parts, axis=1)
        s_old = st_ref[j]
        rhs = jnp.concatenate([jnp.where(lane_lo, xdt, 0.0).astype(BF16),
                               jnp.where(lane_lo, 0.0, xdt).astype(BF16),
                               s_old.astype(BF16)], axis=0)
        y_pair = _dot(lhs, rhs)
        btg = bT[g * SSD_STATE:(g + 1) * SSD_STATE, :]
        bd = jnp.concatenate([btg * jnp.exp(tot_r[h0:h0 + 1, :] - cs_r[h0:h0 + 1, :]),
                              btg * jnp.exp(tot_r[h1:h1 + 1, :] - cs_r[h1:h1 + 1, :])], axis=0)
        s_upd = _dot(bd.astype(BF16), xdt.astype(BF16))
        yield
        carry = jnp.where(row_lo, jnp.exp(tot_c[:, h0:h0 + 1]), jnp.exp(tot_c[:, h1:h1 + 1]))
        st_ref[j] = s_old * carry + jnp.where(blockdiag, s_upd, 0.0)
        y_ref[bi, :, j * CHUNK:(j + 1) * CHUNK] = y_pair.astype(y_ref.dtype)

    return [pair_chain(j) for j in range(SSD_HEADS // 2)]


def _interleave(chains, period=1):
    pending = [(i % period, c) for i, c in enumerate(chains)]
    rnd = 0
    while pending:
        alive = []
        for delay, c in pending:
            if rnd >= delay:
                try:
                    next(c)
                except StopIteration:
                    continue
            alive.append((delay, c))
        pending = alive
        rnd += 1


def _ssd_kernel(xs_f, bc_f, dt_f, dtT_f, xs_r, bc_r, dt_r, dtT_r, bias_r, bias_c, a_r, a_c,
                yf_ref, yr_ref, st_ref, *, n_chunks):
    step = pl.program_id(1)
    n_pairs = SSD_HEADS // 2

    @pl.when(step == 0)
    def _():
        st_ref[...] = jnp.zeros(st_ref.shape, F32)

    scans = []
    for bi in range(xs_f.shape[0]):
        lo = 2 * bi * n_pairs
        scans.append(_ssd_direction(xs_f, bc_f, dt_f, dtT_f, bias_r, bias_c, a_r, a_c, yf_ref,
                                    st_ref.at[lo:lo + n_pairs], bi=bi, reverse=False, is_meta=step == 0))
        scans.append(_ssd_direction(xs_r, bc_r, dt_r, dtT_r, bias_r, bias_c, a_r, a_c, yr_ref,
                                    st_ref.at[lo + n_pairs:lo + 2 * n_pairs], bi=bi, reverse=True,
                                    is_meta=step == n_chunks - 1))
    _interleave([c for group in zip(*scans) for c in group], period=SSD_SKEW)


def _chunk_order(n_chunks, reverse):
    if reverse:
        return lambda c: (2 * n_chunks - 2 - c) % n_chunks
    return lambda c: (c + n_chunks - 1) % n_chunks


def _scan_specs(arrays, n_chunks, reverse, nb):
    order = _chunk_order(n_chunks, reverse)
    specs = []
    for a, transposed in arrays:
        if transposed:
            specs.append(pl.BlockSpec((nb, a.shape[1], CHUNK), lambda i, s: (i, 0, order(s))))
        else:
            specs.append(pl.BlockSpec((nb, CHUNK, a.shape[2]), lambda i, s: (i, order(s), 0)))
    return specs


def _scan_batch(b):
    return 2 if b % 2 == 0 else 1


def _ssd_scan(xs, bc, dt, dtT, consts):
    b, tp, _ = xs.shape
    n_chunks = tp // CHUNK
    nb = _scan_batch(b)
    arrays = [(xs, False), (bc, False), (dt, False), (dtT, True)]
    in_specs = (_scan_specs(arrays, n_chunks, False, nb) + _scan_specs(arrays, n_chunks, True, nb)
                + [_full(c.shape) for c in consts])
    out_specs = (_scan_specs([(xs, False)], n_chunks, False, nb)
                 + _scan_specs([(xs, False)], n_chunks, True, nb))
    out_shape = [jax.ShapeDtypeStruct((b, tp, SSD_INNER), BF16)] * 2
    return pl.pallas_call(
        functools.partial(_ssd_kernel, n_chunks=n_chunks),
        grid=(b // nb, n_chunks), in_specs=in_specs, out_specs=out_specs, out_shape=out_shape,
        scratch_shapes=[pltpu.VMEM((nb * SSD_HEADS, CHUNK, CHUNK), F32)],
        compiler_params=_params("parallel", "arbitrary"),
        name="ssd_scan")(xs, bc, dt, dtT, xs, bc, dt, dtT, *consts)


def _mlstm_direction(q_ref, k_ref, v_ref, gt_ref, gtT_ref, gb_r, gb_c, y_ref, st_ref, m_ref, *,
                     bi, reverse, is_meta):
    d = 1 if reverse else 0
    nh = MLSTM_HEADS
    keep, tri_c, tri_r, row, col = _tri_masks(reverse)
    gt = gt_ref[bi] + gb_r[...]
    gtT = gtT_ref[bi] + gb_c[...]
    i_lo, f_lo = d * nh, 2 * nh + d * nh
    pad_c = jnp.logical_and(is_meta, row[:, :nh] < N_PAD)
    pad_r = jnp.logical_and(is_meta, col[:nh, :] < N_PAD)
    ig_r = jnp.where(pad_r, NEG, gtT[i_lo:i_lo + nh, :])
    fg_c = jnp.where(pad_c, 0.0, _log_sigmoid(gt[:, f_lo:f_lo + nh]))
    fg_r = jnp.where(pad_r, 0.0, _log_sigmoid(gtT[f_lo:f_lo + nh, :]))
    b_c = _cumsum_cols(tri_c, fg_c)
    b_r = _cumsum_rows(fg_r, tri_r)
    last = 0 if reverse else CHUNK - 1

    lane_lo = col < MLSTM_QK
    row_lo = row < MLSTM_QK
    ones = jnp.ones((CHUNK, MLSTM_V), BF16)
    scale = MLSTM_QK ** -0.5

    def head_chain(h):
        pair = h // 2
        lo = (h % 2) == 0
        qp = q_ref[bi, :, pair * CHUNK:(pair + 1) * CHUNK]
        kp = k_ref[bi, :, pair * CHUNK:(pair + 1) * CHUNK]
        qm = (jnp.where(lane_lo if lo else jnp.logical_not(lane_lo), qp, 0.0) * scale).astype(BF16)
        kT = jnp.where(row_lo if lo else jnp.logical_not(row_lo), kp.astype(F32).T, 0.0)
        v_aug = jnp.concatenate([v_ref[bi, :, h * MLSTM_V:(h + 1) * MLSTM_V].astype(BF16), ones], axis=1)
        m_st = m_ref[h][0:1, 0:1]
        bc = b_c[:, h:h + 1]
        br = b_r[h:h + 1, :]
        ir = ig_r[h:h + 1, :]
        s_raw = _dot_nt(qm, kp.astype(BF16))
        c_st = st_ref[h]
        inter_mm = _dot(qm, c_st.astype(BF16))
        yield
        dmat = jnp.where(keep, ir - br, -jnp.inf)
        m_rel = jnp.maximum(jnp.max(dmat, axis=1, keepdims=True), m_st)
        tot = br[:, last:last + 1]
        d_last = tot - br + ir
        m_new = jnp.maximum(tot + m_st, jnp.max(d_last, axis=1, keepdims=True))
        yield
        w_intra = jnp.exp(dmat - m_rel)
        w_inter = jnp.exp(m_st - m_rel)
        w_s = jnp.exp(d_last - m_new)
        w_prev = jnp.exp(tot + m_st - m_new)
        yield
        s = s_raw * w_intra
        intra_mm = _dot(s.astype(BF16), v_aug)
        upd = _dot((kT * w_s).astype(BF16), v_aug)
        yield
        comb = intra_mm + w_inter * inter_mm
        num = comb[:, :MLSTM_V]
        den = jnp.maximum(jnp.abs(comb[:, MLSTM_V:]), jnp.exp(-(bc + m_rel)))
        y_ref[bi, :, h * MLSTM_V:(h + 1) * MLSTM_V] = (num / den).astype(y_ref.dtype)
        st_ref[h] = w_prev * c_st + upd
        m_ref[h] = jnp.broadcast_to(m_new, m_ref.shape[1:])

    return [head_chain(h) for h in range(nh)]


def _mlstm_kernel(q_f, k_f, v_f, gt_f, gtT_f, q_r, k_r, v_r, gt_r, gtT_r, gb_r, gb_c,
                  yf_ref, yr_ref, st_ref, m_ref, *, n_chunks):
    step = pl.program_id(1)
    nh = MLSTM_HEADS

    @pl.when(step == 0)
    def _():
        st_ref[...] = jnp.zeros(st_ref.shape, F32)
        m_ref[...] = jnp.full(m_ref.shape, NEG, F32)

    scans = []
    for bi in range(q_f.shape[0]):
        lo = 2 * bi * nh
        scans.append(_mlstm_direction(q_f, k_f, v_f, gt_f, gtT_f, gb_r, gb_c, yf_ref, st_ref.at[lo:lo + nh],
                                      m_ref.at[lo:lo + nh], bi=bi, reverse=False, is_meta=step == 0))
        scans.append(_mlstm_direction(q_r, k_r, v_r, gt_r, gtT_r, gb_r, gb_c, yr_ref,
                                      st_ref.at[lo + nh:lo + 2 * nh], m_ref.at[lo + nh:lo + 2 * nh],
                                      bi=bi, reverse=True, is_meta=step == n_chunks - 1))
    _interleave([c for group in zip(*scans) for c in group], period=MLSTM_SKEW)


def _mlstm_scan(q, k, v, gt, gtT, consts):
    b, tp, _ = q.shape
    n_chunks = tp // CHUNK
    nb = _scan_batch(b)
    arrays = [(q, False), (k, False), (v, False), (gt, False), (gtT, True)]
    in_specs = (_scan_specs(arrays, n_chunks, False, nb) + _scan_specs(arrays, n_chunks, True, nb)
                + [_full(c.shape) for c in consts])
    out_specs = (_scan_specs([(v, False)], n_chunks, False, nb)
                 + _scan_specs([(v, False)], n_chunks, True, nb))
    out_shape = [jax.ShapeDtypeStruct(v.shape, BF16)] * 2
    return pl.pallas_call(
        functools.partial(_mlstm_kernel, n_chunks=n_chunks),
        grid=(b // nb, n_chunks), in_specs=in_specs, out_specs=out_specs, out_shape=out_shape,
        scratch_shapes=[pltpu.VMEM((nb * 2 * MLSTM_HEADS, CHUNK, 2 * MLSTM_V), F32),
                        pltpu.VMEM((nb * 2 * MLSTM_HEADS, 8, LANE), F32)],
        compiler_params=_params("parallel", "arbitrary"),
        name="mlstm_scan")(q, k, v, gt, gtT, q, k, v, gt, gtT, *consts)


def _out_kernel(ya_ref, yf_ref, yr_ref, xs_ref, z_ref, yc_ref, hf_ref, hr_ref, og_ref, h_ref,
                ag_ref, dsk_ref, ng_ref, mg_ref, w_ref, o_ref, *, tp, seq):
    tm = h_ref.shape[0]
    ya = _rms_rows(ya_ref[...].astype(F32), ag_ref[...])
    z = z_ref[...].astype(F32)
    yb = ((yf_ref[...].astype(F32) + yr_ref[...].astype(F32) + dsk_ref[...] * xs_ref[...])
          * (z * _sigmoid(z)))
    parts = [ya.astype(BF16), _rms_rows(yb, ng_ref[...]).astype(BF16), yc_ref[...].astype(BF16)]
    for hd in range(MLSTM_HEADS):
        lo, hi = hd * MLSTM_V, (hd + 1) * MLSTM_V
        hsum = hf_ref[:, lo:hi].astype(F32) + hr_ref[:, lo:hi].astype(F32)
        gate = _sigmoid(og_ref[:, lo:hi].astype(F32))
        parts.append((gate * _rms_rows(hsum, mg_ref[...])).astype(BF16))
    out = h_ref[...] + _dot(jnp.concatenate(parts, axis=1), w_ref[...])
    t = (pl.program_id(0) * tm + lax.broadcasted_iota(jnp.int32, (tm, 1), 0)) % tp
    is_pad = jnp.logical_and(t >= seq, t < seq + N_PAD)
    o_ref[...] = jnp.where(is_pad, 0.0, out)


def _out_proj(mixer_outs, h, consts, w, *, tp, seq, tm):
    rows, d = h.shape
    row_spec = lambda c: pl.BlockSpec((tm, c), lambda i: (i, 0))
    return pl.pallas_call(
        functools.partial(_out_kernel, tp=tp, seq=seq), grid=(rows // tm,),
        in_specs=([row_spec(a.shape[1]) for a in mixer_outs] + [row_spec(d)]
                  + [_full(c.shape) for c in consts] + [_full(w.shape)]),
        out_specs=row_spec(d), out_shape=jax.ShapeDtypeStruct((rows, d), F32),
        compiler_params=_params("parallel"), name="out_proj")(*mixer_outs, h, *consts, w)


def _ffn_kernel(h_ref, g_ref, wg_ref, wu_ref, wo_ref, o_ref, *, n_split):
    x = h_ref[0]
    hn = _rms_rows(x, g_ref[...]).astype(BF16)
    f = wg_ref.shape[1]
    tf = f // n_split
    acc = x
    for c in range(n_split):
        gate = _dot(hn, wg_ref[:, c * tf:(c + 1) * tf])
        up = _dot(hn, wu_ref[:, c * tf:(c + 1) * tf])
        act = (gate * _sigmoid(gate) * up).astype(BF16)
        acc = acc + _dot(act, wo_ref[c * tf:(c + 1) * tf, :])
    o_ref[0] = acc


def _ffn(h, g, wg, wu, wo, *, t_out, tm):
    b, _, d = h.shape
    f = wg.shape[1]
    n_split = 2 if (f // 2) % LANE == 0 else 1
    row_spec = pl.BlockSpec((1, tm, d), lambda i, j: (i, j, 0))
    resident = lambda a: pl.BlockSpec(a.shape, lambda i, j: (0, 0), pipeline_mode=pl.Buffered(1))
    return pl.pallas_call(
        functools.partial(_ffn_kernel, n_split=n_split), grid=(b, t_out // tm),
        in_specs=[row_spec, _full(g.shape), resident(wg), resident(wu), resident(wo)],
        out_specs=row_spec, out_shape=jax.ShapeDtypeStruct((b, t_out, d), F32),
        compiler_params=_params("parallel", "parallel"), name="ffn")(h, g, wg, wu, wo)


def _rope_tables_t(pos, rot_dim):
    inv = 1.0 / (ROPE_THETA ** (jnp.arange(0, rot_dim, 2, dtype=F32) / rot_dim))
    ang = pos[:, None] * inv[None, :]
    return jnp.cos(ang).T, jnp.sin(ang).T


def _col(v, n=None):
    v = v.astype(F32)
    if n is not None:
        v = jnp.pad(v, (0, n - v.shape[0]))
    return v[:, None]


def _row(v):
    return v.astype(F32)[None, :]


def kernel(x, meta_tokens, attn_norm_g, w_in, mla_q_norm_g, mla_kv_norm_g, mla_w_uq, mla_w_ukv, mla_q_head_g, mla_k_head_g, mla_out_g, ssd_conv_w, ssd_conv_b, ssd_dt_bias, ssd_a_log, ssd_d, ssd_norm_g, diff_q_head_g, diff_k_head_g, diff_lambda, diff_out_g, mlstm_i_bias, mlstm_f_bias, mlstm_out_g, w_out, ffn_norm_g, w_ffn_in, w_ffn_out):
    b, seq, d = x.shape
    depth = w_in.shape[0]
    tp = seq + CHUNK
    assert seq % CHUNK == 0
    tm_in = _pick_tile(tp, (640, 384, 128))
    tm_row = _pick_tile(tp, (1664, 640, 384, 128))
    tq = _pick_tile(tp, (1664, 640, 384, 128))
    tk = _pick_tile(seq, (256, 128))
    tm_flat = _pick_tile(b * tp, (512, 256, 128))
    tm_out = _pick_tile(seq, (512, 256, 128))

    meta = jnp.broadcast_to(meta_tokens[None].astype(x.dtype), (b, N_META, d))
    h = jnp.concatenate([x, jnp.zeros((b, N_PAD, d), x.dtype), meta], axis=1).reshape(b * tp, d)
    pos = jnp.concatenate([N_META + jnp.arange(seq, dtype=F32), jnp.zeros((N_PAD,), F32),
                           jnp.arange(N_META, dtype=F32)])
    cos_m, sin_m = _rope_tables_t(pos, MLA_ROPE)
    cos_d, sin_d = _rope_tables_t(pos, DIFF_ROPE)

    sizes = (MLA_Q_RANK, MLA_KV_RANK, MLA_ROPE, SSD_INNER, SSD_CONV_CH, 2 * SSD_HEADS,
             2 * DIFF_HEADS * DIFF_QK, 2 * DIFF_HEADS * DIFF_QK, DIFF_HEADS * DIFF_V,
             MLSTM_HEADS * MLSTM_QK, MLSTM_HEADS * MLSTM_QK, MLSTM_HEADS * MLSTM_V,
             MLSTM_HEADS * MLSTM_V, 2 * MLSTM_HEADS, 2 * MLSTM_HEADS)
    offs = [0]
    for s_ in sizes:
        offs.append(offs[-1] + s_)

    def cols(w, first, last):
        return w[:, offs[first]:offs[last + 1]]

    def pad_cols(w, n):
        return jnp.pad(w, ((0, 0), (0, n - w.shape[1])))

    for l in range(depth):
        lambda_init = 0.8 - 0.6 * math.exp(-0.3 * l)
        wl = w_in[l]
        h3 = h.reshape(b, tp, d)
        g_attn = _row(attn_norm_g[l])

        w_a = cols(wl, 0, 1).astype(BF16)
        w_kr = cols(wl, 2, 2).T.astype(BF16)
        w_uq = mla_w_uq[l].T
        w_ukv = mla_w_ukv[l].T.reshape(MLA_HEADS, MLA_NOPE + MLA_V, MLA_KV_RANK)
        w_uk = w_ukv[:, :MLA_NOPE].reshape(MLA_HEADS * MLA_NOPE, MLA_KV_RANK)
        w_uv = w_ukv[:, MLA_NOPE:].reshape(MLA_HEADS * MLA_V, MLA_KV_RANK)
        qT_a, k_a, vT_a = _in_proj_call(
            _mla_in_kernel, h3,
            [g_attn, w_a, _row(mla_q_norm_g[l]), _row(mla_kv_norm_g[l]), w_uq.astype(BF16),
             w_uk.astype(BF16), w_uv.astype(BF16), w_kr, _col(mla_q_head_g[l], HEAD_PAD),
             _col(mla_k_head_g[l], HEAD_PAD), (cos_m,), (sin_m,)],
            [((MLA_HEADS * HEAD_PAD, tp), BF16, "col"), ((MLA_HEADS, tp, HEAD_PAD), BF16, "head"),
             ((MLA_HEADS * MLA_V, tp), BF16, "col")], tm_in, "mla_in")
        y_a = _attention(qT_a, k_a, vT_a, [], diff=False, seq=seq, tq=tq, tk=tk, out_scale=1.0,
                         nsh=ATTN_HEADS_PER_STEP, name="mla_attn")

        w_c = cols(wl, 6, 8).T.astype(BF16)
        qT_c, k_c, vT_c = _in_proj_call(
            _diff_in_kernel, h3,
            [g_attn, w_c, _col(diff_q_head_g[l], HEAD_PAD), _col(diff_k_head_g[l], HEAD_PAD),
             (cos_d,), (sin_d,)],
            [((2 * DIFF_HEADS * HEAD_PAD, tp), BF16, "col"), ((2 * DIFF_HEADS, tp, HEAD_PAD), BF16, "head"),
             ((DIFF_HEADS * DIFF_V, tp), BF16, "col")], tm_in, "diff_in")
        lam = diff_lambda[l].astype(F32)
        lam_full = jnp.exp(jnp.sum(lam[0] * lam[1])) - jnp.exp(jnp.sum(lam[2] * lam[3])) + lambda_init
        y_c = _attention(qT_c, k_c, vT_c, [jnp.full((8, LANE), lam_full, F32), _col(diff_out_g[l])],
                         diff=True, seq=seq, tq=tq, tk=tk, out_scale=1.0 - lambda_init, nsh=ATTN_HEADS_PER_STEP,
                         name="diff_attn")

        n_b = SSD_INNER + SSD_CONV_CH + 2 * SSD_HEADS
        w_b = pad_cols(cols(wl, 3, 5), -(-n_b // LANE) * LANE).astype(BF16)
        w_dt = cols(wl, 5, 5).T.astype(BF16)
        z_b, xbc, dt, dtT = _in_proj_call(
            _ssd_in_kernel, h3, [g_attn, w_b, w_dt],
            [((tp, SSD_INNER), BF16, "row"), ((tp, SSD_CONV_CH), F32, "row"),
             ((tp, 2 * SSD_HEADS), F32, "row"), ((2 * SSD_HEADS, tp), F32, "col")], tm_row, "ssd_in")
        w8 = jnp.pad(ssd_conv_w[l].astype(F32), ((0, 8 - SSD_CONV), (0, 0)))
        xs_b, bc_b = _ssd_conv(xbc, w8, _row(ssd_conv_b[l]), tm_in)
        dt_bias = ssd_dt_bias[l].astype(F32).reshape(-1)
        a_neg = -jnp.exp(ssd_a_log[l].astype(F32)).reshape(-1)
        y_bf, y_br = _ssd_scan(xs_b, bc_b, dt, dtT, [_row(dt_bias), _col(dt_bias), _row(a_neg), _col(a_neg)])

        n_d = 2 * MLSTM_HEADS * MLSTM_QK + 2 * MLSTM_HEADS * MLSTM_V + 4 * MLSTM_HEADS
        w_d = pad_cols(cols(wl, 9, 14), -(-n_d // LANE) * LANE).astype(BF16)
        w_g = cols(wl, 13, 14).T.astype(BF16)
        q_d, k_d, v_d, o_d, gt, gtT = _in_proj_call(
            _mlstm_in_kernel, h3, [g_attn, w_d, w_g],
            [((tp, MLSTM_HEADS * MLSTM_QK), BF16, "row"), ((tp, MLSTM_HEADS * MLSTM_QK), BF16, "row"),
             ((tp, MLSTM_HEADS * MLSTM_V), BF16, "row"), ((tp, MLSTM_HEADS * MLSTM_V), BF16, "row"),
             ((tp, 4 * MLSTM_HEADS), F32, "row"), ((4 * MLSTM_HEADS, tp), F32, "col")], tm_row, "mlstm_in")
        gate_bias = jnp.concatenate([mlstm_i_bias[l].reshape(-1), mlstm_f_bias[l].reshape(-1)]).astype(F32)
        h_df, h_dr = _mlstm_scan(q_d, k_d, v_d, gt, gtT, [_row(gate_bias), _col(gate_bias)])

        flat = lambda a: a.reshape(b * tp, a.shape[2])
        mixer_outs = [flat(a) for a in (y_a, y_bf, y_br, xs_b, z_b, y_c, h_df, h_dr, o_d)]
        out_consts = [_row(mla_out_g[l]), _row(jnp.repeat(ssd_d[l], SSD_HEAD_DIM)), _row(ssd_norm_g[l]),
                      _row(mlstm_out_g[l])]
        h = _out_proj(mixer_outs, h, out_consts, w_out[l].astype(BF16), tp=tp, seq=seq, tm=tm_flat)
        f = w_ffn_out.shape[1]
        last = l == depth - 1
        h = _ffn(h.reshape(b, tp, d), _row(ffn_norm_g[l]), w_ffn_in[l][:, :f].astype(BF16),
                 w_ffn_in[l][:, f:].astype(BF16), w_ffn_out[l].astype(BF16),
                 t_out=seq if last else tp, tm=tm_out if last else tm_in)
        h = h.reshape(-1, d)

    return h.reshape(b, seq, d).astype(x.dtype)
```

```python
import functools
import math

import jax
import jax.numpy as jnp
from jax import lax
from jax.experimental import pallas as pl
from jax.experimental.pallas import tpu as pltpu

N_META = 16
ROPE_THETA = 500000.0
EPS = 1e-6
CHUNK = 128
N_PAD = CHUNK - N_META
NEG = -1e30
LOG2E = 1.4426950408889634

MLA_HEADS, MLA_NOPE, MLA_ROPE, MLA_V = 8, 64, 32, 64
MLA_QK = MLA_NOPE + MLA_ROPE
MLA_Q_RANK, MLA_KV_RANK = 384, 256
SSD_HEADS, SSD_HEAD_DIM, SSD_GROUPS, SSD_STATE, SSD_CONV = 8, 64, 2, 64, 5
SSD_INNER = SSD_HEADS * SSD_HEAD_DIM
SSD_CONV_CH = SSD_INNER + 2 * SSD_GROUPS * SSD_STATE
DIFF_HEADS, DIFF_QK = 4, 64
DIFF_V = 2 * DIFF_QK
DIFF_ROPE = DIFF_QK // 4
MLSTM_HEADS, MLSTM_QK, MLSTM_V = 4, 64, 128
HEAD_PAD = 128
LANE = 128
Q_STRIP = 256
ATTN_HEADS_PER_STEP = 4
SSD_SKEW, MLSTM_SKEW = 4, 1
VMEM_LIMIT = 52 * 1024 * 1024

F32 = jnp.float32
BF16 = jnp.bfloat16
EXP_DTYPE = jnp.bfloat16


def _dot(a, b):
    return jnp.dot(a, b, preferred_element_type=F32)


def _dot_nt(a, b):
    return lax.dot_general(a, b, (((1,), (1,)), ((), ())), preferred_element_type=F32)


def _rms_rows(x, g):
    ms = jnp.mean(x * x, axis=-1, keepdims=True)
    return x * lax.rsqrt(ms + EPS) * g


def _split3(a):
    hi = a.astype(BF16)
    r1 = a - hi.astype(F32)
    mid = r1.astype(BF16)
    lo = (r1 - mid.astype(F32)).astype(BF16)
    return hi, mid, lo


def _cumsum_cols(tri, a):
    hi, mid, lo = _split3(a)
    return _dot(tri, hi) + _dot(tri, mid) + _dot(tri, lo)


def _cumsum_rows(a, tri):
    hi, mid, lo = _split3(a)
    return _dot(hi, tri) + _dot(mid, tri) + _dot(lo, tri)


def _softplus(x):
    return jnp.maximum(x, 0.0) + jnp.log(1.0 + jnp.exp(-jnp.abs(x)))


def _log_sigmoid(x):
    return jnp.minimum(x, 0.0) - jnp.log(1.0 + jnp.exp(-jnp.abs(x)))


def _sigmoid(x):
    return 1.0 / (1.0 + jnp.exp(-x))


def _pick_tile(n, candidates):
    for c in candidates:
        if n % c == 0:
            return c
    raise ValueError(f"no tile in {candidates} divides {n}")


def _params(*sem):
    return pltpu.CompilerParams(dimension_semantics=sem, vmem_limit_bytes=VMEM_LIMIT)


def _norm_rope_t(blk, g_col, cos, sin, n_real):
    r = cos.shape[0]
    ms = jnp.sum(blk * blk, axis=0, keepdims=True) * (1.0 / n_real)
    y = blk * lax.rsqrt(ms + EPS) * g_col
    x1, x2, rest = y[:r], y[r:2 * r], y[2 * r:]
    return jnp.concatenate([x1 * cos - x2 * sin, x2 * cos + x1 * sin, rest], axis=0)


def _mla_in_kernel(h_ref, g_ref, wa_ref, gq_ref, gkv_ref, wuq_ref, wuk_ref, wuv_ref, wkr_ref,
                   qhg_ref, khg_ref, cos_ref, sin_ref, qT_ref, k_ref, vT_ref):
    hn = _rms_rows(h_ref[0], g_ref[...]).astype(BF16)
    acc = _dot(hn, wa_ref[...])
    cqn = _rms_rows(acc[:, :MLA_Q_RANK], gq_ref[...]).astype(BF16)
    ckvn = _rms_rows(acc[:, MLA_Q_RANK:], gkv_ref[...]).astype(BF16)
    qT = _dot_nt(wuq_ref[...], cqn)
    knT = _dot_nt(wuk_ref[...], ckvn)
    vT = _dot_nt(wuv_ref[...], ckvn)
    krT = _dot_nt(wkr_ref[...], hn)
    cos, sin = cos_ref[...], sin_ref[...]
    t = krT.shape[1]
    zpad = jnp.zeros((HEAD_PAD - MLA_QK, t), F32)
    q_scale = (MLA_QK ** -0.5) * LOG2E
    for h in range(MLA_HEADS):
        qb = _norm_rope_t(qT[h * MLA_QK:(h + 1) * MLA_QK], qhg_ref[0:MLA_QK, :], cos, sin, MLA_QK)
        qT_ref[0, h * HEAD_PAD:h * HEAD_PAD + MLA_QK, :] = (qb * q_scale).astype(BF16)
        qT_ref[0, h * HEAD_PAD + MLA_QK:(h + 1) * HEAD_PAD, :] = zpad.astype(BF16)
        kb = jnp.concatenate([krT, knT[h * MLA_NOPE:(h + 1) * MLA_NOPE]], axis=0)
        kb = _norm_rope_t(kb, khg_ref[0:MLA_QK, :], cos, sin, MLA_QK)
        k_ref[0, h] = jnp.concatenate([kb, zpad], axis=0).T.astype(BF16)
    vT_ref[0] = vT.astype(BF16)


def _diff_in_kernel(h_ref, g_ref, wc_ref, qhg_ref, khg_ref, cos_ref, sin_ref, qT_ref, k_ref, vT_ref):
    hn = _rms_rows(h_ref[0], g_ref[...]).astype(BF16)
    pT = _dot_nt(wc_ref[...], hn)
    cos, sin = cos_ref[...], sin_ref[...]
    t = pT.shape[1]
    zpad = jnp.zeros((HEAD_PAD - DIFF_QK, t), F32)
    nq = 2 * DIFF_HEADS * DIFF_QK
    q_scale = (DIFF_QK ** -0.5) * LOG2E
    for h in range(2 * DIFF_HEADS):
        qb = _norm_rope_t(pT[h * DIFF_QK:(h + 1) * DIFF_QK], qhg_ref[0:DIFF_QK, :], cos, sin, DIFF_QK)
        qT_ref[0, h * HEAD_PAD:h * HEAD_PAD + DIFF_QK, :] = (qb * q_scale).astype(BF16)
        qT_ref[0, h * HEAD_PAD + DIFF_QK:(h + 1) * HEAD_PAD, :] = zpad.astype(BF16)
        kb = _norm_rope_t(pT[nq + h * DIFF_QK:nq + (h + 1) * DIFF_QK], khg_ref[0:DIFF_QK, :], cos, sin,
                          DIFF_QK)
        k_ref[0, h] = jnp.concatenate([kb, zpad], axis=0).T.astype(BF16)
    vT_ref[0] = pT[2 * nq:].astype(BF16)


def _ssd_in_kernel(h_ref, g_ref, wb_ref, wdt_ref, z_ref, xbc_ref, dt_ref, dtT_ref):
    hn = _rms_rows(h_ref[0], g_ref[...]).astype(BF16)
    acc = _dot(hn, wb_ref[...])
    z_ref[0] = acc[:, :SSD_INNER].astype(z_ref.dtype)
    xbc_ref[0] = acc[:, SSD_INNER:SSD_INNER + SSD_CONV_CH]
    dt_ref[0] = acc[:, SSD_INNER + SSD_CONV_CH:SSD_INNER + SSD_CONV_CH + 2 * SSD_HEADS]
    dtT_ref[0] = _dot_nt(wdt_ref[...], hn)


def _mlstm_in_kernel(h_ref, g_ref, wd_ref, wg_ref, q_ref, k_ref, v_ref, o_ref, gt_ref, gtT_ref):
    hn = _rms_rows(h_ref[0], g_ref[...]).astype(BF16)
    acc = _dot(hn, wd_ref[...])
    nqk = MLSTM_HEADS * MLSTM_QK
    nv = MLSTM_HEADS * MLSTM_V
    q_ref[0] = acc[:, :nqk].astype(q_ref.dtype)
    k_ref[0] = acc[:, nqk:2 * nqk].astype(k_ref.dtype)
    v_ref[0] = acc[:, 2 * nqk:2 * nqk + nv].astype(v_ref.dtype)
    o_ref[0] = acc[:, 2 * nqk + nv:2 * nqk + 2 * nv].astype(o_ref.dtype)
    gt_ref[0] = acc[:, 2 * nqk + 2 * nv:2 * nqk + 2 * nv + 4 * MLSTM_HEADS]
    gtT_ref[0] = _dot_nt(wg_ref[...], hn)


def _full(shape):
    nd = len(shape)
    return pl.BlockSpec(shape, lambda *_: (0,) * nd)


def _in_proj_call(body, h, consts, outs, tm, name):
    b, tp, d = h.shape
    in_specs = [pl.BlockSpec((1, tm, d), lambda i, j: (i, j, 0))]
    for c in consts:
        if isinstance(c, tuple):
            in_specs.append(pl.BlockSpec((c[0].shape[0], tm), lambda i, j: (0, j)))
        else:
            in_specs.append(_full(c.shape))
    out_shapes, out_specs = [], []
    for shape, dtype, kind in outs:
        out_shapes.append(jax.ShapeDtypeStruct((b,) + shape, dtype))
        if kind == "row":
            out_specs.append(pl.BlockSpec((1, tm, shape[1]), lambda i, j: (i, j, 0)))
        elif kind == "col":
            out_specs.append(pl.BlockSpec((1, shape[0], tm), lambda i, j: (i, 0, j)))
        else:
            out_specs.append(pl.BlockSpec((1, shape[0], tm, shape[2]), lambda i, j: (i, 0, j, 0)))
    args = [h] + [c[0] if isinstance(c, tuple) else c for c in consts]
    return pl.pallas_call(
        body, grid=(b, tp // tm), in_specs=in_specs, out_specs=out_specs, out_shape=out_shapes,
        compiler_params=_params("parallel", "parallel"), name=name)(*args)


def _attn_kernel(*refs, diff, seq, tk, out_scale, nsh):
    if diff:
        qT_ref, k_ref, vT_ref, lam_ref, og_ref, o_ref, s_scr, st_scr, m_scr, acc_scr = refs
    else:
        qT_ref, k_ref, vT_ref, o_ref, s_scr, st_scr, m_scr, acc_scr = refs
    dv = DIFF_V if diff else MLA_V
    tq = qT_ref.shape[2]
    n_chunks = seq // tk
    assert n_chunks % 2 == 0
    m_scr[...] = jnp.full(m_scr.shape, NEG, F32)
    acc_scr[...] = jnp.zeros(acc_scr.shape, F32)

    def stage(nxt, cur):
        k_n, v_aug = [], []
        for hh in range(nsh):
            if nxt is not None:
                k_n.append(k_ref[0, hh, pl.ds(nxt[1], nxt[2]), :])
            if cur is not None:
                v_lo = (hh // 2) * dv if diff else hh * dv
                ones = jnp.ones((16, cur[2]), BF16)
                v_aug.append(jnp.concatenate([vT_ref[0, v_lo:v_lo + dv, pl.ds(cur[1], cur[2])], ones],
                                             axis=0))
        pending = None
        for j0 in range(0, tq, Q_STRIP):
            w = min(Q_STRIP, tq - j0)
            for hh in range(nsh):
                if cur is not None:
                    src = st_scr.at[hh] if cur[0] is None else s_scr.at[hh, cur[0]]
                    s = src[:, j0:j0 + w]
                    m_old = m_scr[hh, 0:1, j0:j0 + w]
                    m_new = jnp.maximum(m_old, jnp.max(s, axis=0, keepdims=True))
                    alpha = jnp.exp2(m_old - m_new)
                    p = jnp.exp2((s - m_new).astype(EXP_DTYPE)).astype(BF16)
                    m_scr[hh, 0:1, j0:j0 + w] = m_new
                if nxt is not None:
                    dst = st_scr.at[hh] if nxt[0] is None else s_scr.at[hh, nxt[0]]
                    s_n = _dot(k_n[hh], qT_ref[0, hh * HEAD_PAD:(hh + 1) * HEAD_PAD, j0:j0 + w])
                    if nxt[3]:
                        row = lax.broadcasted_iota(jnp.int32, s_n.shape, 0)
                        s_n = jnp.where(row >= N_PAD, s_n, NEG)
                    dst[:, j0:j0 + w] = s_n
                if cur is not None:
                    if pending is not None:
                        ph, pj, pw, pa, pp = pending
                        acc_scr[ph, :, pj:pj + pw] = pa * acc_scr[ph, :, pj:pj + pw] + _dot(v_aug[ph], pp)
                    pending = (hh, j0, w, alpha, p)
        if pending is not None:
            ph, pj, pw, pa, pp = pending
            acc_scr[ph, :, pj:pj + pw] = pa * acc_scr[ph, :, pj:pj + pw] + _dot(v_aug[ph], pp)

    stage((0, 0, tk, False), None)

    def body(i, carry):
        off = pl.multiple_of(2 * i * tk, 2 * tk)
        stage((1, off + tk, tk, False), (0, off, tk))
        stage((0, off + 2 * tk, tk, False), (1, off + tk, tk))
        return carry

    lax.fori_loop(0, n_chunks // 2 - 1, body, 0)
    off = (n_chunks - 2) * tk
    stage((1, off + tk, tk, False), (0, off, tk))
    stage((None, seq, CHUNK, True), (1, off + tk, tk))
    stage(None, (None, seq, CHUNK))
    outs = []
    for hh in range(nsh):
        acc = acc_scr[hh]
        outs.append(acc[:dv] / acc[dv:dv + 1])
    if diff:
        pairs = []
        for hp in range(nsh // 2):
            o = outs[2 * hp] - lam_ref[0:1, 0:1] * outs[2 * hp + 1]
            ms = jnp.mean(o * o, axis=0, keepdims=True)
            pairs.append(o * lax.rsqrt(ms + EPS) * (og_ref[...] * out_scale))
        o = jnp.concatenate(pairs, axis=0) if len(pairs) > 1 else pairs[0]
    else:
        o = jnp.concatenate(outs, axis=0)
    o_ref[0] = o.T.astype(o_ref.dtype)


def _attention(qT, k, vT, extra, *, diff, seq, tq, tk, out_scale, nsh, name):
    b, _, tp = qT.shape
    groups = qT.shape[1] // (nsh * HEAD_PAD)
    dv = DIFF_V if diff else MLA_V
    v_rows = (nsh // 2) * dv if diff else nsh * dv
    o_cols = (nsh // 2) * LANE
    in_specs = [
        pl.BlockSpec((1, nsh * HEAD_PAD, tq), lambda i, g, j: (i, g, j)),
        pl.BlockSpec((1, nsh, tp, HEAD_PAD), lambda i, g, j: (i, g, 0, 0)),
        pl.BlockSpec((1, v_rows, tp), lambda i, g, j: (i, g, 0)),
    ] + [_full(e.shape) for e in extra]
    return pl.pallas_call(
        functools.partial(_attn_kernel, diff=diff, seq=seq, tk=tk, out_scale=out_scale, nsh=nsh),
        grid=(b, groups, tp // tq),
        in_specs=in_specs,
        out_specs=pl.BlockSpec((1, tq, o_cols), lambda i, g, j: (i, j, g)),
        out_shape=jax.ShapeDtypeStruct((b, tp, groups * o_cols), BF16),
        scratch_shapes=[pltpu.VMEM((nsh, 2, tk, tq), F32), pltpu.VMEM((nsh, CHUNK, tq), F32),
                        pltpu.VMEM((nsh, 8, tq), F32), pltpu.VMEM((nsh, dv + 16, tq), F32)],
        compiler_params=_params("parallel", "parallel", "arbitrary"), name=name)(qT, k, vT, *extra)


def _conv_kernel(x_ref, prev_ref, next_ref, w_ref, b_ref, xs_ref, bc_ref, scr):
    tc = x_ref.shape[1]
    scr[0:8, :] = prev_ref[0]
    scr[8:8 + tc, :] = x_ref[0]
    scr[8 + tc:16 + tc, :] = next_ref[0]
    acc = jnp.broadcast_to(b_ref[...], (tc, b_ref.shape[1]))
    for j in range(SSD_CONV):
        acc = acc + w_ref[j:j + 1, :] * scr[8 - SSD_CONV // 2 + j:8 - SSD_CONV // 2 + j + tc, :]
    act = acc * _sigmoid(acc)
    xs_ref[0] = act[:, :SSD_INNER]
    bc_ref[0] = act[:, SSD_INNER:]


def _ssd_conv(xbc, w8, bias, tc):
    b, tp, c = xbc.shape
    nb8 = tp // 8
    r8 = tc // 8
    row_spec = lambda n: pl.BlockSpec((1, tc, n), lambda i, j: (i, j, 0))
    return pl.pallas_call(
        _conv_kernel, grid=(b, tp // tc),
        in_specs=[row_spec(c),
                  pl.BlockSpec((1, 8, c), lambda i, j: (i, (j * r8 + nb8 - 1) % nb8, 0)),
                  pl.BlockSpec((1, 8, c), lambda i, j: (i, ((j + 1) * r8) % nb8, 0)),
                  _full(w8.shape), _full(bias.shape)],
        out_specs=[row_spec(SSD_INNER), row_spec(c - SSD_INNER)],
        out_shape=[jax.ShapeDtypeStruct((b, tp, SSD_INNER), F32),
                   jax.ShapeDtypeStruct((b, tp, c - SSD_INNER), F32)],
        scratch_shapes=[pltpu.VMEM((tc + 16, c), F32)],
        compiler_params=_params("parallel", "parallel"), name="ssd_conv")(xbc, xbc, xbc, w8, bias)


def _tri_masks(reverse):
    row = lax.broadcasted_iota(jnp.int32, (CHUNK, CHUNK), 0)
    col = lax.broadcasted_iota(jnp.int32, (CHUNK, CHUNK), 1)
    keep = (col >= row) if reverse else (col <= row)
    tri_c = keep.astype(BF16)
    tri_r = ((row >= col) if reverse else (row <= col)).astype(BF16)
    return keep, tri_c, tri_r, row, col


def _ssd_direction(xs_ref, bc_ref, dt_ref, dtT_ref, bias_r, bias_c, a_r, a_c, y_ref, st_ref, *,
                   bi, reverse, is_meta):
    d = 1 if reverse else 0
    h8 = SSD_HEADS
    keep, tri_c, tri_r, row, col = _tri_masks(reverse)
    bm = bc_ref[bi, :, :CHUNK]
    cm = bc_ref[bi, :, CHUNK:]

    dt_c = _softplus(dt_ref[bi][:, d * h8:(d + 1) * h8] + bias_r[:, d * h8:(d + 1) * h8])
    dt_r = _softplus(dtT_ref[bi][d * h8:(d + 1) * h8, :] + bias_c[d * h8:(d + 1) * h8, :])
    pad_c = jnp.logical_and(is_meta, row[:, :h8] < N_PAD)
    pad_r = jnp.logical_and(is_meta, col[:h8, :] < N_PAD)
    dt_c = jnp.where(pad_c, 0.0, dt_c)
    dt_r = jnp.where(pad_r, 0.0, dt_r)
    a_col = dt_c * a_r[:, d * h8:(d + 1) * h8]
    a_row = dt_r * a_c[d * h8:(d + 1) * h8, :]
    cs_c = _cumsum_cols(tri_c, a_col)
    cs_r = _cumsum_rows(a_row, tri_r)
    last = 0 if reverse else CHUNK - 1
    tot_r = cs_r[:, last:last + 1]
    tot_c = cs_c[last:last + 1, :]

    lane_lo = col < SSD_STATE
    row_lo = row < SSD_STATE
    blockdiag = jnp.logical_not(jnp.logical_xor(lane_lo, row_lo))
    cm_sw = pltpu.roll(cm, SSD_STATE, 1)
    c_dup = (jnp.where(lane_lo, cm, cm_sw), jnp.where(lane_lo, cm_sw, cm))
    g_mat = (_dot_nt(jnp.where(lane_lo, cm, 0.0).astype(BF16), bm.astype(BF16)),
             _dot_nt(jnp.where(lane_lo, 0.0, cm).astype(BF16), bm.astype(BF16)))
    bT = bm.T

    heads_per_group = SSD_HEADS // SSD_GROUPS

    def pair_chain(j):
        g = (2 * j) // heads_per_group
        h0, h1 = 2 * j, 2 * j + 1
        xp = xs_ref[bi, :, j * CHUNK:(j + 1) * CHUNK]
        dtp = jnp.where(lane_lo, dt_c[:, h0:h0 + 1], dt_c[:, h1:h1 + 1])
        xdt = xp * dtp
        cs_b = [jnp.broadcast_to(cs_c[:, h:h + 1], (CHUNK, CHUNK)) for h in (h0, h1)]
        yield
        parts = []
        for h, cs_h in zip((h0, h1), cs_b):
            diff_ = cs_h - cs_r[h:h + 1, :]
            parts.append((g_mat[g] * jnp.exp(jnp.where(keep, diff_, NEG))).astype(BF16))
        parts.append((c_dup[g] * jnp.exp(jnp.where(lane_lo, cs_b[0], cs_b[1]))).astype(BF16))
        yield
        lhs = jnp.concatenate(parts, axis=1)
        s_old = st_ref[j]
        rhs = jnp.concatenate([jnp.where(lane_lo, xdt, 0.0).astype(BF16),
                               jnp.where(lane_lo, 0.0, xdt).astype(BF16),
                               s_old.astype(BF16)], axis=0)
        y_pair = _dot(lhs, rhs)
        btg = bT[g * SSD_STATE:(g + 1) * SSD_STATE, :]
        bd = jnp.concatenate([btg * jnp.exp(tot_r[h0:h0 + 1, :] - cs_r[h0:h0 + 1, :]),
                              btg * jnp.exp(tot_r[h1:h1 + 1, :] - cs_r[h1:h1 + 1, :])], axis=0)
        s_upd = _dot(bd.astype(BF16), xdt.astype(BF16))
        yield
        carry = jnp.where(row_lo, jnp.exp(tot_c[:, h0:h0 + 1]), jnp.exp(tot_c[:, h1:h1 + 1]))
        st_ref[j] = s_old * carry + jnp.where(blockdiag, s_upd, 0.0)
        y_ref[bi, :, j * CHUNK:(j + 1) * CHUNK] = y_pair.astype(y_ref.dtype)

    return [pair_chain(j) for j in range(SSD_HEADS // 2)]


def _interleave(chains, period=1):
    pending = [(i % period, c) for i, c in enumerate(chains)]
    rnd = 0
    while pending:
        alive = []
        for delay, c in pending:
            if rnd >= delay:
                try:
                    next(c)
                except StopIteration:
                    continue
            alive.append((delay, c))
        pending = alive
        rnd += 1


def _ssd_kernel(xs_f, bc_f, dt_f, dtT_f, xs_r, bc_r, dt_r, dtT_r, bias_r, bias_c, a_r, a_c,
                yf_ref, yr_ref, st_ref, *, n_chunks):
    step = pl.program_id(1)
    n_pairs = SSD_HEADS // 2

    @pl.when(step == 0)
    def _():
        st_ref[...] = jnp.zeros(st_ref.shape, F32)

    scans = []
    for bi in range(xs_f.shape[0]):
        lo = 2 * bi * n_pairs
        scans.append(_ssd_direction(xs_f, bc_f, dt_f, dtT_f, bias_r, bias_c, a_r, a_c, yf_ref,
                                    st_ref.at[lo:lo + n_pairs], bi=bi, reverse=False, is_meta=step == 0))
        scans.append(_ssd_direction(xs_r, bc_r, dt_r, dtT_r, bias_r, bias_c, a_r, a_c, yr_ref,
                                    st_ref.at[lo + n_pairs:lo + 2 * n_pairs], bi=bi, reverse=True,
                                    is_meta=step == n_chunks - 1))
    _interleave([c for group in zip(*scans) for c in group], period=SSD_SKEW)


def _chunk_order(n_chunks, reverse):
    if reverse:
        return lambda c: (2 * n_chunks - 2 - c) % n_chunks
    return lambda c: (c + n_chunks - 1) % n_chunks


def _scan_specs(arrays, n_chunks, reverse, nb):
    order = _chunk_order(n_chunks, reverse)
    specs = []
    for a, transposed in arrays:
        if transposed:
            specs.append(pl.BlockSpec((nb, a.shape[1], CHUNK), lambda i, s: (i, 0, order(s))))
        else:
            specs.append(pl.BlockSpec((nb, CHUNK, a.shape[2]), lambda i, s: (i, order(s), 0)))
    return specs


def _scan_batch(b):
    return 2 if b % 2 == 0 else 1


def _ssd_scan(xs, bc, dt, dtT, consts):
    b, tp, _ = xs.shape
    n_chunks = tp // CHUNK
    nb = _scan_batch(b)
    arrays = [(xs, False), (bc, False), (dt, False), (dtT, True)]
    in_specs = (_scan_specs(arrays, n_chunks, False, nb) + _scan_specs(arrays, n_chunks, True, nb)
                + [_full(c.shape) for c in consts])
    out_specs = (_scan_specs([(xs, False)], n_chunks, False, nb)
                 + _scan_specs([(xs, False)], n_chunks, True, nb))
    out_shape = [jax.ShapeDtypeStruct((b, tp, SSD_INNER), BF16)] * 2
    return pl.pallas_call(
        functools.partial(_ssd_kernel, n_chunks=n_chunks),
        grid=(b // nb, n_chunks), in_specs=in_specs, out_specs=out_specs, out_shape=out_shape,
        scratch_shapes=[pltpu.VMEM((nb * SSD_HEADS, CHUNK, CHUNK), F32)],
        compiler_params=_params("parallel", "arbitrary"),
        name="ssd_scan")(xs, bc, dt, dtT, xs, bc, dt, dtT, *consts)


def _mlstm_direction(q_ref, k_ref, v_ref, gt_ref, gtT_ref, gb_r, gb_c, y_ref, st_ref, m_ref, *,
                     bi, reverse, is_meta):
    d = 1 if reverse else 0
    nh = MLSTM_HEADS
    keep, tri_c, tri_r, row, col = _tri_masks(reverse)
    gt = gt_ref[bi] + gb_r[...]
    gtT = gtT_ref[bi] + gb_c[...]
    i_lo, f_lo = d * nh, 2 * nh + d * nh
    pad_c = jnp.logical_and(is_meta, row[:, :nh] < N_PAD)
    pad_r = jnp.logical_and(is_meta, col[:nh, :] < N_PAD)
    ig_r = jnp.where(pad_r, NEG, gtT[i_lo:i_lo + nh, :])
    fg_c = jnp.where(pad_c, 0.0, _log_sigmoid(gt[:, f_lo:f_lo + nh]))
    fg_r = jnp.where(pad_r, 0.0, _log_sigmoid(gtT[f_lo:f_lo + nh, :]))
    b_c = _cumsum_cols(tri_c, fg_c)
    b_r = _cumsum_rows(fg_r, tri_r)
    last = 0 if reverse else CHUNK - 1

    lane_lo = col < MLSTM_QK
    row_lo = row < MLSTM_QK
    ones = jnp.ones((CHUNK, MLSTM_V), BF16)
    scale = MLSTM_QK ** -0.5

    def head_chain(h):
        pair = h // 2
        lo = (h % 2) == 0
        qp = q_ref[bi, :, pair * CHUNK:(pair + 1) * CHUNK]
        kp = k_ref[bi, :, pair * CHUNK:(pair + 1) * CHUNK]
        qm = (jnp.where(lane_lo if lo else jnp.logical_not(lane_lo), qp, 0.0) * scale).astype(BF16)
        kT = jnp.where(row_lo if lo else jnp.logical_not(row_lo), kp.astype(F32).T, 0.0)
        v_aug = jnp.concatenate([v_ref[bi, :, h * MLSTM_V:(h + 1) * MLSTM_V].astype(BF16), ones], axis=1)
        m_st = m_ref[h][0:1, 0:1]
        bc = b_c[:, h:h + 1]
        br = b_r[h:h + 1, :]
        ir = ig_r[h:h + 1, :]
        s_raw = _dot_nt(qm, kp.astype(BF16))
        c_st = st_ref[h]
        inter_mm = _dot(qm, c_st.astype(BF16))
        yield
        dmat = jnp.where(keep, ir - br, -jnp.inf)
        m_rel = jnp.maximum(jnp.max(dmat, axis=1, keepdims=True), m_st)
        tot = br[:, last:last + 1]
        d_last = tot - br + ir
        m_new = jnp.maximum(tot + m_st, jnp.max(d_last, axis=1, keepdims=True))
        yield
        m_rel_b = jnp.broadcast_to(m_rel, (CHUNK, CHUNK))
        w_intra = jnp.exp(dmat - m_rel_b)
        w_inter = jnp.exp(m_st - m_rel_b)
        floor = jnp.exp(-(bc + m_rel_b))
        w_s = jnp.exp(d_last - m_new)
        w_prev = jnp.exp(tot + m_st - m_new)
        yield
        s = s_raw * w_intra
        intra_mm = _dot(s.astype(BF16), v_aug)
        upd = _dot((kT * w_s).astype(BF16), v_aug)
        yield
        num = intra_mm[:, :MLSTM_V] + w_inter * inter_mm[:, :MLSTM_V]
        den = intra_mm[:, MLSTM_V:] + w_inter * inter_mm[:, MLSTM_V:]
        den = jnp.maximum(jnp.abs(den), floor)
        y_ref[bi, :, h * MLSTM_V:(h + 1) * MLSTM_V] = (num / den).astype(y_ref.dtype)
        st_ref[h] = w_prev * c_st + upd
        m_ref[h] = jnp.broadcast_to(m_new, m_ref.shape[1:])

    return [head_chain(h) for h in range(nh)]


def _mlstm_kernel(q_f, k_f, v_f, gt_f, gtT_f, q_r, k_r, v_r, gt_r, gtT_r, gb_r, gb_c,
                  yf_ref, yr_ref, st_ref, m_ref, *, n_chunks):
    step = pl.program_id(1)
    nh = MLSTM_HEADS

    @pl.when(step == 0)
    def _():
        st_ref[...] = jnp.zeros(st_ref.shape, F32)
        m_ref[...] = jnp.full(m_ref.shape, NEG, F32)

    scans = []
    for bi in range(q_f.shape[0]):
        lo = 2 * bi * nh
        scans.append(_mlstm_direction(q_f, k_f, v_f, gt_f, gtT_f, gb_r, gb_c, yf_ref, st_ref.at[lo:lo + nh],
                                      m_ref.at[lo:lo + nh], bi=bi, reverse=False, is_meta=step == 0))
        scans.append(_mlstm_direction(q_r, k_r, v_r, gt_r, gtT_r, gb_r, gb_c, yr_ref,
                                      st_ref.at[lo + nh:lo + 2 * nh], m_ref.at[lo + nh:lo + 2 * nh],
                                      bi=bi, reverse=True, is_meta=step == n_chunks - 1))
    _interleave([c for group in zip(*scans) for c in group], period=MLSTM_SKEW)


def _mlstm_scan(q, k, v, gt, gtT, consts):
    b, tp, _ = q.shape
    n_chunks = tp // CHUNK
    nb = _scan_batch(b)
    arrays = [(q, False), (k, False), (v, False), (gt, False), (gtT, True)]
    in_specs = (_scan_specs(arrays, n_chunks, False, nb) + _scan_specs(arrays, n_chunks, True, nb)
                + [_full(c.shape) for c in consts])
    out_specs = (_scan_specs([(v, False)], n_chunks, False, nb)
                 + _scan_specs([(v, False)], n_chunks, True, nb))
    out_shape = [jax.ShapeDtypeStruct(v.shape, BF16)] * 2
    return pl.pallas_call(
        functools.partial(_mlstm_kernel, n_chunks=n_chunks),
        grid=(b // nb, n_chunks), in_specs=in_specs, out_specs=out_specs, out_shape=out_shape,
        scratch_shapes=[pltpu.VMEM((nb * 2 * MLSTM_HEADS, CHUNK, 2 * MLSTM_V), F32),
                        pltpu.VMEM((nb * 2 * MLSTM_HEADS, 8, LANE), F32)],
        compiler_params=_params("parallel", "arbitrary"),
        name="mlstm_scan")(q, k, v, gt, gtT, q, k, v, gt, gtT, *consts)


def _out_kernel(ya_ref, yf_ref, yr_ref, xs_ref, z_ref, yc_ref, hf_ref, hr_ref, og_ref, h_ref,
                ag_ref, dsk_ref, ng_ref, mg_ref, w_ref, o_ref, *, tp, seq):
    tm = h_ref.shape[0]
    ya = _rms_rows(ya_ref[...].astype(F32), ag_ref[...])
    z = z_ref[...].astype(F32)
    yb = ((yf_ref[...].astype(F32) + yr_ref[...].astype(F32) + dsk_ref[...] * xs_ref[...])
          * (z * _sigmoid(z)))
    parts = [ya.astype(BF16), _rms_rows(yb, ng_ref[...]).astype(BF16), yc_ref[...].astype(BF16)]
    for hd in range(MLSTM_HEADS):
        lo, hi = hd * MLSTM_V, (hd + 1) * MLSTM_V
        hsum = hf_ref[:, lo:hi].astype(F32) + hr_ref[:, lo:hi].astype(F32)
        gate = _sigmoid(og_ref[:, lo:hi].astype(F32))
        parts.append((gate * _rms_rows(hsum, mg_ref[...])).astype(BF16))
    out = h_ref[...] + _dot(jnp.concatenate(parts, axis=1), w_ref[...])
    t = (pl.program_id(0) * tm + lax.broadcasted_iota(jnp.int32, (tm, 1), 0)) % tp
    is_pad = jnp.logical_and(t >= seq, t < seq + N_PAD)
    o_ref[...] = jnp.where(is_pad, 0.0, out)


def _out_proj(mixer_outs, h, consts, w, *, tp, seq, tm):
    rows, d = h.shape
    row_spec = lambda c: pl.BlockSpec((tm, c), lambda i: (i, 0))
    return pl.pallas_call(
        functools.partial(_out_kernel, tp=tp, seq=seq), grid=(rows // tm,),
        in_specs=([row_spec(a.shape[1]) for a in mixer_outs] + [row_spec(d)]
                  + [_full(c.shape) for c in consts] + [_full(w.shape)]),
        out_specs=row_spec(d), out_shape=jax.ShapeDtypeStruct((rows, d), F32),
        compiler_params=_params("parallel"), name="out_proj")(*mixer_outs, h, *consts, w)


def _ffn_kernel(h_ref, g_ref, wg_ref, wu_ref, wo_ref, o_ref, *, n_split):
    x = h_ref[0]
    hn = _rms_rows(x, g_ref[...]).astype(BF16)
    f = wg_ref.shape[1]
    tf = f // n_split
    acc = x
    for c in range(n_split):
        gate = _dot(hn, wg_ref[:, c * tf:(c + 1) * tf])
        up = _dot(hn, wu_ref[:, c * tf:(c + 1) * tf])
        act = (gate * _sigmoid(gate) * up).astype(BF16)
        acc = acc + _dot(act, wo_ref[c * tf:(c + 1) * tf, :])
    o_ref[0] = acc


def _ffn(h, g, wg, wu, wo, *, t_out, tm):
    b, _, d = h.shape
    f = wg.shape[1]
    n_split = 2 if (f // 2) % LANE == 0 else 1
    row_spec = pl.BlockSpec((1, tm, d), lambda i, j: (i, j, 0))
    resident = lambda a: pl.BlockSpec(a.shape, lambda i, j: (0, 0), pipeline_mode=pl.Buffered(1))
    return pl.pallas_call(
        functools.partial(_ffn_kernel, n_split=n_split), grid=(b, t_out // tm),
        in_specs=[row_spec, _full(g.shape), resident(wg), resident(wu), resident(wo)],
        out_specs=row_spec, out_shape=jax.ShapeDtypeStruct((b, t_out, d), F32),
        compiler_params=_params("parallel", "parallel"), name="ffn")(h, g, wg, wu, wo)


def _rope_tables_t(pos, rot_dim):
    inv = 1.0 / (ROPE_THETA ** (jnp.arange(0, rot_dim, 2, dtype=F32) / rot_dim))
    ang = pos[:, None] * inv[None, :]
    return jnp.cos(ang).T, jnp.sin(ang).T


def _col(v, n=None):
    v = v.astype(F32)
    if n is not None:
        v = jnp.pad(v, (0, n - v.shape[0]))
    return v[:, None]


def _row(v):
    return v.astype(F32)[None, :]


def kernel(x, meta_tokens, attn_norm_g, w_in, mla_q_norm_g, mla_kv_norm_g, mla_w_uq, mla_w_ukv, mla_q_head_g, mla_k_head_g, mla_out_g, ssd_conv_w, ssd_conv_b, ssd_dt_bias, ssd_a_log, ssd_d, ssd_norm_g, diff_q_head_g, diff_k_head_g, diff_lambda, diff_out_g, mlstm_i_bias, mlstm_f_bias, mlstm_out_g, w_out, ffn_norm_g, w_ffn_in, w_ffn_out):
    b, seq, d = x.shape
    depth = w_in.shape[0]
    tp = seq + CHUNK
    assert seq % CHUNK == 0
    tm_in = _pick_tile(tp, (640, 384, 128))
    tm_row = _pick_tile(tp, (1664, 640, 384, 128))
    tq = _pick_tile(tp, (1664, 640, 384, 128))
    tk = _pick_tile(seq, (256, 128))
    tm_flat = _pick_tile(b * tp, (512, 256, 128))
    tm_out = _pick_tile(seq, (512, 256, 128))

    meta = jnp.broadcast_to(meta_tokens[None].astype(x.dtype), (b, N_META, d))
    h = jnp.concatenate([x, jnp.zeros((b, N_PAD, d), x.dtype), meta], axis=1).reshape(b * tp, d)
    pos = jnp.concatenate([N_META + jnp.arange(seq, dtype=F32), jnp.zeros((N_PAD,), F32),
                           jnp.arange(N_META, dtype=F32)])
    cos_m, sin_m = _rope_tables_t(pos, MLA_ROPE)
    cos_d, sin_d = _rope_tables_t(pos, DIFF_ROPE)

    sizes = (MLA_Q_RANK, MLA_KV_RANK, MLA_ROPE, SSD_INNER, SSD_CONV_CH, 2 * SSD_HEADS,
             2 * DIFF_HEADS * DIFF_QK, 2 * DIFF_HEADS * DIFF_QK, DIFF_HEADS * DIFF_V,
             MLSTM_HEADS * MLSTM_QK, MLSTM_HEADS * MLSTM_QK, MLSTM_HEADS * MLSTM_V,
             MLSTM_HEADS * MLSTM_V, 2 * MLSTM_HEADS, 2 * MLSTM_HEADS)
    offs = [0]
    for s_ in sizes:
        offs.append(offs[-1] + s_)

    def cols(w, first, last):
        return w[:, offs[first]:offs[last + 1]]

    def pad_cols(w, n):
        return jnp.pad(w, ((0, 0), (0, n - w.shape[1])))

    for l in range(depth):
        lambda_init = 0.8 - 0.6 * math.exp(-0.3 * l)
        wl = w_in[l]
        h3 = h.reshape(b, tp, d)
        g_attn = _row(attn_norm_g[l])

        w_a = cols(wl, 0, 1).astype(BF16)
        w_kr = cols(wl, 2, 2).T.astype(BF16)
        w_uq = mla_w_uq[l].T
        w_ukv = mla_w_ukv[l].T.reshape(MLA_HEADS, MLA_NOPE + MLA_V, MLA_KV_RANK)
        w_uk = w_ukv[:, :MLA_NOPE].reshape(MLA_HEADS * MLA_NOPE, MLA_KV_RANK)
        w_uv = w_ukv[:, MLA_NOPE:].reshape(MLA_HEADS * MLA_V, MLA_KV_RANK)
        qT_a, k_a, vT_a = _in_proj_call(
            _mla_in_kernel, h3,
            [g_attn, w_a, _row(mla_q_norm_g[l]), _row(mla_kv_norm_g[l]), w_uq.astype(BF16),
             w_uk.astype(BF16), w_uv.astype(BF16), w_kr, _col(mla_q_head_g[l], HEAD_PAD),
             _col(mla_k_head_g[l], HEAD_PAD), (cos_m,), (sin_m,)],
            [((MLA_HEADS * HEAD_PAD, tp), BF16, "col"), ((MLA_HEADS, tp, HEAD_PAD), BF16, "head"),
             ((MLA_HEADS * MLA_V, tp), BF16, "col")], tm_in, "mla_in")
        y_a = _attention(qT_a, k_a, vT_a, [], diff=False, seq=seq, tq=tq, tk=tk, out_scale=1.0,
                         nsh=ATTN_HEADS_PER_STEP, name="mla_attn")

        w_c = cols(wl, 6, 8).T.astype(BF16)
        qT_c, k_c, vT_c = _in_proj_call(
            _diff_in_kernel, h3,
            [g_attn, w_c, _col(diff_q_head_g[l], HEAD_PAD), _col(diff_k_head_g[l], HEAD_PAD),
             (cos_d,), (sin_d,)],
            [((2 * DIFF_HEADS * HEAD_PAD, tp), BF16, "col"), ((2 * DIFF_HEADS, tp, HEAD_PAD), BF16, "head"),
             ((DIFF_HEADS * DIFF_V, tp), BF16, "col")], tm_in, "diff_in")
        lam = diff_lambda[l].astype(F32)
        lam_full = jnp.exp(jnp.sum(lam[0] * lam[1])) - jnp.exp(jnp.sum(lam[2] * lam[3])) + lambda_init
        y_c = _attention(qT_c, k_c, vT_c, [jnp.full((8, LANE), lam_full, F32), _col(diff_out_g[l])],
                         diff=True, seq=seq, tq=tq, tk=tk, out_scale=1.0 - lambda_init, nsh=ATTN_HEADS_PER_STEP,
                         name="diff_attn")

        n_b = SSD_INNER + SSD_CONV_CH + 2 * SSD_HEADS
        w_b = pad_cols(cols(wl, 3, 5), -(-n_b // LANE) * LANE).astype(BF16)
        w_dt = cols(wl, 5, 5).T.astype(BF16)
        z_b, xbc, dt, dtT = _in_proj_call(
            _ssd_in_kernel, h3, [g_attn, w_b, w_dt],
            [((tp, SSD_INNER), BF16, "row"), ((tp, SSD_CONV_CH), F32, "row"),
             ((tp, 2 * SSD_HEADS), F32, "row"), ((2 * SSD_HEADS, tp), F32, "col")], tm_row, "ssd_in")
        w8 = jnp.pad(ssd_conv_w[l].astype(F32), ((0, 8 - SSD_CONV), (0, 0)))
        xs_b, bc_b = _ssd_conv(xbc, w8, _row(ssd_conv_b[l]), tm_in)
        dt_bias = ssd_dt_bias[l].astype(F32).reshape(-1)
        a_neg = -jnp.exp(ssd_a_log[l].astype(F32)).reshape(-1)
        y_bf, y_br = _ssd_scan(xs_b, bc_b, dt, dtT, [_row(dt_bias), _col(dt_bias), _row(a_neg), _col(a_neg)])

        n_d = 2 * MLSTM_HEADS * MLSTM_QK + 2 * MLSTM_HEADS * MLSTM_V + 4 * MLSTM_HEADS
        w_d = pad_cols(cols(wl, 9, 14), -(-n_d // LANE) * LANE).astype(BF16)
        w_g = cols(wl, 13, 14).T.astype(BF16)
        q_d, k_d, v_d, o_d, gt, gtT = _in_proj_call(
            _mlstm_in_kernel, h3, [g_attn, w_d, w_g],
            [((tp, MLSTM_HEADS * MLSTM_QK), BF16, "row"), ((tp, MLSTM_HEADS * MLSTM_QK), BF16, "row"),
             ((tp, MLSTM_HEADS * MLSTM_V), BF16, "row"), ((tp, MLSTM_HEADS * MLSTM_V), BF16, "row"),
             ((tp, 4 * MLSTM_HEADS), F32, "row"), ((4 * MLSTM_HEADS, tp), F32, "col")], tm_row, "mlstm_in")
        gate_bias = jnp.concatenate([mlstm_i_bias[l].reshape(-1), mlstm_f_bias[l].reshape(-1)]).astype(F32)
        h_df, h_dr = _mlstm_scan(q_d, k_d, v_d, gt, gtT, [_row(gate_bias), _col(gate_bias)])

        flat = lambda a: a.reshape(b * tp, a.shape[2])
        mixer_outs = [flat(a) for a in (y_a, y_bf, y_br, xs_b, z_b, y_c, h_df, h_dr, o_d)]
        out_consts = [_row(mla_out_g[l]), _row(jnp.repeat(ssd_d[l], SSD_HEAD_DIM)), _row(ssd_norm_g[l]),
                      _row(mlstm_out_g[l])]
        h = _out_proj(mixer_outs, h, out_consts, w_out[l].astype(BF16), tp=tp, seq=seq, tm=tm_flat)
        f = w_ffn_out.shape[1]
        last = l == depth - 1
        h = _ffn(h.reshape(b, tp, d), _row(ffn_norm_g[l]), w_ffn_in[l][:, :f].astype(BF16),
                 w_ffn_in[l][:, f:].astype(BF16), w_ffn_out[l].astype(BF16),
                 t_out=seq if last else tp, tm=tm_out if last else tm_in)
        h = h.reshape(-1, d)

    return h.reshape(b, seq, d).astype(x.dtype)
```

```python
import functools
import math

import jax
import jax.numpy as jnp
from jax import lax
from jax.experimental import pallas as pl
from jax.experimental.pallas import tpu as pltpu

N_META = 16
ROPE_THETA = 500000.0
EPS = 1e-6
CHUNK = 128
N_PAD = CHUNK - N_META
NEG = -1e30
LOG2E = 1.4426950408889634

MLA_HEADS, MLA_NOPE, MLA_ROPE, MLA_V = 8, 64, 32, 64
MLA_QK = MLA_NOPE + MLA_ROPE
MLA_Q_RANK, MLA_KV_RANK = 384, 256
SSD_HEADS, SSD_HEAD_DIM, SSD_GROUPS, SSD_STATE, SSD_CONV = 8, 64, 2, 64, 5
SSD_INNER = SSD_HEADS * SSD_HEAD_DIM
SSD_CONV_CH = SSD_INNER + 2 * SSD_GROUPS * SSD_STATE
DIFF_HEADS, DIFF_QK = 4, 64
DIFF_V = 2 * DIFF_QK
DIFF_ROPE = DIFF_QK // 4
MLSTM_HEADS, MLSTM_QK, MLSTM_V = 4, 64, 128
HEAD_PAD = 128
LANE = 128
Q_STRIP = 256
ATTN_HEADS_PER_STEP = 4
SSD_SKEW, MLSTM_SKEW = 4, 1
VMEM_LIMIT = 52 * 1024 * 1024

F32 = jnp.float32
BF16 = jnp.bfloat16
EXP_DTYPE = jnp.bfloat16


def _dot(a, b):
    return jnp.dot(a, b, preferred_element_type=F32)


def _dot_nt(a, b):
    return lax.dot_general(a, b, (((1,), (1,)), ((), ())), preferred_element_type=F32)


def _rms_rows(x, g):
    ms = jnp.mean(x * x, axis=-1, keepdims=True)
    return x * lax.rsqrt(ms + EPS) * g


def _split3(a):
    hi = a.astype(BF16)
    r1 = a - hi.astype(F32)
    mid = r1.astype(BF16)
    lo = (r1 - mid.astype(F32)).astype(BF16)
    return hi, mid, lo


def _cumsum_cols(tri, a):
    hi, mid, lo = _split3(a)
    return _dot(tri, hi) + _dot(tri, mid) + _dot(tri, lo)


def _cumsum_rows(a, tri):
    hi, mid, lo = _split3(a)
    return _dot(hi, tri) + _dot(mid, tri) + _dot(lo, tri)


def _softplus(x):
    return jnp.maximum(x, 0.0) + jnp.log(1.0 + jnp.exp(-jnp.abs(x)))


def _log_sigmoid(x):
    return jnp.minimum(x, 0.0) - jnp.log(1.0 + jnp.exp(-jnp.abs(x)))


def _sigmoid(x):
    return 1.0 / (1.0 + jnp.exp(-x))


def _pick_tile(n, candidates):
    for c in candidates:
        if n % c == 0:
            return c
    raise ValueError(f"no tile in {candidates} divides {n}")


def _params(*sem):
    return pltpu.CompilerParams(dimension_semantics=sem, vmem_limit_bytes=VMEM_LIMIT)


def _norm_rope_t(blk, g_col, cos, sin, n_real):
    r = cos.shape[0]
    ms = jnp.sum(blk * blk, axis=0, keepdims=True) * (1.0 / n_real)
    y = blk * lax.rsqrt(ms + EPS) * g_col
    x1, x2, rest = y[:r], y[r:2 * r], y[2 * r:]
    return jnp.concatenate([x1 * cos - x2 * sin, x2 * cos + x1 * sin, rest], axis=0)


def _mla_in_kernel(h_ref, g_ref, wa_ref, gq_ref, gkv_ref, wuq_ref, wuk_ref, wuv_ref, wkr_ref,
                   qhg_ref, khg_ref, cos_ref, sin_ref, qT_ref, k_ref, vT_ref):
    hn = _rms_rows(h_ref[0], g_ref[...]).astype(BF16)
    acc = _dot(hn, wa_ref[...])
    cqn = _rms_rows(acc[:, :MLA_Q_RANK], gq_ref[...]).astype(BF16)
    ckvn = _rms_rows(acc[:, MLA_Q_RANK:], gkv_ref[...]).astype(BF16)
    qT = _dot_nt(wuq_ref[...], cqn)
    knT = _dot_nt(wuk_ref[...], ckvn)
    vT = _dot_nt(wuv_ref[...], ckvn)
    krT = _dot_nt(wkr_ref[...], hn)
    cos, sin = cos_ref[...], sin_ref[...]
    t = krT.shape[1]
    zpad = jnp.zeros((HEAD_PAD - MLA_QK, t), F32)
    q_scale = (MLA_QK ** -0.5) * LOG2E
    for h in range(MLA_HEADS):
        qb = _norm_rope_t(qT[h * MLA_QK:(h + 1) * MLA_QK], qhg_ref[0:MLA_QK, :], cos, sin, MLA_QK)
        qT_ref[0, h * HEAD_PAD:h * HEAD_PAD + MLA_QK, :] = (qb * q_scale).astype(BF16)
        qT_ref[0, h * HEAD_PAD + MLA_QK:(h + 1) * HEAD_PAD, :] = zpad.astype(BF16)
        kb = jnp.concatenate([krT, knT[h * MLA_NOPE:(h + 1) * MLA_NOPE]], axis=0)
        kb = _norm_rope_t(kb, khg_ref[0:MLA_QK, :], cos, sin, MLA_QK)
        k_ref[0, h] = jnp.concatenate([kb, zpad], axis=0).T.astype(BF16)
    vT_ref[0] = vT.astype(BF16)


def _diff_in_kernel(h_ref, g_ref, wc_ref, qhg_ref, khg_ref, cos_ref, sin_ref, qT_ref, k_ref, vT_ref):
    hn = _rms_rows(h_ref[0], g_ref[...]).astype(BF16)
    pT = _dot_nt(wc_ref[...], hn)
    cos, sin = cos_ref[...], sin_ref[...]
    t = pT.shape[1]
    zpad = jnp.zeros((HEAD_PAD - DIFF_QK, t), F32)
    nq = 2 * DIFF_HEADS * DIFF_QK
    q_scale = (DIFF_QK ** -0.5) * LOG2E
    for h in range(2 * DIFF_HEADS):
        qb = _norm_rope_t(pT[h * DIFF_QK:(h + 1) * DIFF_QK], qhg_ref[0:DIFF_QK, :], cos, sin, DIFF_QK)
        qT_ref[0, h * HEAD_PAD:h * HEAD_PAD + DIFF_QK, :] = (qb * q_scale).astype(BF16)
        qT_ref[0, h * HEAD_PAD + DIFF_QK:(h + 1) * HEAD_PAD, :] = zpad.astype(BF16)
        kb = _norm_rope_t(pT[nq + h * DIFF_QK:nq + (h + 1) * DIFF_QK], khg_ref[0:DIFF_QK, :], cos, sin,
                          DIFF_QK)
        k_ref[0, h] = jnp.concatenate([kb, zpad], axis=0).T.astype(BF16)
    vT_ref[0] = pT[2 * nq:].astype(BF16)


def _ssd_in_kernel(h_ref, g_ref, wb_ref, wdt_ref, z_ref, xbc_ref, dt_ref, dtT_ref):
    hn = _rms_rows(h_ref[0], g_ref[...]).astype(BF16)
    acc = _dot(hn, wb_ref[...])
    z_ref[0] = acc[:, :SSD_INNER].astype(z_ref.dtype)
    xbc_ref[0] = acc[:, SSD_INNER:SSD_INNER + SSD_CONV_CH]
    dt_ref[0] = acc[:, SSD_INNER + SSD_CONV_CH:SSD_INNER + SSD_CONV_CH + 2 * SSD_HEADS]
    dtT_ref[0] = _dot_nt(wdt_ref[...], hn)


def _mlstm_in_kernel(h_ref, g_ref, wd_ref, wg_ref, q_ref, k_ref, v_ref, o_ref, gt_ref, gtT_ref):
    hn = _rms_rows(h_ref[0], g_ref[...]).astype(BF16)
    acc = _dot(hn, wd_ref[...])
    nqk = MLSTM_HEADS * MLSTM_QK
    nv = MLSTM_HEADS * MLSTM_V
    q_ref[0] = acc[:, :nqk].astype(q_ref.dtype)
    k_ref[0] = acc[:, nqk:2 * nqk].astype(k_ref.dtype)
    v_ref[0] = acc[:, 2 * nqk:2 * nqk + nv].astype(v_ref.dtype)
    o_ref[0] = acc[:, 2 * nqk + nv:2 * nqk + 2 * nv].astype(o_ref.dtype)
    gt_ref[0] = acc[:, 2 * nqk + 2 * nv:2 * nqk + 2 * nv + 4 * MLSTM_HEADS]
    gtT_ref[0] = _dot_nt(wg_ref[...], hn)


def _full(shape):
    nd = len(shape)
    return pl.BlockSpec(shape, lambda *_: (0,) * nd)


def _in_proj_call(body, h, consts, outs, tm, name):
    b, tp, d = h.shape
    in_specs = [pl.BlockSpec((1, tm, d), lambda i, j: (i, j, 0))]
    for c in consts:
        if isinstance(c, tuple):
            in_specs.append(pl.BlockSpec((c[0].shape[0], tm), lambda i, j: (0, j)))
        else:
            in_specs.append(_full(c.shape))
    out_shapes, out_specs = [], []
    for shape, dtype, kind in outs:
        out_shapes.append(jax.ShapeDtypeStruct((b,) + shape, dtype))
        if kind == "row":
            out_specs.append(pl.BlockSpec((1, tm, shape[1]), lambda i, j: (i, j, 0)))
        elif kind == "col":
            out_specs.append(pl.BlockSpec((1, shape[0], tm), lambda i, j: (i, 0, j)))
        else:
            out_specs.append(pl.BlockSpec((1, shape[0], tm, shape[2]), lambda i, j: (i, 0, j, 0)))
    args = [h] + [c[0] if isinstance(c, tuple) else c for c in consts]
    return pl.pallas_call(
        body, grid=(b, tp // tm), in_specs=in_specs, out_specs=out_specs, out_shape=out_shapes,
        compiler_params=_params("parallel", "parallel"), name=name)(*args)


def _attn_kernel(*refs, diff, seq, tk, out_scale, nsh):
    if diff:
        qT_ref, k_ref, vT_ref, lam_ref, og_ref, o_ref, s_scr, st_scr, m_scr, acc_scr = refs
    else:
        qT_ref, k_ref, vT_ref, o_ref, s_scr, st_scr, m_scr, acc_scr = refs
    dv = DIFF_V if diff else MLA_V
    tq = qT_ref.shape[2]
    n_chunks = seq // tk
    assert n_chunks % 2 == 0
    m_scr[...] = jnp.full(m_scr.shape, NEG, F32)
    acc_scr[...] = jnp.zeros(acc_scr.shape, F32)

    def stage(nxt, cur):
        k_n, v_aug = [], []
        for hh in range(nsh):
            if nxt is not None:
                k_n.append(k_ref[0, hh, pl.ds(nxt[1], nxt[2]), :])
            if cur is not None:
                v_lo = (hh // 2) * dv if diff else hh * dv
                ones = jnp.ones((16, cur[2]), BF16)
                v_aug.append(jnp.concatenate([vT_ref[0, v_lo:v_lo + dv, pl.ds(cur[1], cur[2])], ones],
                                             axis=0))
        pending = None
        for j0 in range(0, tq, Q_STRIP):
            w = min(Q_STRIP, tq - j0)
            for hh in range(nsh):
                if cur is not None:
                    src = st_scr.at[hh] if cur[0] is None else s_scr.at[hh, cur[0]]
                    s = src[:, j0:j0 + w]
                    m_old = m_scr[hh, 0:1, j0:j0 + w]
                    m_new = jnp.maximum(m_old, jnp.max(s, axis=0, keepdims=True))
                    alpha = jnp.exp2(m_old - m_new)
                    p = jnp.exp2((s - m_new).astype(EXP_DTYPE)).astype(BF16)
                    m_scr[hh, 0:1, j0:j0 + w] = m_new
                if nxt is not None:
                    dst = st_scr.at[hh] if nxt[0] is None else s_scr.at[hh, nxt[0]]
                    s_n = _dot(k_n[hh], qT_ref[0, hh * HEAD_PAD:(hh + 1) * HEAD_PAD, j0:j0 + w])
                    if nxt[3]:
                        row = lax.broadcasted_iota(jnp.int32, s_n.shape, 0)
                        s_n = jnp.where(row >= N_PAD, s_n, NEG)
                    dst[:, j0:j0 + w] = s_n
                if cur is not None:
                    if pending is not None:
                        ph, pj, pw, pa, pp = pending
                        acc_scr[ph, :, pj:pj + pw] = pa * acc_scr[ph, :, pj:pj + pw] + _dot(v_aug[ph], pp)
                    pending = (hh, j0, w, alpha, p)
        if pending is not None:
            ph, pj, pw, pa, pp = pending
            acc_scr[ph, :, pj:pj + pw] = pa * acc_scr[ph, :, pj:pj + pw] + _dot(v_aug[ph], pp)

    stage((0, 0, tk, False), None)

    def body(i, carry):
        off = pl.multiple_of(2 * i * tk, 2 * tk)
        stage((1, off + tk, tk, False), (0, off, tk))
        stage((0, off + 2 * tk, tk, False), (1, off + tk, tk))
        return carry

    lax.fori_loop(0, n_chunks // 2 - 1, body, 0)
    off = (n_chunks - 2) * tk
    stage((1, off + tk, tk, False), (0, off, tk))
    stage((None, seq, CHUNK, True), (1, off + tk, tk))
    stage(None, (None, seq, CHUNK))
    outs = []
    for hh in range(nsh):
        acc = acc_scr[hh]
        outs.append(acc[:dv] / acc[dv:dv + 1])
    if diff:
        pairs = []
        for hp in range(nsh // 2):
            o = outs[2 * hp] - lam_ref[0:1, 0:1] * outs[2 * hp + 1]
            ms = jnp.mean(o * o, axis=0, keepdims=True)
            pairs.append(o * lax.rsqrt(ms + EPS) * (og_ref[...] * out_scale))
        o = jnp.concatenate(pairs, axis=0) if len(pairs) > 1 else pairs[0]
    else:
        o = jnp.concatenate(outs, axis=0)
    o_ref[0] = o.T.astype(o_ref.dtype)


def _attention(qT, k, vT, extra, *, diff, seq, tq, tk, out_scale, nsh, name):
    b, _, tp = qT.shape
    groups = qT.shape[1] // (nsh * HEAD_PAD)
    dv = DIFF_V if diff else MLA_V
    v_rows = (nsh // 2) * dv if diff else nsh * dv
    o_cols = (nsh // 2) * LANE
    in_specs = [
        pl.BlockSpec((1, nsh * HEAD_PAD, tq), lambda i, g, j: (i, g, j)),
        pl.BlockSpec((1, nsh, tp, HEAD_PAD), lambda i, g, j: (i, g, 0, 0)),
        pl.BlockSpec((1, v_rows, tp), lambda i, g, j: (i, g, 0)),
    ] + [_full(e.shape) for e in extra]
    return pl.pallas_call(
        functools.partial(_attn_kernel, diff=diff, seq=seq, tk=tk, out_scale=out_scale, nsh=nsh),
        grid=(b, groups, tp // tq),
        in_specs=in_specs,
        out_specs=pl.BlockSpec((1, tq, o_cols), lambda i, g, j: (i, j, g)),
        out_shape=jax.ShapeDtypeStruct((b, tp, groups * o_cols), BF16),
        scratch_shapes=[pltpu.VMEM((nsh, 2, tk, tq), F32), pltpu.VMEM((nsh, CHUNK, tq), F32),
                        pltpu.VMEM((nsh, 8, tq), F32), pltpu.VMEM((nsh, dv + 16, tq), F32)],
        compiler_params=_params("parallel", "parallel", "arbitrary"), name=name)(qT, k, vT, *extra)


def _conv_kernel(x_ref, prev_ref, next_ref, w_ref, b_ref, xs_ref, bc_ref, scr):
    tc = x_ref.shape[1]
    scr[0:8, :] = prev_ref[0]
    scr[8:8 + tc, :] = x_ref[0]
    scr[8 + tc:16 + tc, :] = next_ref[0]
    acc = jnp.broadcast_to(b_ref[...], (tc, b_ref.shape[1]))
    for j in range(SSD_CONV):
        acc = acc + w_ref[j:j + 1, :] * scr[8 - SSD_CONV // 2 + j:8 - SSD_CONV // 2 + j + tc, :]
    act = acc * _sigmoid(acc)
    xs_ref[0] = act[:, :SSD_INNER]
    bc_ref[0] = act[:, SSD_INNER:]


def _ssd_conv(xbc, w8, bias, tc):
    b, tp, c = xbc.shape
    nb8 = tp // 8
    r8 = tc // 8
    row_spec = lambda n: pl.BlockSpec((1, tc, n), lambda i, j: (i, j, 0))
    return pl.pallas_call(
        _conv_kernel, grid=(b, tp // tc),
        in_specs=[row_spec(c),
                  pl.BlockSpec((1, 8, c), lambda i, j: (i, (j * r8 + nb8 - 1) % nb8, 0)),
                  pl.BlockSpec((1, 8, c), lambda i, j: (i, ((j + 1) * r8) % nb8, 0)),
                  _full(w8.shape), _full(bias.shape)],
        out_specs=[row_spec(SSD_INNER), row_spec(c - SSD_INNER)],
        out_shape=[jax.ShapeDtypeStruct((b, tp, SSD_INNER), F32),
                   jax.ShapeDtypeStruct((b, tp, c - SSD_INNER), F32)],
        scratch_shapes=[pltpu.VMEM((tc + 16, c), F32)],
        compiler_params=_params("parallel", "parallel"), name="ssd_conv")(xbc, xbc, xbc, w8, bias)


def _tri_masks(reverse):
    row = lax.broadcasted_iota(jnp.int32, (CHUNK, CHUNK), 0)
    col = lax.broadcasted_iota(jnp.int32, (CHUNK, CHUNK), 1)
    keep = (col >= row) if reverse else (col <= row)
    tri_c = keep.astype(BF16)
    tri_r = ((row >= col) if reverse else (row <= col)).astype(BF16)
    return keep, tri_c, tri_r, row, col


def _ssd_direction(xs_ref, bc_ref, dt_ref, dtT_ref, bias_r, bias_c, a_r, a_c, y_ref, st_ref, *,
                   bi, reverse, is_meta):
    d = 1 if reverse else 0
    h8 = SSD_HEADS
    keep, tri_c, tri_r, row, col = _tri_masks(reverse)
    bm = bc_ref[bi, :, :CHUNK]
    cm = bc_ref[bi, :, CHUNK:]

    dt_c = _softplus(dt_ref[bi][:, d * h8:(d + 1) * h8] + bias_r[:, d * h8:(d + 1) * h8])
    dt_r = _softplus(dtT_ref[bi][d * h8:(d + 1) * h8, :] + bias_c[d * h8:(d + 1) * h8, :])
    pad_c = jnp.logical_and(is_meta, row[:, :h8] < N_PAD)
    pad_r = jnp.logical_and(is_meta, col[:h8, :] < N_PAD)
    dt_c = jnp.where(pad_c, 0.0, dt_c)
    dt_r = jnp.where(pad_r, 0.0, dt_r)
    a_col = dt_c * a_r[:, d * h8:(d + 1) * h8]
    a_row = dt_r * a_c[d * h8:(d + 1) * h8, :]
    cs_c = _cumsum_cols(tri_c, a_col)
    cs_r = _cumsum_rows(a_row, tri_r)
    last = 0 if reverse else CHUNK - 1
    tot_r = cs_r[:, last:last + 1]
    tot_c = cs_c[last:last + 1, :]

    lane_lo = col < SSD_STATE
    row_lo = row < SSD_STATE
    blockdiag = jnp.logical_not(jnp.logical_xor(lane_lo, row_lo))
    cm_sw = pltpu.roll(cm, SSD_STATE, 1)
    c_dup = (jnp.where(lane_lo, cm, cm_sw), jnp.where(lane_lo, cm_sw, cm))
    g_mat = (_dot_nt(jnp.where(lane_lo, cm, 0.0).astype(BF16), bm.astype(BF16)),
             _dot_nt(jnp.where(lane_lo, 0.0, cm).astype(BF16), bm.astype(BF16)))
    bT = bm.T

    heads_per_group = SSD_HEADS // SSD_GROUPS

    def pair_chain(j):
        g = (2 * j) // heads_per_group
        h0, h1 = 2 * j, 2 * j + 1
        xp = xs_ref[bi, :, j * CHUNK:(j + 1) * CHUNK]
        cs_b = [jnp.broadcast_to(cs_c[:, h:h + 1], (CHUNK, CHUNK)) for h in (h0, h1)]
        yield
        parts = []
        for h, cs_h in zip((h0, h1), cs_b):
            diff_ = cs_h - cs_r[h:h + 1, :]
            decay = jnp.exp(jnp.where(keep, diff_, NEG)) * dt_r[h:h + 1, :]
            parts.append((g_mat[g] * decay).astype(BF16))
        parts.append((c_dup[g] * jnp.exp(jnp.where(lane_lo, cs_b[0], cs_b[1]))).astype(BF16))
        yield
        lhs = jnp.concatenate(parts, axis=1)
        s_old = st_ref[j]
        rhs = jnp.concatenate([jnp.where(lane_lo, xp, 0.0).astype(BF16),
                               jnp.where(lane_lo, 0.0, xp).astype(BF16),
                               s_old.astype(BF16)], axis=0)
        y_pair = _dot(lhs, rhs)
        btg = bT[g * SSD_STATE:(g + 1) * SSD_STATE, :]
        bd = jnp.concatenate(
            [btg * (jnp.exp(tot_r[h:h + 1, :] - cs_r[h:h + 1, :]) * dt_r[h:h + 1, :]) for h in (h0, h1)],
            axis=0)
        s_upd = _dot(bd.astype(BF16), xp.astype(BF16))
        yield
        carry = jnp.where(row_lo, jnp.exp(tot_c[:, h0:h0 + 1]), jnp.exp(tot_c[:, h1:h1 + 1]))
        st_ref[j] = s_old * carry + jnp.where(blockdiag, s_upd, 0.0)
        y_ref[bi, :, j * CHUNK:(j + 1) * CHUNK] = y_pair.astype(y_ref.dtype)

    return [pair_chain(j) for j in range(SSD_HEADS // 2)]


def _interleave(chains, period=1):
    pending = [(i % period, c) for i, c in enumerate(chains)]
    rnd = 0
    while pending:
        alive = []
        for delay, c in pending:
            if rnd >= delay:
                try:
                    next(c)
                except StopIteration:
                    continue
            alive.append((delay, c))
        pending = alive
        rnd += 1


def _ssd_kernel(xs_f, bc_f, dt_f, dtT_f, xs_r, bc_r, dt_r, dtT_r, bias_r, bias_c, a_r, a_c,
                yf_ref, yr_ref, st_ref, *, n_chunks):
    step = pl.program_id(1)
    n_pairs = SSD_HEADS // 2

    @pl.when(step == 0)
    def _():
        st_ref[...] = jnp.zeros(st_ref.shape, F32)

    scans = []
    for bi in range(xs_f.shape[0]):
        lo = 2 * bi * n_pairs
        scans.append(_ssd_direction(xs_f, bc_f, dt_f, dtT_f, bias_r, bias_c, a_r, a_c, yf_ref,
                                    st_ref.at[lo:lo + n_pairs], bi=bi, reverse=False, is_meta=step == 0))
        scans.append(_ssd_direction(xs_r, bc_r, dt_r, dtT_r, bias_r, bias_c, a_r, a_c, yr_ref,
                                    st_ref.at[lo + n_pairs:lo + 2 * n_pairs], bi=bi, reverse=True,
                                    is_meta=step == n_chunks - 1))
    _interleave([c for group in zip(*scans) for c in group], period=SSD_SKEW)


def _chunk_order(n_chunks, reverse):
    if reverse:
        return lambda c: (2 * n_chunks - 2 - c) % n_chunks
    return lambda c: (c + n_chunks - 1) % n_chunks


def _scan_specs(arrays, n_chunks, reverse, nb):
    order = _chunk_order(n_chunks, reverse)
    specs = []
    for a, transposed in arrays:
        if transposed:
            specs.append(pl.BlockSpec((nb, a.shape[1], CHUNK), lambda i, s: (i, 0, order(s))))
        else:
            specs.append(pl.BlockSpec((nb, CHUNK, a.shape[2]), lambda i, s: (i, order(s), 0)))
    return specs


def _scan_batch(b):
    return 2 if b % 2 == 0 else 1


def _ssd_scan(xs, bc, dt, dtT, consts):
    b, tp, _ = xs.shape
    n_chunks = tp // CHUNK
    nb = _scan_batch(b)
    arrays = [(xs, False), (bc, False), (dt, False), (dtT, True)]
    in_specs = (_scan_specs(arrays, n_chunks, False, nb) + _scan_specs(arrays, n_chunks, True, nb)
                + [_full(c.shape) for c in consts])
    out_specs = (_scan_specs([(xs, False)], n_chunks, False, nb)
                 + _scan_specs([(xs, False)], n_chunks, True, nb))
    out_shape = [jax.ShapeDtypeStruct((b, tp, SSD_INNER), BF16)] * 2
    return pl.pallas_call(
        functools.partial(_ssd_kernel, n_chunks=n_chunks),
        grid=(b // nb, n_chunks), in_specs=in_specs, out_specs=out_specs, out_shape=out_shape,
        scratch_shapes=[pltpu.VMEM((nb * SSD_HEADS, CHUNK, CHUNK), F32)],
        compiler_params=_params("parallel", "arbitrary"),
        name="ssd_scan")(xs, bc, dt, dtT, xs, bc, dt, dtT, *consts)


def _mlstm_direction(q_ref, k_ref, v_ref, gt_ref, gtT_ref, gb_r, gb_c, y_ref, st_ref, m_ref, *,
                     bi, reverse, is_meta):
    d = 1 if reverse else 0
    nh = MLSTM_HEADS
    keep, tri_c, tri_r, row, col = _tri_masks(reverse)
    gt = gt_ref[bi] + gb_r[...]
    gtT = gtT_ref[bi] + gb_c[...]
    i_lo, f_lo = d * nh, 2 * nh + d * nh
    pad_c = jnp.logical_and(is_meta, row[:, :nh] < N_PAD)
    pad_r = jnp.logical_and(is_meta, col[:nh, :] < N_PAD)
    ig_r = jnp.where(pad_r, NEG, gtT[i_lo:i_lo + nh, :])
    fg_c = jnp.where(pad_c, 0.0, _log_sigmoid(gt[:, f_lo:f_lo + nh]))
    fg_r = jnp.where(pad_r, 0.0, _log_sigmoid(gtT[f_lo:f_lo + nh, :]))
    b_c = _cumsum_cols(tri_c, fg_c)
    b_r = _cumsum_rows(fg_r, tri_r)
    last = 0 if reverse else CHUNK - 1

    lane_lo = col < MLSTM_QK
    row_lo = row < MLSTM_QK
    ones = jnp.ones((CHUNK, MLSTM_V), BF16)
    scale = MLSTM_QK ** -0.5

    def head_chain(h):
        pair = h // 2
        lo = (h % 2) == 0
        qp = q_ref[bi, :, pair * CHUNK:(pair + 1) * CHUNK]
        kp = k_ref[bi, :, pair * CHUNK:(pair + 1) * CHUNK]
        qm = (jnp.where(lane_lo if lo else jnp.logical_not(lane_lo), qp, 0.0) * scale).astype(BF16)
        kT = jnp.where(row_lo if lo else jnp.logical_not(row_lo), kp.astype(F32).T, 0.0)
        v_aug = jnp.concatenate([v_ref[bi, :, h * MLSTM_V:(h + 1) * MLSTM_V].astype(BF16), ones], axis=1)
        m_st = m_ref[h][0:1, 0:1]
        bc = b_c[:, h:h + 1]
        br = b_r[h:h + 1, :]
        ir = ig_r[h:h + 1, :]
        s_raw = _dot_nt(qm, kp.astype(BF16))
        c_st = st_ref[h]
        inter_mm = _dot(qm, c_st.astype(BF16))
        yield
        dmat = jnp.where(keep, ir - br, -jnp.inf)
        m_rel = jnp.maximum(jnp.max(dmat, axis=1, keepdims=True), m_st)
        tot = br[:, last:last + 1]
        d_last = tot - br + ir
        m_new = jnp.maximum(tot + m_st, jnp.max(d_last, axis=1, keepdims=True))
        yield
        m_rel_b = jnp.broadcast_to(m_rel, (CHUNK, CHUNK))
        w_intra = jnp.exp(dmat - m_rel_b)
        w_inter = jnp.exp(m_st - m_rel_b)
        floor = jnp.exp(-(bc + m_rel_b))
        w_s = jnp.exp(d_last - m_new)
        w_prev = jnp.exp(tot + m_st - m_new)
        yield
        s = s_raw * w_intra
        intra_mm = _dot(s.astype(BF16), v_aug)
        upd = _dot((kT * w_s).astype(BF16), v_aug)
        yield
        num = intra_mm[:, :MLSTM_V] + w_inter * inter_mm[:, :MLSTM_V]
        den = intra_mm[:, MLSTM_V:] + w_inter * inter_mm[:, MLSTM_V:]
        den = jnp.maximum(jnp.abs(den), floor)
        y_ref[bi, :, h * MLSTM_V:(h + 1) * MLSTM_V] = (num / den).astype(y_ref.dtype)
        st_ref[h] = w_prev * c_st + upd
        m_ref[h] = jnp.broadcast_to(m_new, m_ref.shape[1:])

    return [head_chain(h) for h in range(nh)]


def _mlstm_kernel(q_f, k_f, v_f, gt_f, gtT_f, q_r, k_r, v_r, gt_r, gtT_r, gb_r, gb_c,
                  yf_ref, yr_ref, st_ref, m_ref, *, n_chunks):
    step = pl.program_id(1)
    nh = MLSTM_HEADS

    @pl.when(step == 0)
    def _():
        st_ref[...] = jnp.zeros(st_ref.shape, F32)
        m_ref[...] = jnp.full(m_ref.shape, NEG, F32)

    scans = []
    for bi in range(q_f.shape[0]):
        lo = 2 * bi * nh
        scans.append(_mlstm_direction(q_f, k_f, v_f, gt_f, gtT_f, gb_r, gb_c, yf_ref, st_ref.at[lo:lo + nh],
                                      m_ref.at[lo:lo + nh], bi=bi, reverse=False, is_meta=step == 0))
        scans.append(_mlstm_direction(q_r, k_r, v_r, gt_r, gtT_r, gb_r, gb_c, yr_ref,
                                      st_ref.at[lo + nh:lo + 2 * nh], m_ref.at[lo + nh:lo + 2 * nh],
                                      bi=bi, reverse=True, is_meta=step == n_chunks - 1))
    _interleave([c for group in zip(*scans) for c in group], period=MLSTM_SKEW)


def _mlstm_scan(q, k, v, gt, gtT, consts):
    b, tp, _ = q.shape
    n_chunks = tp // CHUNK
    nb = _scan_batch(b)
    arrays = [(q, False), (k, False), (v, False), (gt, False), (gtT, True)]
    in_specs = (_scan_specs(arrays, n_chunks, False, nb) + _scan_specs(arrays, n_chunks, True, nb)
                + [_full(c.shape) for c in consts])
    out_specs = (_scan_specs([(v, False)], n_chunks, False, nb)
                 + _scan_specs([(v, False)], n_chunks, True, nb))
    out_shape = [jax.ShapeDtypeStruct(v.shape, BF16)] * 2
    return pl.pallas_call(
        functools.partial(_mlstm_kernel, n_chunks=n_chunks),
        grid=(b // nb, n_chunks), in_specs=in_specs, out_specs=out_specs, out_shape=out_shape,
        scratch_shapes=[pltpu.VMEM((nb * 2 * MLSTM_HEADS, CHUNK, 2 * MLSTM_V), F32),
                        pltpu.VMEM((nb * 2 * MLSTM_HEADS, 8, LANE), F32)],
        compiler_params=_params("parallel", "arbitrary"),
        name="mlstm_scan")(q, k, v, gt, gtT, q, k, v, gt, gtT, *consts)


def _out_kernel(ya_ref, yf_ref, yr_ref, xs_ref, z_ref, yc_ref, hf_ref, hr_ref, og_ref, h_ref,
                ag_ref, dsk_ref, ng_ref, mg_ref, w_ref, o_ref, *, tp, seq):
    tm = h_ref.shape[0]
    ya = _rms_rows(ya_ref[...].astype(F32), ag_ref[...])
    z = z_ref[...].astype(F32)
    yb = ((yf_ref[...].astype(F32) + yr_ref[...].astype(F32) + dsk_ref[...] * xs_ref[...])
          * (z * _sigmoid(z)))
    parts = [ya.astype(BF16), _rms_rows(yb, ng_ref[...]).astype(BF16), yc_ref[...].astype(BF16)]
    for hd in range(MLSTM_HEADS):
        lo, hi = hd * MLSTM_V, (hd + 1) * MLSTM_V
        hsum = hf_ref[:, lo:hi].astype(F32) + hr_ref[:, lo:hi].astype(F32)
        gate = _sigmoid(og_ref[:, lo:hi].astype(F32))
        parts.append((gate * _rms_rows(hsum, mg_ref[...])).astype(BF16))
    out = h_ref[...] + _dot(jnp.concatenate(parts, axis=1), w_ref[...])
    t = (pl.program_id(0) * tm + lax.broadcasted_iota(jnp.int32, (tm, 1), 0)) % tp
    is_pad = jnp.logical_and(t >= seq, t < seq + N_PAD)
    o_ref[...] = jnp.where(is_pad, 0.0, out)


def _out_proj(mixer_outs, h, consts, w, *, tp, seq, tm):
    rows, d = h.shape
    row_spec = lambda c: pl.BlockSpec((tm, c), lambda i: (i, 0))
    return pl.pallas_call(
        functools.partial(_out_kernel, tp=tp, seq=seq), grid=(rows // tm,),
        in_specs=([row_spec(a.shape[1]) for a in mixer_outs] + [row_spec(d)]
                  + [_full(c.shape) for c in consts] + [_full(w.shape)]),
        out_specs=row_spec(d), out_shape=jax.ShapeDtypeStruct((rows, d), F32),
        compiler_params=_params("parallel"), name="out_proj")(*mixer_outs, h, *consts, w)


def _ffn_kernel(h_ref, g_ref, wg_ref, wu_ref, wo_ref, o_ref, *, n_split):
    x = h_ref[0]
    hn = _rms_rows(x, g_ref[...]).astype(BF16)
    f = wg_ref.shape[1]
    tf = f // n_split
    acc = x
    for c in range(n_split):
        gate = _dot(hn, wg_ref[:, c * tf:(c + 1) * tf])
        up = _dot(hn, wu_ref[:, c * tf:(c + 1) * tf])
        act = (gate * _sigmoid(gate) * up).astype(BF16)
        acc = acc + _dot(act, wo_ref[c * tf:(c + 1) * tf, :])
    o_ref[0] = acc


def _ffn(h, g, wg, wu, wo, *, t_out, tm):
    b, _, d = h.shape
    f = wg.shape[1]
    n_split = 2 if (f // 2) % LANE == 0 else 1
    row_spec = pl.BlockSpec((1, tm, d), lambda i, j: (i, j, 0))
    resident = lambda a: pl.BlockSpec(a.shape, lambda i, j: (0, 0), pipeline_mode=pl.Buffered(1))
    return pl.pallas_call(
        functools.partial(_ffn_kernel, n_split=n_split), grid=(b, t_out // tm),
        in_specs=[row_spec, _full(g.shape), resident(wg), resident(wu), resident(wo)],
        out_specs=row_spec, out_shape=jax.ShapeDtypeStruct((b, t_out, d), F32),
        compiler_params=_params("parallel", "parallel"), name="ffn")(h, g, wg, wu, wo)


def _rope_tables_t(pos, rot_dim):
    inv = 1.0 / (ROPE_THETA ** (jnp.arange(0, rot_dim, 2, dtype=F32) / rot_dim))
    ang = pos[:, None] * inv[None, :]
    return jnp.cos(ang).T, jnp.sin(ang).T


def _col(v, n=None):
    v = v.astype(F32)
    if n is not None:
        v = jnp.pad(v, (0, n - v.shape[0]))
    return v[:, None]


def _row(v):
    return v.astype(F32)[None, :]


def kernel(x, meta_tokens, attn_norm_g, w_in, mla_q_norm_g, mla_kv_norm_g, mla_w_uq, mla_w_ukv, mla_q_head_g, mla_k_head_g, mla_out_g, ssd_conv_w, ssd_conv_b, ssd_dt_bias, ssd_a_log, ssd_d, ssd_norm_g, diff_q_head_g, diff_k_head_g, diff_lambda, diff_out_g, mlstm_i_bias, mlstm_f_bias, mlstm_out_g, w_out, ffn_norm_g, w_ffn_in, w_ffn_out):
    b, seq, d = x.shape
    depth = w_in.shape[0]
    tp = seq + CHUNK
    assert seq % CHUNK == 0
    tm_in = _pick_tile(tp, (640, 384, 128))
    tm_row = _pick_tile(tp, (1664, 640, 384, 128))
    tq = _pick_tile(tp, (1664, 640, 384, 128))
    tk = _pick_tile(seq, (256, 128))
    tm_flat = _pick_tile(b * tp, (512, 256, 128))
    tm_out = _pick_tile(seq, (512, 256, 128))

    meta = jnp.broadcast_to(meta_tokens[None].astype(x.dtype), (b, N_META, d))
    h = jnp.concatenate([x, jnp.zeros((b, N_PAD, d), x.dtype), meta], axis=1).reshape(b * tp, d)
    pos = jnp.concatenate([N_META + jnp.arange(seq, dtype=F32), jnp.zeros((N_PAD,), F32),
                           jnp.arange(N_META, dtype=F32)])
    cos_m, sin_m = _rope_tables_t(pos, MLA_ROPE)
    cos_d, sin_d = _rope_tables_t(pos, DIFF_ROPE)

    sizes = (MLA_Q_RANK, MLA_KV_RANK, MLA_ROPE, SSD_INNER, SSD_CONV_CH, 2 * SSD_HEADS,
             2 * DIFF_HEADS * DIFF_QK, 2 * DIFF_HEADS * DIFF_QK, DIFF_HEADS * DIFF_V,
             MLSTM_HEADS * MLSTM_QK, MLSTM_HEADS * MLSTM_QK, MLSTM_HEADS * MLSTM_V,
             MLSTM_HEADS * MLSTM_V, 2 * MLSTM_HEADS, 2 * MLSTM_HEADS)
    offs = [0]
    for s_ in sizes:
        offs.append(offs[-1] + s_)

    def cols(w, first, last):
        return w[:, offs[first]:offs[last + 1]]

    def pad_cols(w, n):
        return jnp.pad(w, ((0, 0), (0, n - w.shape[1])))

    for l in range(depth):
        lambda_init = 0.8 - 0.6 * math.exp(-0.3 * l)
        wl = w_in[l]
        h3 = h.reshape(b, tp, d)
        g_attn = _row(attn_norm_g[l])

        w_a = cols(wl, 0, 1).astype(BF16)
        w_kr = cols(wl, 2, 2).T.astype(BF16)
        w_uq = mla_w_uq[l].T
        w_ukv = mla_w_ukv[l].T.reshape(MLA_HEADS, MLA_NOPE + MLA_V, MLA_KV_RANK)
        w_uk = w_ukv[:, :MLA_NOPE].reshape(MLA_HEADS * MLA_NOPE, MLA_KV_RANK)
        w_uv = w_ukv[:, MLA_NOPE:].reshape(MLA_HEADS * MLA_V, MLA_KV_RANK)
        qT_a, k_a, vT_a = _in_proj_call(
            _mla_in_kernel, h3,
            [g_attn, w_a, _row(mla_q_norm_g[l]), _row(mla_kv_norm_g[l]), w_uq.astype(BF16),
             w_uk.astype(BF16), w_uv.astype(BF16), w_kr, _col(mla_q_head_g[l], HEAD_PAD),
             _col(mla_k_head_g[l], HEAD_PAD), (cos_m,), (sin_m,)],
            [((MLA_HEADS * HEAD_PAD, tp), BF16, "col"), ((MLA_HEADS, tp, HEAD_PAD), BF16, "head"),
             ((MLA_HEADS * MLA_V, tp), BF16, "col")], tm_in, "mla_in")
        y_a = _attention(qT_a, k_a, vT_a, [], diff=False, seq=seq, tq=tq, tk=tk, out_scale=1.0,
                         nsh=ATTN_HEADS_PER_STEP, name="mla_attn")

        w_c = cols(wl, 6, 8).T.astype(BF16)
        qT_c, k_c, vT_c = _in_proj_call(
            _diff_in_kernel, h3,
            [g_attn, w_c, _col(diff_q_head_g[l], HEAD_PAD), _col(diff_k_head_g[l], HEAD_PAD),
             (cos_d,), (sin_d,)],
            [((2 * DIFF_HEADS * HEAD_PAD, tp), BF16, "col"), ((2 * DIFF_HEADS, tp, HEAD_PAD), BF16, "head"),
             ((DIFF_HEADS * DIFF_V, tp), BF16, "col")], tm_in, "diff_in")
        lam = diff_lambda[l].astype(F32)
        lam_full = jnp.exp(jnp.sum(lam[0] * lam[1])) - jnp.exp(jnp.sum(lam[2] * lam[3])) + lambda_init
        y_c = _attention(qT_c, k_c, vT_c, [jnp.full((8, LANE), lam_full, F32), _col(diff_out_g[l])],
                         diff=True, seq=seq, tq=tq, tk=tk, out_scale=1.0 - lambda_init, nsh=ATTN_HEADS_PER_STEP,
                         name="diff_attn")

        n_b = SSD_INNER + SSD_CONV_CH + 2 * SSD_HEADS
        w_b = pad_cols(cols(wl, 3, 5), -(-n_b // LANE) * LANE).astype(BF16)
        w_dt = cols(wl, 5, 5).T.astype(BF16)
        z_b, xbc, dt, dtT = _in_proj_call(
            _ssd_in_kernel, h3, [g_attn, w_b, w_dt],
            [((tp, SSD_INNER), BF16, "row"), ((tp, SSD_CONV_CH), F32, "row"),
             ((tp, 2 * SSD_HEADS), F32, "row"), ((2 * SSD_HEADS, tp), F32, "col")], tm_row, "ssd_in")
        w8 = jnp.pad(ssd_conv_w[l].astype(F32), ((0, 8 - SSD_CONV), (0, 0)))
        xs_b, bc_b = _ssd_conv(xbc, w8, _row(ssd_conv_b[l]), tm_in)
        dt_bias = ssd_dt_bias[l].astype(F32).reshape(-1)
        a_neg = -jnp.exp(ssd_a_log[l].astype(F32)).reshape(-1)
        y_bf, y_br = _ssd_scan(xs_b, bc_b, dt, dtT, [_row(dt_bias), _col(dt_bias), _row(a_neg), _col(a_neg)])

        n_d = 2 * MLSTM_HEADS * MLSTM_QK + 2 * MLSTM_HEADS * MLSTM_V + 4 * MLSTM_HEADS
        w_d = pad_cols(cols(wl, 9, 14), -(-n_d // LANE) * LANE).astype(BF16)
        w_g = cols(wl, 13, 14).T.astype(BF16)
        q_d, k_d, v_d, o_d, gt, gtT = _in_proj_call(
            _mlstm_in_kernel, h3, [g_attn, w_d, w_g],
            [((tp, MLSTM_HEADS * MLSTM_QK), BF16, "row"), ((tp, MLSTM_HEADS * MLSTM_QK), BF16, "row"),
             ((tp, MLSTM_HEADS * MLSTM_V), BF16, "row"), ((tp, MLSTM_HEADS * MLSTM_V), BF16, "row"),
             ((tp, 4 * MLSTM_HEADS), F32, "row"), ((4 * MLSTM_HEADS, tp), F32, "col")], tm_row, "mlstm_in")
        gate_bias = jnp.concatenate([mlstm_i_bias[l].reshape(-1), mlstm_f_bias[l].reshape(-1)]).astype(F32)
        h_df, h_dr = _mlstm_scan(q_d, k_d, v_d, gt, gtT, [_row(gate_bias), _col(gate_bias)])

        flat = lambda a: a.reshape(b * tp, a.shape[2])
        mixer_outs = [flat(a) for a in (y_a, y_bf, y_br, xs_b, z_b, y_c, h_df, h_dr, o_d)]
        out_consts = [_row(mla_out_g[l]), _row(jnp.repeat(ssd_d[l], SSD_HEAD_DIM)), _row(ssd_norm_g[l]),
                      _row(mlstm_out_g[l])]
        h = _out_proj(mixer_outs, h, out_consts, w_out[l].astype(BF16), tp=tp, seq=seq, tm=tm_flat)
        f = w_ffn_out.shape[1]
        last = l == depth - 1
        h = _ffn(h.reshape(b, tp, d), _row(ffn_norm_g[l]), w_ffn_in[l][:, :f].astype(BF16),
                 w_ffn_in[l][:, f:].astype(BF16), w_ffn_out[l].astype(BF16),
                 t_out=seq if last else tp, tm=tm_out if last else tm_in)
        h = h.reshape(-1, d)

    return h.reshape(b, seq, d).astype(x.dtype)
```

```python
import functools
import math

import jax
import jax.numpy as jnp
from jax import lax
from jax.experimental import pallas as pl
from jax.experimental.pallas import tpu as pltpu

N_META = 16
ROPE_THETA = 500000.0
EPS = 1e-6
CHUNK = 128
N_PAD = CHUNK - N_META
NEG = -1e30
LOG2E = 1.4426950408889634

MLA_HEADS, MLA_NOPE, MLA_ROPE, MLA_V = 8, 64, 32, 64
MLA_QK = MLA_NOPE + MLA_ROPE
MLA_Q_RANK, MLA_KV_RANK = 384, 256
SSD_HEADS, SSD_HEAD_DIM, SSD_GROUPS, SSD_STATE, SSD_CONV = 8, 64, 2, 64, 5
SSD_INNER = SSD_HEADS * SSD_HEAD_DIM
SSD_CONV_CH = SSD_INNER + 2 * SSD_GROUPS * SSD_STATE
DIFF_HEADS, DIFF_QK = 4, 64
DIFF_V = 2 * DIFF_QK
DIFF_ROPE = DIFF_QK // 4
MLSTM_HEADS, MLSTM_QK, MLSTM_V = 4, 64, 128
HEAD_PAD = 128
LANE = 128
Q_STRIP = 256
ATTN_HEADS_PER_STEP = 4
SSD_SKEW, MLSTM_SKEW = 4, 1
VMEM_LIMIT = 52 * 1024 * 1024

F32 = jnp.float32
BF16 = jnp.bfloat16
EXP_DTYPE = jnp.bfloat16


def _dot(a, b):
    return jnp.dot(a, b, preferred_element_type=F32)


def _dot_nt(a, b):
    return lax.dot_general(a, b, (((1,), (1,)), ((), ())), preferred_element_type=F32)


def _rms_rows(x, g):
    ms = jnp.mean(x * x, axis=-1, keepdims=True)
    return x * lax.rsqrt(ms + EPS) * g


def _split3(a):
    hi = a.astype(BF16)
    r1 = a - hi.astype(F32)
    mid = r1.astype(BF16)
    lo = (r1 - mid.astype(F32)).astype(BF16)
    return hi, mid, lo


def _cumsum_cols(tri, a):
    hi, mid, lo = _split3(a)
    return _dot(tri, hi) + _dot(tri, mid) + _dot(tri, lo)


def _cumsum_rows(a, tri):
    hi, mid, lo = _split3(a)
    return _dot(hi, tri) + _dot(mid, tri) + _dot(lo, tri)


def _softplus(x):
    return jnp.maximum(x, 0.0) + jnp.log(1.0 + jnp.exp(-jnp.abs(x)))


def _log_sigmoid(x):
    return jnp.minimum(x, 0.0) - jnp.log(1.0 + jnp.exp(-jnp.abs(x)))


def _sigmoid(x):
    return 1.0 / (1.0 + jnp.exp(-x))


def _pick_tile(n, candidates):
    for c in candidates:
        if n % c == 0:
            return c
    raise ValueError(f"no tile in {candidates} divides {n}")


def _params(*sem):
    return pltpu.CompilerParams(dimension_semantics=sem, vmem_limit_bytes=VMEM_LIMIT)


def _norm_rope_t(blk, g_col, cos, sin, n_real):
    r = cos.shape[0]
    ms = jnp.sum(blk * blk, axis=0, keepdims=True) * (1.0 / n_real)
    y = blk * lax.rsqrt(ms + EPS) * g_col
    x1, x2, rest = y[:r], y[r:2 * r], y[2 * r:]
    return jnp.concatenate([x1 * cos - x2 * sin, x2 * cos + x1 * sin, rest], axis=0)


def _mla_in_kernel(h_ref, g_ref, wa_ref, gq_ref, gkv_ref, wuq_ref, wuk_ref, wuv_ref, wkr_ref,
                   qhg_ref, khg_ref, cos_ref, sin_ref, qT_ref, k_ref, vT_ref):
    tm = h_ref.shape[1]
    q_scale = (MLA_QK ** -0.5) * LOG2E

    def tile_chain(r0, t):
        hn = _rms_rows(h_ref[0, r0:r0 + t, :], g_ref[...]).astype(BF16)
        yield
        acc = _dot(hn, wa_ref[...])
        krT = _dot_nt(wkr_ref[...], hn)
        yield
        cqn = _rms_rows(acc[:, :MLA_Q_RANK], gq_ref[...]).astype(BF16)
        ckvn = _rms_rows(acc[:, MLA_Q_RANK:], gkv_ref[...]).astype(BF16)
        yield
        qT = _dot_nt(wuq_ref[...], cqn)
        knT = _dot_nt(wuk_ref[...], ckvn)
        vT = _dot_nt(wuv_ref[...], ckvn)
        yield
        cos, sin = cos_ref[:, r0:r0 + t], sin_ref[:, r0:r0 + t]
        zpad = jnp.zeros((HEAD_PAD - MLA_QK, t), F32)
        for h in range(MLA_HEADS):
            qb = _norm_rope_t(qT[h * MLA_QK:(h + 1) * MLA_QK], qhg_ref[0:MLA_QK, :], cos, sin, MLA_QK)
            qT_ref[0, h * HEAD_PAD:h * HEAD_PAD + MLA_QK, r0:r0 + t] = (qb * q_scale).astype(BF16)
            qT_ref[0, h * HEAD_PAD + MLA_QK:(h + 1) * HEAD_PAD, r0:r0 + t] = zpad.astype(BF16)
            kb = jnp.concatenate([krT, knT[h * MLA_NOPE:(h + 1) * MLA_NOPE]], axis=0)
            kb = _norm_rope_t(kb, khg_ref[0:MLA_QK, :], cos, sin, MLA_QK)
            k_ref[0, h, r0:r0 + t, :] = jnp.concatenate([kb, zpad], axis=0).T.astype(BF16)
            if h % 2 == 1:
                yield
        vT_ref[0, :, r0:r0 + t] = vT.astype(BF16)

    _interleave([tile_chain(r0, t) for r0, t in _sub_tiles(tm)], period=2)


def _diff_in_kernel(h_ref, g_ref, wc_ref, qhg_ref, khg_ref, cos_ref, sin_ref, qT_ref, k_ref, vT_ref):
    tm = h_ref.shape[1]
    nq = 2 * DIFF_HEADS * DIFF_QK
    q_scale = (DIFF_QK ** -0.5) * LOG2E

    def tile_chain(r0, t):
        hn = _rms_rows(h_ref[0, r0:r0 + t, :], g_ref[...]).astype(BF16)
        yield
        pT = _dot_nt(wc_ref[...], hn)
        yield
        cos, sin = cos_ref[:, r0:r0 + t], sin_ref[:, r0:r0 + t]
        zpad = jnp.zeros((HEAD_PAD - DIFF_QK, t), F32)
        for h in range(2 * DIFF_HEADS):
            qb = _norm_rope_t(pT[h * DIFF_QK:(h + 1) * DIFF_QK], qhg_ref[0:DIFF_QK, :], cos, sin, DIFF_QK)
            qT_ref[0, h * HEAD_PAD:h * HEAD_PAD + DIFF_QK, r0:r0 + t] = (qb * q_scale).astype(BF16)
            qT_ref[0, h * HEAD_PAD + DIFF_QK:(h + 1) * HEAD_PAD, r0:r0 + t] = zpad.astype(BF16)
            kb = _norm_rope_t(pT[nq + h * DIFF_QK:nq + (h + 1) * DIFF_QK], khg_ref[0:DIFF_QK, :], cos, sin,
                              DIFF_QK)
            k_ref[0, h, r0:r0 + t, :] = jnp.concatenate([kb, zpad], axis=0).T.astype(BF16)
            if h % 2 == 1:
                yield
        vT_ref[0, :, r0:r0 + t] = pT[2 * nq:].astype(BF16)

    _interleave([tile_chain(r0, t) for r0, t in _sub_tiles(tm)], period=2)


def _sub_tiles(tm):
    if tm < 1024:
        return [(0, tm)]
    first = (tm // 512) * 256 + (256 if tm % 512 >= 256 else 0)
    return [(0, first), (first, tm - first)] if first < tm else [(0, tm)]


def _ssd_in_kernel(h_ref, g_ref, wb_ref, wdt_ref, z_ref, xbc_ref, dt_ref, dtT_ref):
    hn = _rms_rows(h_ref[0], g_ref[...]).astype(BF16)
    acc = _dot(hn, wb_ref[...])
    z_ref[0] = acc[:, :SSD_INNER].astype(z_ref.dtype)
    xbc_ref[0] = acc[:, SSD_INNER:SSD_INNER + SSD_CONV_CH]
    dt_ref[0] = acc[:, SSD_INNER + SSD_CONV_CH:SSD_INNER + SSD_CONV_CH + 2 * SSD_HEADS]
    dtT_ref[0] = _dot_nt(wdt_ref[...], hn)


def _mlstm_in_kernel(h_ref, g_ref, wd_ref, wg_ref, q_ref, k_ref, v_ref, o_ref, gt_ref, gtT_ref):
    hn = _rms_rows(h_ref[0], g_ref[...]).astype(BF16)
    acc = _dot(hn, wd_ref[...])
    nqk = MLSTM_HEADS * MLSTM_QK
    nv = MLSTM_HEADS * MLSTM_V
    q_ref[0] = acc[:, :nqk].astype(q_ref.dtype)
    k_ref[0] = acc[:, nqk:2 * nqk].astype(k_ref.dtype)
    v_ref[0] = acc[:, 2 * nqk:2 * nqk + nv].astype(v_ref.dtype)
    o_ref[0] = acc[:, 2 * nqk + nv:2 * nqk + 2 * nv].astype(o_ref.dtype)
    gt_ref[0] = acc[:, 2 * nqk + 2 * nv:2 * nqk + 2 * nv + 4 * MLSTM_HEADS]
    gtT_ref[0] = _dot_nt(wg_ref[...], hn)


def _full(shape):
    nd = len(shape)
    return pl.BlockSpec(shape, lambda *_: (0,) * nd)


def _in_proj_call(body, h, consts, outs, tm, name):
    b, tp, d = h.shape
    in_specs = [pl.BlockSpec((1, tm, d), lambda i, j: (i, j, 0))]
    for c in consts:
        if isinstance(c, tuple):
            in_specs.append(pl.BlockSpec((c[0].shape[0], tm), lambda i, j: (0, j)))
        else:
            in_specs.append(_full(c.shape))
    out_shapes, out_specs = [], []
    for shape, dtype, kind in outs:
        out_shapes.append(jax.ShapeDtypeStruct((b,) + shape, dtype))
        if kind == "row":
            out_specs.append(pl.BlockSpec((1, tm, shape[1]), lambda i, j: (i, j, 0)))
        elif kind == "col":
            out_specs.append(pl.BlockSpec((1, shape[0], tm), lambda i, j: (i, 0, j)))
        else:
            out_specs.append(pl.BlockSpec((1, shape[0], tm, shape[2]), lambda i, j: (i, 0, j, 0)))
    args = [h] + [c[0] if isinstance(c, tuple) else c for c in consts]
    return pl.pallas_call(
        body, grid=(b, tp // tm), in_specs=in_specs, out_specs=out_specs, out_shape=out_shapes,
        compiler_params=_params("parallel", "parallel"), name=name)(*args)


def _attn_kernel(*refs, diff, seq, tk, out_scale, nsh):
    if diff:
        qT_ref, k_ref, vT_ref, lam_ref, og_ref, o_ref, s_scr, st_scr, m_scr, acc_scr = refs
    else:
        qT_ref, k_ref, vT_ref, o_ref, s_scr, st_scr, m_scr, acc_scr = refs
    dv = DIFF_V if diff else MLA_V
    tq = qT_ref.shape[2]
    n_chunks = seq // tk
    assert n_chunks % 2 == 0
    m_scr[...] = jnp.full(m_scr.shape, NEG, F32)
    acc_scr[...] = jnp.zeros(acc_scr.shape, F32)

    def stage(nxt, cur):
        k_n, v_aug = [], []
        for hh in range(nsh):
            if nxt is not None:
                k_n.append(k_ref[0, hh, pl.ds(nxt[1], nxt[2]), :])
            if cur is not None:
                v_lo = (hh // 2) * dv if diff else hh * dv
                ones = jnp.ones((16, cur[2]), BF16)
                v_aug.append(jnp.concatenate([vT_ref[0, v_lo:v_lo + dv, pl.ds(cur[1], cur[2])], ones],
                                             axis=0))
        pending = None
        for j0 in range(0, tq, Q_STRIP):
            w = min(Q_STRIP, tq - j0)
            for hh in range(nsh):
                if cur is not None:
                    src = st_scr.at[hh] if cur[0] is None else s_scr.at[hh, cur[0]]
                    s = src[:, j0:j0 + w]
                    m_old = m_scr[hh, 0:1, j0:j0 + w]
                    m_new = jnp.maximum(m_old, jnp.max(s, axis=0, keepdims=True))
                    alpha = jnp.exp2(m_old - m_new)
                    p = jnp.exp2((s - m_new).astype(EXP_DTYPE)).astype(BF16)
                    m_scr[hh, 0:1, j0:j0 + w] = m_new
                if nxt is not None:
                    dst = st_scr.at[hh] if nxt[0] is None else s_scr.at[hh, nxt[0]]
                    s_n = _dot(k_n[hh], qT_ref[0, hh * HEAD_PAD:(hh + 1) * HEAD_PAD, j0:j0 + w])
                    if nxt[3]:
                        row = lax.broadcasted_iota(jnp.int32, s_n.shape, 0)
                        s_n = jnp.where(row >= N_PAD, s_n, NEG)
                    dst[:, j0:j0 + w] = s_n
                if cur is not None:
                    if pending is not None:
                        ph, pj, pw, pa, pp = pending
                        acc_scr[ph, :, pj:pj + pw] = pa * acc_scr[ph, :, pj:pj + pw] + _dot(v_aug[ph], pp)
                    pending = (hh, j0, w, alpha, p)
        if pending is not None:
            ph, pj, pw, pa, pp = pending
            acc_scr[ph, :, pj:pj + pw] = pa * acc_scr[ph, :, pj:pj + pw] + _dot(v_aug[ph], pp)

    stage((0, 0, tk, False), None)

    def body(i, carry):
        off = pl.multiple_of(2 * i * tk, 2 * tk)
        stage((1, off + tk, tk, False), (0, off, tk))
        stage((0, off + 2 * tk, tk, False), (1, off + tk, tk))
        return carry

    lax.fori_loop(0, n_chunks // 2 - 1, body, 0)
    off = (n_chunks - 2) * tk
    stage((1, off + tk, tk, False), (0, off, tk))
    stage((None, seq, CHUNK, True), (1, off + tk, tk))
    stage(None, (None, seq, CHUNK))
    outs = []
    for hh in range(nsh):
        acc = acc_scr[hh]
        outs.append(acc[:dv] / acc[dv:dv + 1])
    if diff:
        pairs = []
        for hp in range(nsh // 2):
            o = outs[2 * hp] - lam_ref[0:1, 0:1] * outs[2 * hp + 1]
            ms = jnp.mean(o * o, axis=0, keepdims=True)
            pairs.append(o * lax.rsqrt(ms + EPS) * (og_ref[...] * out_scale))
        o = jnp.concatenate(pairs, axis=0) if len(pairs) > 1 else pairs[0]
    else:
        o = jnp.concatenate(outs, axis=0)
    o_ref[0] = o.T.astype(o_ref.dtype)


def _attention(qT, k, vT, extra, *, diff, seq, tq, tk, out_scale, nsh, name):
    b, _, tp = qT.shape
    groups = qT.shape[1] // (nsh * HEAD_PAD)
    dv = DIFF_V if diff else MLA_V
    v_rows = (nsh // 2) * dv if diff else nsh * dv
    o_cols = (nsh // 2) * LANE
    in_specs = [
        pl.BlockSpec((1, nsh * HEAD_PAD, tq), lambda i, g, j: (i, g, j)),
        pl.BlockSpec((1, nsh, tp, HEAD_PAD), lambda i, g, j: (i, g, 0, 0)),
        pl.BlockSpec((1, v_rows, tp), lambda i, g, j: (i, g, 0)),
    ] + [_full(e.shape) for e in extra]
    return pl.pallas_call(
        functools.partial(_attn_kernel, diff=diff, seq=seq, tk=tk, out_scale=out_scale, nsh=nsh),
        grid=(b, groups, tp // tq),
        in_specs=in_specs,
        out_specs=pl.BlockSpec((1, tq, o_cols), lambda i, g, j: (i, j, g)),
        out_shape=jax.ShapeDtypeStruct((b, tp, groups * o_cols), BF16),
        scratch_shapes=[pltpu.VMEM((nsh, 2, tk, tq), F32), pltpu.VMEM((nsh, CHUNK, tq), F32),
                        pltpu.VMEM((nsh, 8, tq), F32), pltpu.VMEM((nsh, dv + 16, tq), F32)],
        compiler_params=_params("parallel", "parallel", "arbitrary"), name=name)(qT, k, vT, *extra)


def _conv_kernel(x_ref, prev_ref, next_ref, w_ref, b_ref, xs_ref, bc_ref, scr):
    tc = x_ref.shape[1]
    scr[0:8, :] = prev_ref[0]
    scr[8:8 + tc, :] = x_ref[0]
    scr[8 + tc:16 + tc, :] = next_ref[0]
    acc = jnp.broadcast_to(b_ref[...], (tc, b_ref.shape[1]))
    for j in range(SSD_CONV):
        acc = acc + w_ref[j:j + 1, :] * scr[8 - SSD_CONV // 2 + j:8 - SSD_CONV // 2 + j + tc, :]
    act = acc * _sigmoid(acc)
    xs_ref[0] = act[:, :SSD_INNER]
    bc_ref[0] = act[:, SSD_INNER:]


def _ssd_conv(xbc, w8, bias, tc):
    b, tp, c = xbc.shape
    nb8 = tp // 8
    r8 = tc // 8
    row_spec = lambda n: pl.BlockSpec((1, tc, n), lambda i, j: (i, j, 0))
    return pl.pallas_call(
        _conv_kernel, grid=(b, tp // tc),
        in_specs=[row_spec(c),
                  pl.BlockSpec((1, 8, c), lambda i, j: (i, (j * r8 + nb8 - 1) % nb8, 0)),
                  pl.BlockSpec((1, 8, c), lambda i, j: (i, ((j + 1) * r8) % nb8, 0)),
                  _full(w8.shape), _full(bias.shape)],
        out_specs=[row_spec(SSD_INNER), row_spec(c - SSD_INNER)],
        out_shape=[jax.ShapeDtypeStruct((b, tp, SSD_INNER), F32),
                   jax.ShapeDtypeStruct((b, tp, c - SSD_INNER), F32)],
        scratch_shapes=[pltpu.VMEM((tc + 16, c), F32)],
        compiler_params=_params("parallel", "parallel"), name="ssd_conv")(xbc, xbc, xbc, w8, bias)


def _tri_masks(reverse):
    row = lax.broadcasted_iota(jnp.int32, (CHUNK, CHUNK), 0)
    col = lax.broadcasted_iota(jnp.int32, (CHUNK, CHUNK), 1)
    keep = (col >= row) if reverse else (col <= row)
    tri_c = keep.astype(BF16)
    tri_r = ((row >= col) if reverse else (row <= col)).astype(BF16)
    return keep, tri_c, tri_r, row, col


def _ssd_direction(xs_ref, bc_ref, dt_ref, dtT_ref, bias_r, bias_c, a_r, a_c, y_ref, st_ref, *,
                   bi, reverse, is_meta):
    d = 1 if reverse else 0
    h8 = SSD_HEADS
    keep, tri_c, tri_r, row, col = _tri_masks(reverse)
    bm = bc_ref[bi, :, :CHUNK]
    cm = bc_ref[bi, :, CHUNK:]

    dt_c = _softplus(dt_ref[bi][:, d * h8:(d + 1) * h8] + bias_r[:, d * h8:(d + 1) * h8])
    dt_r = _softplus(dtT_ref[bi][d * h8:(d + 1) * h8, :] + bias_c[d * h8:(d + 1) * h8, :])
    pad_c = jnp.logical_and(is_meta, row[:, :h8] < N_PAD)
    pad_r = jnp.logical_and(is_meta, col[:h8, :] < N_PAD)
    dt_c = jnp.where(pad_c, 0.0, dt_c)
    dt_r = jnp.where(pad_r, 0.0, dt_r)
    a_col = dt_c * a_r[:, d * h8:(d + 1) * h8]
    a_row = dt_r * a_c[d * h8:(d + 1) * h8, :]
    cs_c = _cumsum_cols(tri_c, a_col)
    cs_r = _cumsum_rows(a_row, tri_r)
    last = 0 if reverse else CHUNK - 1
    tot_r = cs_r[:, last:last + 1]
    tot_c = cs_c[last:last + 1, :]

    lane_lo = col < SSD_STATE
    row_lo = row < SSD_STATE
    blockdiag = jnp.logical_not(jnp.logical_xor(lane_lo, row_lo))
    cm_sw = pltpu.roll(cm, SSD_STATE, 1)
    c_dup = (jnp.where(lane_lo, cm, cm_sw), jnp.where(lane_lo, cm_sw, cm))
    g_mat = (_dot_nt(jnp.where(lane_lo, cm, 0.0).astype(BF16), bm.astype(BF16)),
             _dot_nt(jnp.where(lane_lo, 0.0, cm).astype(BF16), bm.astype(BF16)))
    bT = bm.T

    heads_per_group = SSD_HEADS // SSD_GROUPS

    def pair_chain(j):
        g = (2 * j) // heads_per_group
        h0, h1 = 2 * j, 2 * j + 1
        xp = xs_ref[bi, :, j * CHUNK:(j + 1) * CHUNK]
        cs_b = [jnp.broadcast_to(cs_c[:, h:h + 1], (CHUNK, CHUNK)) for h in (h0, h1)]
        yield
        parts = []
        for h, cs_h in zip((h0, h1), cs_b):
            diff_ = cs_h - cs_r[h:h + 1, :]
            decay = jnp.exp(jnp.where(keep, diff_, NEG)) * dt_r[h:h + 1, :]
            parts.append((g_mat[g] * decay).astype(BF16))
        parts.append((c_dup[g] * jnp.exp(jnp.where(lane_lo, cs_b[0], cs_b[1]))).astype(BF16))
        yield
        lhs = jnp.concatenate(parts, axis=1)
        s_old = st_ref[j]
        rhs = jnp.concatenate([jnp.where(lane_lo, xp, 0.0).astype(BF16),
                               jnp.where(lane_lo, 0.0, xp).astype(BF16),
                               s_old.astype(BF16)], axis=0)
        y_pair = _dot(lhs, rhs)
        btg = bT[g * SSD_STATE:(g + 1) * SSD_STATE, :]
        bd = jnp.concatenate(
            [btg * (jnp.exp(tot_r[h:h + 1, :] - cs_r[h:h + 1, :]) * dt_r[h:h + 1, :]) for h in (h0, h1)],
            axis=0)
        s_upd = _dot(bd.astype(BF16), xp.astype(BF16))
        yield
        carry = jnp.where(row_lo, jnp.exp(tot_c[:, h0:h0 + 1]), jnp.exp(tot_c[:, h1:h1 + 1]))
        st_ref[j] = s_old * carry + jnp.where(blockdiag, s_upd, 0.0)
        y_ref[bi, :, j * CHUNK:(j + 1) * CHUNK] = y_pair.astype(y_ref.dtype)

    return [pair_chain(j) for j in range(SSD_HEADS // 2)]


def _interleave(chains, period=1):
    pending = [(i % period, c) for i, c in enumerate(chains)]
    rnd = 0
    while pending:
        alive = []
        for delay, c in pending:
            if rnd >= delay:
                try:
                    next(c)
                except StopIteration:
                    continue
            alive.append((delay, c))
        pending = alive
        rnd += 1


def _ssd_kernel(xs_f, bc_f, dt_f, dtT_f, xs_r, bc_r, dt_r, dtT_r, bias_r, bias_c, a_r, a_c,
                yf_ref, yr_ref, st_ref, *, n_chunks):
    step = pl.program_id(1)
    n_pairs = SSD_HEADS // 2

    @pl.when(step == 0)
    def _():
        st_ref[...] = jnp.zeros(st_ref.shape, F32)

    scans = []
    for bi in range(xs_f.shape[0]):
        lo = 2 * bi * n_pairs
        scans.append(_ssd_direction(xs_f, bc_f, dt_f, dtT_f, bias_r, bias_c, a_r, a_c, yf_ref,
                                    st_ref.at[lo:lo + n_pairs], bi=bi, reverse=False, is_meta=step == 0))
        scans.append(_ssd_direction(xs_r, bc_r, dt_r, dtT_r, bias_r, bias_c, a_r, a_c, yr_ref,
                                    st_ref.at[lo + n_pairs:lo + 2 * n_pairs], bi=bi, reverse=True,
                                    is_meta=step == n_chunks - 1))
    _interleave([c for group in zip(*scans) for c in group], period=SSD_SKEW)


def _chunk_order(n_chunks, reverse):
    if reverse:
        return lambda c: (2 * n_chunks - 2 - c) % n_chunks
    return lambda c: (c + n_chunks - 1) % n_chunks


def _scan_specs(arrays, n_chunks, reverse, nb):
    order = _chunk_order(n_chunks, reverse)
    specs = []
    for a, transposed in arrays:
        if transposed:
            specs.append(pl.BlockSpec((nb, a.shape[1], CHUNK), lambda i, s: (i, 0, order(s))))
        else:
            specs.append(pl.BlockSpec((nb, CHUNK, a.shape[2]), lambda i, s: (i, order(s), 0)))
    return specs


def _scan_batch(b):
    return 2 if b % 2 == 0 else 1


def _ssd_scan(xs, bc, dt, dtT, consts):
    b, tp, _ = xs.shape
    n_chunks = tp // CHUNK
    nb = _scan_batch(b)
    arrays = [(xs, False), (bc, False), (dt, False), (dtT, True)]
    in_specs = (_scan_specs(arrays, n_chunks, False, nb) + _scan_specs(arrays, n_chunks, True, nb)
                + [_full(c.shape) for c in consts])
    out_specs = (_scan_specs([(xs, False)], n_chunks, False, nb)
                 + _scan_specs([(xs, False)], n_chunks, True, nb))
    out_shape = [jax.ShapeDtypeStruct((b, tp, SSD_INNER), BF16)] * 2
    return pl.pallas_call(
        functools.partial(_ssd_kernel, n_chunks=n_chunks),
        grid=(b // nb, n_chunks), in_specs=in_specs, out_specs=out_specs, out_shape=out_shape,
        scratch_shapes=[pltpu.VMEM((nb * SSD_HEADS, CHUNK, CHUNK), F32)],
        compiler_params=_params("parallel", "arbitrary"),
        name="ssd_scan")(xs, bc, dt, dtT, xs, bc, dt, dtT, *consts)


def _mlstm_direction(q_ref, k_ref, v_ref, gt_ref, gtT_ref, gb_r, gb_c, y_ref, st_ref, m_ref, *,
                     bi, reverse, is_meta):
    d = 1 if reverse else 0
    nh = MLSTM_HEADS
    keep, tri_c, tri_r, row, col = _tri_masks(reverse)
    gt = gt_ref[bi] + gb_r[...]
    gtT = gtT_ref[bi] + gb_c[...]
    i_lo, f_lo = d * nh, 2 * nh + d * nh
    pad_c = jnp.logical_and(is_meta, row[:, :nh] < N_PAD)
    pad_r = jnp.logical_and(is_meta, col[:nh, :] < N_PAD)
    ig_r = jnp.where(pad_r, NEG, gtT[i_lo:i_lo + nh, :])
    fg_c = jnp.where(pad_c, 0.0, _log_sigmoid(gt[:, f_lo:f_lo + nh]))
    fg_r = jnp.where(pad_r, 0.0, _log_sigmoid(gtT[f_lo:f_lo + nh, :]))
    b_c = _cumsum_cols(tri_c, fg_c)
    b_r = _cumsum_rows(fg_r, tri_r)
    last = 0 if reverse else CHUNK - 1

    lane_lo = col < MLSTM_QK
    row_lo = row < MLSTM_QK
    ones = jnp.ones((CHUNK, MLSTM_V), BF16)
    scale = MLSTM_QK ** -0.5

    def head_chain(h):
        pair = h // 2
        lo = (h % 2) == 0
        qp = q_ref[bi, :, pair * CHUNK:(pair + 1) * CHUNK]
        kp = k_ref[bi, :, pair * CHUNK:(pair + 1) * CHUNK]
        qm = (jnp.where(lane_lo if lo else jnp.logical_not(lane_lo), qp, 0.0) * scale).astype(BF16)
        kT = jnp.where(row_lo if lo else jnp.logical_not(row_lo), kp.astype(F32).T, 0.0)
        v_aug = jnp.concatenate([v_ref[bi, :, h * MLSTM_V:(h + 1) * MLSTM_V].astype(BF16), ones], axis=1)
        m_st = m_ref[h][0:1, 0:1]
        bc = b_c[:, h:h + 1]
        br = b_r[h:h + 1, :]
        ir = ig_r[h:h + 1, :]
        s_raw = _dot_nt(qm, kp.astype(BF16))
        c_st = st_ref[h]
        inter_mm = _dot(qm, c_st.astype(BF16))
        yield
        dmat = jnp.where(keep, ir - br, -jnp.inf)
        m_rel = jnp.maximum(jnp.max(dmat, axis=1, keepdims=True), m_st)
        tot = br[:, last:last + 1]
        d_last = tot - br + ir
        m_new = jnp.maximum(tot + m_st, jnp.max(d_last, axis=1, keepdims=True))
        yield
        m_rel_b = jnp.broadcast_to(m_rel, (CHUNK, CHUNK))
        w_intra = jnp.exp(dmat - m_rel_b)
        w_inter = jnp.exp(m_st - m_rel_b)
        floor = jnp.exp(-(bc + m_rel_b))
        w_s = jnp.exp(d_last - m_new)
        w_prev = jnp.exp(tot + m_st - m_new)
        yield
        s = s_raw * w_intra
        intra_mm = _dot(s.astype(BF16), v_aug)
        upd = _dot((kT * w_s).astype(BF16), v_aug)
        yield
        num = intra_mm[:, :MLSTM_V] + w_inter * inter_mm[:, :MLSTM_V]
        den = intra_mm[:, MLSTM_V:] + w_inter * inter_mm[:, MLSTM_V:]
        den = jnp.maximum(jnp.abs(den), floor)
        y_ref[bi, :, h * MLSTM_V:(h + 1) * MLSTM_V] = (num / den).astype(y_ref.dtype)
        st_ref[h] = w_prev * c_st + upd
        m_ref[h] = jnp.broadcast_to(m_new, m_ref.shape[1:])

    return [head_chain(h) for h in range(nh)]


def _mlstm_kernel(q_f, k_f, v_f, gt_f, gtT_f, q_r, k_r, v_r, gt_r, gtT_r, gb_r, gb_c,
                  yf_ref, yr_ref, st_ref, m_ref, *, n_chunks):
    step = pl.program_id(1)
    nh = MLSTM_HEADS

    @pl.when(step == 0)
    def _():
        st_ref[...] = jnp.zeros(st_ref.shape, F32)
        m_ref[...] = jnp.full(m_ref.shape, NEG, F32)

    scans = []
    for bi in range(q_f.shape[0]):
        lo = 2 * bi * nh
        scans.append(_mlstm_direction(q_f, k_f, v_f, gt_f, gtT_f, gb_r, gb_c, yf_ref, st_ref.at[lo:lo + nh],
                                      m_ref.at[lo:lo + nh], bi=bi, reverse=False, is_meta=step == 0))
        scans.append(_mlstm_direction(q_r, k_r, v_r, gt_r, gtT_r, gb_r, gb_c, yr_ref,
                                      st_ref.at[lo + nh:lo + 2 * nh], m_ref.at[lo + nh:lo + 2 * nh],
                                      bi=bi, reverse=True, is_meta=step == n_chunks - 1))
    _interleave([c for group in zip(*scans) for c in group], period=MLSTM_SKEW)


def _mlstm_scan(q, k, v, gt, gtT, consts):
    b, tp, _ = q.shape
    n_chunks = tp // CHUNK
    nb = _scan_batch(b)
    arrays = [(q, False), (k, False), (v, False), (gt, False), (gtT, True)]
    in_specs = (_scan_specs(arrays, n_chunks, False, nb) + _scan_specs(arrays, n_chunks, True, nb)
                + [_full(c.shape) for c in consts])
    out_specs = (_scan_specs([(v, False)], n_chunks, False, nb)
                 + _scan_specs([(v, False)], n_chunks, True, nb))
    out_shape = [jax.ShapeDtypeStruct(v.shape, BF16)] * 2
    return pl.pallas_call(
        functools.partial(_mlstm_kernel, n_chunks=n_chunks),
        grid=(b // nb, n_chunks), in_specs=in_specs, out_specs=out_specs, out_shape=out_shape,
        scratch_shapes=[pltpu.VMEM((nb * 2 * MLSTM_HEADS, CHUNK, 2 * MLSTM_V), F32),
                        pltpu.VMEM((nb * 2 * MLSTM_HEADS, 8, LANE), F32)],
        compiler_params=_params("parallel", "arbitrary"),
        name="mlstm_scan")(q, k, v, gt, gtT, q, k, v, gt, gtT, *consts)


def _out_kernel(ya_ref, yf_ref, yr_ref, xs_ref, z_ref, yc_ref, hf_ref, hr_ref, og_ref, h_ref,
                ag_ref, dsk_ref, ng_ref, mg_ref, w_ref, o_ref, *, tp, seq):
    tm = h_ref.shape[0]
    ya = _rms_rows(ya_ref[...].astype(F32), ag_ref[...])
    z = z_ref[...].astype(F32)
    yb = ((yf_ref[...].astype(F32) + yr_ref[...].astype(F32) + dsk_ref[...] * xs_ref[...])
          * (z * _sigmoid(z)))
    parts = [ya.astype(BF16), _rms_rows(yb, ng_ref[...]).astype(BF16), yc_ref[...].astype(BF16)]
    for hd in range(MLSTM_HEADS):
        lo, hi = hd * MLSTM_V, (hd + 1) * MLSTM_V
        hsum = hf_ref[:, lo:hi].astype(F32) + hr_ref[:, lo:hi].astype(F32)
        gate = _sigmoid(og_ref[:, lo:hi].astype(F32))
        parts.append((gate * _rms_rows(hsum, mg_ref[...])).astype(BF16))
    out = h_ref[...] + _dot(jnp.concatenate(parts, axis=1), w_ref[...])
    t = (pl.program_id(0) * tm + lax.broadcasted_iota(jnp.int32, (tm, 1), 0)) % tp
    is_pad = jnp.logical_and(t >= seq, t < seq + N_PAD)
    o_ref[...] = jnp.where(is_pad, 0.0, out)


def _out_proj(mixer_outs, h, consts, w, *, tp, seq, tm):
    rows, d = h.shape
    row_spec = lambda c: pl.BlockSpec((tm, c), lambda i: (i, 0))
    return pl.pallas_call(
        functools.partial(_out_kernel, tp=tp, seq=seq), grid=(rows // tm,),
        in_specs=([row_spec(a.shape[1]) for a in mixer_outs] + [row_spec(d)]
                  + [_full(c.shape) for c in consts] + [_full(w.shape)]),
        out_specs=row_spec(d), out_shape=jax.ShapeDtypeStruct((rows, d), F32),
        compiler_params=_params("parallel"), name="out_proj")(*mixer_outs, h, *consts, w)


def _ffn_kernel(h_ref, g_ref, wg_ref, wu_ref, wo_ref, o_ref, *, n_split):
    x = h_ref[0]
    hn = _rms_rows(x, g_ref[...]).astype(BF16)
    f = wg_ref.shape[1]
    tf = f // n_split
    acc = x
    for c in range(n_split):
        gate = _dot(hn, wg_ref[:, c * tf:(c + 1) * tf])
        up = _dot(hn, wu_ref[:, c * tf:(c + 1) * tf])
        act = (gate * _sigmoid(gate) * up).astype(BF16)
        acc = acc + _dot(act, wo_ref[c * tf:(c + 1) * tf, :])
    o_ref[0] = acc


def _ffn(h, g, wg, wu, wo, *, t_out, tm):
    b, _, d = h.shape
    f = wg.shape[1]
    n_split = 2 if (f // 2) % LANE == 0 else 1
    row_spec = pl.BlockSpec((1, tm, d), lambda i, j: (i, j, 0))
    resident = lambda a: pl.BlockSpec(a.shape, lambda i, j: (0, 0), pipeline_mode=pl.Buffered(1))
    return pl.pallas_call(
        functools.partial(_ffn_kernel, n_split=n_split), grid=(b, t_out // tm),
        in_specs=[row_spec, _full(g.shape), resident(wg), resident(wu), resident(wo)],
        out_specs=row_spec, out_shape=jax.ShapeDtypeStruct((b, t_out, d), F32),
        compiler_params=_params("parallel", "parallel"), name="ffn")(h, g, wg, wu, wo)


def _rope_tables_t(pos, rot_dim):
    inv = 1.0 / (ROPE_THETA ** (jnp.arange(0, rot_dim, 2, dtype=F32) / rot_dim))
    ang = pos[:, None] * inv[None, :]
    return jnp.cos(ang).T, jnp.sin(ang).T


def _col(v, n=None):
    v = v.astype(F32)
    if n is not None:
        v = jnp.pad(v, (0, n - v.shape[0]))
    return v[:, None]


def _row(v):
    return v.astype(F32)[None, :]


def kernel(x, meta_tokens, attn_norm_g, w_in, mla_q_norm_g, mla_kv_norm_g, mla_w_uq, mla_w_ukv, mla_q_head_g, mla_k_head_g, mla_out_g, ssd_conv_w, ssd_conv_b, ssd_dt_bias, ssd_a_log, ssd_d, ssd_norm_g, diff_q_head_g, diff_k_head_g, diff_lambda, diff_out_g, mlstm_i_bias, mlstm_f_bias, mlstm_out_g, w_out, ffn_norm_g, w_ffn_in, w_ffn_out):
    b, seq, d = x.shape
    depth = w_in.shape[0]
    tp = seq + CHUNK
    assert seq % CHUNK == 0
    tm_in = _pick_tile(tp, (640, 384, 128))
    tm_row = _pick_tile(tp, (1664, 640, 384, 128))
    tq = _pick_tile(tp, (1664, 640, 384, 128))
    tk = _pick_tile(seq, (256, 128))
    tm_flat = _pick_tile(b * tp, (512, 256, 128))
    tm_out = _pick_tile(seq, (512, 256, 128))

    meta = jnp.broadcast_to(meta_tokens[None].astype(x.dtype), (b, N_META, d))
    h = jnp.concatenate([x, jnp.zeros((b, N_PAD, d), x.dtype), meta], axis=1).reshape(b * tp, d)
    pos = jnp.concatenate([N_META + jnp.arange(seq, dtype=F32), jnp.zeros((N_PAD,), F32),
                           jnp.arange(N_META, dtype=F32)])
    cos_m, sin_m = _rope_tables_t(pos, MLA_ROPE)
    cos_d, sin_d = _rope_tables_t(pos, DIFF_ROPE)

    sizes = (MLA_Q_RANK, MLA_KV_RANK, MLA_ROPE, SSD_INNER, SSD_CONV_CH, 2 * SSD_HEADS,
             2 * DIFF_HEADS * DIFF_QK, 2 * DIFF_HEADS * DIFF_QK, DIFF_HEADS * DIFF_V,
             MLSTM_HEADS * MLSTM_QK, MLSTM_HEADS * MLSTM_QK, MLSTM_HEADS * MLSTM_V,
             MLSTM_HEADS * MLSTM_V, 2 * MLSTM_HEADS, 2 * MLSTM_HEADS)
    offs = [0]
    for s_ in sizes:
        offs.append(offs[-1] + s_)

    def cols(w, first, last):
        return w[:, offs[first]:offs[last + 1]]

    def pad_cols(w, n):
        return jnp.pad(w, ((0, 0), (0, n - w.shape[1])))

    for l in range(depth):
        lambda_init = 0.8 - 0.6 * math.exp(-0.3 * l)
        wl = w_in[l]
        h3 = h.reshape(b, tp, d)
        g_attn = _row(attn_norm_g[l])

        w_a = cols(wl, 0, 1).astype(BF16)
        w_kr = cols(wl, 2, 2).T.astype(BF16)
        w_uq = mla_w_uq[l].T
        w_ukv = mla_w_ukv[l].T.reshape(MLA_HEADS, MLA_NOPE + MLA_V, MLA_KV_RANK)
        w_uk = w_ukv[:, :MLA_NOPE].reshape(MLA_HEADS * MLA_NOPE, MLA_KV_RANK)
        w_uv = w_ukv[:, MLA_NOPE:].reshape(MLA_HEADS * MLA_V, MLA_KV_RANK)
        qT_a, k_a, vT_a = _in_proj_call(
            _mla_in_kernel, h3,
            [g_attn, w_a, _row(mla_q_norm_g[l]), _row(mla_kv_norm_g[l]), w_uq.astype(BF16),
             w_uk.astype(BF16), w_uv.astype(BF16), w_kr, _col(mla_q_head_g[l], HEAD_PAD),
             _col(mla_k_head_g[l], HEAD_PAD), (cos_m,), (sin_m,)],
            [((MLA_HEADS * HEAD_PAD, tp), BF16, "col"), ((MLA_HEADS, tp, HEAD_PAD), BF16, "head"),
             ((MLA_HEADS * MLA_V, tp), BF16, "col")], tm_row, "mla_in")
        y_a = _attention(qT_a, k_a, vT_a, [], diff=False, seq=seq, tq=tq, tk=tk, out_scale=1.0,
                         nsh=ATTN_HEADS_PER_STEP, name="mla_attn")

        w_c = cols(wl, 6, 8).T.astype(BF16)
        qT_c, k_c, vT_c = _in_proj_call(
            _diff_in_kernel, h3,
            [g_attn, w_c, _col(diff_q_head_g[l], HEAD_PAD), _col(diff_k_head_g[l], HEAD_PAD),
             (cos_d,), (sin_d,)],
            [((2 * DIFF_HEADS * HEAD_PAD, tp), BF16, "col"), ((2 * DIFF_HEADS, tp, HEAD_PAD), BF16, "head"),
             ((DIFF_HEADS * DIFF_V, tp), BF16, "col")], tm_row, "diff_in")
        lam = diff_lambda[l].astype(F32)
        lam_full = jnp.exp(jnp.sum(lam[0] * lam[1])) - jnp.exp(jnp.sum(lam[2] * lam[3])) + lambda_init
        y_c = _attention(qT_c, k_c, vT_c, [jnp.full((8, LANE), lam_full, F32), _col(diff_out_g[l])],
                         diff=True, seq=seq, tq=tq, tk=tk, out_scale=1.0 - lambda_init, nsh=ATTN_HEADS_PER_STEP,
                         name="diff_attn")

        n_b = SSD_INNER + SSD_CONV_CH + 2 * SSD_HEADS
        w_b = pad_cols(cols(wl, 3, 5), -(-n_b // LANE) * LANE).astype(BF16)
        w_dt = cols(wl, 5, 5).T.astype(BF16)
        z_b, xbc, dt, dtT = _in_proj_call(
            _ssd_in_kernel, h3, [g_attn, w_b, w_dt],
            [((tp, SSD_INNER), BF16, "row"), ((tp, SSD_CONV_CH), F32, "row"),
             ((tp, 2 * SSD_HEADS), F32, "row"), ((2 * SSD_HEADS, tp), F32, "col")], tm_row, "ssd_in")
        w8 = jnp.pad(ssd_conv_w[l].astype(F32), ((0, 8 - SSD_CONV), (0, 0)))
        xs_b, bc_b = _ssd_conv(xbc, w8, _row(ssd_conv_b[l]), tm_in)
        dt_bias = ssd_dt_bias[l].astype(F32).reshape(-1)
        a_neg = -jnp.exp(ssd_a_log[l].astype(F32)).reshape(-1)
        y_bf, y_br = _ssd_scan(xs_b, bc_b, dt, dtT, [_row(dt_bias), _col(dt_bias), _row(a_neg), _col(a_neg)])

        n_d = 2 * MLSTM_HEADS * MLSTM_QK + 2 * MLSTM_HEADS * MLSTM_V + 4 * MLSTM_HEADS
        w_d = pad_cols(cols(wl, 9, 14), -(-n_d // LANE) * LANE).astype(BF16)
        w_g = cols(wl, 13, 14).T.astype(BF16)
        q_d, k_d, v_d, o_d, gt, gtT = _in_proj_call(
            _mlstm_in_kernel, h3, [g_attn, w_d, w_g],
            [((tp, MLSTM_HEADS * MLSTM_QK), BF16, "row"), ((tp, MLSTM_HEADS * MLSTM_QK), BF16, "row"),
             ((tp, MLSTM_HEADS * MLSTM_V), BF16, "row"), ((tp, MLSTM_HEADS * MLSTM_V), BF16, "row"),
             ((tp, 4 * MLSTM_HEADS), F32, "row"), ((4 * MLSTM_HEADS, tp), F32, "col")], tm_row, "mlstm_in")
        gate_bias = jnp.concatenate([mlstm_i_bias[l].reshape(-1), mlstm_f_bias[l].reshape(-1)]).astype(F32)
        h_df, h_dr = _mlstm_scan(q_d, k_d, v_d, gt, gtT, [_row(gate_bias), _col(gate_bias)])

        flat = lambda a: a.reshape(b * tp, a.shape[2])
        mixer_outs = [flat(a) for a in (y_a, y_bf, y_br, xs_b, z_b, y_c, h_df, h_dr, o_d)]
        out_consts = [_row(mla_out_g[l]), _row(jnp.repeat(ssd_d[l], SSD_HEAD_DIM)), _row(ssd_norm_g[l]),
                      _row(mlstm_out_g[l])]
        h = _out_proj(mixer_outs, h, out_consts, w_out[l].astype(BF16), tp=tp, seq=seq, tm=tm_flat)
        f = w_ffn_out.shape[1]
        last = l == depth - 1
        h = _ffn(h.reshape(b, tp, d), _row(ffn_norm_g[l]), w_ffn_in[l][:, :f].astype(BF16),
                 w_ffn_in[l][:, f:].astype(BF16), w_ffn_out[l].astype(BF16),
                 t_out=seq if last else tp, tm=tm_out if last else tm_in)
        h = h.reshape(-1, d)

    return h.reshape(b, seq, d).astype(x.dtype)
```

```python
import functools
import math

import jax
import jax.numpy as jnp
from jax import lax
from jax.experimental import pallas as pl
from jax.experimental.pallas import tpu as pltpu

N_META = 16
ROPE_THETA = 500000.0
EPS = 1e-6
CHUNK = 128
N_PAD = CHUNK - N_META
NEG = -1e30
LOG2E = 1.4426950408889634

MLA_HEADS, MLA_NOPE, MLA_ROPE, MLA_V = 8, 64, 32, 64
MLA_QK = MLA_NOPE + MLA_ROPE
MLA_Q_RANK, MLA_KV_RANK = 384, 256
SSD_HEADS, SSD_HEAD_DIM, SSD_GROUPS, SSD_STATE, SSD_CONV = 8, 64, 2, 64, 5
SSD_INNER = SSD_HEADS * SSD_HEAD_DIM
SSD_CONV_CH = SSD_INNER + 2 * SSD_GROUPS * SSD_STATE
DIFF_HEADS, DIFF_QK = 4, 64
DIFF_V = 2 * DIFF_QK
DIFF_ROPE = DIFF_QK // 4
MLSTM_HEADS, MLSTM_QK, MLSTM_V = 4, 64, 128
HEAD_PAD = 128
LANE = 128
Q_STRIP = 256
ATTN_HEADS_PER_STEP = 4
SSD_SKEW, MLSTM_SKEW = 4, 1
VMEM_LIMIT = 52 * 1024 * 1024

F32 = jnp.float32
BF16 = jnp.bfloat16
EXP_DTYPE = jnp.bfloat16


def _dot(a, b):
    return jnp.dot(a, b, preferred_element_type=F32)


def _dot_nt(a, b):
    return lax.dot_general(a, b, (((1,), (1,)), ((), ())), preferred_element_type=F32)


def _rms_rows(x, g):
    ms = jnp.mean(x * x, axis=-1, keepdims=True)
    return x * lax.rsqrt(ms + EPS) * g


def _split3(a):
    hi = a.astype(BF16)
    r1 = a - hi.astype(F32)
    mid = r1.astype(BF16)
    lo = (r1 - mid.astype(F32)).astype(BF16)
    return hi, mid, lo


def _cumsum_cols(tri, a):
    hi, mid, lo = _split3(a)
    return _dot(tri, hi) + _dot(tri, mid) + _dot(tri, lo)


def _cumsum_rows(a, tri):
    hi, mid, lo = _split3(a)
    return _dot(hi, tri) + _dot(mid, tri) + _dot(lo, tri)


def _softplus(x):
    return jnp.maximum(x, 0.0) + jnp.log(1.0 + jnp.exp(-jnp.abs(x)))


def _log_sigmoid(x):
    return jnp.minimum(x, 0.0) - jnp.log(1.0 + jnp.exp(-jnp.abs(x)))


def _sigmoid(x):
    return 1.0 / (1.0 + jnp.exp(-x))


def _pick_tile(n, candidates):
    for c in candidates:
        if n % c == 0:
            return c
    raise ValueError(f"no tile in {candidates} divides {n}")


def _params(*sem):
    return pltpu.CompilerParams(dimension_semantics=sem, vmem_limit_bytes=VMEM_LIMIT)


def _norm_rope_t(blk, g_col, cos, sin, n_real):
    r = cos.shape[0]
    ms = jnp.sum(blk * blk, axis=0, keepdims=True) * (1.0 / n_real)
    y = blk * lax.rsqrt(ms + EPS) * g_col
    x1, x2, rest = y[:r], y[r:2 * r], y[2 * r:]
    return jnp.concatenate([x1 * cos - x2 * sin, x2 * cos + x1 * sin, rest], axis=0)


def _mla_in_kernel(h_ref, g_ref, wa_ref, gq_ref, gkv_ref, wuq_ref, wuk_ref, wuv_ref, wkr_ref,
                   qhg_ref, khg_ref, cos_ref, sin_ref, qT_ref, k_ref, vT_ref):
    tm = h_ref.shape[1]
    q_scale = (MLA_QK ** -0.5) * LOG2E

    def tile_chain(r0, t):
        hn = _rms_rows(h_ref[0, r0:r0 + t, :], g_ref[...]).astype(BF16)
        yield
        acc = _dot(hn, wa_ref[...])
        krT = _dot_nt(wkr_ref[...], hn)
        yield
        cqn = _rms_rows(acc[:, :MLA_Q_RANK], gq_ref[...]).astype(BF16)
        ckvn = _rms_rows(acc[:, MLA_Q_RANK:], gkv_ref[...]).astype(BF16)
        yield
        qT = _dot_nt(wuq_ref[...], cqn)
        knT = _dot_nt(wuk_ref[...], ckvn)
        vT = _dot_nt(wuv_ref[...], ckvn)
        yield
        cos, sin = cos_ref[:, r0:r0 + t], sin_ref[:, r0:r0 + t]
        zpad = jnp.zeros((HEAD_PAD - MLA_QK, t), F32)
        for h in range(MLA_HEADS):
            qb = _norm_rope_t(qT[h * MLA_QK:(h + 1) * MLA_QK], qhg_ref[0:MLA_QK, :], cos, sin, MLA_QK)
            qT_ref[0, h * HEAD_PAD:h * HEAD_PAD + MLA_QK, r0:r0 + t] = (qb * q_scale).astype(BF16)
            qT_ref[0, h * HEAD_PAD + MLA_QK:(h + 1) * HEAD_PAD, r0:r0 + t] = zpad.astype(BF16)
            kb = jnp.concatenate([krT, knT[h * MLA_NOPE:(h + 1) * MLA_NOPE]], axis=0)
            kb = _norm_rope_t(kb, khg_ref[0:MLA_QK, :], cos, sin, MLA_QK)
            k_ref[0, h, r0:r0 + t, :] = jnp.concatenate([kb, zpad], axis=0).T.astype(BF16)
            if h % 2 == 1:
                yield
        vT_ref[0, :, r0:r0 + t] = vT.astype(BF16)

    _interleave([tile_chain(r0, t) for r0, t in _sub_tiles(tm)], period=2)


def _diff_in_kernel(h_ref, g_ref, wc_ref, qhg_ref, khg_ref, cos_ref, sin_ref, qT_ref, k_ref, vT_ref):
    tm = h_ref.shape[1]
    nq = 2 * DIFF_HEADS * DIFF_QK
    q_scale = (DIFF_QK ** -0.5) * LOG2E

    def tile_chain(r0, t):
        hn = _rms_rows(h_ref[0, r0:r0 + t, :], g_ref[...]).astype(BF16)
        yield
        pT = _dot_nt(wc_ref[...], hn)
        yield
        cos, sin = cos_ref[:, r0:r0 + t], sin_ref[:, r0:r0 + t]
        zpad = jnp.zeros((HEAD_PAD - DIFF_QK, t), F32)
        for h in range(2 * DIFF_HEADS):
            qb = _norm_rope_t(pT[h * DIFF_QK:(h + 1) * DIFF_QK], qhg_ref[0:DIFF_QK, :], cos, sin, DIFF_QK)
            qT_ref[0, h * HEAD_PAD:h * HEAD_PAD + DIFF_QK, r0:r0 + t] = (qb * q_scale).astype(BF16)
            qT_ref[0, h * HEAD_PAD + DIFF_QK:(h + 1) * HEAD_PAD, r0:r0 + t] = zpad.astype(BF16)
            kb = _norm_rope_t(pT[nq + h * DIFF_QK:nq + (h + 1) * DIFF_QK], khg_ref[0:DIFF_QK, :], cos, sin,
                              DIFF_QK)
            k_ref[0, h, r0:r0 + t, :] = jnp.concatenate([kb, zpad], axis=0).T.astype(BF16)
            if h % 2 == 1:
                yield
        vT_ref[0, :, r0:r0 + t] = pT[2 * nq:].astype(BF16)

    _interleave([tile_chain(r0, t) for r0, t in _sub_tiles(tm)])


def _sub_tiles(tm):
    if tm < 1024:
        return [(0, tm)]
    first = (tm // 512) * 256 + (256 if tm % 512 >= 256 else 0)
    return [(0, first), (first, tm - first)] if first < tm else [(0, tm)]


def _ssd_in_kernel(h_ref, g_ref, wb_ref, wdt_ref, z_ref, xbc_ref, dt_ref, dtT_ref):
    def tile_chain(r0, t):
        hn = _rms_rows(h_ref[0, r0:r0 + t, :], g_ref[...]).astype(BF16)
        yield
        acc = _dot(hn, wb_ref[...])
        dtT_ref[0, :, r0:r0 + t] = _dot_nt(wdt_ref[...], hn)
        yield
        z_ref[0, r0:r0 + t, :] = acc[:, :SSD_INNER].astype(z_ref.dtype)
        xbc_ref[0, r0:r0 + t, :] = acc[:, SSD_INNER:SSD_INNER + SSD_CONV_CH]
        dt_ref[0, r0:r0 + t, :] = acc[:, SSD_INNER + SSD_CONV_CH:SSD_INNER + SSD_CONV_CH + 2 * SSD_HEADS]

    _interleave([tile_chain(r0, t) for r0, t in _sub_tiles(h_ref.shape[1])])


def _mlstm_in_kernel(h_ref, g_ref, wd_ref, wg_ref, q_ref, k_ref, v_ref, o_ref, gt_ref, gtT_ref):
    nqk = MLSTM_HEADS * MLSTM_QK
    nv = MLSTM_HEADS * MLSTM_V

    def tile_chain(r0, t):
        hn = _rms_rows(h_ref[0, r0:r0 + t, :], g_ref[...]).astype(BF16)
        yield
        acc = _dot(hn, wd_ref[...])
        gtT_ref[0, :, r0:r0 + t] = _dot_nt(wg_ref[...], hn)
        yield
        q_ref[0, r0:r0 + t, :] = acc[:, :nqk].astype(q_ref.dtype)
        k_ref[0, r0:r0 + t, :] = acc[:, nqk:2 * nqk].astype(k_ref.dtype)
        v_ref[0, r0:r0 + t, :] = acc[:, 2 * nqk:2 * nqk + nv].astype(v_ref.dtype)
        o_ref[0, r0:r0 + t, :] = acc[:, 2 * nqk + nv:2 * nqk + 2 * nv].astype(o_ref.dtype)
        gt_ref[0, r0:r0 + t, :] = acc[:, 2 * nqk + 2 * nv:2 * nqk + 2 * nv + 4 * MLSTM_HEADS]

    _interleave([tile_chain(r0, t) for r0, t in _sub_tiles(h_ref.shape[1])])


def _full(shape):
    nd = len(shape)
    return pl.BlockSpec(shape, lambda *_: (0,) * nd)


def _in_proj_call(body, h, consts, outs, tm, name):
    b, tp, d = h.shape
    in_specs = [pl.BlockSpec((1, tm, d), lambda i, j: (i, j, 0))]
    for c in consts:
        if isinstance(c, tuple):
            in_specs.append(pl.BlockSpec((c[0].shape[0], tm), lambda i, j: (0, j)))
        else:
            in_specs.append(_full(c.shape))
    out_shapes, out_specs = [], []
    for shape, dtype, kind in outs:
        out_shapes.append(jax.ShapeDtypeStruct((b,) + shape, dtype))
        if kind == "row":
            out_specs.append(pl.BlockSpec((1, tm, shape[1]), lambda i, j: (i, j, 0)))
        elif kind == "col":
            out_specs.append(pl.BlockSpec((1, shape[0], tm), lambda i, j: (i, 0, j)))
        else:
            out_specs.append(pl.BlockSpec((1, shape[0], tm, shape[2]), lambda i, j: (i, 0, j, 0)))
    args = [h] + [c[0] if isinstance(c, tuple) else c for c in consts]
    return pl.pallas_call(
        body, grid=(b, tp // tm), in_specs=in_specs, out_specs=out_specs, out_shape=out_shapes,
        compiler_params=_params("parallel", "parallel"), name=name)(*args)


def _attn_kernel(*refs, diff, seq, tk, out_scale, nsh):
    if diff:
        qT_ref, k_ref, vT_ref, lam_ref, og_ref, o_ref, s_scr, st_scr, m_scr, acc_scr = refs
    else:
        qT_ref, k_ref, vT_ref, o_ref, s_scr, st_scr, m_scr, acc_scr = refs
    dv = DIFF_V if diff else MLA_V
    tq = qT_ref.shape[2]
    n_chunks = seq // tk
    assert n_chunks % 2 == 0
    m_scr[...] = jnp.full(m_scr.shape, NEG, F32)
    acc_scr[...] = jnp.zeros(acc_scr.shape, F32)

    def stage(nxt, cur):
        k_n, v_aug = [], []
        for hh in range(nsh):
            if nxt is not None:
                k_n.append(k_ref[0, hh, pl.ds(nxt[1], nxt[2]), :])
            if cur is not None:
                v_lo = (hh // 2) * dv if diff else hh * dv
                ones = jnp.ones((16, cur[2]), BF16)
                v_aug.append(jnp.concatenate([vT_ref[0, v_lo:v_lo + dv, pl.ds(cur[1], cur[2])], ones],
                                             axis=0))
        pending = None
        for j0 in range(0, tq, Q_STRIP):
            w = min(Q_STRIP, tq - j0)
            for hh in range(nsh):
                if cur is not None:
                    src = st_scr.at[hh] if cur[0] is None else s_scr.at[hh, cur[0]]
                    s = src[:, j0:j0 + w]
                    m_old = m_scr[hh, 0:1, j0:j0 + w]
                    m_new = jnp.maximum(m_old, jnp.max(s, axis=0, keepdims=True))
                    alpha = jnp.exp2(m_old - m_new)
                    p = jnp.exp2((s - m_new).astype(EXP_DTYPE)).astype(BF16)
                    m_scr[hh, 0:1, j0:j0 + w] = m_new
                if nxt is not None:
                    dst = st_scr.at[hh] if nxt[0] is None else s_scr.at[hh, nxt[0]]
                    s_n = _dot(k_n[hh], qT_ref[0, hh * HEAD_PAD:(hh + 1) * HEAD_PAD, j0:j0 + w])
                    if nxt[3]:
                        row = lax.broadcasted_iota(jnp.int32, s_n.shape, 0)
                        s_n = jnp.where(row >= N_PAD, s_n, NEG)
                    dst[:, j0:j0 + w] = s_n
                if cur is not None:
                    if pending is not None:
                        ph, pj, pw, pa, pp = pending
                        acc_scr[ph, :, pj:pj + pw] = pa * acc_scr[ph, :, pj:pj + pw] + _dot(v_aug[ph], pp)
                    pending = (hh, j0, w, alpha, p)
        if pending is not None:
            ph, pj, pw, pa, pp = pending
            acc_scr[ph, :, pj:pj + pw] = pa * acc_scr[ph, :, pj:pj + pw] + _dot(v_aug[ph], pp)

    stage((0, 0, tk, False), None)

    def body(i, carry):
        off = pl.multiple_of(2 * i * tk, 2 * tk)
        stage((1, off + tk, tk, False), (0, off, tk))
        stage((0, off + 2 * tk, tk, False), (1, off + tk, tk))
        return carry

    lax.fori_loop(0, n_chunks // 2 - 1, body, 0)
    off = (n_chunks - 2) * tk
    stage((1, off + tk, tk, False), (0, off, tk))
    stage((None, seq, CHUNK, True), (1, off + tk, tk))
    stage(None, (None, seq, CHUNK))
    outs = []
    for hh in range(nsh):
        acc = acc_scr[hh]
        outs.append(acc[:dv] / acc[dv:dv + 1])
    if diff:
        pairs = []
        for hp in range(nsh // 2):
            o = outs[2 * hp] - lam_ref[0:1, 0:1] * outs[2 * hp + 1]
            ms = jnp.mean(o * o, axis=0, keepdims=True)
            pairs.append(o * lax.rsqrt(ms + EPS) * (og_ref[...] * out_scale))
        o = jnp.concatenate(pairs, axis=0) if len(pairs) > 1 else pairs[0]
    else:
        o = jnp.concatenate(outs, axis=0)
    o_ref[0] = o.T.astype(o_ref.dtype)


def _attention(qT, k, vT, extra, *, diff, seq, tq, tk, out_scale, nsh, name):
    b, _, tp = qT.shape
    groups = qT.shape[1] // (nsh * HEAD_PAD)
    dv = DIFF_V if diff else MLA_V
    v_rows = (nsh // 2) * dv if diff else nsh * dv
    o_cols = (nsh // 2) * LANE
    in_specs = [
        pl.BlockSpec((1, nsh * HEAD_PAD, tq), lambda i, g, j: (i, g, j)),
        pl.BlockSpec((1, nsh, tp, HEAD_PAD), lambda i, g, j: (i, g, 0, 0)),
        pl.BlockSpec((1, v_rows, tp), lambda i, g, j: (i, g, 0)),
    ] + [_full(e.shape) for e in extra]
    return pl.pallas_call(
        functools.partial(_attn_kernel, diff=diff, seq=seq, tk=tk, out_scale=out_scale, nsh=nsh),
        grid=(b, groups, tp // tq),
        in_specs=in_specs,
        out_specs=pl.BlockSpec((1, tq, o_cols), lambda i, g, j: (i, j, g)),
        out_shape=jax.ShapeDtypeStruct((b, tp, groups * o_cols), BF16),
        scratch_shapes=[pltpu.VMEM((nsh, 2, tk, tq), F32), pltpu.VMEM((nsh, CHUNK, tq), F32),
                        pltpu.VMEM((nsh, 8, tq), F32), pltpu.VMEM((nsh, dv + 16, tq), F32)],
        compiler_params=_params("parallel", "parallel", "arbitrary"), name=name)(qT, k, vT, *extra)


def _conv_kernel(x_ref, prev_ref, next_ref, w_ref, b_ref, xs_ref, bc_ref, scr):
    tc = x_ref.shape[1]
    scr[0:8, :] = prev_ref[0]
    scr[8:8 + tc, :] = x_ref[0]
    scr[8 + tc:16 + tc, :] = next_ref[0]
    acc = jnp.broadcast_to(b_ref[...], (tc, b_ref.shape[1]))
    for j in range(SSD_CONV):
        acc = acc + w_ref[j:j + 1, :] * scr[8 - SSD_CONV // 2 + j:8 - SSD_CONV // 2 + j + tc, :]
    act = acc * _sigmoid(acc)
    xs_ref[0] = act[:, :SSD_INNER].astype(xs_ref.dtype)
    bc_ref[0] = act[:, SSD_INNER:]


def _ssd_conv(xbc, w8, bias, tc):
    b, tp, c = xbc.shape
    nb8 = tp // 8
    r8 = tc // 8
    row_spec = lambda n: pl.BlockSpec((1, tc, n), lambda i, j: (i, j, 0))
    return pl.pallas_call(
        _conv_kernel, grid=(b, tp // tc),
        in_specs=[row_spec(c),
                  pl.BlockSpec((1, 8, c), lambda i, j: (i, (j * r8 + nb8 - 1) % nb8, 0)),
                  pl.BlockSpec((1, 8, c), lambda i, j: (i, ((j + 1) * r8) % nb8, 0)),
                  _full(w8.shape), _full(bias.shape)],
        out_specs=[row_spec(SSD_INNER), row_spec(c - SSD_INNER)],
        out_shape=[jax.ShapeDtypeStruct((b, tp, SSD_INNER), BF16),
                   jax.ShapeDtypeStruct((b, tp, c - SSD_INNER), F32)],
        scratch_shapes=[pltpu.VMEM((tc + 16, c), F32)],
        compiler_params=_params("parallel", "parallel"), name="ssd_conv")(xbc, xbc, xbc, w8, bias)


def _tri_masks(reverse):
    row = lax.broadcasted_iota(jnp.int32, (CHUNK, CHUNK), 0)
    col = lax.broadcasted_iota(jnp.int32, (CHUNK, CHUNK), 1)
    keep = (col >= row) if reverse else (col <= row)
    tri_c = keep.astype(BF16)
    tri_r = ((row >= col) if reverse else (row <= col)).astype(BF16)
    return keep, tri_c, tri_r, row, col


def _ssd_direction(xs_ref, bc_ref, dt_ref, dtT_ref, bias_r, bias_c, a_r, a_c, y_ref, st_ref, *,
                   bi, reverse, is_meta):
    d = 1 if reverse else 0
    h8 = SSD_HEADS
    keep, tri_c, tri_r, row, col = _tri_masks(reverse)
    bm = bc_ref[bi, :, :CHUNK]
    cm = bc_ref[bi, :, CHUNK:]

    dt_c = _softplus(dt_ref[bi][:, d * h8:(d + 1) * h8] + bias_r[:, d * h8:(d + 1) * h8])
    dt_r = _softplus(dtT_ref[bi][d * h8:(d + 1) * h8, :] + bias_c[d * h8:(d + 1) * h8, :])
    pad_c = jnp.logical_and(is_meta, row[:, :h8] < N_PAD)
    pad_r = jnp.logical_and(is_meta, col[:h8, :] < N_PAD)
    dt_c = jnp.where(pad_c, 0.0, dt_c)
    dt_r = jnp.where(pad_r, 0.0, dt_r)
    a_col = dt_c * a_r[:, d * h8:(d + 1) * h8]
    a_row = dt_r * a_c[d * h8:(d + 1) * h8, :]
    cs_c = _cumsum_cols(tri_c, a_col)
    cs_r = _cumsum_rows(a_row, tri_r)
    last = 0 if reverse else CHUNK - 1
    tot_r = cs_r[:, last:last + 1]
    tot_c = cs_c[last:last + 1, :]

    lane_lo = col < SSD_STATE
    row_lo = row < SSD_STATE
    blockdiag = jnp.logical_not(jnp.logical_xor(lane_lo, row_lo))
    cm_sw = pltpu.roll(cm, SSD_STATE, 1)
    c_dup = (jnp.where(lane_lo, cm, cm_sw), jnp.where(lane_lo, cm_sw, cm))
    g_mat = (_dot_nt(jnp.where(lane_lo, cm, 0.0).astype(BF16), bm.astype(BF16)),
             _dot_nt(jnp.where(lane_lo, 0.0, cm).astype(BF16), bm.astype(BF16)))
    bT = bm.T

    heads_per_group = SSD_HEADS // SSD_GROUPS

    def pair_chain(j):
        g = (2 * j) // heads_per_group
        h0, h1 = 2 * j, 2 * j + 1
        xp = xs_ref[bi, :, j * CHUNK:(j + 1) * CHUNK]
        cs_b = [jnp.broadcast_to(cs_c[:, h:h + 1], (CHUNK, CHUNK)) for h in (h0, h1)]
        yield
        parts = []
        for h, cs_h in zip((h0, h1), cs_b):
            diff_ = cs_h - cs_r[h:h + 1, :]
            decay = jnp.exp(jnp.where(keep, diff_, NEG)) * dt_r[h:h + 1, :]
            parts.append((g_mat[g] * decay).astype(BF16))
        parts.append((c_dup[g] * jnp.exp(jnp.where(lane_lo, cs_b[0], cs_b[1]))).astype(BF16))
        yield
        lhs = jnp.concatenate(parts, axis=1)
        s_old = st_ref[j]
        rhs = jnp.concatenate([jnp.where(lane_lo, xp, 0.0).astype(BF16),
                               jnp.where(lane_lo, 0.0, xp).astype(BF16),
                               s_old.astype(BF16)], axis=0)
        y_pair = _dot(lhs, rhs)
        btg = bT[g * SSD_STATE:(g + 1) * SSD_STATE, :]
        bd = jnp.concatenate(
            [btg * (jnp.exp(tot_r[h:h + 1, :] - cs_r[h:h + 1, :]) * dt_r[h:h + 1, :]) for h in (h0, h1)],
            axis=0)
        s_upd = _dot(bd.astype(BF16), xp.astype(BF16))
        yield
        carry = jnp.where(row_lo, jnp.exp(tot_c[:, h0:h0 + 1]), jnp.exp(tot_c[:, h1:h1 + 1]))
        st_ref[j] = s_old * carry + jnp.where(blockdiag, s_upd, 0.0)
        y_ref[bi, :, j * CHUNK:(j + 1) * CHUNK] = y_pair.astype(y_ref.dtype)

    return [pair_chain(j) for j in range(SSD_HEADS // 2)]


def _interleave(chains, period=1):
    pending = [(i % period, c) for i, c in enumerate(chains)]
    rnd = 0
    while pending:
        alive = []
        for delay, c in pending:
            if rnd >= delay:
                try:
                    next(c)
                except StopIteration:
                    continue
            alive.append((delay, c))
        pending = alive
        rnd += 1


def _ssd_kernel(xs_f, bc_f, dt_f, dtT_f, xs_r, bc_r, dt_r, dtT_r, bias_r, bias_c, a_r, a_c,
                yf_ref, yr_ref, st_ref, *, n_chunks):
    step = pl.program_id(1)
    n_pairs = SSD_HEADS // 2

    @pl.when(step == 0)
    def _():
        st_ref[...] = jnp.zeros(st_ref.shape, F32)

    scans = []
    for bi in range(xs_f.shape[0]):
        lo = 2 * bi * n_pairs
        scans.append(_ssd_direction(xs_f, bc_f, dt_f, dtT_f, bias_r, bias_c, a_r, a_c, yf_ref,
                                    st_ref.at[lo:lo + n_pairs], bi=bi, reverse=False, is_meta=step == 0))
        scans.append(_ssd_direction(xs_r, bc_r, dt_r, dtT_r, bias_r, bias_c, a_r, a_c, yr_ref,
                                    st_ref.at[lo + n_pairs:lo + 2 * n_pairs], bi=bi, reverse=True,
                                    is_meta=step == n_chunks - 1))
    _interleave([c for group in zip(*scans) for c in group], period=SSD_SKEW)


def _chunk_order(n_chunks, reverse):
    if reverse:
        return lambda c: (2 * n_chunks - 2 - c) % n_chunks
    return lambda c: (c + n_chunks - 1) % n_chunks


def _scan_specs(arrays, n_chunks, reverse, nb):
    order = _chunk_order(n_chunks, reverse)
    specs = []
    for a, transposed in arrays:
        if transposed:
            specs.append(pl.BlockSpec((nb, a.shape[1], CHUNK), lambda i, s: (i, 0, order(s))))
        else:
            specs.append(pl.BlockSpec((nb, CHUNK, a.shape[2]), lambda i, s: (i, order(s), 0)))
    return specs


def _scan_batch(b):
    return 4 if b % 4 == 0 else (2 if b % 2 == 0 else 1)


def _ssd_scan(xs, bc, dt, dtT, consts):
    b, tp, _ = xs.shape
    n_chunks = tp // CHUNK
    nb = _scan_batch(b)
    arrays = [(xs, False), (bc, False), (dt, False), (dtT, True)]
    in_specs = (_scan_specs(arrays, n_chunks, False, nb) + _scan_specs(arrays, n_chunks, True, nb)
                + [_full(c.shape) for c in consts])
    out_specs = (_scan_specs([(xs, False)], n_chunks, False, nb)
                 + _scan_specs([(xs, False)], n_chunks, True, nb))
    out_shape = [jax.ShapeDtypeStruct((b, tp, SSD_INNER), BF16)] * 2
    return pl.pallas_call(
        functools.partial(_ssd_kernel, n_chunks=n_chunks),
        grid=(b // nb, n_chunks), in_specs=in_specs, out_specs=out_specs, out_shape=out_shape,
        scratch_shapes=[pltpu.VMEM((nb * SSD_HEADS, CHUNK, CHUNK), F32)],
        compiler_params=_params("parallel", "arbitrary"),
        name="ssd_scan")(xs, bc, dt, dtT, xs, bc, dt, dtT, *consts)


def _mlstm_direction(q_ref, k_ref, v_ref, gt_ref, gtT_ref, gb_r, gb_c, y_ref, st_ref, m_ref, *,
                     bi, reverse, is_meta):
    d = 1 if reverse else 0
    nh = MLSTM_HEADS
    keep, tri_c, tri_r, row, col = _tri_masks(reverse)
    gt = gt_ref[bi] + gb_r[...]
    gtT = gtT_ref[bi] + gb_c[...]
    i_lo, f_lo = d * nh, 2 * nh + d * nh
    pad_c = jnp.logical_and(is_meta, row[:, :nh] < N_PAD)
    pad_r = jnp.logical_and(is_meta, col[:nh, :] < N_PAD)
    ig_r = jnp.where(pad_r, NEG, gtT[i_lo:i_lo + nh, :])
    fg_c = jnp.where(pad_c, 0.0, _log_sigmoid(gt[:, f_lo:f_lo + nh]))
    fg_r = jnp.where(pad_r, 0.0, _log_sigmoid(gtT[f_lo:f_lo + nh, :]))
    b_c = _cumsum_cols(tri_c, fg_c)
    b_r = _cumsum_rows(fg_r, tri_r)
    last = 0 if reverse else CHUNK - 1

    lane_lo = col < MLSTM_QK
    row_lo = row < MLSTM_QK
    ones = jnp.ones((CHUNK, MLSTM_V), BF16)
    scale = MLSTM_QK ** -0.5

    def head_chain(h):
        pair = h // 2
        lo = (h % 2) == 0
        qp = q_ref[bi, :, pair * CHUNK:(pair + 1) * CHUNK]
        kp = k_ref[bi, :, pair * CHUNK:(pair + 1) * CHUNK]
        qm = (jnp.where(lane_lo if lo else jnp.logical_not(lane_lo), qp, 0.0) * scale).astype(BF16)
        kT = jnp.where(row_lo if lo else jnp.logical_not(row_lo), kp.astype(F32).T, 0.0)
        v_aug = jnp.concatenate([v_ref[bi, :, h * MLSTM_V:(h + 1) * MLSTM_V].astype(BF16), ones], axis=1)
        m_st = m_ref[h][0:1, 0:1]
        bc = b_c[:, h:h + 1]
        br = b_r[h:h + 1, :]
        ir = ig_r[h:h + 1, :]
        s_raw = _dot_nt(qm, kp.astype(BF16))
        c_st = st_ref[h]
        inter_mm = _dot(qm, c_st.astype(BF16))
        yield
        dmat = jnp.where(keep, ir - br, -jnp.inf)
        m_rel = jnp.maximum(jnp.max(dmat, axis=1, keepdims=True), m_st)
        tot = br[:, last:last + 1]
        d_last = tot - br + ir
        m_new = jnp.maximum(tot + m_st, jnp.max(d_last, axis=1, keepdims=True))
        yield
        m_rel_b = jnp.broadcast_to(m_rel, (CHUNK, CHUNK))
        w_intra = jnp.exp(dmat - m_rel_b)
        w_inter = jnp.exp(m_st - m_rel_b)
        floor = jnp.exp(-(bc + m_rel_b))
        w_s = jnp.exp(d_last - m_new)
        w_prev = jnp.exp(tot + m_st - m_new)
        yield
        s = s_raw * w_intra
        intra_mm = _dot(s.astype(BF16), v_aug)
        upd = _dot((kT * w_s).astype(BF16), v_aug)
        yield
        num = intra_mm[:, :MLSTM_V] + w_inter * inter_mm[:, :MLSTM_V]
        den = intra_mm[:, MLSTM_V:] + w_inter * inter_mm[:, MLSTM_V:]
        den = jnp.maximum(jnp.abs(den), floor)
        y_ref[bi, :, h * MLSTM_V:(h + 1) * MLSTM_V] = (num / den).astype(y_ref.dtype)
        st_ref[h] = w_prev * c_st + upd
        m_ref[h] = jnp.broadcast_to(m_new, m_ref.shape[1:])

    return [head_chain(h) for h in range(nh)]


def _mlstm_kernel(q_f, k_f, v_f, gt_f, gtT_f, q_r, k_r, v_r, gt_r, gtT_r, gb_r, gb_c,
                  yf_ref, yr_ref, st_ref, m_ref, *, n_chunks):
    step = pl.program_id(1)
    nh = MLSTM_HEADS

    @pl.when(step == 0)
    def _():
        st_ref[...] = jnp.zeros(st_ref.shape, F32)
        m_ref[...] = jnp.full(m_ref.shape, NEG, F32)

    scans = []
    for bi in range(q_f.shape[0]):
        lo = 2 * bi * nh
        scans.append(_mlstm_direction(q_f, k_f, v_f, gt_f, gtT_f, gb_r, gb_c, yf_ref, st_ref.at[lo:lo + nh],
                                      m_ref.at[lo:lo + nh], bi=bi, reverse=False, is_meta=step == 0))
        scans.append(_mlstm_direction(q_r, k_r, v_r, gt_r, gtT_r, gb_r, gb_c, yr_ref,
                                      st_ref.at[lo + nh:lo + 2 * nh], m_ref.at[lo + nh:lo + 2 * nh],
                                      bi=bi, reverse=True, is_meta=step == n_chunks - 1))
    _interleave([c for group in zip(*scans) for c in group], period=MLSTM_SKEW)


def _mlstm_scan(q, k, v, gt, gtT, consts):
    b, tp, _ = q.shape
    n_chunks = tp // CHUNK
    nb = _scan_batch(b)
    arrays = [(q, False), (k, False), (v, False), (gt, False), (gtT, True)]
    in_specs = (_scan_specs(arrays, n_chunks, False, nb) + _scan_specs(arrays, n_chunks, True, nb)
                + [_full(c.shape) for c in consts])
    out_specs = (_scan_specs([(v, False)], n_chunks, False, nb)
                 + _scan_specs([(v, False)], n_chunks, True, nb))
    out_shape = [jax.ShapeDtypeStruct(v.shape, BF16)] * 2
    return pl.pallas_call(
        functools.partial(_mlstm_kernel, n_chunks=n_chunks),
        grid=(b // nb, n_chunks), in_specs=in_specs, out_specs=out_specs, out_shape=out_shape,
        scratch_shapes=[pltpu.VMEM((nb * 2 * MLSTM_HEADS, CHUNK, 2 * MLSTM_V), F32),
                        pltpu.VMEM((nb * 2 * MLSTM_HEADS, 8, LANE), F32)],
        compiler_params=_params("parallel", "arbitrary"),
        name="mlstm_scan")(q, k, v, gt, gtT, q, k, v, gt, gtT, *consts)


def _out_kernel(ya_ref, yf_ref, yr_ref, xs_ref, z_ref, yc_ref, hf_ref, hr_ref, og_ref, h_ref,
                ag_ref, dsk_ref, ng_ref, mg_ref, w_ref, o_ref, *, tp, seq):
    tm = h_ref.shape[0]
    ya = _rms_rows(ya_ref[...].astype(F32), ag_ref[...])
    z = z_ref[...].astype(F32)
    yb = ((yf_ref[...].astype(F32) + yr_ref[...].astype(F32) + dsk_ref[...] * xs_ref[...].astype(F32))
          * (z * _sigmoid(z)))
    parts = [ya.astype(BF16), _rms_rows(yb, ng_ref[...]).astype(BF16), yc_ref[...].astype(BF16)]
    for hd in range(MLSTM_HEADS):
        lo, hi = hd * MLSTM_V, (hd + 1) * MLSTM_V
        hsum = hf_ref[:, lo:hi].astype(F32) + hr_ref[:, lo:hi].astype(F32)
        gate = _sigmoid(og_ref[:, lo:hi].astype(F32))
        parts.append((gate * _rms_rows(hsum, mg_ref[...])).astype(BF16))
    out = h_ref[...] + _dot(jnp.concatenate(parts, axis=1), w_ref[...])
    t = (pl.program_id(0) * tm + lax.broadcasted_iota(jnp.int32, (tm, 1), 0)) % tp
    is_pad = jnp.logical_and(t >= seq, t < seq + N_PAD)
    o_ref[...] = jnp.where(is_pad, 0.0, out)


def _out_proj(mixer_outs, h, consts, w, *, tp, seq, tm):
    rows, d = h.shape
    row_spec = lambda c: pl.BlockSpec((tm, c), lambda i: (i, 0))
    return pl.pallas_call(
        functools.partial(_out_kernel, tp=tp, seq=seq), grid=(rows // tm,),
        in_specs=([row_spec(a.shape[1]) for a in mixer_outs] + [row_spec(d)]
                  + [_full(c.shape) for c in consts] + [_full(w.shape)]),
        out_specs=row_spec(d), out_shape=jax.ShapeDtypeStruct((rows, d), F32),
        compiler_params=_params("parallel"), name="out_proj")(*mixer_outs, h, *consts, w)


def _ffn_kernel(h_ref, g_ref, wg_ref, wu_ref, wo_ref, o_ref, *, n_split):
    x = h_ref[0]
    hn = _rms_rows(x, g_ref[...]).astype(BF16)
    f = wg_ref.shape[1]
    tf = f // n_split
    acc = x
    for c in range(n_split):
        gate = _dot(hn, wg_ref[:, c * tf:(c + 1) * tf])
        up = _dot(hn, wu_ref[:, c * tf:(c + 1) * tf])
        act = (gate * _sigmoid(gate) * up).astype(BF16)
        acc = acc + _dot(act, wo_ref[c * tf:(c + 1) * tf, :])
    o_ref[0] = acc


def _ffn(h, g, wg, wu, wo, *, t_out, tm):
    b, _, d = h.shape
    f = wg.shape[1]
    n_split = 2 if (f // 2) % LANE == 0 else 1
    row_spec = pl.BlockSpec((1, tm, d), lambda i, j: (i, j, 0))
    resident = lambda a: pl.BlockSpec(a.shape, lambda i, j: (0, 0), pipeline_mode=pl.Buffered(1))
    return pl.pallas_call(
        functools.partial(_ffn_kernel, n_split=n_split), grid=(b, t_out // tm),
        in_specs=[row_spec, _full(g.shape), resident(wg), resident(wu), resident(wo)],
        out_specs=row_spec, out_shape=jax.ShapeDtypeStruct((b, t_out, d), F32),
        compiler_params=_params("parallel", "parallel"), name="ffn")(h, g, wg, wu, wo)


def _rope_tables_t(pos, rot_dim):
    inv = 1.0 / (ROPE_THETA ** (jnp.arange(0, rot_dim, 2, dtype=F32) / rot_dim))
    ang = pos[:, None] * inv[None, :]
    return jnp.cos(ang).T, jnp.sin(ang).T


def _col(v, n=None):
    v = v.astype(F32)
    if n is not None:
        v = jnp.pad(v, (0, n - v.shape[0]))
    return v[:, None]


def _row(v):
    return v.astype(F32)[None, :]


def kernel(x, meta_tokens, attn_norm_g, w_in, mla_q_norm_g, mla_kv_norm_g, mla_w_uq, mla_w_ukv, mla_q_head_g, mla_k_head_g, mla_out_g, ssd_conv_w, ssd_conv_b, ssd_dt_bias, ssd_a_log, ssd_d, ssd_norm_g, diff_q_head_g, diff_k_head_g, diff_lambda, diff_out_g, mlstm_i_bias, mlstm_f_bias, mlstm_out_g, w_out, ffn_norm_g, w_ffn_in, w_ffn_out):
    b, seq, d = x.shape
    depth = w_in.shape[0]
    tp = seq + CHUNK
    assert seq % CHUNK == 0
    tm_in = _pick_tile(tp, (640, 384, 128))
    tm_row = _pick_tile(tp, (1664, 640, 384, 128))
    tq = _pick_tile(tp, (1664, 640, 384, 128))
    tk = _pick_tile(seq, (256, 128))
    tm_flat = _pick_tile(b * tp, (512, 256, 128))
    tm_out = _pick_tile(seq, (512, 256, 128))

    meta = jnp.broadcast_to(meta_tokens[None].astype(x.dtype), (b, N_META, d))
    h = jnp.concatenate([x, jnp.zeros((b, N_PAD, d), x.dtype), meta], axis=1).reshape(b * tp, d)
    pos = jnp.concatenate([N_META + jnp.arange(seq, dtype=F32), jnp.zeros((N_PAD,), F32),
                           jnp.arange(N_META, dtype=F32)])
    cos_m, sin_m = _rope_tables_t(pos, MLA_ROPE)
    cos_d, sin_d = _rope_tables_t(pos, DIFF_ROPE)

    sizes = (MLA_Q_RANK, MLA_KV_RANK, MLA_ROPE, SSD_INNER, SSD_CONV_CH, 2 * SSD_HEADS,
             2 * DIFF_HEADS * DIFF_QK, 2 * DIFF_HEADS * DIFF_QK, DIFF_HEADS * DIFF_V,
             MLSTM_HEADS * MLSTM_QK, MLSTM_HEADS * MLSTM_QK, MLSTM_HEADS * MLSTM_V,
             MLSTM_HEADS * MLSTM_V, 2 * MLSTM_HEADS, 2 * MLSTM_HEADS)
    offs = [0]
    for s_ in sizes:
        offs.append(offs[-1] + s_)

    def cols(w, first, last):
        return w[:, offs[first]:offs[last + 1]]

    def pad_cols(w, n):
        return jnp.pad(w, ((0, 0), (0, n - w.shape[1])))

    for l in range(depth):
        lambda_init = 0.8 - 0.6 * math.exp(-0.3 * l)
        wl = w_in[l]
        h3 = h.reshape(b, tp, d)
        g_attn = _row(attn_norm_g[l])

        w_a = cols(wl, 0, 1).astype(BF16)
        w_kr = cols(wl, 2, 2).T.astype(BF16)
        w_uq = mla_w_uq[l].T
        w_ukv = mla_w_ukv[l].T.reshape(MLA_HEADS, MLA_NOPE + MLA_V, MLA_KV_RANK)
        w_uk = w_ukv[:, :MLA_NOPE].reshape(MLA_HEADS * MLA_NOPE, MLA_KV_RANK)
        w_uv = w_ukv[:, MLA_NOPE:].reshape(MLA_HEADS * MLA_V, MLA_KV_RANK)
        qT_a, k_a, vT_a = _in_proj_call(
            _mla_in_kernel, h3,
            [g_attn, w_a, _row(mla_q_norm_g[l]), _row(mla_kv_norm_g[l]), w_uq.astype(BF16),
             w_uk.astype(BF16), w_uv.astype(BF16), w_kr, _col(mla_q_head_g[l], HEAD_PAD),
             _col(mla_k_head_g[l], HEAD_PAD), (cos_m,), (sin_m,)],
            [((MLA_HEADS * HEAD_PAD, tp), BF16, "col"), ((MLA_HEADS, tp, HEAD_PAD), BF16, "head"),
             ((MLA_HEADS * MLA_V, tp), BF16, "col")], tm_row, "mla_in")
        y_a = _attention(qT_a, k_a, vT_a, [], diff=False, seq=seq, tq=tq, tk=tk, out_scale=1.0,
                         nsh=ATTN_HEADS_PER_STEP, name="mla_attn")

        w_c = cols(wl, 6, 8).T.astype(BF16)
        qT_c, k_c, vT_c = _in_proj_call(
            _diff_in_kernel, h3,
            [g_attn, w_c, _col(diff_q_head_g[l], HEAD_PAD), _col(diff_k_head_g[l], HEAD_PAD),
             (cos_d,), (sin_d,)],
            [((2 * DIFF_HEADS * HEAD_PAD, tp), BF16, "col"), ((2 * DIFF_HEADS, tp, HEAD_PAD), BF16, "head"),
             ((DIFF_HEADS * DIFF_V, tp), BF16, "col")], tm_row, "diff_in")
        lam = diff_lambda[l].astype(F32)
        lam_full = jnp.exp(jnp.sum(lam[0] * lam[1])) - jnp.exp(jnp.sum(lam[2] * lam[3])) + lambda_init
        y_c = _attention(qT_c, k_c, vT_c, [jnp.full((8, LANE), lam_full, F32), _col(diff_out_g[l])],
                         diff=True, seq=seq, tq=tq, tk=tk, out_scale=1.0 - lambda_init, nsh=ATTN_HEADS_PER_STEP,
                         name="diff_attn")

        n_b = SSD_INNER + SSD_CONV_CH + 2 * SSD_HEADS
        w_b = pad_cols(cols(wl, 3, 5), -(-n_b // LANE) * LANE).astype(BF16)
        w_dt = cols(wl, 5, 5).T.astype(BF16)
        z_b, xbc, dt, dtT = _in_proj_call(
            _ssd_in_kernel, h3, [g_attn, w_b, w_dt],
            [((tp, SSD_INNER), BF16, "row"), ((tp, SSD_CONV_CH), F32, "row"),
             ((tp, 2 * SSD_HEADS), F32, "row"), ((2 * SSD_HEADS, tp), F32, "col")], tm_row, "ssd_in")
        w8 = jnp.pad(ssd_conv_w[l].astype(F32), ((0, 8 - SSD_CONV), (0, 0)))
        xs_b, bc_b = _ssd_conv(xbc, w8, _row(ssd_conv_b[l]), tm_in)
        dt_bias = ssd_dt_bias[l].astype(F32).reshape(-1)
        a_neg = -jnp.exp(ssd_a_log[l].astype(F32)).reshape(-1)
        y_bf, y_br = _ssd_scan(xs_b, bc_b, dt, dtT, [_row(dt_bias), _col(dt_bias), _row(a_neg), _col(a_neg)])

        n_d = 2 * MLSTM_HEADS * MLSTM_QK + 2 * MLSTM_HEADS * MLSTM_V + 4 * MLSTM_HEADS
        w_d = pad_cols(cols(wl, 9, 14), -(-n_d // LANE) * LANE).astype(BF16)
        w_g = cols(wl, 13, 14).T.astype(BF16)
        q_d, k_d, v_d, o_d, gt, gtT = _in_proj_call(
            _mlstm_in_kernel, h3, [g_attn, w_d, w_g],
            [((tp, MLSTM_HEADS * MLSTM_QK), BF16, "row"), ((tp, MLSTM_HEADS * MLSTM_QK), BF16, "row"),
             ((tp, MLSTM_HEADS * MLSTM_V), BF16, "row"), ((tp, MLSTM_HEADS * MLSTM_V), BF16, "row"),
             ((tp, 4 * MLSTM_HEADS), F32, "row"), ((4 * MLSTM_HEADS, tp), F32, "col")], tm_row, "mlstm_in")
        gate_bias = jnp.concatenate([mlstm_i_bias[l].reshape(-1), mlstm_f_bias[l].reshape(-1)]).astype(F32)
        h_df, h_dr = _mlstm_scan(q_d, k_d, v_d, gt, gtT, [_row(gate_bias), _col(gate_bias)])

        flat = lambda a: a.reshape(b * tp, a.shape[2])
        mixer_outs = [flat(a) for a in (y_a, y_bf, y_br, xs_b, z_b, y_c, h_df, h_dr, o_d)]
        out_consts = [_row(mla_out_g[l]), _row(jnp.repeat(ssd_d[l], SSD_HEAD_DIM)), _row(ssd_norm_g[l]),
                      _row(mlstm_out_g[l])]
        h = _out_proj(mixer_outs, h, out_consts, w_out[l].astype(BF16), tp=tp, seq=seq, tm=tm_flat)
        f = w_ffn_out.shape[1]
        last = l == depth - 1
        h = _ffn(h.reshape(b, tp, d), _row(ffn_norm_g[l]), w_ffn_in[l][:, :f].astype(BF16),
                 w_ffn_in[l][:, f:].astype(BF16), w_ffn_out[l].astype(BF16),
                 t_out=seq if last else tp, tm=tm_out if last else tm_in)
        h = h.reshape(-1, d)

    return h.reshape(b, seq, d).astype(x.dtype)
```

```python
import functools
import math

import jax
import jax.numpy as jnp
from jax import lax
from jax.experimental import pallas as pl
from jax.experimental.pallas import tpu as pltpu

N_META = 16
ROPE_THETA = 500000.0
EPS = 1e-6
CHUNK = 128
N_PAD = CHUNK - N_META
NEG = -1e30
LOG2E = 1.4426950408889634

MLA_HEADS, MLA_NOPE, MLA_ROPE, MLA_V = 8, 64, 32, 64
MLA_QK = MLA_NOPE + MLA_ROPE
MLA_Q_RANK, MLA_KV_RANK = 384, 256
SSD_HEADS, SSD_HEAD_DIM, SSD_GROUPS, SSD_STATE, SSD_CONV = 8, 64, 2, 64, 5
SSD_INNER = SSD_HEADS * SSD_HEAD_DIM
SSD_CONV_CH = SSD_INNER + 2 * SSD_GROUPS * SSD_STATE
DIFF_HEADS, DIFF_QK = 4, 64
DIFF_V = 2 * DIFF_QK
DIFF_ROPE = DIFF_QK // 4
MLSTM_HEADS, MLSTM_QK, MLSTM_V = 4, 64, 128
HEAD_PAD = 128
LANE = 128
Q_STRIP = 256
ATTN_HEADS_PER_STEP = 4
SSD_SKEW, MLSTM_SKEW = 4, 1
VMEM_LIMIT = 52 * 1024 * 1024

F32 = jnp.float32
BF16 = jnp.bfloat16
EXP_DTYPE = jnp.bfloat16


def _dot(a, b):
    return jnp.dot(a, b, preferred_element_type=F32)


def _dot_nt(a, b):
    return lax.dot_general(a, b, (((1,), (1,)), ((), ())), preferred_element_type=F32)


def _rms_rows(x, g):
    ms = jnp.mean(x * x, axis=-1, keepdims=True)
    return x * lax.rsqrt(ms + EPS) * g


def _split3(a):
    hi = a.astype(BF16)
    r1 = a - hi.astype(F32)
    mid = r1.astype(BF16)
    lo = (r1 - mid.astype(F32)).astype(BF16)
    return hi, mid, lo


def _cumsum_cols(tri, a):
    hi, mid, lo = _split3(a)
    return _dot(tri, hi) + _dot(tri, mid) + _dot(tri, lo)


def _cumsum_rows(a, tri):
    hi, mid, lo = _split3(a)
    return _dot(hi, tri) + _dot(mid, tri) + _dot(lo, tri)


def _softplus(x):
    return jnp.maximum(x, 0.0) + jnp.log(1.0 + jnp.exp(-jnp.abs(x)))


def _log_sigmoid(x):
    return jnp.minimum(x, 0.0) - jnp.log(1.0 + jnp.exp(-jnp.abs(x)))


def _sigmoid(x):
    return 1.0 / (1.0 + jnp.exp(-x))


def _pick_tile(n, candidates):
    for c in candidates:
        if n % c == 0:
            return c
    raise ValueError(f"no tile in {candidates} divides {n}")


def _params(*sem):
    return pltpu.CompilerParams(dimension_semantics=sem, vmem_limit_bytes=VMEM_LIMIT)


def _norm_rope_t(blk, g_col, cos, sin, n_real):
    r = cos.shape[0]
    ms = jnp.sum(blk * blk, axis=0, keepdims=True) * (1.0 / n_real)
    y = blk * lax.rsqrt(ms + EPS) * g_col
    x1, x2, rest = y[:r], y[r:2 * r], y[2 * r:]
    return jnp.concatenate([x1 * cos - x2 * sin, x2 * cos + x1 * sin, rest], axis=0)


def _mla_in_kernel(h_ref, g_ref, wa_ref, gq_ref, gkv_ref, wuq_ref, wuk_ref, wuv_ref, wkr_ref,
                   qhg_ref, khg_ref, cos_ref, sin_ref, qT_ref, k_ref, vT_ref):
    tm = h_ref.shape[1]
    q_scale = (MLA_QK ** -0.5) * LOG2E

    def tile_chain(r0, t):
        hn = _rms_rows(h_ref[0, r0:r0 + t, :], g_ref[...]).astype(BF16)
        yield
        acc = _dot(hn, wa_ref[...])
        krT = _dot_nt(wkr_ref[...], hn)
        yield
        cqn = _rms_rows(acc[:, :MLA_Q_RANK], gq_ref[...]).astype(BF16)
        ckvn = _rms_rows(acc[:, MLA_Q_RANK:], gkv_ref[...]).astype(BF16)
        yield
        qT = _dot_nt(wuq_ref[...], cqn)
        knT = _dot_nt(wuk_ref[...], ckvn)
        vT = _dot_nt(wuv_ref[...], ckvn)
        yield
        cos, sin = cos_ref[:, r0:r0 + t], sin_ref[:, r0:r0 + t]
        zpad = jnp.zeros((HEAD_PAD - MLA_QK, t), F32)
        for h in range(MLA_HEADS):
            qb = _norm_rope_t(qT[h * MLA_QK:(h + 1) * MLA_QK], qhg_ref[0:MLA_QK, :], cos, sin, MLA_QK)
            qT_ref[0, h * HEAD_PAD:h * HEAD_PAD + MLA_QK, r0:r0 + t] = (qb * q_scale).astype(BF16)
            qT_ref[0, h * HEAD_PAD + MLA_QK:(h + 1) * HEAD_PAD, r0:r0 + t] = zpad.astype(BF16)
            kb = jnp.concatenate([krT, knT[h * MLA_NOPE:(h + 1) * MLA_NOPE]], axis=0)
            kb = _norm_rope_t(kb, khg_ref[0:MLA_QK, :], cos, sin, MLA_QK)
            k_ref[0, h, r0:r0 + t, :] = jnp.concatenate([kb, zpad], axis=0).T.astype(BF16)
            if h % 2 == 1:
                yield
        vT_ref[0, :, r0:r0 + t] = vT.astype(BF16)

    _interleave([tile_chain(r0, t) for r0, t in _sub_tiles(tm)], period=2)


def _diff_in_kernel(h_ref, g_ref, wc_ref, qhg_ref, khg_ref, cos_ref, sin_ref, qT_ref, k_ref, vT_ref):
    tm = h_ref.shape[1]
    nq = 2 * DIFF_HEADS * DIFF_QK
    q_scale = (DIFF_QK ** -0.5) * LOG2E

    def tile_chain(r0, t):
        hn = _rms_rows(h_ref[0, r0:r0 + t, :], g_ref[...]).astype(BF16)
        yield
        pT = _dot_nt(wc_ref[...], hn)
        yield
        cos, sin = cos_ref[:, r0:r0 + t], sin_ref[:, r0:r0 + t]
        zpad = jnp.zeros((HEAD_PAD - DIFF_QK, t), F32)
        for h in range(2 * DIFF_HEADS):
            qb = _norm_rope_t(pT[h * DIFF_QK:(h + 1) * DIFF_QK], qhg_ref[0:DIFF_QK, :], cos, sin, DIFF_QK)
            qT_ref[0, h * HEAD_PAD:h * HEAD_PAD + DIFF_QK, r0:r0 + t] = (qb * q_scale).astype(BF16)
            qT_ref[0, h * HEAD_PAD + DIFF_QK:(h + 1) * HEAD_PAD, r0:r0 + t] = zpad.astype(BF16)
            kb = _norm_rope_t(pT[nq + h * DIFF_QK:nq + (h + 1) * DIFF_QK], khg_ref[0:DIFF_QK, :], cos, sin,
                              DIFF_QK)
            k_ref[0, h, r0:r0 + t, :] = jnp.concatenate([kb, zpad], axis=0).T.astype(BF16)
            if h % 2 == 1:
                yield
        vT_ref[0, :, r0:r0 + t] = pT[2 * nq:].astype(BF16)

    _interleave([tile_chain(r0, t) for r0, t in _sub_tiles(tm)])


def _sub_tiles(tm):
    if tm < 1024:
        return [(0, tm)]
    first = (tm // 512) * 256 + (256 if tm % 512 >= 256 else 0)
    return [(0, first), (first, tm - first)] if first < tm else [(0, tm)]


def _ssd_in_kernel(h_ref, g_ref, wb_ref, wdt_ref, z_ref, xbc_ref, dt_ref, dtT_ref):
    def tile_chain(r0, t):
        hn = _rms_rows(h_ref[0, r0:r0 + t, :], g_ref[...]).astype(BF16)
        yield
        acc = _dot(hn, wb_ref[...])
        dtT_ref[0, :, r0:r0 + t] = _dot_nt(wdt_ref[...], hn)
        yield
        z_ref[0, r0:r0 + t, :] = acc[:, :SSD_INNER].astype(z_ref.dtype)
        xbc_ref[0, r0:r0 + t, :] = acc[:, SSD_INNER:SSD_INNER + SSD_CONV_CH]
        dt_ref[0, r0:r0 + t, :] = acc[:, SSD_INNER + SSD_CONV_CH:SSD_INNER + SSD_CONV_CH + 2 * SSD_HEADS]

    _interleave([tile_chain(r0, t) for r0, t in _sub_tiles(h_ref.shape[1])])


def _mlstm_in_kernel(h_ref, g_ref, wd_ref, wg_ref, q_ref, k_ref, v_ref, o_ref, gt_ref, gtT_ref):
    nqk = MLSTM_HEADS * MLSTM_QK
    nv = MLSTM_HEADS * MLSTM_V

    def tile_chain(r0, t):
        hn = _rms_rows(h_ref[0, r0:r0 + t, :], g_ref[...]).astype(BF16)
        yield
        acc = _dot(hn, wd_ref[...])
        gtT_ref[0, :, r0:r0 + t] = _dot_nt(wg_ref[...], hn)
        yield
        q_ref[0, r0:r0 + t, :] = acc[:, :nqk].astype(q_ref.dtype)
        k_ref[0, r0:r0 + t, :] = acc[:, nqk:2 * nqk].astype(k_ref.dtype)
        v_ref[0, r0:r0 + t, :] = acc[:, 2 * nqk:2 * nqk + nv].astype(v_ref.dtype)
        o_ref[0, r0:r0 + t, :] = acc[:, 2 * nqk + nv:2 * nqk + 2 * nv].astype(o_ref.dtype)
        gt_ref[0, r0:r0 + t, :] = acc[:, 2 * nqk + 2 * nv:2 * nqk + 2 * nv + 4 * MLSTM_HEADS]

    _interleave([tile_chain(r0, t) for r0, t in _sub_tiles(h_ref.shape[1])])


def _full(shape):
    nd = len(shape)
    return pl.BlockSpec(shape, lambda *_: (0,) * nd)


def _in_proj_call(body, h, consts, outs, tm, name):
    b, tp, d = h.shape
    in_specs = [pl.BlockSpec((1, tm, d), lambda i, j: (i, j, 0))]
    for c in consts:
        if isinstance(c, tuple):
            in_specs.append(pl.BlockSpec((c[0].shape[0], tm), lambda i, j: (0, j)))
        else:
            in_specs.append(_full(c.shape))
    out_shapes, out_specs = [], []
    for shape, dtype, kind in outs:
        out_shapes.append(jax.ShapeDtypeStruct((b,) + shape, dtype))
        if kind == "row":
            out_specs.append(pl.BlockSpec((1, tm, shape[1]), lambda i, j: (i, j, 0)))
        elif kind == "col":
            out_specs.append(pl.BlockSpec((1, shape[0], tm), lambda i, j: (i, 0, j)))
        else:
            out_specs.append(pl.BlockSpec((1, shape[0], tm, shape[2]), lambda i, j: (i, 0, j, 0)))
    args = [h] + [c[0] if isinstance(c, tuple) else c for c in consts]
    return pl.pallas_call(
        body, grid=(b, tp // tm), in_specs=in_specs, out_specs=out_specs, out_shape=out_shapes,
        compiler_params=_params("parallel", "parallel"), name=name)(*args)


def _attn_kernel(*refs, diff, seq, tk, out_scale, nsh):
    if diff:
        qT_ref, k_ref, vT_ref, lam_ref, og_ref, o_ref, s_scr, st_scr, m_scr, acc_scr = refs
    else:
        qT_ref, k_ref, vT_ref, o_ref, s_scr, st_scr, m_scr, acc_scr = refs
    dv = DIFF_V if diff else MLA_V
    tq = qT_ref.shape[2]
    n_chunks = seq // tk
    assert n_chunks % 2 == 0
    m_scr[...] = jnp.full(m_scr.shape, NEG, F32)
    acc_scr[...] = jnp.zeros(acc_scr.shape, F32)

    def stage(nxt, cur):
        k_n, v_aug = [], []
        for hh in range(nsh):
            if nxt is not None:
                k_n.append(k_ref[0, hh, pl.ds(nxt[1], nxt[2]), :])
            if cur is not None:
                v_lo = (hh // 2) * dv if diff else hh * dv
                ones = jnp.ones((16, cur[2]), BF16)
                v_aug.append(jnp.concatenate([vT_ref[0, v_lo:v_lo + dv, pl.ds(cur[1], cur[2])], ones],
                                             axis=0))
        pending = None
        for j0 in range(0, tq, Q_STRIP):
            w = min(Q_STRIP, tq - j0)
            for hh in range(nsh):
                if cur is not None:
                    src = st_scr.at[hh] if cur[0] is None else s_scr.at[hh, cur[0]]
                    s = src[:, j0:j0 + w]
                    m_old = m_scr[hh, 0:1, j0:j0 + w]
                    m_new = jnp.maximum(m_old, jnp.max(s, axis=0, keepdims=True))
                    alpha = jnp.exp2(m_old - m_new)
                    p = jnp.exp2((s - m_new).astype(EXP_DTYPE)).astype(BF16)
                    m_scr[hh, 0:1, j0:j0 + w] = m_new
                if nxt is not None:
                    dst = st_scr.at[hh] if nxt[0] is None else s_scr.at[hh, nxt[0]]
                    s_n = _dot(k_n[hh], qT_ref[0, hh * HEAD_PAD:(hh + 1) * HEAD_PAD, j0:j0 + w])
                    if nxt[3]:
                        row = lax.broadcasted_iota(jnp.int32, s_n.shape, 0)
                        s_n = jnp.where(row >= N_PAD, s_n, NEG)
                    dst[:, j0:j0 + w] = s_n
                if cur is not None:
                    if pending is not None:
                        ph, pj, pw, pa, pp = pending
                        acc_scr[ph, :, pj:pj + pw] = pa * acc_scr[ph, :, pj:pj + pw] + _dot(v_aug[ph], pp)
                    pending = (hh, j0, w, alpha, p)
        if pending is not None:
            ph, pj, pw, pa, pp = pending
            acc_scr[ph, :, pj:pj + pw] = pa * acc_scr[ph, :, pj:pj + pw] + _dot(v_aug[ph], pp)

    stage((0, 0, tk, False), None)

    def body(i, carry):
        off = pl.multiple_of(2 * i * tk, 2 * tk)
        stage((1, off + tk, tk, False), (0, off, tk))
        stage((0, off + 2 * tk, tk, False), (1, off + tk, tk))
        return carry

    lax.fori_loop(0, n_chunks // 2 - 1, body, 0)
    off = (n_chunks - 2) * tk
    stage((1, off + tk, tk, False), (0, off, tk))
    stage((None, seq, CHUNK, True), (1, off + tk, tk))
    stage(None, (None, seq, CHUNK))
    outs = []
    for hh in range(nsh):
        acc = acc_scr[hh]
        outs.append(acc[:dv] / acc[dv:dv + 1])
    if diff:
        pairs = []
        for hp in range(nsh // 2):
            o = outs[2 * hp] - lam_ref[0:1, 0:1] * outs[2 * hp + 1]
            ms = jnp.mean(o * o, axis=0, keepdims=True)
            pairs.append(o * lax.rsqrt(ms + EPS) * (og_ref[...] * out_scale))
        o = jnp.concatenate(pairs, axis=0) if len(pairs) > 1 else pairs[0]
    else:
        o = jnp.concatenate(outs, axis=0)
    o_ref[0] = o.T.astype(o_ref.dtype)


def _attention(qT, k, vT, extra, *, diff, seq, tq, tk, out_scale, nsh, name):
    b, _, tp = qT.shape
    groups = qT.shape[1] // (nsh * HEAD_PAD)
    dv = DIFF_V if diff else MLA_V
    v_rows = (nsh // 2) * dv if diff else nsh * dv
    o_cols = (nsh // 2) * LANE
    in_specs = [
        pl.BlockSpec((1, nsh * HEAD_PAD, tq), lambda i, g, j: (i, g, j)),
        pl.BlockSpec((1, nsh, tp, HEAD_PAD), lambda i, g, j: (i, g, 0, 0)),
        pl.BlockSpec((1, v_rows, tp), lambda i, g, j: (i, g, 0)),
    ] + [_full(e.shape) for e in extra]
    return pl.pallas_call(
        functools.partial(_attn_kernel, diff=diff, seq=seq, tk=tk, out_scale=out_scale, nsh=nsh),
        grid=(b, groups, tp // tq),
        in_specs=in_specs,
        out_specs=pl.BlockSpec((1, tq, o_cols), lambda i, g, j: (i, j, g)),
        out_shape=jax.ShapeDtypeStruct((b, tp, groups * o_cols), BF16),
        scratch_shapes=[pltpu.VMEM((nsh, 2, tk, tq), F32), pltpu.VMEM((nsh, CHUNK, tq), F32),
                        pltpu.VMEM((nsh, 8, tq), F32), pltpu.VMEM((nsh, dv + 16, tq), F32)],
        compiler_params=_params("parallel", "parallel", "arbitrary"), name=name)(qT, k, vT, *extra)


def _conv_kernel(x_ref, prev_ref, next_ref, w_ref, b_ref, xs_ref, bc_ref, scr):
    tc = x_ref.shape[1]
    scr[0:8, :] = prev_ref[0]
    scr[8:8 + tc, :] = x_ref[0]
    scr[8 + tc:16 + tc, :] = next_ref[0]
    acc = jnp.broadcast_to(b_ref[...], (tc, b_ref.shape[1]))
    for j in range(SSD_CONV):
        acc = acc + w_ref[j:j + 1, :] * scr[8 - SSD_CONV // 2 + j:8 - SSD_CONV // 2 + j + tc, :]
    act = acc * _sigmoid(acc)
    xs_ref[0] = act[:, :SSD_INNER]
    bc_ref[0] = act[:, SSD_INNER:]


def _ssd_conv(xbc, w8, bias, tc):
    b, tp, c = xbc.shape
    nb8 = tp // 8
    r8 = tc // 8
    row_spec = lambda n: pl.BlockSpec((1, tc, n), lambda i, j: (i, j, 0))
    return pl.pallas_call(
        _conv_kernel, grid=(b, tp // tc),
        in_specs=[row_spec(c),
                  pl.BlockSpec((1, 8, c), lambda i, j: (i, (j * r8 + nb8 - 1) % nb8, 0)),
                  pl.BlockSpec((1, 8, c), lambda i, j: (i, ((j + 1) * r8) % nb8, 0)),
                  _full(w8.shape), _full(bias.shape)],
        out_specs=[row_spec(SSD_INNER), row_spec(c - SSD_INNER)],
        out_shape=[jax.ShapeDtypeStruct((b, tp, SSD_INNER), F32),
                   jax.ShapeDtypeStruct((b, tp, c - SSD_INNER), F32)],
        scratch_shapes=[pltpu.VMEM((tc + 16, c), F32)],
        compiler_params=_params("parallel", "parallel"), name="ssd_conv")(xbc, xbc, xbc, w8, bias)


def _tri_masks(reverse):
    row = lax.broadcasted_iota(jnp.int32, (CHUNK, CHUNK), 0)
    col = lax.broadcasted_iota(jnp.int32, (CHUNK, CHUNK), 1)
    keep = (col >= row) if reverse else (col <= row)
    tri_c = keep.astype(BF16)
    tri_r = ((row >= col) if reverse else (row <= col)).astype(BF16)
    return keep, tri_c, tri_r, row, col


def _ssd_direction(xs_ref, bc_ref, dt_ref, dtT_ref, bias_r, bias_c, a_r, a_c, y_ref, st_ref, *,
                   bi, reverse, is_meta):
    d = 1 if reverse else 0
    h8 = SSD_HEADS
    keep, tri_c, tri_r, row, col = _tri_masks(reverse)
    bm = bc_ref[bi, :, :CHUNK]
    cm = bc_ref[bi, :, CHUNK:]

    dt_c = _softplus(dt_ref[bi][:, d * h8:(d + 1) * h8] + bias_r[:, d * h8:(d + 1) * h8])
    dt_r = _softplus(dtT_ref[bi][d * h8:(d + 1) * h8, :] + bias_c[d * h8:(d + 1) * h8, :])
    pad_c = jnp.logical_and(is_meta, row[:, :h8] < N_PAD)
    pad_r = jnp.logical_and(is_meta, col[:h8, :] < N_PAD)
    dt_c = jnp.where(pad_c, 0.0, dt_c)
    dt_r = jnp.where(pad_r, 0.0, dt_r)
    a_col = dt_c * a_r[:, d * h8:(d + 1) * h8]
    a_row = dt_r * a_c[d * h8:(d + 1) * h8, :]
    cs_c = _cumsum_cols(tri_c, a_col)
    cs_r = _cumsum_rows(a_row, tri_r)
    last = 0 if reverse else CHUNK - 1
    tot_r = cs_r[:, last:last + 1]
    tot_c = cs_c[last:last + 1, :]

    lane_lo = col < SSD_STATE
    row_lo = row < SSD_STATE
    blockdiag = jnp.logical_not(jnp.logical_xor(lane_lo, row_lo))
    cm_sw = pltpu.roll(cm, SSD_STATE, 1)
    c_dup = (jnp.where(lane_lo, cm, cm_sw), jnp.where(lane_lo, cm_sw, cm))
    g_mat = (_dot_nt(jnp.where(lane_lo, cm, 0.0).astype(BF16), bm.astype(BF16)),
             _dot_nt(jnp.where(lane_lo, 0.0, cm).astype(BF16), bm.astype(BF16)))
    bT = bm.T

    heads_per_group = SSD_HEADS // SSD_GROUPS

    def pair_chain(j):
        g = (2 * j) // heads_per_group
        h0, h1 = 2 * j, 2 * j + 1
        xp = xs_ref[bi, :, j * CHUNK:(j + 1) * CHUNK]
        cs_b = [jnp.broadcast_to(cs_c[:, h:h + 1], (CHUNK, CHUNK)) for h in (h0, h1)]
        yield
        parts = []
        for h, cs_h in zip((h0, h1), cs_b):
            diff_ = cs_h - cs_r[h:h + 1, :]
            decay = jnp.exp(jnp.where(keep, diff_, NEG)) * dt_r[h:h + 1, :]
            parts.append((g_mat[g] * decay).astype(BF16))
        parts.append((c_dup[g] * jnp.exp(jnp.where(lane_lo, cs_b[0], cs_b[1]))).astype(BF16))
        yield
        lhs = jnp.concatenate(parts, axis=1)
        s_old = st_ref[j]
        rhs = jnp.concatenate([jnp.where(lane_lo, xp, 0.0).astype(BF16),
                               jnp.where(lane_lo, 0.0, xp).astype(BF16),
                               s_old.astype(BF16)], axis=0)
        y_pair = _dot(lhs, rhs)
        btg = bT[g * SSD_STATE:(g + 1) * SSD_STATE, :]
        bd = jnp.concatenate(
            [btg * (jnp.exp(tot_r[h:h + 1, :] - cs_r[h:h + 1, :]) * dt_r[h:h + 1, :]) for h in (h0, h1)],
            axis=0)
        s_upd = _dot(bd.astype(BF16), xp.astype(BF16))
        yield
        carry = jnp.where(row_lo, jnp.exp(tot_c[:, h0:h0 + 1]), jnp.exp(tot_c[:, h1:h1 + 1]))
        st_ref[j] = s_old * carry + jnp.where(blockdiag, s_upd, 0.0)
        y_ref[bi, :, j * CHUNK:(j + 1) * CHUNK] = y_pair.astype(y_ref.dtype)

    return [pair_chain(j) for j in range(SSD_HEADS // 2)]


def _interleave(chains, period=1):
    pending = [(i % period, c) for i, c in enumerate(chains)]
    rnd = 0
    while pending:
        alive = []
        for delay, c in pending:
            if rnd >= delay:
                try:
                    next(c)
                except StopIteration:
                    continue
            alive.append((delay, c))
        pending = alive
        rnd += 1


def _ssd_kernel(xs_f, bc_f, dt_f, dtT_f, xs_r, bc_r, dt_r, dtT_r, bias_r, bias_c, a_r, a_c,
                yf_ref, yr_ref, st_ref, *, n_chunks):
    step = pl.program_id(1)
    n_pairs = SSD_HEADS // 2

    @pl.when(step == 0)
    def _():
        st_ref[...] = jnp.zeros(st_ref.shape, F32)

    scans = []
    for bi in range(xs_f.shape[0]):
        lo = 2 * bi * n_pairs
        scans.append(_ssd_direction(xs_f, bc_f, dt_f, dtT_f, bias_r, bias_c, a_r, a_c, yf_ref,
                                    st_ref.at[lo:lo + n_pairs], bi=bi, reverse=False, is_meta=step == 0))
        scans.append(_ssd_direction(xs_r, bc_r, dt_r, dtT_r, bias_r, bias_c, a_r, a_c, yr_ref,
                                    st_ref.at[lo + n_pairs:lo + 2 * n_pairs], bi=bi, reverse=True,
                                    is_meta=step == n_chunks - 1))
    _interleave([c for group in zip(*scans) for c in group], period=SSD_SKEW)


def _chunk_order(n_chunks, reverse):
    if reverse:
        return lambda c: (2 * n_chunks - 2 - c) % n_chunks
    return lambda c: (c + n_chunks - 1) % n_chunks


def _scan_specs(arrays, n_chunks, reverse, nb):
    order = _chunk_order(n_chunks, reverse)
    specs = []
    for a, transposed in arrays:
        if transposed:
            specs.append(pl.BlockSpec((nb, a.shape[1], CHUNK), lambda i, s: (i, 0, order(s))))
        else:
            specs.append(pl.BlockSpec((nb, CHUNK, a.shape[2]), lambda i, s: (i, order(s), 0)))
    return specs


def _scan_batch(b):
    return 4 if b % 4 == 0 else (2 if b % 2 == 0 else 1)


def _ssd_scan(xs, bc, dt, dtT, consts):
    b, tp, _ = xs.shape
    n_chunks = tp // CHUNK
    nb = _scan_batch(b)
    arrays = [(xs, False), (bc, False), (dt, False), (dtT, True)]
    in_specs = (_scan_specs(arrays, n_chunks, False, nb) + _scan_specs(arrays, n_chunks, True, nb)
                + [_full(c.shape) for c in consts])
    out_specs = (_scan_specs([(xs, False)], n_chunks, False, nb)
                 + _scan_specs([(xs, False)], n_chunks, True, nb))
    out_shape = [jax.ShapeDtypeStruct((b, tp, SSD_INNER), BF16)] * 2
    return pl.pallas_call(
        functools.partial(_ssd_kernel, n_chunks=n_chunks),
        grid=(b // nb, n_chunks), in_specs=in_specs, out_specs=out_specs, out_shape=out_shape,
        scratch_shapes=[pltpu.VMEM((nb * SSD_HEADS, CHUNK, CHUNK), F32)],
        compiler_params=_params("parallel", "arbitrary"),
        name="ssd_scan")(xs, bc, dt, dtT, xs, bc, dt, dtT, *consts)


def _mlstm_direction(q_ref, k_ref, v_ref, gt_ref, gtT_ref, gb_r, gb_c, y_ref, st_ref, m_ref, *,
                     bi, reverse, is_meta):
    d = 1 if reverse else 0
    nh = MLSTM_HEADS
    keep, tri_c, tri_r, row, col = _tri_masks(reverse)
    gt = gt_ref[bi] + gb_r[...]
    gtT = gtT_ref[bi] + gb_c[...]
    i_lo, f_lo = d * nh, 2 * nh + d * nh
    pad_c = jnp.logical_and(is_meta, row[:, :nh] < N_PAD)
    pad_r = jnp.logical_and(is_meta, col[:nh, :] < N_PAD)
    ig_r = jnp.where(pad_r, NEG, gtT[i_lo:i_lo + nh, :])
    fg_c = jnp.where(pad_c, 0.0, _log_sigmoid(gt[:, f_lo:f_lo + nh]))
    fg_r = jnp.where(pad_r, 0.0, _log_sigmoid(gtT[f_lo:f_lo + nh, :]))
    b_c = _cumsum_cols(tri_c, fg_c)
    b_r = _cumsum_rows(fg_r, tri_r)
    last = 0 if reverse else CHUNK - 1

    lane_lo = col < MLSTM_QK
    row_lo = row < MLSTM_QK
    ones = jnp.ones((CHUNK, MLSTM_V), BF16)
    scale = MLSTM_QK ** -0.5

    def head_chain(h):
        pair = h // 2
        lo = (h % 2) == 0
        qp = q_ref[bi, :, pair * CHUNK:(pair + 1) * CHUNK]
        kp = k_ref[bi, :, pair * CHUNK:(pair + 1) * CHUNK]
        qm = (jnp.where(lane_lo if lo else jnp.logical_not(lane_lo), qp, 0.0) * scale).astype(BF16)
        kT = jnp.where(row_lo if lo else jnp.logical_not(row_lo), kp.astype(F32).T, 0.0)
        v_aug = jnp.concatenate([v_ref[bi, :, h * MLSTM_V:(h + 1) * MLSTM_V].astype(BF16), ones], axis=1)
        m_st = m_ref[h][0:1, 0:1]
        bc = b_c[:, h:h + 1]
        br = b_r[h:h + 1, :]
        ir = ig_r[h:h + 1, :]
        s_raw = _dot_nt(qm, kp.astype(BF16))
        c_st = st_ref[h]
        inter_mm = _dot(qm, c_st.astype(BF16))
        yield
        dmat = jnp.where(keep, ir - br, -jnp.inf)
        m_rel = jnp.maximum(jnp.max(dmat, axis=1, keepdims=True), m_st)
        tot = br[:, last:last + 1]
        d_last = tot - br + ir
        m_new = jnp.maximum(tot + m_st, jnp.max(d_last, axis=1, keepdims=True))
        yield
        m_rel_b = jnp.broadcast_to(m_rel, (CHUNK, CHUNK))
        w_intra = jnp.exp(dmat - m_rel_b)
        w_inter = jnp.exp(m_st - m_rel_b)
        floor = jnp.exp(-(bc + m_rel_b))
        w_s = jnp.exp(d_last - m_new)
        w_prev = jnp.exp(tot + m_st - m_new)
        yield
        s = s_raw * w_intra
        intra_mm = _dot(s.astype(BF16), v_aug)
        upd = _dot((kT * w_s).astype(BF16), v_aug)
        yield
        num = intra_mm[:, :MLSTM_V] + w_inter * inter_mm[:, :MLSTM_V]
        den = intra_mm[:, MLSTM_V:] + w_inter * inter_mm[:, MLSTM_V:]
        den = jnp.maximum(jnp.abs(den), floor)
        y_ref[bi, :, h * MLSTM_V:(h + 1) * MLSTM_V] = (num / den).astype(y_ref.dtype)
        st_ref[h] = w_prev * c_st + upd
        m_ref[h] = jnp.broadcast_to(m_new, m_ref.shape[1:])

    return [head_chain(h) for h in range(nh)]


def _mlstm_kernel(q_f, k_f, v_f, gt_f, gtT_f, q_r, k_r, v_r, gt_r, gtT_r, gb_r, gb_c,
                  yf_ref, yr_ref, st_ref, m_ref, *, n_chunks):
    step = pl.program_id(1)
    nh = MLSTM_HEADS

    @pl.when(step == 0)
    def _():
        st_ref[...] = jnp.zeros(st_ref.shape, F32)
        m_ref[...] = jnp.full(m_ref.shape, NEG, F32)

    scans = []
    for bi in range(q_f.shape[0]):
        lo = 2 * bi * nh
        scans.append(_mlstm_direction(q_f, k_f, v_f, gt_f, gtT_f, gb_r, gb_c, yf_ref, st_ref.at[lo:lo + nh],
                                      m_ref.at[lo:lo + nh], bi=bi, reverse=False, is_meta=step == 0))
        scans.append(_mlstm_direction(q_r, k_r, v_r, gt_r, gtT_r, gb_r, gb_c, yr_ref,
                                      st_ref.at[lo + nh:lo + 2 * nh], m_ref.at[lo + nh:lo + 2 * nh],
                                      bi=bi, reverse=True, is_meta=step == n_chunks - 1))
    _interleave([c for group in zip(*scans) for c in group], period=MLSTM_SKEW)


def _mlstm_scan(q, k, v, gt, gtT, consts):
    b, tp, _ = q.shape
    n_chunks = tp // CHUNK
    nb = _scan_batch(b)
    arrays = [(q, False), (k, False), (v, False), (gt, False), (gtT, True)]
    in_specs = (_scan_specs(arrays, n_chunks, False, nb) + _scan_specs(arrays, n_chunks, True, nb)
                + [_full(c.shape) for c in consts])
    out_specs = (_scan_specs([(v, False)], n_chunks, False, nb)
                 + _scan_specs([(v, False)], n_chunks, True, nb))
    out_shape = [jax.ShapeDtypeStruct(v.shape, BF16)] * 2
    return pl.pallas_call(
        functools.partial(_mlstm_kernel, n_chunks=n_chunks),
        grid=(b // nb, n_chunks), in_specs=in_specs, out_specs=out_specs, out_shape=out_shape,
        scratch_shapes=[pltpu.VMEM((nb * 2 * MLSTM_HEADS, CHUNK, 2 * MLSTM_V), F32),
                        pltpu.VMEM((nb * 2 * MLSTM_HEADS, 8, LANE), F32)],
        compiler_params=_params("parallel", "arbitrary"),
        name="mlstm_scan")(q, k, v, gt, gtT, q, k, v, gt, gtT, *consts)


def _out_kernel(ya_ref, yf_ref, yr_ref, xs_ref, z_ref, yc_ref, hf_ref, hr_ref, og_ref, h_ref,
                ag_ref, dsk_ref, ng_ref, mg_ref, w_ref, o_ref, *, tp, seq):
    tm = h_ref.shape[0]
    ya = _rms_rows(ya_ref[...].astype(F32), ag_ref[...])
    z = z_ref[...].astype(F32)
    yb = ((yf_ref[...].astype(F32) + yr_ref[...].astype(F32) + dsk_ref[...] * xs_ref[...])
          * (z * _sigmoid(z)))
    parts = [ya.astype(BF16), _rms_rows(yb, ng_ref[...]).astype(BF16), yc_ref[...].astype(BF16)]
    for hd in range(MLSTM_HEADS):
        lo, hi = hd * MLSTM_V, (hd + 1) * MLSTM_V
        hsum = hf_ref[:, lo:hi].astype(F32) + hr_ref[:, lo:hi].astype(F32)
        gate = _sigmoid(og_ref[:, lo:hi].astype(F32))
        parts.append((gate * _rms_rows(hsum, mg_ref[...])).astype(BF16))
    out = h_ref[...] + _dot(jnp.concatenate(parts, axis=1), w_ref[...])
    t = (pl.program_id(0) * tm + lax.broadcasted_iota(jnp.int32, (tm, 1), 0)) % tp
    is_pad = jnp.logical_and(t >= seq, t < seq + N_PAD)
    o_ref[...] = jnp.where(is_pad, 0.0, out)


def _out_proj(mixer_outs, h, consts, w, *, tp, seq, tm):
    rows, d = h.shape
    row_spec = lambda c: pl.BlockSpec((tm, c), lambda i: (i, 0))
    return pl.pallas_call(
        functools.partial(_out_kernel, tp=tp, seq=seq), grid=(rows // tm,),
        in_specs=([row_spec(a.shape[1]) for a in mixer_outs] + [row_spec(d)]
                  + [_full(c.shape) for c in consts] + [_full(w.shape)]),
        out_specs=row_spec(d), out_shape=jax.ShapeDtypeStruct((rows, d), F32),
        compiler_params=_params("parallel"), name="out_proj")(*mixer_outs, h, *consts, w)


def _ffn_kernel(h_ref, g_ref, wg_ref, wu_ref, wo_ref, o_ref, *, n_split):
    f = wg_ref.shape[1]
    tf = f // n_split
    tm = h_ref.shape[1]
    half = (tm // 2 // 8) * 8

    def tile_chain(r0, t):
        x = h_ref[0, r0:r0 + t, :]
        hn = _rms_rows(x, g_ref[...]).astype(BF16)
        yield
        acc = x
        for c in range(n_split):
            gate = _dot(hn, wg_ref[:, c * tf:(c + 1) * tf])
            up = _dot(hn, wu_ref[:, c * tf:(c + 1) * tf])
            yield
            act = (gate * _sigmoid(gate) * up).astype(BF16)
            acc = acc + _dot(act, wo_ref[c * tf:(c + 1) * tf, :])
            yield
        o_ref[0, r0:r0 + t, :] = acc

    _interleave([tile_chain(0, half), tile_chain(half, tm - half)])


def _ffn(h, g, wg, wu, wo, *, t_out, tm):
    b, _, d = h.shape
    f = wg.shape[1]
    n_split = 2 if (f // 2) % LANE == 0 else 1
    row_spec = pl.BlockSpec((1, tm, d), lambda i, j: (i, j, 0))
    resident = lambda a: pl.BlockSpec(a.shape, lambda i, j: (0, 0), pipeline_mode=pl.Buffered(1))
    return pl.pallas_call(
        functools.partial(_ffn_kernel, n_split=n_split), grid=(b, t_out // tm),
        in_specs=[row_spec, _full(g.shape), resident(wg), resident(wu), resident(wo)],
        out_specs=row_spec, out_shape=jax.ShapeDtypeStruct((b, t_out, d), F32),
        compiler_params=_params("parallel", "parallel"), name="ffn")(h, g, wg, wu, wo)


def _rope_tables_t(pos, rot_dim):
    inv = 1.0 / (ROPE_THETA ** (jnp.arange(0, rot_dim, 2, dtype=F32) / rot_dim))
    ang = pos[:, None] * inv[None, :]
    return jnp.cos(ang).T, jnp.sin(ang).T


def _col(v, n=None):
    v = v.astype(F32)
    if n is not None:
        v = jnp.pad(v, (0, n - v.shape[0]))
    return v[:, None]


def _row(v):
    return v.astype(F32)[None, :]


def kernel(x, meta_tokens, attn_norm_g, w_in, mla_q_norm_g, mla_kv_norm_g, mla_w_uq, mla_w_ukv, mla_q_head_g, mla_k_head_g, mla_out_g, ssd_conv_w, ssd_conv_b, ssd_dt_bias, ssd_a_log, ssd_d, ssd_norm_g, diff_q_head_g, diff_k_head_g, diff_lambda, diff_out_g, mlstm_i_bias, mlstm_f_bias, mlstm_out_g, w_out, ffn_norm_g, w_ffn_in, w_ffn_out):
    b, seq, d = x.shape
    depth = w_in.shape[0]
    tp = seq + CHUNK
    assert seq % CHUNK == 0
    tm_in = _pick_tile(tp, (640, 384, 128))
    tm_row = _pick_tile(tp, (1664, 640, 384, 128))
    tq = _pick_tile(tp, (1664, 640, 384, 128))
    tk = _pick_tile(seq, (256, 128))
    tm_flat = _pick_tile(b * tp, (512, 256, 128))
    tm_out = _pick_tile(seq, (512, 256, 128))

    meta = jnp.broadcast_to(meta_tokens[None].astype(x.dtype), (b, N_META, d))
    h = jnp.concatenate([x, jnp.zeros((b, N_PAD, d), x.dtype), meta], axis=1).reshape(b * tp, d)
    pos = jnp.concatenate([N_META + jnp.arange(seq, dtype=F32), jnp.zeros((N_PAD,), F32),
                           jnp.arange(N_META, dtype=F32)])
    cos_m, sin_m = _rope_tables_t(pos, MLA_ROPE)
    cos_d, sin_d = _rope_tables_t(pos, DIFF_ROPE)

    sizes = (MLA_Q_RANK, MLA_KV_RANK, MLA_ROPE, SSD_INNER, SSD_CONV_CH, 2 * SSD_HEADS,
             2 * DIFF_HEADS * DIFF_QK, 2 * DIFF_HEADS * DIFF_QK, DIFF_HEADS * DIFF_V,
             MLSTM_HEADS * MLSTM_QK, MLSTM_HEADS * MLSTM_QK, MLSTM_HEADS * MLSTM_V,
             MLSTM_HEADS * MLSTM_V, 2 * MLSTM_HEADS, 2 * MLSTM_HEADS)
    offs = [0]
    for s_ in sizes:
        offs.append(offs[-1] + s_)

    def cols(w, first, last):
        return w[:, offs[first]:offs[last + 1]]

    def pad_cols(w, n):
        return jnp.pad(w, ((0, 0), (0, n - w.shape[1])))

    for l in range(depth):
        lambda_init = 0.8 - 0.6 * math.exp(-0.3 * l)
        wl = w_in[l]
        h3 = h.reshape(b, tp, d)
        g_attn = _row(attn_norm_g[l])

        w_a = cols(wl, 0, 1).astype(BF16)
        w_kr = cols(wl, 2, 2).T.astype(BF16)
        w_uq = mla_w_uq[l].T
        w_ukv = mla_w_ukv[l].T.reshape(MLA_HEADS, MLA_NOPE + MLA_V, MLA_KV_RANK)
        w_uk = w_ukv[:, :MLA_NOPE].reshape(MLA_HEADS * MLA_NOPE, MLA_KV_RANK)
        w_uv = w_ukv[:, MLA_NOPE:].reshape(MLA_HEADS * MLA_V, MLA_KV_RANK)
        qT_a, k_a, vT_a = _in_proj_call(
            _mla_in_kernel, h3,
            [g_attn, w_a, _row(mla_q_norm_g[l]), _row(mla_kv_norm_g[l]), w_uq.astype(BF16),
             w_uk.astype(BF16), w_uv.astype(BF16), w_kr, _col(mla_q_head_g[l], HEAD_PAD),
             _col(mla_k_head_g[l], HEAD_PAD), (cos_m,), (sin_m,)],
            [((MLA_HEADS * HEAD_PAD, tp), BF16, "col"), ((MLA_HEADS, tp, HEAD_PAD), BF16, "head"),
             ((MLA_HEADS * MLA_V, tp), BF16, "col")], tm_row, "mla_in")
        y_a = _attention(qT_a, k_a, vT_a, [], diff=False, seq=seq, tq=tq, tk=tk, out_scale=1.0,
                         nsh=ATTN_HEADS_PER_STEP, name="mla_attn")

        w_c = cols(wl, 6, 8).T.astype(BF16)
        qT_c, k_c, vT_c = _in_proj_call(
            _diff_in_kernel, h3,
            [g_attn, w_c, _col(diff_q_head_g[l], HEAD_PAD), _col(diff_k_head_g[l], HEAD_PAD),
             (cos_d,), (sin_d,)],
            [((2 * DIFF_HEADS * HEAD_PAD, tp), BF16, "col"), ((2 * DIFF_HEADS, tp, HEAD_PAD), BF16, "head"),
             ((DIFF_HEADS * DIFF_V, tp), BF16, "col")], tm_row, "diff_in")
        lam = diff_lambda[l].astype(F32)
        lam_full = jnp.exp(jnp.sum(lam[0] * lam[1])) - jnp.exp(jnp.sum(lam[2] * lam[3])) + lambda_init
        y_c = _attention(qT_c, k_c, vT_c, [jnp.full((8, LANE), lam_full, F32), _col(diff_out_g[l])],
                         diff=True, seq=seq, tq=tq, tk=tk, out_scale=1.0 - lambda_init, nsh=ATTN_HEADS_PER_STEP,
                         name="diff_attn")

        n_b = SSD_INNER + SSD_CONV_CH + 2 * SSD_HEADS
        w_b = pad_cols(cols(wl, 3, 5), -(-n_b // LANE) * LANE).astype(BF16)
        w_dt = cols(wl, 5, 5).T.astype(BF16)
        z_b, xbc, dt, dtT = _in_proj_call(
            _ssd_in_kernel, h3, [g_attn, w_b, w_dt],
            [((tp, SSD_INNER), BF16, "row"), ((tp, SSD_CONV_CH), F32, "row"),
             ((tp, 2 * SSD_HEADS), F32, "row"), ((2 * SSD_HEADS, tp), F32, "col")], tm_row, "ssd_in")
        w8 = jnp.pad(ssd_conv_w[l].astype(F32), ((0, 8 - SSD_CONV), (0, 0)))
        xs_b, bc_b = _ssd_conv(xbc, w8, _row(ssd_conv_b[l]), tm_in)
        dt_bias = ssd_dt_bias[l].astype(F32).reshape(-1)
        a_neg = -jnp.exp(ssd_a_log[l].astype(F32)).reshape(-1)
        y_bf, y_br = _ssd_scan(xs_b, bc_b, dt, dtT, [_row(dt_bias), _col(dt_bias), _row(a_neg), _col(a_neg)])

        n_d = 2 * MLSTM_HEADS * MLSTM_QK + 2 * MLSTM_HEADS * MLSTM_V + 4 * MLSTM_HEADS
        w_d = pad_cols(cols(wl, 9, 14), -(-n_d // LANE) * LANE).astype(BF16)
        w_g = cols(wl, 13, 14).T.astype(BF16)
        q_d, k_d, v_d, o_d, gt, gtT = _in_proj_call(
            _mlstm_in_kernel, h3, [g_attn, w_d, w_g],
            [((tp, MLSTM_HEADS * MLSTM_QK), BF16, "row"), ((tp, MLSTM_HEADS * MLSTM_QK), BF16, "row"),
             ((tp, MLSTM_HEADS * MLSTM_V), BF16, "row"), ((tp, MLSTM_HEADS * MLSTM_V), BF16, "row"),
             ((tp, 4 * MLSTM_HEADS), F32, "row"), ((4 * MLSTM_HEADS, tp), F32, "col")], tm_row, "mlstm_in")
        gate_bias = jnp.concatenate([mlstm_i_bias[l].reshape(-1), mlstm_f_bias[l].reshape(-1)]).astype(F32)
        h_df, h_dr = _mlstm_scan(q_d, k_d, v_d, gt, gtT, [_row(gate_bias), _col(gate_bias)])

        flat = lambda a: a.reshape(b * tp, a.shape[2])
        mixer_outs = [flat(a) for a in (y_a, y_bf, y_br, xs_b, z_b, y_c, h_df, h_dr, o_d)]
        out_consts = [_row(mla_out_g[l]), _row(jnp.repeat(ssd_d[l], SSD_HEAD_DIM)), _row(ssd_norm_g[l]),
                      _row(mlstm_out_g[l])]
        h = _out_proj(mixer_outs, h, out_consts, w_out[l].astype(BF16), tp=tp, seq=seq, tm=tm_flat)
        f = w_ffn_out.shape[1]
        last = l == depth - 1
        h = _ffn(h.reshape(b, tp, d), _row(ffn_norm_g[l]), w_ffn_in[l][:, :f].astype(BF16),
                 w_ffn_in[l][:, f:].astype(BF16), w_ffn_out[l].astype(BF16),
                 t_out=seq if last else tp, tm=tm_out if last else _pick_tile(tp, (832, 640, 384, 128)))
        h = h.reshape(-1, d)

    return h.reshape(b, seq, d).astype(x.dtype)
```
